```python
import math
import jax, jax.numpy as jnp
from jax import lax
import numpy as np

D_MODEL = 1024
BATCH = 4
SEQ = 4096
DEPTH = 4
DEC_BATCH = 128
DEC_SEQ = 8
PAST_LEN = 8192
PAGE_SIZE = 128

N_A_LAYERS = DEPTH // 2
N_B_LAYERS = DEPTH - N_A_LAYERS
A_HEADS = 8
A_DQK = D_MODEL // 16
A_DV = D_MODEL // A_HEADS
A_CHUNK = 64
A_F_BIAS_LO = 3.0
A_F_BIAS_HI = 6.0
B_HEADS = 8
B_D_NOPE = 128
B_D_ROPE = 64
B_D_V = 128
B_KV_RANK = D_MODEL // 4
B_Q_RANK = D_MODEL // 2
Q_BLOCK = 128
ROPE_THETA = 10000.0
ATTN_SCALE = (B_D_NOPE + B_D_ROPE) ** -0.5
N_EXPERTS = 16
N_GROUPS = 4
EXPERTS_PER_GROUP = N_EXPERTS // N_GROUPS
TOP_K = 2
D_EXPERT = D_MODEL // 2
MOE_BLOCK = 128
ALPHA = (2 * DEPTH) ** 0.25
BETA = (8 * DEPTH) ** -0.25
EPS = 1e-6

kernel_name = 'yoco_mlstm_mla_grouped_moe_decoder_step'

f32 = jnp.float32


def _layer_norm(x, g, b):
    x32 = x.astype(f32)
    mu = x32.mean(-1, keepdims=True)
    var = jnp.square(x32 - mu).mean(-1, keepdims=True)
    return ((x32 - mu) * lax.rsqrt(var + EPS) * g.astype(f32) + b.astype(f32)).astype(x.dtype)


def _rms_norm(x, g):
    x32 = x.astype(f32)
    return (x32 * lax.rsqrt(jnp.square(x32).mean(-1, keepdims=True) + EPS) * g.astype(f32)).astype(x.dtype)


def _rope(x, pos):
    half = x.shape[-1] // 2
    inv = jnp.power(ROPE_THETA, -jnp.arange(half, dtype=f32) / half)
    ang = pos.astype(f32)[:, None] * inv[None, :]
    cos = jnp.cos(ang)[None, :, None, :]
    sin = jnp.sin(ang)[None, :, None, :]
    x32 = x.astype(f32)
    x1, x2 = x32[..., :half], x32[..., half:]
    return jnp.concatenate([x1 * cos - x2 * sin, x1 * sin + x2 * cos], axis=-1).astype(x.dtype)


def _mlstm_chunkwise(q, k, v, log_i, log_f, C0, n0, m0, chunk):
    B, T = q.shape[:2]
    nc = T // chunk

    def blocks(a):
        a = a.astype(f32).reshape((B, nc, chunk) + a.shape[2:])
        return jnp.moveaxis(a, 1, 0)

    causal = jnp.tril(jnp.ones((chunk, chunk), bool))

    def step(carry, xs):
        C, n, m = carry
        qc, kc, vc, li, lf = xs
        b = jnp.cumsum(lf, axis=1)
        d = b[:, :, None, :] - b[:, None, :, :] + li[:, None, :, :]
        d = jnp.where(causal[None, :, :, None], d, -jnp.inf)
        a = b + m[:, None, :]
        m_t = jnp.maximum(a, d.max(axis=2))
        s = jnp.einsum('bthk,bshk->btsh', qc, kc) * jnp.exp(d - m_t[:, :, None, :])
        inter = jnp.exp(a - m_t)
        num = jnp.einsum('btsh,bshv->bthv', s, vc) + inter[..., None] * jnp.einsum('bthk,bhkv->bthv', qc, C)
        den = s.sum(axis=2) + inter * jnp.einsum('bthk,bhk->bth', qc, n)
        h = num / jnp.maximum(jnp.abs(den), jnp.exp(-m_t))[..., None]
        b_end = b[:, -1]
        g = b_end[:, None, :] - b + li
        m_new = jnp.maximum(b_end + m, g.max(axis=1))
        w = jnp.exp(g - m_new[:, None, :])
        decay = jnp.exp(b_end + m - m_new)
        C_new = decay[..., None, None] * C + jnp.einsum('bsh,bshk,bshv->bhkv', w, kc, vc)
        n_new = decay[..., None] * n + jnp.einsum('bsh,bshk->bhk', w, kc)
        return (C_new, n_new, m_new), h

    carry0 = (C0.astype(f32), n0.astype(f32), m0.astype(f32))
    xs = (blocks(q), blocks(k), blocks(v), blocks(log_i), blocks(log_f))
    (C, n, m), h = lax.scan(step, carry0, xs)
    h = jnp.moveaxis(h, 0, 1).reshape((B, T) + h.shape[3:])
    return h, C, n, m


def _mlstm_mixer(h, w_in, b_gates, norm_g, w_out, C0, n0, m0):
    B, T, _ = h.shape
    hqk, hv = A_HEADS * A_DQK, A_HEADS * A_DV
    q, k, v, o, gates = jnp.split(h @ w_in, [hqk, 2 * hqk, 2 * hqk + hv, 2 * hqk + 2 * hv], axis=-1)
    q = q.reshape(B, T, A_HEADS, A_DQK)
    k = k.reshape(B, T, A_HEADS, A_DQK) * (A_DQK ** -0.5)
    v = v.reshape(B, T, A_HEADS, A_DV)
    gates = gates.astype(f32) + b_gates.astype(f32)
    log_i = gates[..., :A_HEADS]
    log_f = jax.nn.log_sigmoid(gates[..., A_HEADS:])
    chunk = A_CHUNK if T % A_CHUNK == 0 else T
    hh, C, n, m = _mlstm_chunkwise(q, k, v, log_i, log_f, C0, n0, m0, chunk)
    mu = hh.mean(-1, keepdims=True)
    var = jnp.square(hh - mu).mean(-1, keepdims=True)
    hn = ((hh - mu) * lax.rsqrt(var + EPS)).reshape(B, T, hv) * norm_g.astype(f32)
    y = (hn.astype(h.dtype) * jax.nn.sigmoid(o)) @ w_out
    return y, C.astype(C0.dtype), n.astype(n0.dtype), m.astype(m0.dtype)


def _shared_kv(x_mid, pos, w_kv_a, kv_norm_g):
    kva = x_mid @ w_kv_a
    lat = _rms_norm(kva[..., :B_KV_RANK], kv_norm_g)
    kr = _rope(kva[..., B_KV_RANK:][:, :, None, :], pos)[:, :, 0, :]
    return lat, kr


def _attend_block(ql, qr, qp, kv_lat, kv_rope, k_pos):
    s = jnp.einsum('bthc,bsc->bhts', ql, kv_lat) + jnp.einsum('bthr,bsr->bhts', qr, kv_rope)
    s = s.astype(f32) * ATTN_SCALE
    s = jnp.where((k_pos[None, :] <= qp[:, None])[None, None], s, -jnp.inf)
    p = jax.nn.softmax(s, axis=-1)
    return jnp.einsum('bhts,bsc->bthc', p.astype(kv_lat.dtype), kv_lat)


def _attend(q_lat, q_rope, q_pos, kv_lat, kv_rope, k_pos):
    B, T = q_lat.shape[:2]
    if T % Q_BLOCK != 0:
        return _attend_block(q_lat, q_rope, q_pos, kv_lat, kv_rope, k_pos)
    nb = T // Q_BLOCK

    def split(a):
        return jnp.moveaxis(a.reshape((B, nb, Q_BLOCK) + a.shape[2:]), 1, 0)

    out = lax.map(lambda blk: _attend_block(blk[0], blk[1], blk[2], kv_lat, kv_rope, k_pos),
                  (split(q_lat), split(q_rope), q_pos.reshape(nb, Q_BLOCK)))
    return jnp.moveaxis(out, 0, 1).reshape((B, T) + out.shape[3:])


def _mla_mixer(h, pos, w_dq, q_norm_g, w_uq, w_o, w_uk, w_uv, kv_lat, kv_rope, k_pos):
    B, T, _ = h.shape
    cq = _rms_norm(h @ w_dq, q_norm_g)
    q = (cq @ w_uq).reshape(B, T, B_HEADS, B_D_NOPE + B_D_ROPE)
    q_nope, q_rope = q[..., :B_D_NOPE], _rope(q[..., B_D_NOPE:], pos)
    q_lat = jnp.einsum('bthn,chn->bthc', q_nope, w_uk)
    o_lat = _attend(q_lat, q_rope, pos, kv_lat, kv_rope, k_pos)
    o = jnp.einsum('bthc,chv->bthv', o_lat, w_uv).reshape(B, T, B_HEADS * B_D_V)
    return o @ w_o


def _moe(h, router_w, router_b, w_gate, w_up, w_down):
    B, T, D = h.shape
    N = B * T
    xf = h.reshape(N, D)
    scores = jax.nn.sigmoid((xf @ router_w).astype(f32))
    sel = (scores + router_b.astype(f32)).reshape(N, N_GROUPS, EXPERTS_PER_GROUP)
    group_score = lax.top_k(sel, 2)[0].sum(-1)
    grp = jnp.argmax(group_score, axis=-1)
    in_group = jnp.take_along_axis(sel, grp[:, None, None], axis=1)[:, 0]
    _, local = lax.top_k(in_group, TOP_K)
    expert = grp[:, None] * EXPERTS_PER_GROUP + local
    wts = jnp.take_along_axis(scores, expert, axis=1)
    wts = wts / wts.sum(-1, keepdims=True)
    NK = N * TOP_K
    flat_e = expert.reshape(NK)
    flat_t = jnp.repeat(jnp.arange(N, dtype=jnp.int32), TOP_K)
    flat_w = wts.reshape(NK)
    order = jnp.argsort(flat_e)
    se, st, sw = flat_e[order], flat_t[order], flat_w[order]
    counts = jnp.bincount(flat_e, length=N_EXPERTS)
    starts = jnp.cumsum(counts) - counts
    padded = (counts + MOE_BLOCK - 1) // MOE_BLOCK * MOE_BLOCK
    pends = jnp.cumsum(padded)
    pstarts = pends - padded
    dest = pstarts[se] + jnp.arange(NK) - starts[se]
    n_blocks = -(-NK // MOE_BLOCK) + N_EXPERTS
    P = n_blocks * MOE_BLOCK
    tok_pad = jnp.full((P,), N, jnp.int32).at[dest].set(st)
    w_pad = jnp.zeros((P,), f32).at[dest].set(sw)
    block_e = jnp.minimum(jnp.searchsorted(pends, jnp.arange(n_blocks) * MOE_BLOCK, side='right'), N_EXPERTS - 1)
    x_pad = jnp.concatenate([xf, jnp.zeros((1, D), xf.dtype)], axis=0)[tok_pad].reshape(n_blocks, MOE_BLOCK, D)

    def expert_block(args):
        xb, e = args
        hid = jax.nn.silu(xb @ w_gate[e]) * (xb @ w_up[e])
        return hid @ w_down[e]

    yb = lax.map(expert_block, (x_pad, block_e)).reshape(P, D)
    out = jnp.zeros((N + 1, D), yb.dtype).at[tok_pad].add(yb * w_pad[:, None].astype(yb.dtype))
    return out[:N].reshape(B, T, D).astype(h.dtype)


def setup_inputs(seed: int = 0) -> dict:
    key = jax.random.key(seed)
    ks = iter(jax.random.split(key, 48))

    def nrm(shape, scale):
        return jax.random.normal(next(ks), shape, f32) * scale

    n_pages = PAST_LEN // PAGE_SIZE
    n_pool = (DEC_BATCH * n_pages * 5) // 4
    page_table = jax.random.permutation(next(ks), n_pool)[:DEC_BATCH * n_pages].reshape(DEC_BATCH, n_pages).astype(jnp.int32)
    hqk, hv = A_HEADS * A_DQK, A_HEADS * A_DV
    a_in = 2 * hqk + 2 * hv + 2 * A_HEADS
    b_i = nrm((N_A_LAYERS, A_HEADS), 0.1)
    b_f = jnp.linspace(A_F_BIAS_LO, A_F_BIAS_HI, A_HEADS)[None, :] + nrm((N_A_LAYERS, A_HEADS), 0.1)
    return {
        'x_prompt': nrm((BATCH, SEQ, D_MODEL), 1.0),
        'x_sample': nrm((DEC_BATCH, DEC_SEQ, D_MODEL), 1.0),
        'cache_latent': nrm((n_pool, PAGE_SIZE, B_KV_RANK), 1.0),
        'cache_krope': nrm((n_pool, PAGE_SIZE, B_D_ROPE), 1.0),
        'page_table': page_table,
        'state_C': nrm((N_A_LAYERS, DEC_BATCH, A_HEADS, A_DQK, A_DV), 0.3),
        'state_n': nrm((N_A_LAYERS, DEC_BATCH, A_HEADS, A_DQK), 0.3),
        'state_m': nrm((N_A_LAYERS, DEC_BATCH, A_HEADS), 1.0),
        'c_prompt': nrm((BATCH, D_MODEL), 1.0),
        'c_sample': nrm((DEC_BATCH, D_MODEL), 1.0),
        'ada_w': nrm((DEPTH, D_MODEL, 6 * D_MODEL), 0.5 * D_MODEL ** -0.5),
        'ada_b': nrm((DEPTH, 6 * D_MODEL), 0.02),
        'ln_g': 1.0 + nrm((DEPTH, 2, D_MODEL), 0.02),
        'ln_b': nrm((DEPTH, 2, D_MODEL), 0.02),
        'a_w_in': nrm((N_A_LAYERS, D_MODEL, a_in), D_MODEL ** -0.5),
        'a_b_gates': jnp.concatenate([b_i, b_f], axis=-1),
        'a_norm_g': 1.0 + nrm((N_A_LAYERS, hv), 0.02),
        'a_w_out': nrm((N_A_LAYERS, hv, D_MODEL), BETA * hv ** -0.5),
        'b_w_kv_a': nrm((D_MODEL, B_KV_RANK + B_D_ROPE), D_MODEL ** -0.5),
        'b_kv_norm_g': 1.0 + nrm((B_KV_RANK,), 0.02),
        'b_w_uk': nrm((B_KV_RANK, B_HEADS, B_D_NOPE), B_KV_RANK ** -0.5),
        'b_w_uv': nrm((B_KV_RANK, B_HEADS, B_D_V), B_KV_RANK ** -0.5),
        'b_w_dq': nrm((N_B_LAYERS, D_MODEL, B_Q_RANK), D_MODEL ** -0.5),
        'b_q_norm_g': 1.0 + nrm((N_B_LAYERS, B_Q_RANK), 0.02),
        'b_w_uq': nrm((N_B_LAYERS, B_Q_RANK, B_HEADS * (B_D_NOPE + B_D_ROPE)), B_Q_RANK ** -0.5),
        'b_w_o': nrm((N_B_LAYERS, B_HEADS * B_D_V, D_MODEL), BETA * (B_HEADS * B_D_V) ** -0.5),
        'router_w': nrm((D_MODEL, N_EXPERTS), D_MODEL ** -0.5),
        'router_b': nrm((N_EXPERTS,), 0.01),
        'e_w_gate': nrm((DEPTH, N_EXPERTS, D_MODEL, D_EXPERT), D_MODEL ** -0.5),
        'e_w_up': nrm((DEPTH, N_EXPERTS, D_MODEL, D_EXPERT), D_MODEL ** -0.5),
        'e_w_down': nrm((DEPTH, N_EXPERTS, D_EXPERT, D_MODEL), BETA * D_EXPERT ** -0.5),
    }


def reference(x_prompt, x_sample, cache_latent, cache_krope, page_table, state_C, state_n, state_m,
              c_prompt, c_sample, ada_w, ada_b, ln_g, ln_b, a_w_in, a_b_gates, a_norm_g, a_w_out,
              b_w_kv_a, b_kv_norm_g, b_w_uk, b_w_uv, b_w_dq, b_q_norm_g, b_w_uq, b_w_o,
              router_w, router_b, e_w_gate, e_w_up, e_w_down):

    def trunk(x, c, pos, C0, n0, m0, past_lat, past_kr, past_pos):
        mods = jnp.einsum('bd,lde->lbe', jax.nn.silu(c), ada_w) + ada_b[:, None, :]
        new_C, new_n, new_m = [], [], []
        kv_lat = kv_rope = k_pos = lat_rows = rope_rows = None
        for l in range(DEPTH):
            shift1, scale1, gate1, shift2, scale2, gate2 = jnp.split(mods[l][:, None, :], 6, axis=-1)
            h = x * (1 + scale1) + shift1
            if l < N_A_LAYERS:
                y, Cl, nl, ml = _mlstm_mixer(h, a_w_in[l], a_b_gates[l], a_norm_g[l], a_w_out[l], C0[l], n0[l], m0[l])
                new_C.append(Cl)
                new_n.append(nl)
                new_m.append(ml)
            else:
                j = l - N_A_LAYERS
                y = _mla_mixer(h, pos, b_w_dq[j], b_q_norm_g[j], b_w_uq[j], b_w_o[j], b_w_uk, b_w_uv,
                               kv_lat, kv_rope, k_pos)
            x = _layer_norm(ALPHA * x + gate1 * y, ln_g[l, 0], ln_b[l, 0])
            h = x * (1 + scale2) + shift2
            y = _moe(h, router_w, router_b, e_w_gate[l], e_w_up[l], e_w_down[l])
            x = _layer_norm(ALPHA * x + gate2 * y, ln_g[l, 1], ln_b[l, 1])
            if l == N_A_LAYERS - 1:
                lat_rows, rope_rows = _shared_kv(x, pos, b_w_kv_a, b_kv_norm_g)
                kv_lat = jnp.concatenate([past_lat, lat_rows], axis=1)
                kv_rope = jnp.concatenate([past_kr, rope_rows], axis=1)
                k_pos = jnp.concatenate([past_pos, pos])
        return x, jnp.stack(new_C), jnp.stack(new_n), jnp.stack(new_m), lat_rows, rope_rows

    Bp, Tp = x_prompt.shape[:2]
    dt = x_prompt.dtype
    y_prompt, C_p, n_p, m_p, lat_p, krope_p = trunk(
        x_prompt, c_prompt, jnp.arange(Tp, dtype=jnp.int32),
        jnp.zeros((N_A_LAYERS, Bp, A_HEADS, A_DQK, A_DV), dt),
        jnp.zeros((N_A_LAYERS, Bp, A_HEADS, A_DQK), dt),
        jnp.zeros((N_A_LAYERS, Bp, A_HEADS), dt),
        jnp.zeros((Bp, 0, B_KV_RANK), dt), jnp.zeros((Bp, 0, B_D_ROPE), dt), jnp.zeros((0,), jnp.int32))

    Bs, Ts = x_sample.shape[:2]
    n_pages = page_table.shape[1]
    past_len = n_pages * PAGE_SIZE
    past_lat = cache_latent[page_table].reshape(Bs, past_len, B_KV_RANK)
    past_kr = cache_krope[page_table].reshape(Bs, past_len, B_D_ROPE)
    y_sample, C_s, n_s, m_s, lat_s, krope_s = trunk(
        x_sample, c_sample, past_len + jnp.arange(Ts, dtype=jnp.int32),
        state_C, state_n, state_m, past_lat, past_kr, jnp.arange(past_len, dtype=jnp.int32))

    return (y_prompt, y_sample, C_p, n_p, m_p, lat_p, krope_p, C_s, n_s, m_s, lat_s, krope_s)
```

```python
import functools

import jax
import jax.numpy as jnp
from jax import lax
from jax.experimental import pallas as pl
from jax.experimental.pallas import tpu as pltpu

f32 = jnp.float32
bf16 = jnp.bfloat16

DEPTH = 4
N_A_LAYERS = 2
HEADS = 8
A_DQK = 64
A_DV = 128
B_D_NOPE = 128
B_D_ROPE = 64
B_KV_RANK = 256
ROPE_THETA = 10000.0
ATTN_SCALE = (B_D_NOPE + B_D_ROPE) ** -0.5
N_EXPERTS = 16
N_GROUPS = 4
PER_GROUP = 4
PAGE = 128
ALPHA = (2 * DEPTH) ** 0.25
EPS = 1e-6
NEG_INF = float("-inf")

VMEM_LIMIT_BYTES = 56 * 1024 * 1024
LANES = 128
MLSTM_CHUNK = 128
SAMPLE_GROUP = 16
MOE_BLOCK = 256
PAGES_PER_STEP = 16


def _cparams(sem):
    return pltpu.CompilerParams(dimension_semantics=sem, vmem_limit_bytes=VMEM_LIMIT_BYTES)


def _dot(a, b):
    return jnp.dot(a, b, preferred_element_type=f32)


def _dot_nt(a, b):
    return lax.dot_general(a, b, (((1,), (1,)), ((), ())), preferred_element_type=f32)


def _dot_exact(a, b):
    return jnp.dot(a, b, preferred_element_type=f32, precision=lax.Precision.HIGHEST)


def _sigmoid(x):
    return 1.0 / (1.0 + jnp.exp(-x))


def _log_sigmoid(x):
    return jnp.minimum(x, 0.0) - jnp.log(1.0 + jnp.exp(-jnp.abs(x)))


def _ada_kernel(c_ref, w_ref, b_ref, o_ref):
    c = c_ref[...]
    sc = (c * _sigmoid(c)).astype(bf16)
    o_ref[0] = _dot(sc, w_ref[0].astype(bf16)) + b_ref[0]


def _ada_mods(c_all, ada_w, ada_b):
    depth, d, e = ada_w.shape
    bc = c_all.shape[0]
    tn = 1536
    return pl.pallas_call(
        _ada_kernel,
        grid=(depth, e // tn),
        in_specs=[
            pl.BlockSpec((bc, d), lambda l, j: (0, 0)),
            pl.BlockSpec((1, d, tn), lambda l, j: (l, 0, j)),
            pl.BlockSpec((1, 1, tn), lambda l, j: (l, 0, j)),
        ],
        out_specs=pl.BlockSpec((1, bc, tn), lambda l, j: (l, 0, j)),
        out_shape=jax.ShapeDtypeStruct((depth, bc, e), f32),
        compiler_params=_cparams(("parallel", "parallel")),
        name="ada_mods",
    )(c_all, ada_w, ada_b.reshape(depth, 1, e))


def _mod_spec(mod, tm):
    d = mod.shape[-1]
    if mod.shape[1] == 1:
        return pl.BlockSpec((1, 1, d), lambda g, i: (g, 0, 0))
    return pl.BlockSpec((1, tm, d), lambda g, i: (g, i, 0))


def _tok_spec(tm, d):
    return pl.BlockSpec((1, tm, d), lambda g, i: (g, i, 0))


def _full_spec(shape):
    nd = len(shape)
    return pl.BlockSpec(shape, lambda g, i: (0,) * nd)


def _route(h2, rw_ref, rwh_ref, rb_ref):
    hi = h2.astype(bf16)
    lo = (h2 - hi.astype(f32)).astype(bf16)
    p = _dot(hi, rw_ref[...]) + _dot(lo, rwh_ref[...])
    pt = p.T
    logits = pt[0:N_EXPERTS] + pt[N_EXPERTS:2 * N_EXPERTS]
    sc = _sigmoid(logits)
    sel = sc + rb_ref[...]
    a = [sel[PER_GROUP * j:PER_GROUP * (j + 1)] for j in range(PER_GROUP)]
    s = [sc[PER_GROUP * j:PER_GROUP * (j + 1)] for j in range(PER_GROUP)]
    hi01, lo01 = jnp.maximum(a[0], a[1]), jnp.minimum(a[0], a[1])
    hi23, lo23 = jnp.maximum(a[2], a[3]), jnp.minimum(a[2], a[3])
    gs = jnp.maximum(hi01, hi23) + jnp.maximum(jnp.minimum(hi01, hi23), jnp.maximum(lo01, lo23))
    best = gs[0:1]
    grp = jnp.zeros_like(best)
    for g in range(1, N_GROUPS):
        better = gs[g:g + 1] > best
        grp = jnp.where(better, float(g), grp)
        best = jnp.where(better, gs[g:g + 1], best)
    mv, sv = [], []
    for j in range(PER_GROUP):
        m_j = a[j][0:1]
        s_j = s[j][0:1]
        for g in range(1, N_GROUPS):
            m_j = jnp.where(grp == float(g), a[j][g:g + 1], m_j)
            s_j = jnp.where(grp == float(g), s[j][g:g + 1], s_j)
        mv.append(m_j)
        sv.append(s_j)

    def first_argmax(vals):
        bv, bi = vals[0], jnp.zeros_like(vals[0])
        for j in range(1, PER_GROUP):
            better = vals[j] > bv
            bi = jnp.where(better, float(j), bi)
            bv = jnp.where(better, vals[j], bv)
        return bi

    i1 = first_argmax(mv)
    i2 = first_argmax([jnp.where(i1 == float(j), NEG_INF, mv[j]) for j in range(PER_GROUP)])
    w1, w2 = sv[0], sv[0]
    for j in range(1, PER_GROUP):
        w1 = jnp.where(i1 == float(j), sv[j], w1)
        w2 = jnp.where(i2 == float(j), sv[j], w2)
    tot = w1 + w2
    e1 = grp * float(PER_GROUP) + i1
    e2 = grp * float(PER_GROUP) + i2
    z = jnp.zeros_like(e1)
    return jnp.concatenate([e1, e2, w1 / tot, w2 / tot, z, z, z, z], axis=0)


def _residual_ln_route(y, x_ref, gate_ref, lng_ref, lnb_ref, sc_ref, sh_ref, rw_ref, rwh_ref, rb_ref,
                       x1_ref, h2_ref, route_ref):
    r = ALPHA * x_ref[0] + gate_ref[0] * y
    mu = jnp.mean(r, axis=-1, keepdims=True)
    cen = r - mu
    var = jnp.mean(cen * cen, axis=-1, keepdims=True)
    x1 = cen * lax.rsqrt(var + EPS) * lng_ref[...] + lnb_ref[...]
    x1_ref[0] = x1
    h2 = x1 * (1.0 + sc_ref[0]) + sh_ref[0]
    h2_ref[0] = h2.astype(bf16)
    route_ref[0] = _route(h2, rw_ref, rwh_ref, rb_ref)


def _post_specs(x, gate, scale2, shift2, tm, d):
    in_specs = [
        _tok_spec(tm, d), _mod_spec(gate, tm), _full_spec((1, d)), _full_spec((1, d)),
        _mod_spec(scale2, tm), _mod_spec(shift2, tm),
        _full_spec((d, LANES)), _full_spec((d, LANES)), _full_spec((N_EXPERTS, 1)),
    ]
    g, t, _ = x.shape
    out_specs = [
        _tok_spec(tm, d), _tok_spec(tm, d),
        pl.BlockSpec((1, 8, tm), lambda gi, i: (gi, 0, i)),
    ]
    out_shape = [
        jax.ShapeDtypeStruct((g, t, d), f32),
        jax.ShapeDtypeStruct((g, t, d), bf16),
        jax.ShapeDtypeStruct((g, 8, t), f32),
    ]
    return in_specs, out_specs, out_shape


def _mlstm_in_kernel(x_ref, sc_ref, sh_ref, w_ref, bg_ref, qkv_ref, o_ref, g_ref):
    h = x_ref[0] * (1.0 + sc_ref[0]) + sh_ref[0]
    z = _dot(h.astype(bf16), w_ref[...])
    hqk2 = 2 * HEADS * A_DQK
    hv = HEADS * A_DV
    qkv_ref[0] = z[:, :hqk2 + hv].astype(bf16)
    o_ref[0] = z[:, hqk2 + hv:hqk2 + 2 * hv]
    g_ref[0] = z[:, hqk2 + 2 * hv:] + bg_ref[...]


def _mlstm_in(x, scale, shift, w_pad, bg_pad, tm):
    g, t, d = x.shape
    n_qkv = 2 * HEADS * A_DQK + HEADS * A_DV
    hv = HEADS * A_DV
    return pl.pallas_call(
        _mlstm_in_kernel,
        grid=(g, t // tm),
        in_specs=[_tok_spec(tm, d), _mod_spec(scale, tm), _mod_spec(shift, tm),
                  _full_spec(w_pad.shape), _full_spec((1, LANES))],
        out_specs=[_tok_spec(tm, n_qkv), _tok_spec(tm, hv), _tok_spec(tm, LANES)],
        out_shape=[jax.ShapeDtypeStruct((g, t, n_qkv), bf16),
                   jax.ShapeDtypeStruct((g, t, hv), f32),
                   jax.ShapeDtypeStruct((g, t, LANES), f32)],
        compiler_params=_cparams(("parallel", "parallel")),
        name="mlstm_in",
    )(x, scale, shift, w_pad, bg_pad)


def _mlstm_out_kernel(u_ref, w_ref, x_ref, gate_ref, lng_ref, lnb_ref, sc_ref, sh_ref, rw_ref, rwh_ref,
                      rb_ref, x1_ref, h2_ref, route_ref):
    y = _dot(u_ref[0], w_ref[...])
    _residual_ln_route(y, x_ref, gate_ref, lng_ref, lnb_ref, sc_ref, sh_ref, rw_ref, rwh_ref, rb_ref,
                       x1_ref, h2_ref, route_ref)


def _mlstm_out(u, w_out, x, gate, ln_g, ln_b, scale2, shift2, rw, rwh, rb, tm):
    g, t, d = x.shape
    post_in, out_specs, out_shape = _post_specs(x, gate, scale2, shift2, tm, d)
    return pl.pallas_call(
        _mlstm_out_kernel,
        grid=(g, t // tm),
        in_specs=[_tok_spec(tm, u.shape[-1]), _full_spec(w_out.shape)] + post_in,
        out_specs=out_specs,
        out_shape=out_shape,
        compiler_params=_cparams(("parallel", "parallel")),
        name="mlstm_out",
    )(u, w_out, x, gate, ln_g, ln_b, scale2, shift2, rw, rwh, rb)


def _head_norm_gate(hh, ng, o):
    mu = jnp.mean(hh, axis=-1, keepdims=True)
    cen = hh - mu
    var = jnp.mean(cen * cen, axis=-1, keepdims=True)
    return (cen * lax.rsqrt(var + EPS) * ng) * _sigmoid(o)


def _mlstm_prompt_kernel(qkv_ref, kt_ref, o_ref, gcol_ref, grow_ref, ng_ref,
                         u_ref, c_out, n_out, m_out, c_s, n_s, m_s):
    c = pl.program_id(1)
    nc = pl.num_programs(1)
    L = qkv_ref.shape[1]
    hqk = HEADS * A_DQK

    @pl.when(c == 0)
    def _():
        c_s[...] = jnp.zeros_like(c_s)
        n_s[...] = jnp.zeros_like(n_s)
        m_s[...] = jnp.zeros_like(m_s)

    r_i = lax.broadcasted_iota(jnp.int32, (L, L), 0)
    c_i = lax.broadcasted_iota(jnp.int32, (L, L), 1)
    causal = c_i <= r_i
    tri = causal.astype(f32)
    tri_t = (r_i <= c_i).astype(f32)

    gcol = gcol_ref[0]
    grow = grow_ref[0]
    b_col = _dot_exact(tri, _log_sigmoid(gcol))
    b_row = _dot_exact(_log_sigmoid(grow[HEADS:2 * HEADS]), tri_t)
    li_row = grow[0:HEADS]

    for h in range(HEADS):
        q_h = qkv_ref[0, :, A_DQK * h:A_DQK * (h + 1)]
        v_h = qkv_ref[0, :, 2 * hqk + A_DV * h:2 * hqk + A_DV * (h + 1)]
        kt_h = kt_ref[0, A_DQK * h:A_DQK * (h + 1), :]
        s_qk = _dot(q_h, kt_h)
        bc = b_col[:, HEADS + h:HEADS + h + 1]
        br = b_row[h:h + 1]
        lir = li_row[h:h + 1]
        m_prev = m_s[h:h + 1, 0:1]
        d = jnp.where(causal, bc - br + lir, NEG_INF)
        a = bc + m_prev
        m_t = jnp.maximum(a, jnp.max(d, axis=-1, keepdims=True))
        p = s_qk * jnp.exp(d - m_t)
        inter = jnp.exp(a - m_t)
        c_aug = jnp.concatenate([c_s[h], n_s[h]], axis=1).astype(bf16)
        qc = _dot(q_h, c_aug)
        num = _dot(p.astype(bf16), v_h) + inter * qc[:, :A_DV]
        den = jnp.sum(p, axis=-1, keepdims=True) + inter * qc[:, A_DV:A_DV + 1]
        hh = num / jnp.maximum(jnp.abs(den), jnp.exp(-m_t))
        u = _head_norm_gate(hh, ng_ref[:, A_DV * h:A_DV * (h + 1)], o_ref[0, :, A_DV * h:A_DV * (h + 1)])
        u_ref[0, :, A_DV * h:A_DV * (h + 1)] = u.astype(bf16)

        b_end = br[:, L - 1:L]
        g_row = b_end - br + lir
        m_new = jnp.maximum(b_end + m_prev, jnp.max(g_row, axis=-1, keepdims=True))
        w_row = jnp.exp(g_row - m_new)
        decay = jnp.exp(b_end + m_prev - m_new)
        kw = kt_h.astype(f32) * w_row
        c_s[h] = decay * c_s[h] + _dot(kw.astype(bf16), v_h)
        n_s[h] = decay * n_s[h] + jnp.sum(kw, axis=-1, keepdims=True)
        m_s[h:h + 1, :] = jnp.broadcast_to(m_new, (1, LANES))

    @pl.when(c == nc - 1)
    def _():
        c_out[0] = c_s[...]
        n_out[0] = n_s[...]
        m_out[0] = m_s[...]


def _mlstm_prompt(qkv, kt, o, gcol, grow, ng):
    b, t, _ = qkv.shape
    L = MLSTM_CHUNK
    hv = HEADS * A_DV
    return pl.pallas_call(
        _mlstm_prompt_kernel,
        grid=(b, t // L),
        in_specs=[
            pl.BlockSpec((1, L, qkv.shape[-1]), lambda g, c: (g, c, 0)),
            pl.BlockSpec((1, HEADS * A_DQK, L), lambda g, c: (g, 0, c)),
            pl.BlockSpec((1, L, hv), lambda g, c: (g, c, 0)),
            pl.BlockSpec((1, L, LANES), lambda g, c: (g, c, 0)),
            pl.BlockSpec((1, 2 * HEADS, L), lambda g, c: (g, 0, c)),
            pl.BlockSpec((1, hv), lambda g, c: (0, 0)),
        ],
        out_specs=[
            pl.BlockSpec((1, L, hv), lambda g, c: (g, c, 0)),
            pl.BlockSpec((1, HEADS, A_DQK, A_DV), lambda g, c: (g, 0, 0, 0)),
            pl.BlockSpec((1, HEADS, A_DQK, LANES), lambda g, c: (g, 0, 0, 0)),
            pl.BlockSpec((1, HEADS, LANES), lambda g, c: (g, 0, 0)),
        ],
        out_shape=[
            jax.ShapeDtypeStruct((b, t, hv), bf16),
            jax.ShapeDtypeStruct((b, HEADS, A_DQK, A_DV), f32),
            jax.ShapeDtypeStruct((b, HEADS, A_DQK, LANES), f32),
            jax.ShapeDtypeStruct((b, HEADS, LANES), f32),
        ],
        scratch_shapes=[
            pltpu.VMEM((HEADS, A_DQK, A_DV), f32),
            pltpu.VMEM((HEADS, A_DQK, LANES), f32),
            pltpu.VMEM((HEADS, LANES), f32),
        ],
        compiler_params=_cparams(("parallel", "arbitrary")),
        name="mlstm_prompt",
    )(qkv, kt, o, gcol, grow, ng)


def _mlstm_sample_kernel(qkv_ref, kt_ref, o_ref, gcol_ref, grow_ref, ng_ref, c0_ref, n0_ref, nrows_ref,
                         m0_ref, mcol_ref, mrow_ref, u_ref, c_out, n_out, m_out, *, ts):
    R = qkv_ref.shape[1]
    G = R // ts
    shift = ts.bit_length() - 1
    hqk = HEADS * A_DQK

    r_i = lax.broadcasted_iota(jnp.int32, (R, R), 0)
    c_i = lax.broadcasted_iota(jnp.int32, (R, R), 1)
    same = (r_i >> shift) == (c_i >> shift)
    valid = same & (c_i <= r_i)
    tri = valid.astype(f32)
    tri_t = (same & (r_i <= c_i)).astype(f32)
    same_f = same.astype(f32)
    row_seq = lax.broadcasted_iota(jnp.int32, (R, 1), 0) >> shift
    g_i = lax.broadcasted_iota(jnp.int32, (G, R), 0)
    s_i = lax.broadcasted_iota(jnp.int32, (G, R), 1)
    bmask = (s_i >> shift) == g_i
    lastmask = s_i == g_i * ts + (ts - 1)
    g3 = lax.broadcasted_iota(jnp.int32, (G, A_DQK, R), 0)
    s3 = lax.broadcasted_iota(jnp.int32, (G, A_DQK, R), 2)
    bmask3 = (s3 >> shift) == g3
    lane_i = lax.broadcasted_iota(jnp.int32, (G, LANES), 1)

    gcol = gcol_ref[0]
    grow = grow_ref[0]
    lf_row = _log_sigmoid(grow[HEADS:2 * HEADS])
    b_col = _dot_exact(tri, _log_sigmoid(gcol))
    b_row = _dot_exact(lf_row, tri_t)
    bend_row = _dot_exact(lf_row, same_f)
    li_row = grow[0:HEADS]
    mcol = mcol_ref[0]
    mrow = mrow_ref[0]
    m0 = m0_ref[...]

    m_acc = jnp.zeros((G, LANES), f32)
    n_parts = []
    for h in range(HEADS):
        q_h = qkv_ref[0, :, A_DQK * h:A_DQK * (h + 1)]
        k_h = qkv_ref[0, :, hqk + A_DQK * h:hqk + A_DQK * (h + 1)]
        v_h = qkv_ref[0, :, 2 * hqk + A_DV * h:2 * hqk + A_DV * (h + 1)]
        kt_h = kt_ref[0, A_DQK * h:A_DQK * (h + 1), :]
        s_qk = _dot(q_h, kt_h)
        bc = b_col[:, HEADS + h:HEADS + h + 1]
        br = b_row[h:h + 1]
        lir = li_row[h:h + 1]
        d = jnp.where(valid, bc - br + lir, NEG_INF)
        a = bc + mcol[:, h:h + 1]
        m_t = jnp.maximum(a, jnp.max(d, axis=-1, keepdims=True))
        p = s_qk * jnp.exp(d - m_t)
        inter = jnp.exp(a - m_t)
        c_all = jnp.concatenate([c0_ref[g, h] for g in range(G)], axis=1).astype(bf16)
        qc = _dot(q_h, c_all)
        inter_c = jnp.zeros((R, A_DV), f32)
        for g in range(G):
            inter_c = jnp.where(row_seq == g, qc[:, A_DV * g:A_DV * (g + 1)], inter_c)
        qn = jnp.sum(q_h.astype(f32) * nrows_ref[0, :, A_DQK * h:A_DQK * (h + 1)], axis=-1, keepdims=True)
        num = _dot(p.astype(bf16), v_h) + inter * inter_c
        den = jnp.sum(p, axis=-1, keepdims=True) + inter * qn
        hh = num / jnp.maximum(jnp.abs(den), jnp.exp(-m_t))
        u = _head_norm_gate(hh, ng_ref[:, A_DV * h:A_DV * (h + 1)], o_ref[0, :, A_DV * h:A_DV * (h + 1)])
        u_ref[0, :, A_DV * h:A_DV * (h + 1)] = u.astype(bf16)

        g_row = bend_row[h:h + 1] - br + lir
        gmax = jnp.max(jnp.where(bmask, g_row, NEG_INF), axis=-1, keepdims=True)
        bend_b = jnp.sum(jnp.where(lastmask, br, 0.0), axis=-1, keepdims=True)
        m0_h = m0[:, h:h + 1]
        mnew_b = jnp.maximum(bend_b + m0_h, gmax)
        mnew_row = jnp.sum(jnp.where(bmask, mnew_b, 0.0), axis=0, keepdims=True)
        w_row = jnp.exp(g_row - mnew_row)
        decay_b = jnp.exp(bend_b + m0_h - mnew_b)
        kw = kt_h.astype(f32) * w_row
        kw3 = jnp.where(bmask3, jnp.broadcast_to(kw[None], (G, A_DQK, R)), 0.0)
        upd = _dot(kw3.reshape(G * A_DQK, R).astype(bf16), v_h)
        for g in range(G):
            c_out[g, h] = decay_b[g:g + 1, :] * c0_ref[g, h] + upd[A_DQK * g:A_DQK * (g + 1)]
        wm = jnp.where(bmask, w_row, 0.0)
        n_parts.append(decay_b * n0_ref[:, A_DQK * h:A_DQK * (h + 1)] + _dot(wm.astype(bf16), k_h))
        m_acc = jnp.where(lane_i == h, mnew_b, m_acc)

    n_out[...] = jnp.concatenate(n_parts, axis=1)
    m_out[...] = m_acc[:, :HEADS]


def _mlstm_sample(qkv, kt, o, gcol, grow, ng, c0, n0, nrows, m0, mcol, mrow, ts):
    _, t, _ = qkv.shape
    bs = c0.shape[0]
    G = SAMPLE_GROUP
    R = G * ts
    hv = HEADS * A_DV
    hqk = HEADS * A_DQK
    return pl.pallas_call(
        functools.partial(_mlstm_sample_kernel, ts=ts),
        grid=(t // R,),
        in_specs=[
            pl.BlockSpec((1, R, qkv.shape[-1]), lambda i: (0, i, 0)),
            pl.BlockSpec((1, hqk, R), lambda i: (0, 0, i)),
            pl.BlockSpec((1, R, hv), lambda i: (0, i, 0)),
            pl.BlockSpec((1, R, LANES), lambda i: (0, i, 0)),
            pl.BlockSpec((1, 2 * HEADS, R), lambda i: (0, 0, i)),
            pl.BlockSpec((1, hv), lambda i: (0, 0)),
            pl.BlockSpec((G, HEADS, A_DQK, A_DV), lambda i: (i, 0, 0, 0)),
            pl.BlockSpec((G, hqk), lambda i: (i, 0)),
            pl.BlockSpec((1, R, hqk), lambda i: (0, i, 0)),
            pl.BlockSpec((G, HEADS), lambda i: (i, 0)),
            pl.BlockSpec((1, R, LANES), lambda i: (0, i, 0)),
            pl.BlockSpec((1, HEADS, R), lambda i: (0, 0, i)),
        ],
        out_specs=[
            pl.BlockSpec((1, R, hv), lambda i: (0, i, 0)),
            pl.BlockSpec((G, HEADS, A_DQK, A_DV), lambda i: (i, 0, 0, 0)),
            pl.BlockSpec((G, hqk), lambda i: (i, 0)),
            pl.BlockSpec((G, HEADS), lambda i: (i, 0)),
        ],
        out_shape=[
            jax.ShapeDtypeStruct((1, t, hv), bf16),
            jax.ShapeDtypeStruct((bs, HEADS, A_DQK, A_DV), f32),
            jax.ShapeDtypeStruct((bs, hqk), f32),
            jax.ShapeDtypeStruct((bs, HEADS), f32),
        ],
        compiler_params=_cparams(("parallel",)),
        name="mlstm_sample",
    )(qkv, kt, o, gcol, grow, ng, c0, n0, nrows, m0, mcol, mrow)


def _expert_kernel(be_ref, nu_ref, x_ref, wg_ref, wu_ref, wd_ref, y_ref):
    @pl.when(pl.program_id(0) < nu_ref[0])
    def _():
        x = x_ref[...]
        g = _dot(x, wg_ref[0])
        u = _dot(x, wu_ref[0])
        hid = (g * _sigmoid(g)) * u
        y_ref[...] = _dot(hid.astype(bf16), wd_ref[0])

    @pl.when(pl.program_id(0) >= nu_ref[0])
    def _():
        y_ref[...] = jnp.zeros_like(y_ref)


def _experts(x_pad, block_e, n_used, wg, wu, wd):
    p, d = x_pad.shape
    de = wg.shape[-1]
    bm = MOE_BLOCK
    return pl.pallas_call(
        _expert_kernel,
        grid_spec=pltpu.PrefetchScalarGridSpec(
            num_scalar_prefetch=2,
            grid=(p // bm,),
            in_specs=[
                pl.BlockSpec((bm, d), lambda i, be, nu: (i, 0)),
                pl.BlockSpec((1, d, de), lambda i, be, nu: (be[i], 0, 0)),
                pl.BlockSpec((1, d, de), lambda i, be, nu: (be[i], 0, 0)),
                pl.BlockSpec((1, de, d), lambda i, be, nu: (be[i], 0, 0)),
            ],
            out_specs=pl.BlockSpec((bm, d), lambda i, be, nu: (i, 0)),
        ),
        out_shape=jax.ShapeDtypeStruct((p, d), f32),
        compiler_params=_cparams(("arbitrary",)),
        name="moe_experts",
    )(block_e, n_used, x_pad, wg, wu, wd)


def _combine_kernel(x_ref, ya_ref, yb_ref, w_ref, gate_ref, lng_ref, lnb_ref, o_ref):
    w = w_ref[0]
    y = w[:, 0:1] * ya_ref[0] + w[:, 1:2] * yb_ref[0]
    r = ALPHA * x_ref[0] + gate_ref[0] * y
    mu = jnp.mean(r, axis=-1, keepdims=True)
    cen = r - mu
    var = jnp.mean(cen * cen, axis=-1, keepdims=True)
    o_ref[0] = cen * lax.rsqrt(var + EPS) * lng_ref[...] + lnb_ref[...]


def _combine(x1, ya, yb, wexp, gate, ln_g, ln_b, tm):
    g, t, d = x1.shape
    return pl.pallas_call(
        _combine_kernel,
        grid=(g, t // tm),
        in_specs=[_tok_spec(tm, d), _tok_spec(tm, d), _tok_spec(tm, d), _tok_spec(tm, LANES),
                  _mod_spec(gate, tm), _full_spec((1, d)), _full_spec((1, d))],
        out_specs=_tok_spec(tm, d),
        out_shape=jax.ShapeDtypeStruct((g, t, d), f32),
        compiler_params=_cparams(("parallel", "parallel")),
        name="moe_combine",
    )(x1, ya, yb, wexp, gate, ln_g, ln_b)


def _moe(x1, h2, route, gate2, ln_g, ln_b, wg, wu, wd, tm):
    g, t, d = x1.shape
    n = g * t
    nk = 2 * n
    bm = MOE_BLOCK
    expert = jnp.transpose(route[:, 0:2, :], (0, 2, 1)).reshape(n, 2).astype(jnp.int32)
    wts = jnp.transpose(route[:, 2:4, :], (0, 2, 1))
    e_flat = expert.reshape(nk)
    onehot = (e_flat[:, None] == jnp.arange(N_EXPERTS, dtype=jnp.int32)[None, :]).astype(jnp.int32)
    csum = jnp.cumsum(onehot, axis=0)
    counts = csum[-1]
    rank = jnp.sum(csum * onehot, axis=1) - 1
    padded = (counts + bm - 1) // bm * bm
    pends = jnp.cumsum(padded)
    pstarts = pends - padded
    dest = pstarts[e_flat] + rank
    n_blocks = nk // bm + N_EXPERTS
    n_used = (pends[-1] // bm).astype(jnp.int32)
    blk = jnp.minimum(jnp.arange(n_blocks, dtype=jnp.int32), n_used - 1)
    block_e = jnp.minimum(jnp.searchsorted(pends, blk * bm, side="right"), N_EXPERTS - 1).astype(jnp.int32)
    tok = jnp.arange(nk, dtype=jnp.int32) // 2
    tok_pad = jnp.zeros((n_blocks * bm,), jnp.int32).at[dest].set(tok)
    x_pad = jnp.take(h2.reshape(n, d), tok_pad, axis=0)
    y = _experts(x_pad, block_e, n_used.reshape(1), wg, wu, wd)
    dest2 = dest.reshape(n, 2)
    ya = jnp.take(y, dest2[:, 0], axis=0).reshape(g, t, d)
    yb = jnp.take(y, dest2[:, 1], axis=0).reshape(g, t, d)
    wexp = jnp.pad(wts, ((0, 0), (0, 0), (0, LANES - 2)))
    return _combine(x1, ya, yb, wexp, gate2, ln_g, ln_b, tm)


def _kv_kernel(x_ref, w_ref, g_ref, cos_ref, sin_ref, lat_ref, kr_ref, latb_ref, krb_ref):
    kva = _dot(x_ref[0].astype(bf16), w_ref[...])
    latp = kva[:, :B_KV_RANK]
    lat = latp * lax.rsqrt(jnp.mean(latp * latp, axis=-1, keepdims=True) + EPS) * g_ref[...]
    kr = kva[:, B_KV_RANK:B_KV_RANK + B_D_ROPE] * cos_ref[0] \
        + kva[:, B_KV_RANK + LANES:B_KV_RANK + LANES + B_D_ROPE] * sin_ref[0]
    lat_ref[0] = lat
    kr_ref[0] = kr
    latb_ref[0] = lat.astype(bf16)
    krb_ref[0] = kr.astype(bf16)


def _shared_kv(x, w_pad, g, cos, sin, tm):
    gg, t, d = x.shape
    rope_spec = pl.BlockSpec((1, tm, B_D_ROPE), lambda gi, i: (0, i, 0)) if cos.shape[0] == 1 else \
        _tok_spec(tm, B_D_ROPE)
    return pl.pallas_call(
        _kv_kernel,
        grid=(gg, t // tm),
        in_specs=[_tok_spec(tm, d), _full_spec(w_pad.shape), _full_spec((1, B_KV_RANK)), rope_spec, rope_spec],
        out_specs=[_tok_spec(tm, B_KV_RANK), _tok_spec(tm, B_D_ROPE),
                   _tok_spec(tm, B_KV_RANK), _tok_spec(tm, B_D_ROPE)],
        out_shape=[jax.ShapeDtypeStruct((gg, t, B_KV_RANK), f32), jax.ShapeDtypeStruct((gg, t, B_D_ROPE), f32),
                   jax.ShapeDtypeStruct((gg, t, B_KV_RANK), bf16), jax.ShapeDtypeStruct((gg, t, B_D_ROPE), bf16)],
        compiler_params=_cparams(("parallel", "parallel")),
        name="shared_kv",
    )(x, w_pad, g, cos, sin)


def _mla_q_kernel(x_ref, sc_ref, sh_ref, wdq_ref, qg_ref, wuq_ref, wuk_ref, cos_ref, sin_ref, ql_ref, qr_ref):
    h = x_ref[0] * (1.0 + sc_ref[0]) + sh_ref[0]
    cq = _dot(h.astype(bf16), wdq_ref[...])
    cq = cq * lax.rsqrt(jnp.mean(cq * cq, axis=-1, keepdims=True) + EPS) * qg_ref[...]
    q = _dot(cq.astype(bf16), wuq_ref[...])
    n_nope = HEADS * B_D_NOPE
    n_rope = HEADS * B_D_ROPE
    rot = q[:, n_nope:n_nope + n_rope] * cos_ref[0] + q[:, n_nope + n_rope:] * sin_ref[0]
    for hd in range(HEADS):
        qn = q[:, B_D_NOPE * hd:B_D_NOPE * (hd + 1)].astype(bf16)
        ql_ref[0, hd] = _dot(qn, wuk_ref[hd]).astype(bf16)
        qr_ref[0, hd] = rot[:, B_D_ROPE * hd:B_D_ROPE * (hd + 1)].astype(bf16)


def _mla_q(x, scale, shift, wdq, qg, wuq, wuk, cos8, sin8, tm):
    g, t, d = x.shape
    n_rope = HEADS * B_D_ROPE
    rope_spec = pl.BlockSpec((1, tm, n_rope), lambda gi, i: (0, i, 0)) if cos8.shape[0] == 1 else \
        _tok_spec(tm, n_rope)
    return pl.pallas_call(
        _mla_q_kernel,
        grid=(g, t // tm),
        in_specs=[_tok_spec(tm, d), _mod_spec(scale, tm), _mod_spec(shift, tm),
                  _full_spec(wdq.shape), _full_spec(qg.shape), _full_spec(wuq.shape), _full_spec(wuk.shape),
                  rope_spec, rope_spec],
        out_specs=[pl.BlockSpec((1, HEADS, tm, B_KV_RANK), lambda gi, i: (gi, 0, i, 0)),
                   pl.BlockSpec((1, HEADS, tm, B_D_ROPE), lambda gi, i: (gi, 0, i, 0))],
        out_shape=[jax.ShapeDtypeStruct((g, HEADS, t, B_KV_RANK), bf16),
                   jax.ShapeDtypeStruct((g, HEADS, t, B_D_ROPE), bf16)],
        compiler_params=_cparams(("parallel", "parallel")),
        name="mla_q",
    )(x, scale, shift, wdq, qg, wuq, wuk, cos8, sin8)


def _softmax_step(s, v, m_s, l_s, acc_s):
    m_old = m_s[...]
    m_new = jnp.maximum(m_old, jnp.max(s, axis=-1, keepdims=True))
    p = jnp.exp(s - m_new[:, 0:1])
    alpha = jnp.exp(m_old - m_new)
    l_s[...] = alpha * l_s[...] + jnp.sum(p, axis=-1, keepdims=True)
    acc_s[...] = alpha[:, 0:1] * acc_s[...] + _dot(p.astype(bf16), v)
    m_s[...] = m_new


def _flash_kernel(ql_ref, qr_ref, lat_ref, kr_ref, o_ref, m_s, l_s, acc_s, *, tq):
    i = pl.program_id(1)
    r_i = lax.broadcasted_iota(jnp.int32, (tq, tq), 0)
    c_i = lax.broadcasted_iota(jnp.int32, (tq, tq), 1)
    causal = c_i <= r_i

    def head_body(h, carry):
        ql = ql_ref[0, h]
        qr = qr_ref[0, h]
        m_s[...] = jnp.full_like(m_s, NEG_INF)
        l_s[...] = jnp.zeros_like(l_s)
        acc_s[...] = jnp.zeros_like(acc_s)

        def scores(j):
            start = pl.multiple_of(j * tq, tq)
            k_lat = lat_ref[0, pl.ds(start, tq), :]
            k_r = kr_ref[0, pl.ds(start, tq), :]
            return (_dot_nt(ql, k_lat) + _dot_nt(qr, k_r)) * ATTN_SCALE, k_lat

        def kv_body(j, c2):
            s, k_lat = scores(j)
            _softmax_step(s, k_lat, m_s, l_s, acc_s)
            return c2

        lax.fori_loop(0, i, kv_body, 0)
        s, k_lat = scores(i)
        _softmax_step(jnp.where(causal, s, NEG_INF), k_lat, m_s, l_s, acc_s)
        o_ref[0, h] = (acc_s[...] / l_s[:, 0:1]).astype(bf16)
        return carry

    lax.fori_loop(0, HEADS, head_body, 0)


def _flash(ql, qr, latb, krb, tq):
    b, _, t, _ = ql.shape
    return pl.pallas_call(
        functools.partial(_flash_kernel, tq=tq),
        grid=(b, t // tq),
        in_specs=[
            pl.BlockSpec((1, HEADS, tq, B_KV_RANK), lambda g, i: (g, 0, i, 0)),
            pl.BlockSpec((1, HEADS, tq, B_D_ROPE), lambda g, i: (g, 0, i, 0)),
            pl.BlockSpec((1, t, B_KV_RANK), lambda g, i: (g, 0, 0)),
            pl.BlockSpec((1, t, B_D_ROPE), lambda g, i: (g, 0, 0)),
        ],
        out_specs=pl.BlockSpec((1, HEADS, tq, B_KV_RANK), lambda g, i: (g, 0, i, 0)),
        out_shape=jax.ShapeDtypeStruct((b, HEADS, t, B_KV_RANK), bf16),
        scratch_shapes=[pltpu.VMEM((tq, LANES), f32), pltpu.VMEM((tq, LANES), f32),
                        pltpu.VMEM((tq, B_KV_RANK), f32)],
        compiler_params=_cparams(("parallel", "arbitrary")),
        name="mla_flash",
    )(ql, qr, latb, krb)


def _paged_kernel(pt_ref, ql_ref, qr_ref, *rest, ts, new_pad):
    np_ = PAGES_PER_STEP
    lat_pages = rest[:np_]
    kr_pages = rest[np_:2 * np_]
    nl_ref, nk_ref, o_ref, m_s, l_s, acc_s = rest[2 * np_:]
    s_id = pl.program_id(1)
    ql = ql_ref[0]
    qr = qr_ref[0]

    @pl.when(s_id == 0)
    def _():
        m_s[...] = jnp.full_like(m_s, NEG_INF)
        l_s[...] = jnp.zeros_like(l_s)
        acc_s[...] = jnp.zeros_like(acc_s)

    k_lat = jnp.concatenate([p[0] for p in lat_pages], axis=0).astype(bf16)
    k_r = jnp.concatenate([p[0] for p in kr_pages], axis=0).astype(bf16)
    s = (_dot_nt(ql, k_lat) + _dot_nt(qr, k_r)) * ATTN_SCALE
    _softmax_step(s, k_lat, m_s, l_s, acc_s)

    @pl.when(s_id == pl.num_programs(1) - 1)
    def _():
        rows = ql.shape[0]
        n_lat = nl_ref[0]
        n_kr = nk_ref[0]
        s_new = (_dot_nt(ql, n_lat) + _dot_nt(qr, n_kr)) * ATTN_SCALE
        t_row = lax.broadcasted_iota(jnp.int32, (rows, new_pad), 0) & (ts - 1)
        c_new = lax.broadcasted_iota(jnp.int32, (rows, new_pad), 1)
        _softmax_step(jnp.where(c_new <= t_row, s_new, NEG_INF), n_lat, m_s, l_s, acc_s)
        o_ref[0] = (acc_s[...] / l_s[:, 0:1]).astype(bf16)


def _paged_attention(page_table, ql, qr, cache_lat, cache_kr, new_lat, new_kr, ts):
    bs, rows, _ = ql.shape
    n_pages = page_table.shape[1]
    np_ = PAGES_PER_STEP
    new_pad = new_lat.shape[1]
    lat_specs = [pl.BlockSpec((1, PAGE, B_KV_RANK), functools.partial(
        lambda b, s, pt, r: (pt[b, s * np_ + r], 0, 0), r=r)) for r in range(np_)]
    kr_specs = [pl.BlockSpec((1, PAGE, B_D_ROPE), functools.partial(
        lambda b, s, pt, r: (pt[b, s * np_ + r], 0, 0), r=r)) for r in range(np_)]
    return pl.pallas_call(
        functools.partial(_paged_kernel, ts=ts, new_pad=new_pad),
        grid_spec=pltpu.PrefetchScalarGridSpec(
            num_scalar_prefetch=1,
            grid=(bs, n_pages // np_),
            in_specs=[pl.BlockSpec((1, rows, B_KV_RANK), lambda b, s, pt: (b, 0, 0)),
                      pl.BlockSpec((1, rows, B_D_ROPE), lambda b, s, pt: (b, 0, 0))]
            + lat_specs + kr_specs
            + [pl.BlockSpec((1, new_pad, B_KV_RANK), lambda b, s, pt: (b, 0, 0)),
               pl.BlockSpec((1, new_pad, B_D_ROPE), lambda b, s, pt: (b, 0, 0))],
            out_specs=pl.BlockSpec((1, rows, B_KV_RANK), lambda b, s, pt: (b, 0, 0)),
            scratch_shapes=[pltpu.VMEM((rows, LANES), f32), pltpu.VMEM((rows, LANES), f32),
                            pltpu.VMEM((rows, B_KV_RANK), f32)],
        ),
        out_shape=jax.ShapeDtypeStruct((bs, rows, B_KV_RANK), bf16),
        compiler_params=_cparams(("parallel", "arbitrary")),
        name="mla_paged",
    )(page_table, ql, qr, *([cache_lat] * np_), *([cache_kr] * np_), new_lat, new_kr)


def _mla_out_kernel(ol_ref, wuv_ref, wo_ref, x_ref, gate_ref, lng_ref, lnb_ref, sc_ref, sh_ref, rw_ref,
                    rwh_ref, rb_ref, x1_ref, h2_ref, route_ref):
    o = jnp.concatenate([_dot(ol_ref[0, hd], wuv_ref[hd]) for hd in range(HEADS)], axis=1)
    y = _dot(o.astype(bf16), wo_ref[...])
    _residual_ln_route(y, x_ref, gate_ref, lng_ref, lnb_ref, sc_ref, sh_ref, rw_ref, rwh_ref, rb_ref,
                       x1_ref, h2_ref, route_ref)


def _mla_out(ol, wuv, wo, x, gate, ln_g, ln_b, scale2, shift2, rw, rwh, rb, tm):
    g, t, d = x.shape
    post_in, out_specs, out_shape = _post_specs(x, gate, scale2, shift2, tm, d)
    return pl.pallas_call(
        _mla_out_kernel,
        grid=(g, t // tm),
        in_specs=[pl.BlockSpec((1, HEADS, tm, B_KV_RANK), lambda gi, i: (gi, 0, i, 0)),
                  _full_spec(wuv.shape), _full_spec(wo.shape)] + post_in,
        out_specs=out_specs,
        out_shape=out_shape,
        compiler_params=_cparams(("parallel", "parallel")),
        name="mla_out",
    )(ol, wuv, wo, x, gate, ln_g, ln_b, scale2, shift2, rw, rwh, rb)


def _rope_tables(pos):
    half = B_D_ROPE // 2
    inv = jnp.power(ROPE_THETA, -jnp.arange(half, dtype=f32) / half)
    ang = pos.astype(f32)[:, None] * inv[None, :]
    cos, sin = jnp.cos(ang), jnp.sin(ang)
    return jnp.concatenate([cos, cos], axis=-1), jnp.concatenate([-sin, sin], axis=-1)


def _swap_halves(w, width):
    lead = w.shape[:-1]
    w2 = w.reshape(lead + (-1, 2, width // 2))
    return w2[..., ::-1, :].reshape(w.shape)


def kernel(x_prompt, x_sample, cache_latent, cache_krope, page_table, state_C, state_n, state_m, c_prompt, c_sample, ada_w, ada_b, ln_g, ln_b, a_w_in, a_b_gates, a_norm_g, a_w_out, b_w_kv_a, b_kv_norm_g, b_w_uk, b_w_uv, b_w_dq, b_q_norm_g, b_w_uq, b_w_o, router_w, router_b, e_w_gate, e_w_up, e_w_down):
    bp, tp, d = x_prompt.shape
    bs, ts, _ = x_sample.shape
    hqk = HEADS * A_DQK
    hv = HEADS * A_DV

    w_in = a_w_in.at[:, :, hqk:2 * hqk].multiply(A_DQK ** -0.5)
    w_in = jnp.pad(w_in, ((0, 0), (0, 0), (0, LANES - 2 * HEADS))).astype(bf16)
    bg = jnp.pad(a_b_gates, ((0, 0), (0, LANES - 2 * HEADS))).reshape(N_A_LAYERS, 1, LANES)
    w_out = a_w_out.astype(bf16)
    ng = a_norm_g.reshape(N_A_LAYERS, 1, hv)
    perm = jnp.array([(r % N_GROUPS) * PER_GROUP + r // N_GROUPS for r in range(N_EXPERTS)], jnp.int32)
    rw_perm = router_w[:, perm]
    rw_hi = rw_perm.astype(bf16)
    rw_lo = (rw_perm - rw_hi.astype(f32)).astype(bf16)
    rw = jnp.pad(jnp.concatenate([rw_hi, rw_lo], axis=1), ((0, 0), (0, LANES - 2 * N_EXPERTS)))
    rwh = jnp.pad(rw_hi, ((0, 0), (0, LANES - N_EXPERTS)))
    rb = router_b[perm].reshape(N_EXPERTS, 1)
    wg, wu, wd = e_w_gate.astype(bf16), e_w_up.astype(bf16), e_w_down.astype(bf16)
    w_lat = b_w_kv_a[:, :B_KV_RANK]
    w_kr = b_w_kv_a[:, B_KV_RANK:]
    zpad = jnp.zeros((d, LANES - B_D_ROPE), f32)
    w_kv = jnp.concatenate([w_lat, w_kr, zpad, _swap_halves(w_kr, B_D_ROPE), zpad], axis=1).astype(bf16)
    kvg = b_kv_norm_g.reshape(1, B_KV_RANK)
    wdq = b_w_dq.astype(bf16)
    qg = b_q_norm_g.reshape(-1, 1, b_q_norm_g.shape[-1])
    uq = b_w_uq.reshape(b_w_uq.shape[0], b_w_uq.shape[1], HEADS, B_D_NOPE + B_D_ROPE)
    uq_nope = uq[..., :B_D_NOPE].reshape(uq.shape[0], uq.shape[1], HEADS * B_D_NOPE)
    uq_rope = uq[..., B_D_NOPE:].reshape(uq.shape[0], uq.shape[1], HEADS * B_D_ROPE)
    wuq = jnp.concatenate([uq_nope, uq_rope, _swap_halves(uq_rope, B_D_ROPE)], axis=-1).astype(bf16)
    wuk = jnp.transpose(b_w_uk, (1, 2, 0)).astype(bf16)
    wuv = jnp.transpose(b_w_uv, (1, 0, 2)).astype(bf16)
    wo = b_w_o.astype(bf16)

    mods = _ada_mods(jnp.concatenate([c_prompt, c_sample], axis=0), ada_w, ada_b)

    cos_p, sin_p = _rope_tables(jnp.arange(tp, dtype=jnp.int32))
    past_len = page_table.shape[1] * PAGE
    cos_s, sin_s = _rope_tables(jnp.tile(past_len + jnp.arange(ts, dtype=jnp.int32), bs))

    def trunk(x, mods_g, per_token, cos, sin, sample):
        g, t, _ = x.shape
        tm = 512
        cos1, sin1 = cos[None], sin[None]
        cos8, sin8 = jnp.tile(cos, (1, HEADS))[None], jnp.tile(sin, (1, HEADS))[None]
        new_c, new_n, new_m = [], [], []
        lat = kr = latb = krb = None
        for l in range(DEPTH):
            m6 = mods_g[l].reshape(mods_g.shape[1], 6, d)
            if per_token:
                parts = [jnp.repeat(m6[:, i], ts, axis=0)[None] for i in range(6)]
            else:
                parts = [m6[:, i][:, None, :] for i in range(6)]
            shift1, scale1, gate1, shift2, scale2, gate2 = parts
            lg = ln_g[l].reshape(2, 1, d)
            lb = ln_b[l].reshape(2, 1, d)
            if l < N_A_LAYERS:
                qkv, o, gates = _mlstm_in(x, scale1, shift1, w_in[l], bg[l], tm)
                kt = jnp.swapaxes(qkv[:, :, hqk:2 * hqk], 1, 2)
                grow = jnp.swapaxes(gates[:, :, :2 * HEADS], 1, 2)
                if sample:
                    c0 = state_C[l]
                    n0 = state_n[l].reshape(bs, hqk)
                    m0 = state_m[l]
                    nrows = jnp.repeat(n0, ts, axis=0)[None]
                    m_tok = jnp.repeat(m0, ts, axis=0)
                    mcol = jnp.pad(m_tok, ((0, 0), (0, LANES - HEADS)))[None]
                    mrow = m_tok.T[None]
                    u, c_new, n_new, m_new = _mlstm_sample(qkv, kt, o, gates, grow, ng[l], c0, n0, nrows, m0,
                                                           mcol, mrow, ts)
                    n_new = n_new.reshape(bs, HEADS, A_DQK)
                else:
                    u, c_new, n_rep, m_rep = _mlstm_prompt(qkv, kt, o, gates, grow, ng[l])
                    n_new = n_rep[..., 0]
                    m_new = m_rep[..., 0]
                new_c.append(c_new)
                new_n.append(n_new)
                new_m.append(m_new)
                x1, h2, route = _mlstm_out(u, w_out[l], x, gate1, lg[0], lb[0], scale2, shift2, rw, rwh, rb, tm)
            else:
                j = l - N_A_LAYERS
                ql, qr = _mla_q(x, scale1, shift1, wdq[j], qg[j], wuq[j], wuk, cos8, sin8, tm)
                if sample:
                    ql_s = jnp.transpose(ql.reshape(HEADS, bs, ts, B_KV_RANK), (1, 0, 2, 3)).reshape(
                        bs, HEADS * ts, B_KV_RANK)
                    qr_s = jnp.transpose(qr.reshape(HEADS, bs, ts, B_D_ROPE), (1, 0, 2, 3)).reshape(
                        bs, HEADS * ts, B_D_ROPE)
                    new_pad = 16
                    nl = jnp.pad(latb.reshape(bs, ts, B_KV_RANK), ((0, 0), (0, new_pad - ts), (0, 0)))
                    nk = jnp.pad(krb.reshape(bs, ts, B_D_ROPE), ((0, 0), (0, new_pad - ts), (0, 0)))
                    ol = _paged_attention(page_table, ql_s, qr_s, cache_latent, cache_krope, nl, nk, ts)
                    ol = jnp.transpose(ol.reshape(bs, HEADS, ts, B_KV_RANK), (1, 0, 2, 3)).reshape(
                        1, HEADS, bs * ts, B_KV_RANK)
                else:
                    ol = _flash(ql, qr, latb, krb, 512)
                x1, h2, route = _mla_out(ol, wuv, wo[j], x, gate1, lg[0], lb[0], scale2, shift2, rw, rwh, rb, tm)
            x = _moe(x1, h2, route, gate2, lg[1], lb[1], wg[l], wu[l], wd[l], tm)
            if l == N_A_LAYERS - 1:
                lat, kr, latb, krb = _shared_kv(x, w_kv, kvg, cos1, sin1, tm)
        return x, jnp.stack(new_c), jnp.stack(new_n), jnp.stack(new_m), lat, kr

    y_p, c_p, n_p, m_p, lat_p, kr_p = trunk(x_prompt, mods[:, :bp], False, cos_p, sin_p, False)
    y_s, c_s, n_s, m_s, lat_s, kr_s = trunk(x_sample.reshape(1, bs * ts, d), mods[:, bp:], True, cos_s, sin_s, True)
    return (y_p, y_s.reshape(bs, ts, d), c_p, n_p, m_p, lat_p, kr_p,
            c_s, n_s, m_s, lat_s.reshape(bs, ts, B_KV_RANK), kr_s.reshape(bs, ts, B_D_ROPE))
```

```python
import functools

import jax
import jax.numpy as jnp
from jax import lax
from jax.experimental import pallas as pl
from jax.experimental.pallas import tpu as pltpu

f32 = jnp.float32
bf16 = jnp.bfloat16

DEPTH = 4
N_A_LAYERS = 2
HEADS = 8
A_DQK = 64
A_DV = 128
B_D_NOPE = 128
B_D_ROPE = 64
B_KV_RANK = 256
ROPE_THETA = 10000.0
ATTN_SCALE = (B_D_NOPE + B_D_ROPE) ** -0.5
N_EXPERTS = 16
N_GROUPS = 4
PER_GROUP = 4
PAGE = 128
ALPHA = (2 * DEPTH) ** 0.25
EPS = 1e-6
NEG_INF = float("-inf")
LOG2E = 1.4426950408889634
SOFTMAX_EXP2_SCALE = ATTN_SCALE * LOG2E

VMEM_LIMIT_BYTES = 56 * 1024 * 1024
LANES = 128
MLSTM_CHUNK = 128
SAMPLE_GROUP = 16
MOE_BLOCK = 256
PAGES_PER_STEP = 16
FLASH_HEADS = 4


def _cparams(sem):
    return pltpu.CompilerParams(dimension_semantics=sem, vmem_limit_bytes=VMEM_LIMIT_BYTES)


def _dot(a, b):
    return jnp.dot(a, b, preferred_element_type=f32)


def _dot_nt(a, b):
    return lax.dot_general(a, b, (((1,), (1,)), ((), ())), preferred_element_type=f32)


def _dot_exact(a, b):
    return jnp.dot(a, b, preferred_element_type=f32, precision=lax.Precision.HIGHEST)


def _sigmoid(x):
    return 1.0 / (1.0 + jnp.exp(-x))


def _log_sigmoid(x):
    return jnp.minimum(x, 0.0) - jnp.log(1.0 + jnp.exp(-jnp.abs(x)))


def _ada_kernel(c_ref, w_ref, b_ref, o_ref):
    c = c_ref[...]
    sc = (c * _sigmoid(c)).astype(bf16)
    o_ref[0] = _dot(sc, w_ref[0].astype(bf16)) + b_ref[0]


def _ada_mods(c_all, ada_w, ada_b):
    depth, d, e = ada_w.shape
    bc = c_all.shape[0]
    tn = 1536
    return pl.pallas_call(
        _ada_kernel,
        grid=(depth, e // tn),
        in_specs=[
            pl.BlockSpec((bc, d), lambda l, j: (0, 0)),
            pl.BlockSpec((1, d, tn), lambda l, j: (l, 0, j)),
            pl.BlockSpec((1, 1, tn), lambda l, j: (l, 0, j)),
        ],
        out_specs=pl.BlockSpec((1, bc, tn), lambda l, j: (l, 0, j)),
        out_shape=jax.ShapeDtypeStruct((depth, bc, e), f32),
        compiler_params=_cparams(("parallel", "parallel")),
        name="ada_mods",
    )(c_all, ada_w, ada_b.reshape(depth, 1, e))


def _mod_spec(mod, tm):
    d = mod.shape[-1]
    if mod.shape[1] == 1:
        return pl.BlockSpec((1, 1, d), lambda g, i: (g, 0, 0))
    return pl.BlockSpec((1, tm, d), lambda g, i: (g, i, 0))


def _tok_spec(tm, d):
    return pl.BlockSpec((1, tm, d), lambda g, i: (g, i, 0))


def _full_spec(shape):
    nd = len(shape)
    return pl.BlockSpec(shape, lambda g, i: (0,) * nd)


def _route(h2, rw_ref, rwh_ref, rb_ref):
    hi = h2.astype(bf16)
    lo = (h2 - hi.astype(f32)).astype(bf16)
    p = _dot(hi, rw_ref[...]) + _dot(lo, rwh_ref[...])
    pt = p.T
    logits = pt[0:N_EXPERTS] + pt[N_EXPERTS:2 * N_EXPERTS]
    sc = _sigmoid(logits)
    sel = sc + rb_ref[...]
    a = [sel[PER_GROUP * j:PER_GROUP * (j + 1)] for j in range(PER_GROUP)]
    s = [sc[PER_GROUP * j:PER_GROUP * (j + 1)] for j in range(PER_GROUP)]
    hi01, lo01 = jnp.maximum(a[0], a[1]), jnp.minimum(a[0], a[1])
    hi23, lo23 = jnp.maximum(a[2], a[3]), jnp.minimum(a[2], a[3])
    gs = jnp.maximum(hi01, hi23) + jnp.maximum(jnp.minimum(hi01, hi23), jnp.maximum(lo01, lo23))
    best = gs[0:1]
    grp = jnp.zeros_like(best)
    for g in range(1, N_GROUPS):
        better = gs[g:g + 1] > best
        grp = jnp.where(better, float(g), grp)
        best = jnp.where(better, gs[g:g + 1], best)
    mv, sv = [], []
    for j in range(PER_GROUP):
        m_j = a[j][0:1]
        s_j = s[j][0:1]
        for g in range(1, N_GROUPS):
            m_j = jnp.where(grp == float(g), a[j][g:g + 1], m_j)
            s_j = jnp.where(grp == float(g), s[j][g:g + 1], s_j)
        mv.append(m_j)
        sv.append(s_j)

    def first_argmax(vals):
        bv, bi = vals[0], jnp.zeros_like(vals[0])
        for j in range(1, PER_GROUP):
            better = vals[j] > bv
            bi = jnp.where(better, float(j), bi)
            bv = jnp.where(better, vals[j], bv)
        return bi

    i1 = first_argmax(mv)
    i2 = first_argmax([jnp.where(i1 == float(j), NEG_INF, mv[j]) for j in range(PER_GROUP)])
    w1, w2 = sv[0], sv[0]
    for j in range(1, PER_GROUP):
        w1 = jnp.where(i1 == float(j), sv[j], w1)
        w2 = jnp.where(i2 == float(j), sv[j], w2)
    tot = w1 + w2
    e1 = grp * float(PER_GROUP) + i1
    e2 = grp * float(PER_GROUP) + i2
    z = jnp.zeros_like(e1)
    return jnp.concatenate([e1, e2, w1 / tot, w2 / tot, z, z, z, z], axis=0)


N_POST_IN = 9


def _residual_ln_route(y, post_refs, x1_ref, h2_ref, route_ref):
    x_ref, gate_ref, lng_ref, lnb_ref, sc_ref, sh_ref, rw_ref, rwh_ref, rb_ref = post_refs
    r = ALPHA * x_ref[0] + gate_ref[0] * y
    mu = jnp.mean(r, axis=-1, keepdims=True)
    cen = r - mu
    var = jnp.mean(cen * cen, axis=-1, keepdims=True)
    x1 = cen * lax.rsqrt(var + EPS) * lng_ref[...] + lnb_ref[...]
    x1_ref[0] = x1
    h2 = x1 * (1.0 + sc_ref[0]) + sh_ref[0]
    h2_ref[...] = h2
    route_ref[0] = _route(h2, rw_ref, rwh_ref, rb_ref)


def _post_call(kernel_fn, name, mixer_args, mixer_specs, x, gate, ln_g, ln_b, scale2, shift2, rw, rwh, rb, tm,
               h2_rows, h2_row0, h2_prev):
    g, t, d = x.shape
    steps = t // tm
    blk0 = h2_row0 // tm
    in_specs = list(mixer_specs) + [
        _tok_spec(tm, d), _mod_spec(gate, tm), _full_spec((1, d)), _full_spec((1, d)),
        _mod_spec(scale2, tm), _mod_spec(shift2, tm),
        _full_spec((d, LANES)), _full_spec((d, LANES)), _full_spec((N_EXPERTS, 1)),
    ]
    args = list(mixer_args) + [x, gate, ln_g, ln_b, scale2, shift2, rw, rwh, rb]
    aliases = {}
    if h2_prev is not None:
        aliases = {len(args): 1}
        in_specs.append(pl.BlockSpec(memory_space=pl.ANY))
        args.append(h2_prev)
    return pl.pallas_call(
        kernel_fn,
        grid=(g, steps),
        in_specs=in_specs,
        out_specs=[_tok_spec(tm, d),
                   pl.BlockSpec((tm, d), lambda gi, i: (blk0 + gi * steps + i, 0)),
                   pl.BlockSpec((1, 8, tm), lambda gi, i: (gi, 0, i))],
        out_shape=[jax.ShapeDtypeStruct((g, t, d), f32),
                   jax.ShapeDtypeStruct((h2_rows, d), f32),
                   jax.ShapeDtypeStruct((g, 8, t), f32)],
        input_output_aliases=aliases,
        compiler_params=_cparams(("parallel", "parallel")),
        name=name,
    )(*args)


def _mlstm_in_kernel(x_ref, sc_ref, sh_ref, w_ref, bg_ref, qkv_ref, o_ref, g_ref):
    h = x_ref[0] * (1.0 + sc_ref[0]) + sh_ref[0]
    z = _dot(h.astype(bf16), w_ref[...])
    hqk2 = 2 * HEADS * A_DQK
    hv = HEADS * A_DV
    qkv_ref[0] = z[:, :hqk2 + hv].astype(bf16)
    o_ref[0] = z[:, hqk2 + hv:hqk2 + 2 * hv]
    g_ref[0] = z[:, hqk2 + 2 * hv:] + bg_ref[...]


def _mlstm_in(x, scale, shift, w_pad, bg_pad, tm):
    g, t, d = x.shape
    n_qkv = 2 * HEADS * A_DQK + HEADS * A_DV
    hv = HEADS * A_DV
    return pl.pallas_call(
        _mlstm_in_kernel,
        grid=(g, t // tm),
        in_specs=[_tok_spec(tm, d), _mod_spec(scale, tm), _mod_spec(shift, tm),
                  _full_spec(w_pad.shape), _full_spec((1, LANES))],
        out_specs=[_tok_spec(tm, n_qkv), _tok_spec(tm, hv), _tok_spec(tm, LANES)],
        out_shape=[jax.ShapeDtypeStruct((g, t, n_qkv), bf16),
                   jax.ShapeDtypeStruct((g, t, hv), f32),
                   jax.ShapeDtypeStruct((g, t, LANES), f32)],
        compiler_params=_cparams(("parallel", "parallel")),
        name="mlstm_in",
    )(x, scale, shift, w_pad, bg_pad)


def _mlstm_out_kernel(u_ref, w_ref, *refs):
    y = _dot(u_ref[0], w_ref[...])
    _residual_ln_route(y, refs[:N_POST_IN], *refs[-3:])


def _mlstm_out(u, w_out, tm, **post):
    return _post_call(_mlstm_out_kernel, "mlstm_out", [u, w_out],
                      [_tok_spec(tm, u.shape[-1]), _full_spec(w_out.shape)], tm=tm, **post)


def _head_norm_gate(hh, ng, o):
    mu = jnp.mean(hh, axis=-1, keepdims=True)
    cen = hh - mu
    var = jnp.mean(cen * cen, axis=-1, keepdims=True)
    return (cen * lax.rsqrt(var + EPS) * ng) * _sigmoid(o)


def _mlstm_prompt_kernel(qkv_ref, kt_ref, o_ref, gcol_ref, grow_ref, ng_ref,
                         u_ref, c_out, n_out, m_out, c_s, n_s, m_s):
    c = pl.program_id(1)
    nc = pl.num_programs(1)
    L = qkv_ref.shape[1]
    hqk = HEADS * A_DQK

    @pl.when(c == 0)
    def _():
        c_s[...] = jnp.zeros_like(c_s)
        n_s[...] = jnp.zeros_like(n_s)
        m_s[...] = jnp.zeros_like(m_s)

    r_i = lax.broadcasted_iota(jnp.int32, (L, L), 0)
    c_i = lax.broadcasted_iota(jnp.int32, (L, L), 1)
    causal = c_i <= r_i
    tri = causal.astype(f32)
    tri_t = (r_i <= c_i).astype(f32)

    gcol = gcol_ref[0]
    grow = grow_ref[0]
    b_col = _dot_exact(tri, _log_sigmoid(gcol))
    b_row = _dot_exact(_log_sigmoid(grow[HEADS:2 * HEADS]), tri_t)
    li_row = grow[0:HEADS]

    for h in range(HEADS):
        q_h = qkv_ref[0, :, A_DQK * h:A_DQK * (h + 1)]
        v_h = qkv_ref[0, :, 2 * hqk + A_DV * h:2 * hqk + A_DV * (h + 1)]
        kt_h = kt_ref[0, A_DQK * h:A_DQK * (h + 1), :]
        s_qk = _dot(q_h, kt_h)
        bc = b_col[:, HEADS + h:HEADS + h + 1]
        br = b_row[h:h + 1]
        lir = li_row[h:h + 1]
        m_prev = m_s[h:h + 1, 0:1]
        d = jnp.where(causal, bc - br + lir, NEG_INF)
        a = bc + m_prev
        m_t = jnp.maximum(a, jnp.max(d, axis=-1, keepdims=True))
        p = s_qk * jnp.exp(d - m_t)
        inter = jnp.exp(a - m_t)
        c_aug = jnp.concatenate([c_s[h], n_s[h]], axis=1).astype(bf16)
        qc = _dot(q_h, c_aug)
        num = _dot(p.astype(bf16), v_h) + inter * qc[:, :A_DV]
        den = jnp.sum(p, axis=-1, keepdims=True) + inter * qc[:, A_DV:A_DV + 1]
        hh = num / jnp.maximum(jnp.abs(den), jnp.exp(-m_t))
        u = _head_norm_gate(hh, ng_ref[:, A_DV * h:A_DV * (h + 1)], o_ref[0, :, A_DV * h:A_DV * (h + 1)])
        u_ref[0, :, A_DV * h:A_DV * (h + 1)] = u.astype(bf16)

        b_end = br[:, L - 1:L]
        g_row = b_end - br + lir
        m_new = jnp.maximum(b_end + m_prev, jnp.max(g_row, axis=-1, keepdims=True))
        w_row = jnp.exp(g_row - m_new)
        decay = jnp.exp(b_end + m_prev - m_new)
        kw = kt_h.astype(f32) * w_row
        c_s[h] = decay * c_s[h] + _dot(kw.astype(bf16), v_h)
        n_s[h] = decay * n_s[h] + jnp.sum(kw, axis=-1, keepdims=True)
        m_s[h:h + 1, :] = jnp.broadcast_to(m_new, (1, LANES))

    @pl.when(c == nc - 1)
    def _():
        c_out[0] = c_s[...]
        n_out[0] = n_s[...]
        m_out[0] = m_s[...]


def _mlstm_prompt(qkv, kt, o, gcol, grow, ng):
    b, t, _ = qkv.shape
    L = MLSTM_CHUNK
    hv = HEADS * A_DV
    return pl.pallas_call(
        _mlstm_prompt_kernel,
        grid=(b, t // L),
        in_specs=[
            pl.BlockSpec((1, L, qkv.shape[-1]), lambda g, c: (g, c, 0)),
            pl.BlockSpec((1, HEADS * A_DQK, L), lambda g, c: (g, 0, c)),
            pl.BlockSpec((1, L, hv), lambda g, c: (g, c, 0)),
            pl.BlockSpec((1, L, LANES), lambda g, c: (g, c, 0)),
            pl.BlockSpec((1, 2 * HEADS, L), lambda g, c: (g, 0, c)),
            pl.BlockSpec((1, hv), lambda g, c: (0, 0)),
        ],
        out_specs=[
            pl.BlockSpec((1, L, hv), lambda g, c: (g, c, 0)),
            pl.BlockSpec((1, HEADS, A_DQK, A_DV), lambda g, c: (g, 0, 0, 0)),
            pl.BlockSpec((1, HEADS, A_DQK, LANES), lambda g, c: (g, 0, 0, 0)),
            pl.BlockSpec((1, HEADS, LANES), lambda g, c: (g, 0, 0)),
        ],
        out_shape=[
            jax.ShapeDtypeStruct((b, t, hv), bf16),
            jax.ShapeDtypeStruct((b, HEADS, A_DQK, A_DV), f32),
            jax.ShapeDtypeStruct((b, HEADS, A_DQK, LANES), f32),
            jax.ShapeDtypeStruct((b, HEADS, LANES), f32),
        ],
        scratch_shapes=[
            pltpu.VMEM((HEADS, A_DQK, A_DV), f32),
            pltpu.VMEM((HEADS, A_DQK, LANES), f32),
            pltpu.VMEM((HEADS, LANES), f32),
        ],
        compiler_params=_cparams(("parallel", "arbitrary")),
        name="mlstm_prompt",
    )(qkv, kt, o, gcol, grow, ng)


def _mlstm_sample_kernel(qkv_ref, kt_ref, o_ref, gcol_ref, grow_ref, ng_ref, c0_ref, n0_ref, nrows_ref,
                         m0_ref, mcol_ref, mrow_ref, *refs, ts):
    u_ref, c_out, n_out, m_out = refs[-4:]
    R = qkv_ref.shape[1]
    G = R // ts
    shift = ts.bit_length() - 1
    hqk = HEADS * A_DQK

    r_i = lax.broadcasted_iota(jnp.int32, (R, R), 0)
    c_i = lax.broadcasted_iota(jnp.int32, (R, R), 1)
    same = (r_i >> shift) == (c_i >> shift)
    valid = same & (c_i <= r_i)
    tri = valid.astype(f32)
    tri_t = (same & (r_i <= c_i)).astype(f32)
    same_f = same.astype(f32)
    row_seq = lax.broadcasted_iota(jnp.int32, (R, 1), 0) >> shift
    g_i = lax.broadcasted_iota(jnp.int32, (G, R), 0)
    s_i = lax.broadcasted_iota(jnp.int32, (G, R), 1)
    bmask = (s_i >> shift) == g_i
    lastmask = s_i == g_i * ts + (ts - 1)
    g3 = lax.broadcasted_iota(jnp.int32, (G, A_DQK, R), 0)
    s3 = lax.broadcasted_iota(jnp.int32, (G, A_DQK, R), 2)
    bmask3 = (s3 >> shift) == g3
    lane_i = lax.broadcasted_iota(jnp.int32, (G, LANES), 1)

    gcol = gcol_ref[0]
    grow = grow_ref[0]
    lf_row = _log_sigmoid(grow[HEADS:2 * HEADS])
    b_col = _dot_exact(tri, _log_sigmoid(gcol))
    b_row = _dot_exact(lf_row, tri_t)
    bend_row = _dot_exact(lf_row, same_f)
    li_row = grow[0:HEADS]
    mcol = mcol_ref[0]
    mrow = mrow_ref[0]
    m0 = m0_ref[...]

    m_acc = jnp.zeros((G, LANES), f32)
    n_parts = []
    for h in range(HEADS):
        q_h = qkv_ref[0, :, A_DQK * h:A_DQK * (h + 1)]
        k_h = qkv_ref[0, :, hqk + A_DQK * h:hqk + A_DQK * (h + 1)]
        v_h = qkv_ref[0, :, 2 * hqk + A_DV * h:2 * hqk + A_DV * (h + 1)]
        kt_h = kt_ref[0, A_DQK * h:A_DQK * (h + 1), :]
        s_qk = _dot(q_h, kt_h)
        bc = b_col[:, HEADS + h:HEADS + h + 1]
        br = b_row[h:h + 1]
        lir = li_row[h:h + 1]
        d = jnp.where(valid, bc - br + lir, NEG_INF)
        a = bc + mcol[:, h:h + 1]
        m_t = jnp.maximum(a, jnp.max(d, axis=-1, keepdims=True))
        p = s_qk * jnp.exp(d - m_t)
        inter = jnp.exp(a - m_t)
        c_all = jnp.concatenate([c0_ref[0, g, h] for g in range(G)], axis=1).astype(bf16)
        qc = _dot(q_h, c_all)
        inter_c = jnp.zeros((R, A_DV), f32)
        for g in range(G):
            inter_c = jnp.where(row_seq == g, qc[:, A_DV * g:A_DV * (g + 1)], inter_c)
        qn = jnp.sum(q_h.astype(f32) * nrows_ref[0, :, A_DQK * h:A_DQK * (h + 1)], axis=-1, keepdims=True)
        num = _dot(p.astype(bf16), v_h) + inter * inter_c
        den = jnp.sum(p, axis=-1, keepdims=True) + inter * qn
        hh = num / jnp.maximum(jnp.abs(den), jnp.exp(-m_t))
        u = _head_norm_gate(hh, ng_ref[:, A_DV * h:A_DV * (h + 1)], o_ref[0, :, A_DV * h:A_DV * (h + 1)])
        u_ref[0, :, A_DV * h:A_DV * (h + 1)] = u.astype(bf16)

        g_row = bend_row[h:h + 1] - br + lir
        gmax = jnp.max(jnp.where(bmask, g_row, NEG_INF), axis=-1, keepdims=True)
        bend_b = jnp.sum(jnp.where(lastmask, br, 0.0), axis=-1, keepdims=True)
        m0_h = m0[:, h:h + 1]
        mnew_b = jnp.maximum(bend_b + m0_h, gmax)
        mnew_row = jnp.sum(jnp.where(bmask, mnew_b, 0.0), axis=0, keepdims=True)
        w_row = jnp.exp(g_row - mnew_row)
        decay_b = jnp.exp(bend_b + m0_h - mnew_b)
        kw = kt_h.astype(f32) * w_row
        kw3 = jnp.where(bmask3, jnp.broadcast_to(kw[None], (G, A_DQK, R)), 0.0)
        upd = _dot(kw3.reshape(G * A_DQK, R).astype(bf16), v_h)
        for g in range(G):
            c_out[0, g, h] = decay_b[g:g + 1, :] * c0_ref[0, g, h] + upd[A_DQK * g:A_DQK * (g + 1)]
        wm = jnp.where(bmask, w_row, 0.0)
        n_parts.append(decay_b * n0_ref[:, A_DQK * h:A_DQK * (h + 1)] + _dot(wm.astype(bf16), k_h))
        m_acc = jnp.where(lane_i == h, mnew_b, m_acc)

    n_out[...] = jnp.concatenate(n_parts, axis=1)
    m_out[...] = m_acc[:, :HEADS]


def _mlstm_sample(qkv, kt, o, gcol, grow, ng, c0_all, layer, c_prev, n0, nrows, m0, mcol, mrow, ts):
    _, t, _ = qkv.shape
    n_layers, bs = c0_all.shape[:2]
    G = SAMPLE_GROUP
    R = G * ts
    hv = HEADS * A_DV
    hqk = HEADS * A_DQK
    args = [qkv, kt, o, gcol, grow, ng, c0_all, n0, nrows, m0, mcol, mrow]
    extra_specs, aliases = [], {}
    if c_prev is not None:
        aliases = {len(args): 1}
        extra_specs = [pl.BlockSpec(memory_space=pl.ANY)]
        args.append(c_prev)
    return pl.pallas_call(
        functools.partial(_mlstm_sample_kernel, ts=ts),
        grid=(t // R,),
        input_output_aliases=aliases,
        in_specs=[
            pl.BlockSpec((1, R, qkv.shape[-1]), lambda i: (0, i, 0)),
            pl.BlockSpec((1, hqk, R), lambda i: (0, 0, i)),
            pl.BlockSpec((1, R, hv), lambda i: (0, i, 0)),
            pl.BlockSpec((1, R, LANES), lambda i: (0, i, 0)),
            pl.BlockSpec((1, 2 * HEADS, R), lambda i: (0, 0, i)),
            pl.BlockSpec((1, hv), lambda i: (0, 0)),
            pl.BlockSpec((1, G, HEADS, A_DQK, A_DV), lambda i: (layer, i, 0, 0, 0)),
            pl.BlockSpec((G, hqk), lambda i: (i, 0)),
            pl.BlockSpec((1, R, hqk), lambda i: (0, i, 0)),
            pl.BlockSpec((G, HEADS), lambda i: (i, 0)),
            pl.BlockSpec((1, R, LANES), lambda i: (0, i, 0)),
            pl.BlockSpec((1, HEADS, R), lambda i: (0, 0, i)),
        ] + extra_specs,
        out_specs=[
            pl.BlockSpec((1, R, hv), lambda i: (0, i, 0)),
            pl.BlockSpec((1, G, HEADS, A_DQK, A_DV), lambda i: (layer, i, 0, 0, 0)),
            pl.BlockSpec((G, hqk), lambda i: (i, 0)),
            pl.BlockSpec((G, HEADS), lambda i: (i, 0)),
        ],
        out_shape=[
            jax.ShapeDtypeStruct((1, t, hv), bf16),
            jax.ShapeDtypeStruct((n_layers, bs, HEADS, A_DQK, A_DV), f32),
            jax.ShapeDtypeStruct((bs, hqk), f32),
            jax.ShapeDtypeStruct((bs, HEADS), f32),
        ],
        compiler_params=_cparams(("parallel",)),
        name="mlstm_sample",
    )(*args)


def _expert_kernel(be_ref, nu_ref, first_ref, x_ref, wg_ref, wu_ref, wd_ref, y_ref, wg_s, wu_s, wd_s):
    i = pl.program_id(0)

    @pl.when(first_ref[i] == 1)
    def _():
        wg_s[...] = wg_ref[0, 0].astype(bf16)
        wu_s[...] = wu_ref[0, 0].astype(bf16)
        wd_s[...] = wd_ref[0, 0].astype(bf16)

    @pl.when(i < nu_ref[0])
    def _():
        x = x_ref[...].astype(bf16)
        g = _dot(x, wg_s[...])
        u = _dot(x, wu_s[...])
        hid = (g * _sigmoid(g)) * u
        y_ref[...] = _dot(hid.astype(bf16), wd_s[...])

    @pl.when(i >= nu_ref[0])
    def _():
        y_ref[...] = jnp.zeros_like(y_ref)


def _experts(x_pad, block_e, n_used, first, wg, wu, wd, layer):
    p, d = x_pad.shape
    de = wg.shape[-1]
    bm = MOE_BLOCK
    w_map = lambda i, be, nu, fi: (layer, be[i], 0, 0)
    return pl.pallas_call(
        _expert_kernel,
        grid_spec=pltpu.PrefetchScalarGridSpec(
            num_scalar_prefetch=3,
            grid=(p // bm,),
            in_specs=[
                pl.BlockSpec((bm, d), lambda i, be, nu, fi: (i, 0)),
                pl.BlockSpec((1, 1, d, de), w_map),
                pl.BlockSpec((1, 1, d, de), w_map),
                pl.BlockSpec((1, 1, de, d), w_map),
            ],
            out_specs=pl.BlockSpec((bm, d), lambda i, be, nu, fi: (i, 0)),
            scratch_shapes=[pltpu.VMEM((d, de), bf16), pltpu.VMEM((d, de), bf16), pltpu.VMEM((de, d), bf16)],
        ),
        out_shape=jax.ShapeDtypeStruct((p, d), f32),
        compiler_params=_cparams(("arbitrary",)),
        name="moe_experts",
    )(block_e, n_used, first, x_pad, wg, wu, wd)


def _combine_kernel(x_ref, ya_ref, yb_ref, w_ref, gate_ref, lng_ref, lnb_ref, o_ref):
    w = w_ref[0]
    y = w[:, 0:1] * ya_ref[0] + w[:, 1:2] * yb_ref[0]
    r = ALPHA * x_ref[0] + gate_ref[0] * y
    mu = jnp.mean(r, axis=-1, keepdims=True)
    cen = r - mu
    var = jnp.mean(cen * cen, axis=-1, keepdims=True)
    o_ref[0] = cen * lax.rsqrt(var + EPS) * lng_ref[...] + lnb_ref[...]


def _combine(x1, ya, yb, wexp, gate, ln_g, ln_b, tm):
    g, t, d = x1.shape
    return pl.pallas_call(
        _combine_kernel,
        grid=(g, t // tm),
        in_specs=[_tok_spec(tm, d), _tok_spec(tm, d), _tok_spec(tm, d), _tok_spec(tm, LANES),
                  _mod_spec(gate, tm), _full_spec((1, d)), _full_spec((1, d))],
        out_specs=_tok_spec(tm, d),
        out_shape=jax.ShapeDtypeStruct((g, t, d), f32),
        compiler_params=_cparams(("parallel", "parallel")),
        name="moe_combine",
    )(x1, ya, yb, wexp, gate, ln_g, ln_b)


def _moe(groups, h2_all, ln_g, ln_b, wg, wu, wd, layer, tm):
    bm = MOE_BLOCK
    d = groups[0][0].shape[-1]
    sizes = [x1.shape[0] * x1.shape[1] for x1, _, _ in groups]
    n = sum(sizes)
    nk = 2 * n
    expert = jnp.concatenate([jnp.transpose(r[:, 0:2, :], (0, 2, 1)).reshape(-1, 2) for _, r, _ in groups],
                             axis=0).astype(jnp.int32)
    e_flat = expert.reshape(nk)
    onehot = (e_flat[:, None] == jnp.arange(N_EXPERTS, dtype=jnp.int32)[None, :]).astype(jnp.int32)
    csum = jnp.cumsum(onehot, axis=0)
    counts = csum[-1]
    rank = jnp.sum(csum * onehot, axis=1) - 1
    padded = (counts + bm - 1) // bm * bm
    pends = jnp.cumsum(padded)
    pstarts = pends - padded
    dest = jnp.sum(onehot * pstarts[None, :], axis=1) + rank
    n_blocks = nk // bm + N_EXPERTS
    n_used = (pends[-1] // bm).astype(jnp.int32)
    blk = jnp.minimum(jnp.arange(n_blocks, dtype=jnp.int32), n_used - 1)
    block_e = jnp.minimum(jnp.sum((pends[None, :] <= (blk * bm)[:, None]).astype(jnp.int32), axis=1),
                          N_EXPERTS - 1).astype(jnp.int32)
    first = jnp.concatenate([jnp.ones((1,), jnp.int32), (block_e[1:] != block_e[:-1]).astype(jnp.int32)])
    tok = jnp.arange(nk, dtype=jnp.int32) // 2
    tok_pad = jnp.zeros((n_blocks * bm,), jnp.int32).at[dest].set(tok, unique_indices=True, mode="promise_in_bounds")
    x_pad = jnp.take(h2_all, tok_pad, axis=0, mode="clip")
    y = _experts(x_pad, block_e, n_used.reshape(1), first, wg, wu, wd, layer)
    dest2 = dest.reshape(n, 2)
    outs, off = [], 0
    for (x1, route, gate2), sz in zip(groups, sizes):
        g, t, _ = x1.shape
        ya = jnp.take(y, dest2[off:off + sz, 0], axis=0, mode="clip").reshape(g, t, d)
        yb = jnp.take(y, dest2[off:off + sz, 1], axis=0, mode="clip").reshape(g, t, d)
        wexp = jnp.pad(jnp.transpose(route[:, 2:4, :], (0, 2, 1)), ((0, 0), (0, 0), (0, LANES - 2)))
        outs.append(_combine(x1, ya, yb, wexp, gate2, ln_g, ln_b, tm))
        off += sz
    return outs


def _kv_kernel(x_ref, w_ref, g_ref, cos_ref, sin_ref, lat_ref, kr_ref, latb_ref, krb_ref):
    kva = _dot(x_ref[0].astype(bf16), w_ref[...])
    latp = kva[:, :B_KV_RANK]
    lat = latp * lax.rsqrt(jnp.mean(latp * latp, axis=-1, keepdims=True) + EPS) * g_ref[...]
    kr = kva[:, B_KV_RANK:B_KV_RANK + B_D_ROPE] * cos_ref[0] \
        + kva[:, B_KV_RANK + LANES:B_KV_RANK + LANES + B_D_ROPE] * sin_ref[0]
    lat_ref[0] = lat
    kr_ref[0] = kr
    latb_ref[0] = lat.astype(bf16)
    krb_ref[0] = kr.astype(bf16)


def _shared_kv(x, w_pad, g, cos, sin, tm):
    gg, t, d = x.shape
    rope_spec = pl.BlockSpec((1, tm, B_D_ROPE), lambda gi, i: (0, i, 0)) if cos.shape[0] == 1 else \
        _tok_spec(tm, B_D_ROPE)
    return pl.pallas_call(
        _kv_kernel,
        grid=(gg, t // tm),
        in_specs=[_tok_spec(tm, d), _full_spec(w_pad.shape), _full_spec((1, B_KV_RANK)), rope_spec, rope_spec],
        out_specs=[_tok_spec(tm, B_KV_RANK), _tok_spec(tm, B_D_ROPE),
                   _tok_spec(tm, B_KV_RANK), _tok_spec(tm, B_D_ROPE)],
        out_shape=[jax.ShapeDtypeStruct((gg, t, B_KV_RANK), f32), jax.ShapeDtypeStruct((gg, t, B_D_ROPE), f32),
                   jax.ShapeDtypeStruct((gg, t, B_KV_RANK), bf16), jax.ShapeDtypeStruct((gg, t, B_D_ROPE), bf16)],
        compiler_params=_cparams(("parallel", "parallel")),
        name="shared_kv",
    )(x, w_pad, g, cos, sin)


def _mla_q_kernel(x_ref, sc_ref, sh_ref, wdq_ref, qg_ref, wuq_ref, wuk_ref, cos_ref, sin_ref, ql_ref, qr_ref):
    h = x_ref[0] * (1.0 + sc_ref[0]) + sh_ref[0]
    cq = _dot(h.astype(bf16), wdq_ref[...])
    cq = cq * lax.rsqrt(jnp.mean(cq * cq, axis=-1, keepdims=True) + EPS) * qg_ref[...]
    q = _dot(cq.astype(bf16), wuq_ref[...])
    n_nope = HEADS * B_D_NOPE
    n_rope = HEADS * B_D_ROPE
    rot = q[:, n_nope:n_nope + n_rope] * cos_ref[0] + q[:, n_nope + n_rope:] * sin_ref[0]
    for hd in range(HEADS):
        qn = q[:, B_D_NOPE * hd:B_D_NOPE * (hd + 1)].astype(bf16)
        ql_ref[0, hd] = _dot(qn, wuk_ref[hd]).astype(bf16)
        qr_ref[0, hd] = rot[:, B_D_ROPE * hd:B_D_ROPE * (hd + 1)].astype(bf16)


def _mla_q(x, scale, shift, wdq, qg, wuq, wuk, cos8, sin8, tm):
    g, t, d = x.shape
    n_rope = HEADS * B_D_ROPE
    rope_spec = pl.BlockSpec((1, tm, n_rope), lambda gi, i: (0, i, 0)) if cos8.shape[0] == 1 else \
        _tok_spec(tm, n_rope)
    return pl.pallas_call(
        _mla_q_kernel,
        grid=(g, t // tm),
        in_specs=[_tok_spec(tm, d), _mod_spec(scale, tm), _mod_spec(shift, tm),
                  _full_spec(wdq.shape), _full_spec(qg.shape), _full_spec(wuq.shape), _full_spec(wuk.shape),
                  rope_spec, rope_spec],
        out_specs=[pl.BlockSpec((1, HEADS, tm, B_KV_RANK), lambda gi, i: (gi, 0, i, 0)),
                   pl.BlockSpec((1, HEADS, tm, B_D_ROPE), lambda gi, i: (gi, 0, i, 0))],
        out_shape=[jax.ShapeDtypeStruct((g, HEADS, t, B_KV_RANK), bf16),
                   jax.ShapeDtypeStruct((g, HEADS, t, B_D_ROPE), bf16)],
        compiler_params=_cparams(("parallel", "parallel")),
        name="mla_q",
    )(x, scale, shift, wdq, qg, wuq, wuk, cos8, sin8)


def _softmax_step(s, v, m_s, l_s, acc_s, slot):
    rows, n = s.shape
    m_old = m_s[slot]
    if n % LANES == 0:
        chunks = [s[:, LANES * c:LANES * (c + 1)] for c in range(n // LANES)]
        m_new = jnp.maximum(m_old, jnp.max(functools.reduce(jnp.maximum, chunks), axis=-1, keepdims=True))
        ps = [jnp.exp2((ch - m_new) * SOFTMAX_EXP2_SCALE) for ch in chunks]
        psum = functools.reduce(jnp.add, ps)
        p = jnp.concatenate(ps, axis=1)
    else:
        m_new = jnp.maximum(m_old, jnp.max(s, axis=-1, keepdims=True))
        p = jnp.exp2((s - m_new[:, 0:1]) * SOFTMAX_EXP2_SCALE)
        lane = lax.broadcasted_iota(jnp.int32, (rows, LANES), 1)
        psum = jnp.where(lane == 0, jnp.sum(p, axis=-1, keepdims=True), 0.0)
    alpha = jnp.exp2((m_old - m_new) * SOFTMAX_EXP2_SCALE)
    l_s[slot] = alpha * l_s[slot] + psum
    alpha_v = jnp.concatenate([alpha] * (v.shape[1] // LANES), axis=1)
    acc_s[slot] = alpha_v * acc_s[slot] + _dot(p.astype(bf16), v)
    m_s[slot] = m_new


def _flash_kernel(ql_ref, qr_ref, lat_ref, kr_ref, o_ref, m_s, l_s, acc_s, *, tq):
    i = pl.program_id(1)
    nh = FLASH_HEADS
    r_i = lax.broadcasted_iota(jnp.int32, (tq, tq), 0)
    c_i = lax.broadcasted_iota(jnp.int32, (tq, tq), 1)
    causal = c_i <= r_i

    def group_body(hg, carry):
        m_s[...] = jnp.full_like(m_s, NEG_INF)
        l_s[...] = jnp.zeros_like(l_s)
        acc_s[...] = jnp.zeros_like(acc_s)

        def block(j, masked):
            start = pl.multiple_of(j * tq, tq)
            k_lat = lat_ref[0, pl.ds(start, tq), :]
            k_r = kr_ref[0, pl.ds(start, tq), :]
            for hs in range(nh):
                h = hg * nh + hs
                s = _dot_nt(ql_ref[0, h], k_lat) + _dot_nt(qr_ref[0, h], k_r)
                if masked:
                    s = jnp.where(causal, s, NEG_INF)
                _softmax_step(s, k_lat, m_s, l_s, acc_s, hs)

        def kv_body(j, c2):
            block(j, False)
            return c2

        lax.fori_loop(0, i, kv_body, 0)
        block(i, True)
        for hs in range(nh):
            l = jnp.sum(l_s[hs], axis=-1, keepdims=True)
            o_ref[0, hg * nh + hs] = (acc_s[hs] / l).astype(bf16)
        return carry

    lax.fori_loop(0, HEADS // nh, group_body, 0)


def _flash(ql, qr, latb, krb, tq):
    b, _, t, _ = ql.shape
    nh = FLASH_HEADS
    return pl.pallas_call(
        functools.partial(_flash_kernel, tq=tq),
        grid=(b, t // tq),
        in_specs=[
            pl.BlockSpec((1, HEADS, tq, B_KV_RANK), lambda g, i: (g, 0, i, 0)),
            pl.BlockSpec((1, HEADS, tq, B_D_ROPE), lambda g, i: (g, 0, i, 0)),
            pl.BlockSpec((1, t, B_KV_RANK), lambda g, i: (g, 0, 0)),
            pl.BlockSpec((1, t, B_D_ROPE), lambda g, i: (g, 0, 0)),
        ],
        out_specs=pl.BlockSpec((1, HEADS, tq, B_KV_RANK), lambda g, i: (g, 0, i, 0)),
        out_shape=jax.ShapeDtypeStruct((b, HEADS, t, B_KV_RANK), bf16),
        scratch_shapes=[pltpu.VMEM((nh, tq, LANES), f32), pltpu.VMEM((nh, tq, LANES), f32),
                        pltpu.VMEM((nh, tq, B_KV_RANK), f32)],
        compiler_params=_cparams(("parallel", "arbitrary")),
        name="mla_flash",
    )(ql, qr, latb, krb)


def _paged_kernel(pt_ref, ql_ref, qr_ref, *rest, ts, new_pad):
    np_ = PAGES_PER_STEP
    half = np_ // 2
    lat_pages = rest[:np_]
    krt_pages = rest[np_:2 * np_]
    nl_ref, nk_ref, o_ref, m_s, l_s, acc_s = rest[2 * np_:]
    s_id = pl.program_id(1)
    ql = ql_ref[0]
    qr = qr_ref[0]

    @pl.when(s_id == 0)
    def _():
        m_s[...] = jnp.full_like(m_s, NEG_INF)
        l_s[...] = jnp.zeros_like(l_s)
        acc_s[...] = jnp.zeros_like(acc_s)

    for c in range(2):
        k_lat = jnp.concatenate([p[0] for p in lat_pages[c * half:(c + 1) * half]], axis=0).astype(bf16)
        k_rt = jnp.concatenate([p[0] for p in krt_pages[c * half:(c + 1) * half]], axis=1).astype(bf16)
        _softmax_step(_dot_nt(ql, k_lat) + _dot(qr, k_rt), k_lat, m_s, l_s, acc_s, c)

    @pl.when(s_id == pl.num_programs(1) - 1)
    def _():
        rows = ql.shape[0]
        n_lat = nl_ref[0]
        s_new = _dot_nt(ql, n_lat) + _dot_nt(qr, nk_ref[0])
        t_row = lax.broadcasted_iota(jnp.int32, (rows, new_pad), 0) & (ts - 1)
        c_new = lax.broadcasted_iota(jnp.int32, (rows, new_pad), 1)
        _softmax_step(jnp.where(c_new <= t_row, s_new, NEG_INF), n_lat, m_s, l_s, acc_s, 0)
        m = jnp.maximum(m_s[0], m_s[1])
        a0 = jnp.exp2((m_s[0] - m) * SOFTMAX_EXP2_SCALE)
        a1 = jnp.exp2((m_s[1] - m) * SOFTMAX_EXP2_SCALE)
        l = jnp.sum(a0 * l_s[0] + a1 * l_s[1], axis=-1, keepdims=True)
        nv = B_KV_RANK // LANES
        acc = jnp.concatenate([a0] * nv, axis=1) * acc_s[0] + jnp.concatenate([a1] * nv, axis=1) * acc_s[1]
        o_ref[0] = (acc / l).astype(bf16)


def _paged_attention(page_table, ql, qr, cache_lat, cache_krt, new_lat, new_kr, ts):
    bs, rows, _ = ql.shape
    n_pages = page_table.shape[1]
    np_ = PAGES_PER_STEP
    new_pad = new_lat.shape[1]
    page_map = [functools.partial(lambda b, s, pt, r: (pt[b, s * np_ + r], 0, 0), r=r) for r in range(np_)]
    lat_specs = [pl.BlockSpec((1, PAGE, B_KV_RANK), page_map[r]) for r in range(np_)]
    kr_specs = [pl.BlockSpec((1, B_D_ROPE, PAGE), page_map[r]) for r in range(np_)]
    return pl.pallas_call(
        functools.partial(_paged_kernel, ts=ts, new_pad=new_pad),
        grid_spec=pltpu.PrefetchScalarGridSpec(
            num_scalar_prefetch=1,
            grid=(bs, n_pages // np_),
            in_specs=[pl.BlockSpec((1, rows, B_KV_RANK), lambda b, s, pt: (b, 0, 0)),
                      pl.BlockSpec((1, rows, B_D_ROPE), lambda b, s, pt: (b, 0, 0))]
            + lat_specs + kr_specs
            + [pl.BlockSpec((1, new_pad, B_KV_RANK), lambda b, s, pt: (b, 0, 0)),
               pl.BlockSpec((1, new_pad, B_D_ROPE), lambda b, s, pt: (b, 0, 0))],
            out_specs=pl.BlockSpec((1, rows, B_KV_RANK), lambda b, s, pt: (b, 0, 0)),
            scratch_shapes=[pltpu.VMEM((2, rows, LANES), f32), pltpu.VMEM((2, rows, LANES), f32),
                            pltpu.VMEM((2, rows, B_KV_RANK), f32)],
        ),
        out_shape=jax.ShapeDtypeStruct((bs, rows, B_KV_RANK), bf16),
        compiler_params=_cparams(("parallel", "arbitrary")),
        name="mla_paged",
    )(page_table, ql, qr, *([cache_lat] * np_), *([cache_krt] * np_), new_lat, new_kr)


def _mla_out_kernel(ol_ref, wuv_ref, wo_ref, *refs):
    o = jnp.concatenate([_dot(ol_ref[0, hd], wuv_ref[hd]) for hd in range(HEADS)], axis=1)
    y = _dot(o.astype(bf16), wo_ref[...])
    _residual_ln_route(y, refs[:N_POST_IN], *refs[-3:])


def _mla_out(ol, wuv, wo, tm, **post):
    return _post_call(_mla_out_kernel, "mla_out", [ol, wuv, wo],
                      [pl.BlockSpec((1, HEADS, tm, B_KV_RANK), lambda gi, i: (gi, 0, i, 0)),
                       _full_spec(wuv.shape), _full_spec(wo.shape)], tm=tm, **post)


def _rope_tables(pos):
    half = B_D_ROPE // 2
    inv = jnp.power(ROPE_THETA, -jnp.arange(half, dtype=f32) / half)
    ang = pos.astype(f32)[:, None] * inv[None, :]
    cos, sin = jnp.cos(ang), jnp.sin(ang)
    return jnp.concatenate([cos, cos], axis=-1), jnp.concatenate([-sin, sin], axis=-1)


def _swap_halves(w, width):
    lead = w.shape[:-1]
    w2 = w.reshape(lead + (-1, 2, width // 2))
    return w2[..., ::-1, :].reshape(w.shape)


def kernel(x_prompt, x_sample, cache_latent, cache_krope, page_table, state_C, state_n, state_m, c_prompt, c_sample, ada_w, ada_b, ln_g, ln_b, a_w_in, a_b_gates, a_norm_g, a_w_out, b_w_kv_a, b_kv_norm_g, b_w_uk, b_w_uv, b_w_dq, b_q_norm_g, b_w_uq, b_w_o, router_w, router_b, e_w_gate, e_w_up, e_w_down):
    bp, tp, d = x_prompt.shape
    bs, ts, _ = x_sample.shape
    hqk = HEADS * A_DQK
    hv = HEADS * A_DV

    w_in = a_w_in.at[:, :, hqk:2 * hqk].multiply(A_DQK ** -0.5)
    w_in = jnp.pad(w_in, ((0, 0), (0, 0), (0, LANES - 2 * HEADS))).astype(bf16)
    bg = jnp.pad(a_b_gates, ((0, 0), (0, LANES - 2 * HEADS))).reshape(N_A_LAYERS, 1, LANES)
    w_out = a_w_out.astype(bf16)
    ng = a_norm_g.reshape(N_A_LAYERS, 1, hv)
    perm = jnp.array([(r % N_GROUPS) * PER_GROUP + r // N_GROUPS for r in range(N_EXPERTS)], jnp.int32)
    rw_perm = router_w[:, perm]
    rw_hi = rw_perm.astype(bf16)
    rw_lo = (rw_perm - rw_hi.astype(f32)).astype(bf16)
    rw = jnp.pad(jnp.concatenate([rw_hi, rw_lo], axis=1), ((0, 0), (0, LANES - 2 * N_EXPERTS)))
    rwh = jnp.pad(rw_hi, ((0, 0), (0, LANES - N_EXPERTS)))
    rb = router_b[perm].reshape(N_EXPERTS, 1)
    w_lat = b_w_kv_a[:, :B_KV_RANK]
    w_kr = b_w_kv_a[:, B_KV_RANK:]
    zpad = jnp.zeros((d, LANES - B_D_ROPE), f32)
    w_kv = jnp.concatenate([w_lat, w_kr, zpad, _swap_halves(w_kr, B_D_ROPE), zpad], axis=1).astype(bf16)
    kvg = b_kv_norm_g.reshape(1, B_KV_RANK)
    wdq = b_w_dq.astype(bf16)
    qg = b_q_norm_g.reshape(-1, 1, b_q_norm_g.shape[-1])
    uq = b_w_uq.reshape(b_w_uq.shape[0], b_w_uq.shape[1], HEADS, B_D_NOPE + B_D_ROPE)
    uq_nope = uq[..., :B_D_NOPE].reshape(uq.shape[0], uq.shape[1], HEADS * B_D_NOPE)
    uq_rope = uq[..., B_D_NOPE:].reshape(uq.shape[0], uq.shape[1], HEADS * B_D_ROPE)
    wuq = jnp.concatenate([uq_nope, uq_rope, _swap_halves(uq_rope, B_D_ROPE)], axis=-1).astype(bf16)
    wuk = jnp.transpose(b_w_uk, (1, 2, 0)).astype(bf16)
    wuv = jnp.transpose(b_w_uv, (1, 0, 2)).astype(bf16)
    wo = b_w_o.astype(bf16)

    mods = _ada_mods(jnp.concatenate([c_prompt, c_sample], axis=0), ada_w, ada_b)

    cos_p, sin_p = _rope_tables(jnp.arange(tp, dtype=jnp.int32))
    past_len = page_table.shape[1] * PAGE
    cos_s, sin_s = _rope_tables(jnp.tile(past_len + jnp.arange(ts, dtype=jnp.int32), bs))

    tm = 512
    cache_krt = jnp.swapaxes(cache_krope, 1, 2)
    streams = []
    for x0, mods_g, cos, sin, sample in ((x_prompt, mods[:, :bp], cos_p, sin_p, False),
                                         (x_sample.reshape(1, bs * ts, d), mods[:, bp:], cos_s, sin_s, True)):
        streams.append(dict(x=x0, mods=mods_g, sample=sample, cos1=cos[None], sin1=sin[None],
                            cos8=jnp.tile(cos, (1, HEADS))[None], sin8=jnp.tile(sin, (1, HEADS))[None],
                            new_c=[], new_n=[], new_m=[]))

    for l in range(DEPTH):
        lg = ln_g[l].reshape(2, 1, d)
        lb = ln_b[l].reshape(2, 1, d)
        moe_in = []
        h2_all, row0 = None, 0
        n_all = sum(st["x"].shape[0] * st["x"].shape[1] for st in streams)
        for st in streams:
            x, sample = st["x"], st["sample"]
            cos8, sin8 = st["cos8"], st["sin8"]
            latb, krb = st.get("latb"), st.get("krb")
            new_c, new_n, new_m = st["new_c"], st["new_n"], st["new_m"]
            m6 = st["mods"][l].reshape(st["mods"].shape[1], 6, d)
            if sample:
                parts = [jnp.repeat(m6[:, i], ts, axis=0)[None] for i in range(6)]
            else:
                parts = [m6[:, i][:, None, :] for i in range(6)]
            shift1, scale1, gate1, shift2, scale2, gate2 = parts
            post = dict(x=x, gate=gate1, ln_g=lg[0], ln_b=lb[0], scale2=scale2, shift2=shift2, rw=rw, rwh=rwh, rb=rb,
                        h2_rows=n_all, h2_row0=row0, h2_prev=h2_all)
            if l < N_A_LAYERS:
                qkv, o, gates = _mlstm_in(x, scale1, shift1, w_in[l], bg[l], tm)
                kt = jnp.swapaxes(qkv[:, :, hqk:2 * hqk], 1, 2)
                grow = jnp.swapaxes(gates[:, :, :2 * HEADS], 1, 2)
                if sample:
                    n0 = state_n[l].reshape(bs, hqk)
                    m0 = state_m[l]
                    nrows = jnp.repeat(n0, ts, axis=0)[None]
                    m_tok = jnp.repeat(m0, ts, axis=0)
                    mcol = jnp.pad(m_tok, ((0, 0), (0, LANES - HEADS)))[None]
                    mrow = m_tok.T[None]
                    u, c_stack, n_new, m_new = _mlstm_sample(qkv, kt, o, gates, grow, ng[l], state_C, l,
                                                             st.get("c_stack"), n0, nrows, m0, mcol, mrow, ts)
                    st["c_stack"] = c_stack
                    n_new = n_new.reshape(bs, HEADS, A_DQK)
                else:
                    u, c_new, n_rep, m_rep = _mlstm_prompt(qkv, kt, o, gates, grow, ng[l])
                    n_new = n_rep[..., 0]
                    m_new = m_rep[..., 0]
                    new_c.append(c_new)
                new_n.append(n_new)
                new_m.append(m_new)
                x1, h2_all, route = _mlstm_out(u, w_out[l], tm, **post)
            else:
                j = l - N_A_LAYERS
                ql, qr = _mla_q(x, scale1, shift1, wdq[j], qg[j], wuq[j], wuk, cos8, sin8, tm)
                if sample:
                    ql_s = jnp.transpose(ql.reshape(HEADS, bs, ts, B_KV_RANK), (1, 0, 2, 3)).reshape(
                        bs, HEADS * ts, B_KV_RANK)
                    qr_s = jnp.transpose(qr.reshape(HEADS, bs, ts, B_D_ROPE), (1, 0, 2, 3)).reshape(
                        bs, HEADS * ts, B_D_ROPE)
                    new_pad = 16
                    nl = jnp.pad(latb.reshape(bs, ts, B_KV_RANK), ((0, 0), (0, new_pad - ts), (0, 0)))
                    nk = jnp.pad(krb.reshape(bs, ts, B_D_ROPE), ((0, 0), (0, new_pad - ts), (0, 0)))
                    ol = _paged_attention(page_table, ql_s, qr_s, cache_latent, cache_krt, nl, nk, ts)
                    ol = jnp.transpose(ol.reshape(bs, HEADS, ts, B_KV_RANK), (1, 0, 2, 3)).reshape(
                        1, HEADS, bs * ts, B_KV_RANK)
                else:
                    ol = _flash(ql, qr, latb, krb, 512)
                x1, h2_all, route = _mla_out(ol, wuv, wo[j], tm, **post)
            moe_in.append((x1, route, gate2))
            row0 += x.shape[0] * x.shape[1]
        for st, x_new in zip(streams, _moe(moe_in, h2_all, lg[1], lb[1], e_w_gate, e_w_up, e_w_down, l, tm)):
            st["x"] = x_new
            if l == N_A_LAYERS - 1:
                st["lat"], st["kr"], st["latb"], st["krb"] = _shared_kv(x_new, w_kv, kvg, st["cos1"], st["sin1"], tm)

    sp, ss = streams
    return (sp["x"], ss["x"].reshape(bs, ts, d),
            jnp.stack(sp["new_c"]), jnp.stack(sp["new_n"]), jnp.stack(sp["new_m"]), sp["lat"], sp["kr"],
            ss["c_stack"], jnp.stack(ss["new_n"]), jnp.stack(ss["new_m"]),
            ss["lat"].reshape(bs, ts, B_KV_RANK), ss["kr"].reshape(bs, ts, B_D_ROPE))
```

```python
import functools

import jax
import jax.numpy as jnp
from jax import lax
from jax.experimental import pallas as pl
from jax.experimental.pallas import tpu as pltpu

f32 = jnp.float32
bf16 = jnp.bfloat16

DEPTH = 4
N_A_LAYERS = 2
HEADS = 8
A_DQK = 64
A_DV = 128
B_D_NOPE = 128
B_D_ROPE = 64
B_KV_RANK = 256
ROPE_THETA = 10000.0
ATTN_SCALE = (B_D_NOPE + B_D_ROPE) ** -0.5
N_EXPERTS = 16
N_GROUPS = 4
PER_GROUP = 4
PAGE = 128
ALPHA = (2 * DEPTH) ** 0.25
EPS = 1e-6
NEG_INF = float("-inf")
LOG2E = 1.4426950408889634
SOFTMAX_EXP2_SCALE = ATTN_SCALE * LOG2E

VMEM_LIMIT_BYTES = 56 * 1024 * 1024
LANES = 128
MLSTM_CHUNK = 128
SAMPLE_GROUP = 16
MOE_BLOCK = 256
MOE_CHUNKS = 4
PAGES_PER_STEP = 32
PAGED_CHAINS = 4
FLASH_HEADS = 4


def _cparams(sem):
    return pltpu.CompilerParams(dimension_semantics=sem, vmem_limit_bytes=VMEM_LIMIT_BYTES)


def _dot(a, b):
    return jnp.dot(a, b, preferred_element_type=f32)


def _dot_nt(a, b):
    return lax.dot_general(a, b, (((1,), (1,)), ((), ())), preferred_element_type=f32)


def _dot_exact(a, b):
    return jnp.dot(a, b, preferred_element_type=f32, precision=lax.Precision.HIGHEST)


def _sigmoid(x):
    return 1.0 / (1.0 + jnp.exp(-x))


def _log_sigmoid(x):
    return jnp.minimum(x, 0.0) - jnp.log(1.0 + jnp.exp(-jnp.abs(x)))


def _ada_kernel(c_ref, w_ref, b_ref, o_ref):
    c = c_ref[...]
    sc = (c * _sigmoid(c)).astype(bf16)
    o_ref[0] = _dot(sc, w_ref[0].astype(bf16)) + b_ref[0]


def _ada_mods(c_all, ada_w, ada_b):
    depth, d, e = ada_w.shape
    bc = c_all.shape[0]
    tn = 1536
    return pl.pallas_call(
        _ada_kernel,
        grid=(depth, e // tn),
        in_specs=[
            pl.BlockSpec((bc, d), lambda l, j: (0, 0)),
            pl.BlockSpec((1, d, tn), lambda l, j: (l, 0, j)),
            pl.BlockSpec((1, 1, tn), lambda l, j: (l, 0, j)),
        ],
        out_specs=pl.BlockSpec((1, bc, tn), lambda l, j: (l, 0, j)),
        out_shape=jax.ShapeDtypeStruct((depth, bc, e), f32),
        compiler_params=_cparams(("parallel", "parallel")),
        name="ada_mods",
    )(c_all, ada_w, ada_b.reshape(depth, 1, e))


def _mod_spec(mod, tm):
    d = mod.shape[-1]
    if mod.shape[1] == 1:
        return pl.BlockSpec((1, 1, d), lambda g, i: (g, 0, 0))
    return pl.BlockSpec((1, tm, d), lambda g, i: (g, i, 0))


def _tok_spec(tm, d):
    return pl.BlockSpec((1, tm, d), lambda g, i: (g, i, 0))


def _full_spec(shape):
    nd = len(shape)
    return pl.BlockSpec(shape, lambda g, i: (0,) * nd)


def _route(h2, rw_ref, rwh_ref, rb_ref):
    hi = h2.astype(bf16)
    lo = (h2 - hi.astype(f32)).astype(bf16)
    p = _dot(hi, rw_ref[...]) + _dot(lo, rwh_ref[...])
    pt = p.T
    logits = pt[0:N_EXPERTS] + pt[N_EXPERTS:2 * N_EXPERTS]
    sc = _sigmoid(logits)
    sel = sc + rb_ref[...]
    a = [sel[PER_GROUP * j:PER_GROUP * (j + 1)] for j in range(PER_GROUP)]
    s = [sc[PER_GROUP * j:PER_GROUP * (j + 1)] for j in range(PER_GROUP)]
    hi01, lo01 = jnp.maximum(a[0], a[1]), jnp.minimum(a[0], a[1])
    hi23, lo23 = jnp.maximum(a[2], a[3]), jnp.minimum(a[2], a[3])
    gs = jnp.maximum(hi01, hi23) + jnp.maximum(jnp.minimum(hi01, hi23), jnp.maximum(lo01, lo23))
    best = gs[0:1]
    grp = jnp.zeros_like(best)
    for g in range(1, N_GROUPS):
        better = gs[g:g + 1] > best
        grp = jnp.where(better, float(g), grp)
        best = jnp.where(better, gs[g:g + 1], best)
    mv, sv = [], []
    for j in range(PER_GROUP):
        m_j = a[j][0:1]
        s_j = s[j][0:1]
        for g in range(1, N_GROUPS):
            m_j = jnp.where(grp == float(g), a[j][g:g + 1], m_j)
            s_j = jnp.where(grp == float(g), s[j][g:g + 1], s_j)
        mv.append(m_j)
        sv.append(s_j)

    def first_argmax(vals):
        bv, bi = vals[0], jnp.zeros_like(vals[0])
        for j in range(1, PER_GROUP):
            better = vals[j] > bv
            bi = jnp.where(better, float(j), bi)
            bv = jnp.where(better, vals[j], bv)
        return bi

    i1 = first_argmax(mv)
    i2 = first_argmax([jnp.where(i1 == float(j), NEG_INF, mv[j]) for j in range(PER_GROUP)])
    w1, w2 = sv[0], sv[0]
    for j in range(1, PER_GROUP):
        w1 = jnp.where(i1 == float(j), sv[j], w1)
        w2 = jnp.where(i2 == float(j), sv[j], w2)
    tot = w1 + w2
    e1 = grp * float(PER_GROUP) + i1
    e2 = grp * float(PER_GROUP) + i2
    z = jnp.zeros_like(e1)
    return jnp.concatenate([e1, e2, w1 / tot, w2 / tot, z, z, z, z], axis=0)


N_POST_IN = 9


def _residual_ln_route(y, post_refs, x1_ref, h2_ref, route_ref):
    x_ref, gate_ref, lng_ref, lnb_ref, sc_ref, sh_ref, rw_ref, rwh_ref, rb_ref = post_refs
    r = ALPHA * x_ref[0] + gate_ref[0] * y
    mu = jnp.mean(r, axis=-1, keepdims=True)
    cen = r - mu
    var = jnp.mean(cen * cen, axis=-1, keepdims=True)
    x1 = cen * lax.rsqrt(var + EPS) * lng_ref[...] + lnb_ref[...]
    x1_ref[0] = x1
    h2 = x1 * (1.0 + sc_ref[0]) + sh_ref[0]
    h2_ref[...] = h2
    route_ref[0] = _route(h2, rw_ref, rwh_ref, rb_ref)


def _post_call(kernel_fn, name, mixer_args, mixer_specs, x, gate, ln_g, ln_b, scale2, shift2, rw, rwh, rb, tm,
               h2_rows, h2_row0, h2_prev):
    g, t, d = x.shape
    steps = t // tm
    blk0 = h2_row0 // tm
    in_specs = list(mixer_specs) + [
        _tok_spec(tm, d), _mod_spec(gate, tm), _full_spec((1, d)), _full_spec((1, d)),
        _mod_spec(scale2, tm), _mod_spec(shift2, tm),
        _full_spec((d, LANES)), _full_spec((d, LANES)), _full_spec((N_EXPERTS, 1)),
    ]
    args = list(mixer_args) + [x, gate, ln_g, ln_b, scale2, shift2, rw, rwh, rb]
    aliases = {}
    if h2_prev is not None:
        aliases = {len(args): 1}
        in_specs.append(pl.BlockSpec(memory_space=pl.ANY))
        args.append(h2_prev)
    return pl.pallas_call(
        kernel_fn,
        grid=(g, steps),
        in_specs=in_specs,
        out_specs=[_tok_spec(tm, d),
                   pl.BlockSpec((tm, d), lambda gi, i: (blk0 + gi * steps + i, 0)),
                   pl.BlockSpec((1, 8, tm), lambda gi, i: (gi, 0, i))],
        out_shape=[jax.ShapeDtypeStruct((g, t, d), f32),
                   jax.ShapeDtypeStruct((h2_rows, d), f32),
                   jax.ShapeDtypeStruct((g, 8, t), f32)],
        input_output_aliases=aliases,
        compiler_params=_cparams(("parallel", "parallel")),
        name=name,
    )(*args)


def _mlstm_in_kernel(x_ref, sc_ref, sh_ref, w_ref, bg_ref, qkv_ref, o_ref, g_ref):
    h = x_ref[0] * (1.0 + sc_ref[0]) + sh_ref[0]
    z = _dot(h.astype(bf16), w_ref[...])
    hqk2 = 2 * HEADS * A_DQK
    hv = HEADS * A_DV
    qkv_ref[0] = z[:, :hqk2 + hv].astype(bf16)
    o_ref[0] = z[:, hqk2 + hv:hqk2 + 2 * hv]
    g_ref[0] = z[:, hqk2 + 2 * hv:] + bg_ref[...]


def _mlstm_in(x, scale, shift, w_pad, bg_pad, tm):
    g, t, d = x.shape
    n_qkv = 2 * HEADS * A_DQK + HEADS * A_DV
    hv = HEADS * A_DV
    return pl.pallas_call(
        _mlstm_in_kernel,
        grid=(g, t // tm),
        in_specs=[_tok_spec(tm, d), _mod_spec(scale, tm), _mod_spec(shift, tm),
                  _full_spec(w_pad.shape), _full_spec((1, LANES))],
        out_specs=[_tok_spec(tm, n_qkv), _tok_spec(tm, hv), _tok_spec(tm, LANES)],
        out_shape=[jax.ShapeDtypeStruct((g, t, n_qkv), bf16),
                   jax.ShapeDtypeStruct((g, t, hv), f32),
                   jax.ShapeDtypeStruct((g, t, LANES), f32)],
        compiler_params=_cparams(("parallel", "parallel")),
        name="mlstm_in",
    )(x, scale, shift, w_pad, bg_pad)


def _mlstm_out_kernel(u_ref, w_ref, *refs):
    y = _dot(u_ref[0], w_ref[...])
    _residual_ln_route(y, refs[:N_POST_IN], *refs[-3:])


def _mlstm_out(u, w_out, tm, **post):
    return _post_call(_mlstm_out_kernel, "mlstm_out", [u, w_out],
                      [_tok_spec(tm, u.shape[-1]), _full_spec(w_out.shape)], tm=tm, **post)


def _head_norm_gate(hh, ng, o):
    mu = jnp.mean(hh, axis=-1, keepdims=True)
    cen = hh - mu
    var = jnp.mean(cen * cen, axis=-1, keepdims=True)
    return (cen * lax.rsqrt(var + EPS) * ng) * _sigmoid(o)


def _mlstm_prompt_kernel(qkv_ref, kt_ref, o_ref, gcol_ref, grow_ref, ng_ref,
                         u_ref, c_out, n_out, m_out, c_s, n_s, m_s):
    c = pl.program_id(1)
    nc = pl.num_programs(1)
    L = qkv_ref.shape[1]
    hqk = HEADS * A_DQK

    @pl.when(c == 0)
    def _():
        c_s[...] = jnp.zeros_like(c_s)
        n_s[...] = jnp.zeros_like(n_s)
        m_s[...] = jnp.zeros_like(m_s)

    r_i = lax.broadcasted_iota(jnp.int32, (L, L), 0)
    c_i = lax.broadcasted_iota(jnp.int32, (L, L), 1)
    causal = c_i <= r_i
    tri = causal.astype(f32)
    tri_t = (r_i <= c_i).astype(f32)

    gcol = gcol_ref[0]
    grow = grow_ref[0]
    b_col = _dot_exact(tri, _log_sigmoid(gcol))
    b_row = _dot_exact(_log_sigmoid(grow[HEADS:2 * HEADS]), tri_t)
    li_row = grow[0:HEADS]

    for h in range(HEADS):
        q_h = qkv_ref[0, :, A_DQK * h:A_DQK * (h + 1)]
        v_h = qkv_ref[0, :, 2 * hqk + A_DV * h:2 * hqk + A_DV * (h + 1)]
        kt_h = kt_ref[0, A_DQK * h:A_DQK * (h + 1), :]
        s_qk = _dot(q_h, kt_h)
        bc = b_col[:, HEADS + h:HEADS + h + 1]
        br = b_row[h:h + 1]
        lir = li_row[h:h + 1]
        m_prev = m_s[h:h + 1, 0:1]
        d = jnp.where(causal, bc - br + lir, NEG_INF)
        a = bc + m_prev
        m_t = jnp.maximum(a, jnp.max(d, axis=-1, keepdims=True))
        p = s_qk * jnp.exp(d - m_t)
        inter = jnp.exp(a - m_t)
        c_aug = jnp.concatenate([c_s[h], n_s[h]], axis=1).astype(bf16)
        qc = _dot(q_h, c_aug)
        num = _dot(p.astype(bf16), v_h) + inter * qc[:, :A_DV]
        den = jnp.sum(p, axis=-1, keepdims=True) + inter * qc[:, A_DV:A_DV + 1]
        hh = num / jnp.maximum(jnp.abs(den), jnp.exp(-m_t))
        u = _head_norm_gate(hh, ng_ref[:, A_DV * h:A_DV * (h + 1)], o_ref[0, :, A_DV * h:A_DV * (h + 1)])
        u_ref[0, :, A_DV * h:A_DV * (h + 1)] = u.astype(bf16)

        b_end = br[:, L - 1:L]
        g_row = b_end - br + lir
        m_new = jnp.maximum(b_end + m_prev, jnp.max(g_row, axis=-1, keepdims=True))
        w_row = jnp.exp(g_row - m_new)
        decay = jnp.exp(b_end + m_prev - m_new)
        kw = kt_h.astype(f32) * w_row
        c_s[h] = decay * c_s[h] + _dot(kw.astype(bf16), v_h)
        n_s[h] = decay * n_s[h] + jnp.sum(kw, axis=-1, keepdims=True)
        m_s[h:h + 1, :] = jnp.broadcast_to(m_new, (1, LANES))

    @pl.when(c == nc - 1)
    def _():
        c_out[0] = c_s[...]
        n_out[0] = n_s[...]
        m_out[0] = m_s[...]


def _mlstm_prompt(qkv, kt, o, gcol, grow, ng):
    b, t, _ = qkv.shape
    L = MLSTM_CHUNK
    hv = HEADS * A_DV
    return pl.pallas_call(
        _mlstm_prompt_kernel,
        grid=(b, t // L),
        in_specs=[
            pl.BlockSpec((1, L, qkv.shape[-1]), lambda g, c: (g, c, 0)),
            pl.BlockSpec((1, HEADS * A_DQK, L), lambda g, c: (g, 0, c)),
            pl.BlockSpec((1, L, hv), lambda g, c: (g, c, 0)),
            pl.BlockSpec((1, L, LANES), lambda g, c: (g, c, 0)),
            pl.BlockSpec((1, 2 * HEADS, L), lambda g, c: (g, 0, c)),
            pl.BlockSpec((1, hv), lambda g, c: (0, 0)),
        ],
        out_specs=[
            pl.BlockSpec((1, L, hv), lambda g, c: (g, c, 0)),
            pl.BlockSpec((1, HEADS, A_DQK, A_DV), lambda g, c: (g, 0, 0, 0)),
            pl.BlockSpec((1, HEADS, A_DQK, LANES), lambda g, c: (g, 0, 0, 0)),
            pl.BlockSpec((1, HEADS, LANES), lambda g, c: (g, 0, 0)),
        ],
        out_shape=[
            jax.ShapeDtypeStruct((b, t, hv), bf16),
            jax.ShapeDtypeStruct((b, HEADS, A_DQK, A_DV), f32),
            jax.ShapeDtypeStruct((b, HEADS, A_DQK, LANES), f32),
            jax.ShapeDtypeStruct((b, HEADS, LANES), f32),
        ],
        scratch_shapes=[
            pltpu.VMEM((HEADS, A_DQK, A_DV), f32),
            pltpu.VMEM((HEADS, A_DQK, LANES), f32),
            pltpu.VMEM((HEADS, LANES), f32),
        ],
        compiler_params=_cparams(("parallel", "arbitrary")),
        name="mlstm_prompt",
    )(qkv, kt, o, gcol, grow, ng)


def _mlstm_sample_kernel(qkv_ref, kt_ref, o_ref, gcol_ref, grow_ref, ng_ref, c0_ref, n0_ref, nrows_ref,
                         m0_ref, mcol_ref, mrow_ref, *refs, ts):
    u_ref, c_out, n_out, m_out = refs[-4:]
    R = qkv_ref.shape[1]
    G = R // ts
    shift = ts.bit_length() - 1
    hqk = HEADS * A_DQK

    r_i = lax.broadcasted_iota(jnp.int32, (R, R), 0)
    c_i = lax.broadcasted_iota(jnp.int32, (R, R), 1)
    same = (r_i >> shift) == (c_i >> shift)
    valid = same & (c_i <= r_i)
    tri = valid.astype(f32)
    tri_t = (same & (r_i <= c_i)).astype(f32)
    same_f = same.astype(f32)
    row_seq = lax.broadcasted_iota(jnp.int32, (R, 1), 0) >> shift
    g_i = lax.broadcasted_iota(jnp.int32, (G, R), 0)
    s_i = lax.broadcasted_iota(jnp.int32, (G, R), 1)
    bmask = (s_i >> shift) == g_i
    lastmask = s_i == g_i * ts + (ts - 1)
    g3 = lax.broadcasted_iota(jnp.int32, (G, A_DQK, R), 0)
    s3 = lax.broadcasted_iota(jnp.int32, (G, A_DQK, R), 2)
    bmask3 = (s3 >> shift) == g3
    lane_i = lax.broadcasted_iota(jnp.int32, (G, LANES), 1)

    gcol = gcol_ref[0]
    grow = grow_ref[0]
    lf_row = _log_sigmoid(grow[HEADS:2 * HEADS])
    b_col = _dot_exact(tri, _log_sigmoid(gcol))
    b_row = _dot_exact(lf_row, tri_t)
    bend_row = _dot_exact(lf_row, same_f)
    li_row = grow[0:HEADS]
    mcol = mcol_ref[0]
    mrow = mrow_ref[0]
    m0 = m0_ref[...]

    m_acc = jnp.zeros((G, LANES), f32)
    n_parts = []
    for h in range(HEADS):
        q_h = qkv_ref[0, :, A_DQK * h:A_DQK * (h + 1)]
        k_h = qkv_ref[0, :, hqk + A_DQK * h:hqk + A_DQK * (h + 1)]
        v_h = qkv_ref[0, :, 2 * hqk + A_DV * h:2 * hqk + A_DV * (h + 1)]
        kt_h = kt_ref[0, A_DQK * h:A_DQK * (h + 1), :]
        s_qk = _dot(q_h, kt_h)
        bc = b_col[:, HEADS + h:HEADS + h + 1]
        br = b_row[h:h + 1]
        lir = li_row[h:h + 1]
        d = jnp.where(valid, bc - br + lir, NEG_INF)
        a = bc + mcol[:, h:h + 1]
        m_t = jnp.maximum(a, jnp.max(d, axis=-1, keepdims=True))
        p = s_qk * jnp.exp(d - m_t)
        inter = jnp.exp(a - m_t)
        c_all = jnp.concatenate([c0_ref[0, g, h] for g in range(G)], axis=1).astype(bf16)
        qc = _dot(q_h, c_all)
        inter_c = jnp.zeros((R, A_DV), f32)
        for g in range(G):
            inter_c = jnp.where(row_seq == g, qc[:, A_DV * g:A_DV * (g + 1)], inter_c)
        qn = jnp.sum(q_h.astype(f32) * nrows_ref[0, :, A_DQK * h:A_DQK * (h + 1)], axis=-1, keepdims=True)
        num = _dot(p.astype(bf16), v_h) + inter * inter_c
        den = jnp.sum(p, axis=-1, keepdims=True) + inter * qn
        hh = num / jnp.maximum(jnp.abs(den), jnp.exp(-m_t))
        u = _head_norm_gate(hh, ng_ref[:, A_DV * h:A_DV * (h + 1)], o_ref[0, :, A_DV * h:A_DV * (h + 1)])
        u_ref[0, :, A_DV * h:A_DV * (h + 1)] = u.astype(bf16)

        g_row = bend_row[h:h + 1] - br + lir
        gmax = jnp.max(jnp.where(bmask, g_row, NEG_INF), axis=-1, keepdims=True)
        bend_b = jnp.sum(jnp.where(lastmask, br, 0.0), axis=-1, keepdims=True)
        m0_h = m0[:, h:h + 1]
        mnew_b = jnp.maximum(bend_b + m0_h, gmax)
        mnew_row = jnp.sum(jnp.where(bmask, mnew_b, 0.0), axis=0, keepdims=True)
        w_row = jnp.exp(g_row - mnew_row)
        decay_b = jnp.exp(bend_b + m0_h - mnew_b)
        kw = kt_h.astype(f32) * w_row
        kw3 = jnp.where(bmask3, jnp.broadcast_to(kw[None], (G, A_DQK, R)), 0.0)
        upd = _dot(kw3.reshape(G * A_DQK, R).astype(bf16), v_h)
        for g in range(G):
            c_out[0, g, h] = decay_b[g:g + 1, :] * c0_ref[0, g, h] + upd[A_DQK * g:A_DQK * (g + 1)]
        wm = jnp.where(bmask, w_row, 0.0)
        n_parts.append(decay_b * n0_ref[:, A_DQK * h:A_DQK * (h + 1)] + _dot(wm.astype(bf16), k_h))
        m_acc = jnp.where(lane_i == h, mnew_b, m_acc)

    n_out[...] = jnp.concatenate(n_parts, axis=1)
    m_out[...] = m_acc[:, :HEADS]


def _mlstm_sample(qkv, kt, o, gcol, grow, ng, c0_all, layer, c_prev, n0, nrows, m0, mcol, mrow, ts):
    _, t, _ = qkv.shape
    n_layers, bs = c0_all.shape[:2]
    G = SAMPLE_GROUP
    R = G * ts
    hv = HEADS * A_DV
    hqk = HEADS * A_DQK
    args = [qkv, kt, o, gcol, grow, ng, c0_all, n0, nrows, m0, mcol, mrow]
    extra_specs, aliases = [], {}
    if c_prev is not None:
        aliases = {len(args): 1}
        extra_specs = [pl.BlockSpec(memory_space=pl.ANY)]
        args.append(c_prev)
    return pl.pallas_call(
        functools.partial(_mlstm_sample_kernel, ts=ts),
        grid=(t // R,),
        input_output_aliases=aliases,
        in_specs=[
            pl.BlockSpec((1, R, qkv.shape[-1]), lambda i: (0, i, 0)),
            pl.BlockSpec((1, hqk, R), lambda i: (0, 0, i)),
            pl.BlockSpec((1, R, hv), lambda i: (0, i, 0)),
            pl.BlockSpec((1, R, LANES), lambda i: (0, i, 0)),
            pl.BlockSpec((1, 2 * HEADS, R), lambda i: (0, 0, i)),
            pl.BlockSpec((1, hv), lambda i: (0, 0)),
            pl.BlockSpec((1, G, HEADS, A_DQK, A_DV), lambda i: (layer, i, 0, 0, 0)),
            pl.BlockSpec((G, hqk), lambda i: (i, 0)),
            pl.BlockSpec((1, R, hqk), lambda i: (0, i, 0)),
            pl.BlockSpec((G, HEADS), lambda i: (i, 0)),
            pl.BlockSpec((1, R, LANES), lambda i: (0, i, 0)),
            pl.BlockSpec((1, HEADS, R), lambda i: (0, 0, i)),
        ] + extra_specs,
        out_specs=[
            pl.BlockSpec((1, R, hv), lambda i: (0, i, 0)),
            pl.BlockSpec((1, G, HEADS, A_DQK, A_DV), lambda i: (layer, i, 0, 0, 0)),
            pl.BlockSpec((G, hqk), lambda i: (i, 0)),
            pl.BlockSpec((G, HEADS), lambda i: (i, 0)),
        ],
        out_shape=[
            jax.ShapeDtypeStruct((1, t, hv), bf16),
            jax.ShapeDtypeStruct((n_layers, bs, HEADS, A_DQK, A_DV), f32),
            jax.ShapeDtypeStruct((bs, hqk), f32),
            jax.ShapeDtypeStruct((bs, HEADS), f32),
        ],
        compiler_params=_cparams(("parallel",)),
        name="mlstm_sample",
    )(*args)


def _expert_kernel(be_ref, nu_ref, first_ref, x_ref, wg_ref, wu_ref, wd_ref, *refs, blk0):
    y_ref, wg_s, wu_s, wd_s = refs[-4:]
    i = blk0 + pl.program_id(0)

    @pl.when((first_ref[i] == 1) | (pl.program_id(0) == 0))
    def _():
        wg_s[...] = wg_ref[0, 0].astype(bf16)
        wu_s[...] = wu_ref[0, 0].astype(bf16)
        wd_s[...] = wd_ref[0, 0].astype(bf16)

    @pl.when(i < nu_ref[0])
    def _():
        x = x_ref[...].astype(bf16)
        g = _dot(x, wg_s[...])
        u = _dot(x, wu_s[...])
        hid = (g * _sigmoid(g)) * u
        y_ref[...] = _dot(hid.astype(bf16), wd_s[...])

    @pl.when(i >= nu_ref[0])
    def _():
        y_ref[...] = jnp.zeros_like(y_ref)


def _experts(x_chunk, blk0, p_total, y_prev, block_e, n_used, first, wg, wu, wd, layer):
    pc, d = x_chunk.shape
    de = wg.shape[-1]
    bm = MOE_BLOCK
    w_map = lambda i, be, nu, fi: (layer, be[blk0 + i], 0, 0)
    args = [block_e, n_used, first, x_chunk, wg, wu, wd]
    in_specs = [
        pl.BlockSpec((bm, d), lambda i, be, nu, fi: (i, 0)),
        pl.BlockSpec((1, 1, d, de), w_map),
        pl.BlockSpec((1, 1, d, de), w_map),
        pl.BlockSpec((1, 1, de, d), w_map),
    ]
    aliases = {}
    if y_prev is not None:
        aliases = {len(args): 0}
        in_specs.append(pl.BlockSpec(memory_space=pl.ANY))
        args.append(y_prev)
    return pl.pallas_call(
        functools.partial(_expert_kernel, blk0=blk0),
        grid_spec=pltpu.PrefetchScalarGridSpec(
            num_scalar_prefetch=3,
            grid=(pc // bm,),
            in_specs=in_specs,
            out_specs=pl.BlockSpec((bm, d), lambda i, be, nu, fi: (blk0 + i, 0)),
            scratch_shapes=[pltpu.VMEM((d, de), bf16), pltpu.VMEM((d, de), bf16), pltpu.VMEM((de, d), bf16)],
        ),
        out_shape=jax.ShapeDtypeStruct((p_total, d), f32),
        input_output_aliases=aliases,
        compiler_params=_cparams(("arbitrary",)),
        name="moe_experts",
    )(*args)


def _combine_kernel(x_ref, y2_ref, w_ref, gate_ref, lng_ref, lnb_ref, o_ref):
    w = w_ref[0]
    d = x_ref.shape[-1]
    y = w[:, 0:1] * y2_ref[:, :d] + w[:, 1:2] * y2_ref[:, d:]
    r = ALPHA * x_ref[0] + gate_ref[0] * y
    mu = jnp.mean(r, axis=-1, keepdims=True)
    cen = r - mu
    var = jnp.mean(cen * cen, axis=-1, keepdims=True)
    o_ref[0] = cen * lax.rsqrt(var + EPS) * lng_ref[...] + lnb_ref[...]


def _combine(x1, y2, row0, wexp, gate, ln_g, ln_b, tm):
    g, t, d = x1.shape
    steps = t // tm
    blk0 = row0 // tm
    return pl.pallas_call(
        _combine_kernel,
        grid=(g, steps),
        in_specs=[_tok_spec(tm, d), pl.BlockSpec((tm, 2 * d), lambda gi, i: (blk0 + gi * steps + i, 0)),
                  _tok_spec(tm, LANES), _mod_spec(gate, tm), _full_spec((1, d)), _full_spec((1, d))],
        out_specs=_tok_spec(tm, d),
        out_shape=jax.ShapeDtypeStruct((g, t, d), f32),
        compiler_params=_cparams(("parallel", "parallel")),
        name="moe_combine",
    )(x1, y2, wexp, gate, ln_g, ln_b)


def _moe(groups, h2_all, ln_g, ln_b, wg, wu, wd, layer, tm):
    bm = MOE_BLOCK
    d = groups[0][0].shape[-1]
    sizes = [x1.shape[0] * x1.shape[1] for x1, _, _ in groups]
    n = sum(sizes)
    nk = 2 * n
    expert = jnp.concatenate([jnp.transpose(r[:, 0:2, :], (0, 2, 1)).reshape(-1, 2) for _, r, _ in groups],
                             axis=0).astype(jnp.int32)
    e_flat = expert.reshape(nk)
    onehot = (e_flat[:, None] == jnp.arange(N_EXPERTS, dtype=jnp.int32)[None, :]).astype(jnp.int32)
    csum = jnp.cumsum(onehot, axis=0)
    counts = csum[-1]
    rank = jnp.sum(csum * onehot, axis=1) - 1
    padded = (counts + bm - 1) // bm * bm
    pends = jnp.cumsum(padded)
    pstarts = pends - padded
    dest = jnp.sum(onehot * pstarts[None, :], axis=1) + rank
    n_blocks = nk // bm + N_EXPERTS
    n_used = (pends[-1] // bm).astype(jnp.int32)
    blk = jnp.minimum(jnp.arange(n_blocks, dtype=jnp.int32), n_used - 1)
    block_e = jnp.minimum(jnp.sum((pends[None, :] <= (blk * bm)[:, None]).astype(jnp.int32), axis=1),
                          N_EXPERTS - 1).astype(jnp.int32)
    first = jnp.concatenate([jnp.ones((1,), jnp.int32), (block_e[1:] != block_e[:-1]).astype(jnp.int32)])
    tok = jnp.arange(nk, dtype=jnp.int32) // 2
    tok_pad = jnp.zeros((n_blocks * bm,), jnp.int32).at[dest].set(tok, unique_indices=True, mode="promise_in_bounds")
    chunk_blocks = n_blocks // MOE_CHUNKS
    y = None
    for c in range(MOE_CHUNKS):
        blk0 = c * chunk_blocks
        x_c = jnp.take(h2_all, tok_pad[blk0 * bm:(blk0 + chunk_blocks) * bm], axis=0, mode="clip")
        y = _experts(x_c, blk0, n_blocks * bm, y, block_e, n_used.reshape(1), first, wg, wu, wd, layer)
    y2 = jnp.take(y, dest, axis=0, mode="clip").reshape(n, 2 * d)
    outs, off = [], 0
    for (x1, route, gate2), sz in zip(groups, sizes):
        wexp = jnp.pad(jnp.transpose(route[:, 2:4, :], (0, 2, 1)), ((0, 0), (0, 0), (0, LANES - 2)))
        outs.append(_combine(x1, y2, off, wexp, gate2, ln_g, ln_b, tm))
        off += sz
    return outs


def _kv_kernel(x_ref, w_ref, g_ref, cos_ref, sin_ref, lat_ref, kr_ref, latb_ref, krb_ref):
    kva = _dot(x_ref[0].astype(bf16), w_ref[...])
    latp = kva[:, :B_KV_RANK]
    lat = latp * lax.rsqrt(jnp.mean(latp * latp, axis=-1, keepdims=True) + EPS) * g_ref[...]
    kr = kva[:, B_KV_RANK:B_KV_RANK + B_D_ROPE] * cos_ref[0] \
        + kva[:, B_KV_RANK + LANES:B_KV_RANK + LANES + B_D_ROPE] * sin_ref[0]
    lat_ref[0] = lat
    kr_ref[0] = kr
    latb_ref[0] = lat.astype(bf16)
    krb_ref[0] = kr.astype(bf16)


def _shared_kv(x, w_pad, g, cos, sin, tm):
    gg, t, d = x.shape
    rope_spec = pl.BlockSpec((1, tm, B_D_ROPE), lambda gi, i: (0, i, 0)) if cos.shape[0] == 1 else \
        _tok_spec(tm, B_D_ROPE)
    return pl.pallas_call(
        _kv_kernel,
        grid=(gg, t // tm),
        in_specs=[_tok_spec(tm, d), _full_spec(w_pad.shape), _full_spec((1, B_KV_RANK)), rope_spec, rope_spec],
        out_specs=[_tok_spec(tm, B_KV_RANK), _tok_spec(tm, B_D_ROPE),
                   _tok_spec(tm, B_KV_RANK), _tok_spec(tm, B_D_ROPE)],
        out_shape=[jax.ShapeDtypeStruct((gg, t, B_KV_RANK), f32), jax.ShapeDtypeStruct((gg, t, B_D_ROPE), f32),
                   jax.ShapeDtypeStruct((gg, t, B_KV_RANK), bf16), jax.ShapeDtypeStruct((gg, t, B_D_ROPE), bf16)],
        compiler_params=_cparams(("parallel", "parallel")),
        name="shared_kv",
    )(x, w_pad, g, cos, sin)


def _mla_q_kernel(x_ref, sc_ref, sh_ref, wdq_ref, qg_ref, wuq_ref, wuk_ref, cos_ref, sin_ref, ql_ref, qr_ref):
    h = x_ref[0] * (1.0 + sc_ref[0]) + sh_ref[0]
    cq = _dot(h.astype(bf16), wdq_ref[...])
    cq = cq * lax.rsqrt(jnp.mean(cq * cq, axis=-1, keepdims=True) + EPS) * qg_ref[...]
    q = _dot(cq.astype(bf16), wuq_ref[...])
    n_nope = HEADS * B_D_NOPE
    n_rope = HEADS * B_D_ROPE
    rot = q[:, n_nope:n_nope + n_rope] * cos_ref[0] + q[:, n_nope + n_rope:] * sin_ref[0]
    for hd in range(HEADS):
        qn = q[:, B_D_NOPE * hd:B_D_NOPE * (hd + 1)].astype(bf16)
        ql_ref[0, hd] = _dot(qn, wuk_ref[hd]).astype(bf16)
        qr_ref[0, hd] = rot[:, B_D_ROPE * hd:B_D_ROPE * (hd + 1)].astype(bf16)


def _mla_q(x, scale, shift, wdq, qg, wuq, wuk, cos8, sin8, tm):
    g, t, d = x.shape
    n_rope = HEADS * B_D_ROPE
    rope_spec = pl.BlockSpec((1, tm, n_rope), lambda gi, i: (0, i, 0)) if cos8.shape[0] == 1 else \
        _tok_spec(tm, n_rope)
    return pl.pallas_call(
        _mla_q_kernel,
        grid=(g, t // tm),
        in_specs=[_tok_spec(tm, d), _mod_spec(scale, tm), _mod_spec(shift, tm),
                  _full_spec(wdq.shape), _full_spec(qg.shape), _full_spec(wuq.shape), _full_spec(wuk.shape),
                  rope_spec, rope_spec],
        out_specs=[pl.BlockSpec((1, HEADS, tm, B_KV_RANK), lambda gi, i: (gi, 0, i, 0)),
                   pl.BlockSpec((1, HEADS, tm, B_D_ROPE), lambda gi, i: (gi, 0, i, 0))],
        out_shape=[jax.ShapeDtypeStruct((g, HEADS, t, B_KV_RANK), bf16),
                   jax.ShapeDtypeStruct((g, HEADS, t, B_D_ROPE), bf16)],
        compiler_params=_cparams(("parallel", "parallel")),
        name="mla_q",
    )(x, scale, shift, wdq, qg, wuq, wuk, cos8, sin8)


def _softmax_step(s, v, m_s, l_s, acc_s, slot):
    rows, n = s.shape
    m_old = m_s[slot]
    if n % LANES == 0:
        chunks = [s[:, LANES * c:LANES * (c + 1)] for c in range(n // LANES)]
        m_new = jnp.maximum(m_old, jnp.max(functools.reduce(jnp.maximum, chunks), axis=-1, keepdims=True))
        ps = [jnp.exp2((ch - m_new) * SOFTMAX_EXP2_SCALE) for ch in chunks]
        psum = functools.reduce(jnp.add, ps)
        p = jnp.concatenate(ps, axis=1)
    else:
        m_new = jnp.maximum(m_old, jnp.max(s, axis=-1, keepdims=True))
        p = jnp.exp2((s - m_new[:, 0:1]) * SOFTMAX_EXP2_SCALE)
        lane = lax.broadcasted_iota(jnp.int32, (rows, LANES), 1)
        psum = jnp.where(lane == 0, jnp.sum(p, axis=-1, keepdims=True), 0.0)
    alpha = jnp.exp2((m_old - m_new) * SOFTMAX_EXP2_SCALE)
    l_s[slot] = alpha * l_s[slot] + psum
    alpha_v = jnp.concatenate([alpha] * (v.shape[1] // LANES), axis=1)
    acc_s[slot] = alpha_v * acc_s[slot] + _dot(p.astype(bf16), v)
    m_s[slot] = m_new


def _flash_kernel(ql_ref, qr_ref, lat_ref, kr_ref, o_ref, m_s, l_s, acc_s, *, tq):
    i = pl.program_id(1)
    nh = FLASH_HEADS
    r_i = lax.broadcasted_iota(jnp.int32, (tq, tq), 0)
    c_i = lax.broadcasted_iota(jnp.int32, (tq, tq), 1)
    causal = c_i <= r_i

    def group_body(hg, carry):
        m_s[...] = jnp.full_like(m_s, NEG_INF)
        l_s[...] = jnp.zeros_like(l_s)
        acc_s[...] = jnp.zeros_like(acc_s)

        def block(j, masked):
            start = pl.multiple_of(j * tq, tq)
            k_lat = lat_ref[0, pl.ds(start, tq), :]
            k_r = kr_ref[0, pl.ds(start, tq), :]
            for hs in range(nh):
                h = hg * nh + hs
                s = _dot_nt(ql_ref[0, h], k_lat) + _dot_nt(qr_ref[0, h], k_r)
                if masked:
                    s = jnp.where(causal, s, NEG_INF)
                _softmax_step(s, k_lat, m_s, l_s, acc_s, hs)

        def kv_body(j, c2):
            block(j, False)
            return c2

        lax.fori_loop(0, i, kv_body, 0)
        block(i, True)
        for hs in range(nh):
            l = jnp.sum(l_s[hs], axis=-1, keepdims=True)
            o_ref[0, hg * nh + hs] = (acc_s[hs] / l).astype(bf16)
        return carry

    lax.fori_loop(0, HEADS // nh, group_body, 0)


def _flash(ql, qr, latb, krb, tq):
    b, _, t, _ = ql.shape
    nh = FLASH_HEADS
    return pl.pallas_call(
        functools.partial(_flash_kernel, tq=tq),
        grid=(b, t // tq),
        in_specs=[
            pl.BlockSpec((1, HEADS, tq, B_KV_RANK), lambda g, i: (g, 0, i, 0)),
            pl.BlockSpec((1, HEADS, tq, B_D_ROPE), lambda g, i: (g, 0, i, 0)),
            pl.BlockSpec((1, t, B_KV_RANK), lambda g, i: (g, 0, 0)),
            pl.BlockSpec((1, t, B_D_ROPE), lambda g, i: (g, 0, 0)),
        ],
        out_specs=pl.BlockSpec((1, HEADS, tq, B_KV_RANK), lambda g, i: (g, 0, i, 0)),
        out_shape=jax.ShapeDtypeStruct((b, HEADS, t, B_KV_RANK), bf16),
        scratch_shapes=[pltpu.VMEM((nh, tq, LANES), f32), pltpu.VMEM((nh, tq, LANES), f32),
                        pltpu.VMEM((nh, tq, B_KV_RANK), f32)],
        compiler_params=_cparams(("parallel", "arbitrary")),
        name="mla_flash",
    )(ql, qr, latb, krb)


def _paged_kernel(pt_ref, ql_ref, qr_ref, *rest, ts, new_pad):
    np_ = PAGES_PER_STEP
    nc = PAGED_CHAINS
    per = np_ // nc
    lat_pages = rest[:np_]
    krt_pages = rest[np_:2 * np_]
    nl_ref, nk_ref, o_ref, m_s, l_s, acc_s = rest[2 * np_:]
    s_id = pl.program_id(1)
    ql = ql_ref[0]
    qr = qr_ref[0]

    @pl.when(s_id == 0)
    def _():
        m_s[...] = jnp.full_like(m_s, NEG_INF)
        l_s[...] = jnp.zeros_like(l_s)
        acc_s[...] = jnp.zeros_like(acc_s)

    for c in range(nc):
        k_lat = jnp.concatenate([p[0] for p in lat_pages[c * per:(c + 1) * per]], axis=0).astype(bf16)
        k_rt = jnp.concatenate([p[0] for p in krt_pages[c * per:(c + 1) * per]], axis=1).astype(bf16)
        _softmax_step(_dot_nt(ql, k_lat) + _dot(qr, k_rt), k_lat, m_s, l_s, acc_s, c)

    @pl.when(s_id == pl.num_programs(1) - 1)
    def _():
        rows = ql.shape[0]
        n_lat = nl_ref[0]
        s_new = _dot_nt(ql, n_lat) + _dot_nt(qr, nk_ref[0])
        t_row = lax.broadcasted_iota(jnp.int32, (rows, new_pad), 0) & (ts - 1)
        c_new = lax.broadcasted_iota(jnp.int32, (rows, new_pad), 1)
        _softmax_step(jnp.where(c_new <= t_row, s_new, NEG_INF), n_lat, m_s, l_s, acc_s, 0)
        m = functools.reduce(jnp.maximum, [m_s[c] for c in range(nc)])
        nv = B_KV_RANK // LANES
        l = jnp.zeros_like(m)
        acc = jnp.zeros_like(acc_s[0])
        for c in range(nc):
            a_c = jnp.exp2((m_s[c] - m) * SOFTMAX_EXP2_SCALE)
            l = l + a_c * l_s[c]
            acc = acc + jnp.concatenate([a_c] * nv, axis=1) * acc_s[c]
        o_ref[0] = (acc / jnp.sum(l, axis=-1, keepdims=True)).astype(bf16)


def _paged_attention(page_table, ql, qr, cache_lat, cache_krt, new_lat, new_kr, ts):
    bs, rows, _ = ql.shape
    n_pages = page_table.shape[1]
    np_ = PAGES_PER_STEP
    new_pad = new_lat.shape[1]
    page_map = [functools.partial(lambda b, s, pt, r: (pt[b, s * np_ + r], 0, 0), r=r) for r in range(np_)]
    lat_specs = [pl.BlockSpec((1, PAGE, B_KV_RANK), page_map[r]) for r in range(np_)]
    kr_specs = [pl.BlockSpec((1, B_D_ROPE, PAGE), page_map[r]) for r in range(np_)]
    return pl.pallas_call(
        functools.partial(_paged_kernel, ts=ts, new_pad=new_pad),
        grid_spec=pltpu.PrefetchScalarGridSpec(
            num_scalar_prefetch=1,
            grid=(bs, n_pages // np_),
            in_specs=[pl.BlockSpec((1, rows, B_KV_RANK), lambda b, s, pt: (b, 0, 0)),
                      pl.BlockSpec((1, rows, B_D_ROPE), lambda b, s, pt: (b, 0, 0))]
            + lat_specs + kr_specs
            + [pl.BlockSpec((1, new_pad, B_KV_RANK), lambda b, s, pt: (b, 0, 0)),
               pl.BlockSpec((1, new_pad, B_D_ROPE), lambda b, s, pt: (b, 0, 0))],
            out_specs=pl.BlockSpec((1, rows, B_KV_RANK), lambda b, s, pt: (b, 0, 0)),
            scratch_shapes=[pltpu.VMEM((PAGED_CHAINS, rows, LANES), f32), pltpu.VMEM((PAGED_CHAINS, rows, LANES), f32),
                            pltpu.VMEM((PAGED_CHAINS, rows, B_KV_RANK), f32)],
        ),
        out_shape=jax.ShapeDtypeStruct((bs, rows, B_KV_RANK), bf16),
        compiler_params=_cparams(("parallel", "arbitrary")),
        name="mla_paged",
    )(page_table, ql, qr, *([cache_lat] * np_), *([cache_krt] * np_), new_lat, new_kr)


def _mla_out_kernel(ol_ref, wuv_ref, wo_ref, *refs):
    o = jnp.concatenate([_dot(ol_ref[0, hd], wuv_ref[hd]) for hd in range(HEADS)], axis=1)
    y = _dot(o.astype(bf16), wo_ref[...])
    _residual_ln_route(y, refs[:N_POST_IN], *refs[-3:])


def _mla_out(ol, wuv, wo, tm, **post):
    return _post_call(_mla_out_kernel, "mla_out", [ol, wuv, wo],
                      [pl.BlockSpec((1, HEADS, tm, B_KV_RANK), lambda gi, i: (gi, 0, i, 0)),
                       _full_spec(wuv.shape), _full_spec(wo.shape)], tm=tm, **post)


def _rope_tables(pos):
    half = B_D_ROPE // 2
    inv = jnp.power(ROPE_THETA, -jnp.arange(half, dtype=f32) / half)
    ang = pos.astype(f32)[:, None] * inv[None, :]
    cos, sin = jnp.cos(ang), jnp.sin(ang)
    return jnp.concatenate([cos, cos], axis=-1), jnp.concatenate([-sin, sin], axis=-1)


def _swap_halves(w, width):
    lead = w.shape[:-1]
    w2 = w.reshape(lead + (-1, 2, width // 2))
    return w2[..., ::-1, :].reshape(w.shape)


def kernel(x_prompt, x_sample, cache_latent, cache_krope, page_table, state_C, state_n, state_m, c_prompt, c_sample, ada_w, ada_b, ln_g, ln_b, a_w_in, a_b_gates, a_norm_g, a_w_out, b_w_kv_a, b_kv_norm_g, b_w_uk, b_w_uv, b_w_dq, b_q_norm_g, b_w_uq, b_w_o, router_w, router_b, e_w_gate, e_w_up, e_w_down):
    bp, tp, d = x_prompt.shape
    bs, ts, _ = x_sample.shape
    hqk = HEADS * A_DQK
    hv = HEADS * A_DV

    w_in = a_w_in.at[:, :, hqk:2 * hqk].multiply(A_DQK ** -0.5)
    w_in = jnp.pad(w_in, ((0, 0), (0, 0), (0, LANES - 2 * HEADS))).astype(bf16)
    bg = jnp.pad(a_b_gates, ((0, 0), (0, LANES - 2 * HEADS))).reshape(N_A_LAYERS, 1, LANES)
    w_out = a_w_out.astype(bf16)
    ng = a_norm_g.reshape(N_A_LAYERS, 1, hv)
    perm = jnp.array([(r % N_GROUPS) * PER_GROUP + r // N_GROUPS for r in range(N_EXPERTS)], jnp.int32)
    rw_perm = router_w[:, perm]
    rw_hi = rw_perm.astype(bf16)
    rw_lo = (rw_perm - rw_hi.astype(f32)).astype(bf16)
    rw = jnp.pad(jnp.concatenate([rw_hi, rw_lo], axis=1), ((0, 0), (0, LANES - 2 * N_EXPERTS)))
    rwh = jnp.pad(rw_hi, ((0, 0), (0, LANES - N_EXPERTS)))
    rb = router_b[perm].reshape(N_EXPERTS, 1)
    w_lat = b_w_kv_a[:, :B_KV_RANK]
    w_kr = b_w_kv_a[:, B_KV_RANK:]
    zpad = jnp.zeros((d, LANES - B_D_ROPE), f32)
    w_kv = jnp.concatenate([w_lat, w_kr, zpad, _swap_halves(w_kr, B_D_ROPE), zpad], axis=1).astype(bf16)
    kvg = b_kv_norm_g.reshape(1, B_KV_RANK)
    wdq = b_w_dq.astype(bf16)
    qg = b_q_norm_g.reshape(-1, 1, b_q_norm_g.shape[-1])
    uq = b_w_uq.reshape(b_w_uq.shape[0], b_w_uq.shape[1], HEADS, B_D_NOPE + B_D_ROPE)
    uq_nope = uq[..., :B_D_NOPE].reshape(uq.shape[0], uq.shape[1], HEADS * B_D_NOPE)
    uq_rope = uq[..., B_D_NOPE:].reshape(uq.shape[0], uq.shape[1], HEADS * B_D_ROPE)
    wuq = jnp.concatenate([uq_nope, uq_rope, _swap_halves(uq_rope, B_D_ROPE)], axis=-1).astype(bf16)
    wuk = jnp.transpose(b_w_uk, (1, 2, 0)).astype(bf16)
    wuv = jnp.transpose(b_w_uv, (1, 0, 2)).astype(bf16)
    wo = b_w_o.astype(bf16)

    mods = _ada_mods(jnp.concatenate([c_prompt, c_sample], axis=0), ada_w, ada_b)

    cos_p, sin_p = _rope_tables(jnp.arange(tp, dtype=jnp.int32))
    past_len = page_table.shape[1] * PAGE
    cos_s, sin_s = _rope_tables(jnp.tile(past_len + jnp.arange(ts, dtype=jnp.int32), bs))

    tm = 512
    cache_krt = jnp.swapaxes(cache_krope, 1, 2)
    streams = []
    for x0, mods_g, cos, sin, sample in ((x_prompt, mods[:, :bp], cos_p, sin_p, False),
                                         (x_sample.reshape(1, bs * ts, d), mods[:, bp:], cos_s, sin_s, True)):
        streams.append(dict(x=x0, mods=mods_g, sample=sample, cos1=cos[None], sin1=sin[None],
                            cos8=jnp.tile(cos, (1, HEADS))[None], sin8=jnp.tile(sin, (1, HEADS))[None],
                            new_c=[], new_n=[], new_m=[]))

    for l in range(DEPTH):
        lg = ln_g[l].reshape(2, 1, d)
        lb = ln_b[l].reshape(2, 1, d)
        moe_in = []
        h2_all, row0 = None, 0
        n_all = sum(st["x"].shape[0] * st["x"].shape[1] for st in streams)
        for st in streams:
            x, sample = st["x"], st["sample"]
            cos8, sin8 = st["cos8"], st["sin8"]
            latb, krb = st.get("latb"), st.get("krb")
            new_c, new_n, new_m = st["new_c"], st["new_n"], st["new_m"]
            m6 = st["mods"][l].reshape(st["mods"].shape[1], 6, d)
            if sample:
                parts = [jnp.repeat(m6[:, i], ts, axis=0)[None] for i in range(6)]
            else:
                parts = [m6[:, i][:, None, :] for i in range(6)]
            shift1, scale1, gate1, shift2, scale2, gate2 = parts
            post = dict(x=x, gate=gate1, ln_g=lg[0], ln_b=lb[0], scale2=scale2, shift2=shift2, rw=rw, rwh=rwh, rb=rb,
                        h2_rows=n_all, h2_row0=row0, h2_prev=h2_all)
            if l < N_A_LAYERS:
                qkv, o, gates = _mlstm_in(x, scale1, shift1, w_in[l], bg[l], tm)
                kt = jnp.swapaxes(qkv[:, :, hqk:2 * hqk], 1, 2)
                grow = jnp.swapaxes(gates[:, :, :2 * HEADS], 1, 2)
                if sample:
                    n0 = state_n[l].reshape(bs, hqk)
                    m0 = state_m[l]
                    nrows = jnp.repeat(n0, ts, axis=0)[None]
                    m_tok = jnp.repeat(m0, ts, axis=0)
                    mcol = jnp.pad(m_tok, ((0, 0), (0, LANES - HEADS)))[None]
                    mrow = m_tok.T[None]
                    u, c_stack, n_new, m_new = _mlstm_sample(qkv, kt, o, gates, grow, ng[l], state_C, l,
                                                             st.get("c_stack"), n0, nrows, m0, mcol, mrow, ts)
                    st["c_stack"] = c_stack
                    n_new = n_new.reshape(bs, HEADS, A_DQK)
                else:
                    u, c_new, n_rep, m_rep = _mlstm_prompt(qkv, kt, o, gates, grow, ng[l])
                    n_new = n_rep[..., 0]
                    m_new = m_rep[..., 0]
                    new_c.append(c_new)
                new_n.append(n_new)
                new_m.append(m_new)
                x1, h2_all, route = _mlstm_out(u, w_out[l], tm, **post)
            else:
                j = l - N_A_LAYERS
                ql, qr = _mla_q(x, scale1, shift1, wdq[j], qg[j], wuq[j], wuk, cos8, sin8, tm)
                if sample:
                    ql_s = jnp.transpose(ql.reshape(HEADS, bs, ts, B_KV_RANK), (1, 0, 2, 3)).reshape(
                        bs, HEADS * ts, B_KV_RANK)
                    qr_s = jnp.transpose(qr.reshape(HEADS, bs, ts, B_D_ROPE), (1, 0, 2, 3)).reshape(
                        bs, HEADS * ts, B_D_ROPE)
                    new_pad = 16
                    nl = jnp.pad(latb.reshape(bs, ts, B_KV_RANK), ((0, 0), (0, new_pad - ts), (0, 0)))
                    nk = jnp.pad(krb.reshape(bs, ts, B_D_ROPE), ((0, 0), (0, new_pad - ts), (0, 0)))
                    ol = _paged_attention(page_table, ql_s, qr_s, cache_latent, cache_krt, nl, nk, ts)
                    ol = jnp.transpose(ol.reshape(bs, HEADS, ts, B_KV_RANK), (1, 0, 2, 3)).reshape(
                        1, HEADS, bs * ts, B_KV_RANK)
                else:
                    ol = _flash(ql, qr, latb, krb, 512)
                x1, h2_all, route = _mla_out(ol, wuv, wo[j], tm, **post)
            moe_in.append((x1, route, gate2))
            row0 += x.shape[0] * x.shape[1]
        for st, x_new in zip(streams, _moe(moe_in, h2_all, lg[1], lb[1], e_w_gate, e_w_up, e_w_down, l, tm)):
            st["x"] = x_new
            if l == N_A_LAYERS - 1:
                st["lat"], st["kr"], st["latb"], st["krb"] = _shared_kv(x_new, w_kv, kvg, st["cos1"], st["sin1"], tm)

    sp, ss = streams
    return (sp["x"], ss["x"].reshape(bs, ts, d),
            jnp.stack(sp["new_c"]), jnp.stack(sp["new_n"]), jnp.stack(sp["new_m"]), sp["lat"], sp["kr"],
            ss["c_stack"], jnp.stack(ss["new_n"]), jnp.stack(ss["new_m"]),
            ss["lat"].reshape(bs, ts, B_KV_RANK), ss["kr"].reshape(bs, ts, B_D_ROPE))
```

```python
import functools

import jax
import jax.numpy as jnp
from jax import lax
from jax.experimental import pallas as pl
from jax.experimental.pallas import tpu as pltpu

f32 = jnp.float32
bf16 = jnp.bfloat16

DEPTH = 4
N_A_LAYERS = 2
HEADS = 8
A_DQK = 64
A_DV = 128
B_D_NOPE = 128
B_D_ROPE = 64
B_KV_RANK = 256
ROPE_THETA = 10000.0
ATTN_SCALE = (B_D_NOPE + B_D_ROPE) ** -0.5
N_EXPERTS = 16
N_GROUPS = 4
PER_GROUP = 4
PAGE = 128
ALPHA = (2 * DEPTH) ** 0.25
EPS = 1e-6
NEG_INF = float("-inf")
LOG2E = 1.4426950408889634
SOFTMAX_EXP2_SCALE = ATTN_SCALE * LOG2E

VMEM_LIMIT_BYTES = 56 * 1024 * 1024
LANES = 128
MLSTM_CHUNK = 128
SAMPLE_GROUP = 16
MOE_BLOCK = 256
MOE_CHUNKS = 1
PAGES_PER_STEP = 16
PAGED_CHAINS = 2
FLASH_HEADS = 4


def _cparams(sem):
    return pltpu.CompilerParams(dimension_semantics=sem, vmem_limit_bytes=VMEM_LIMIT_BYTES)


def _dot(a, b):
    return jnp.dot(a, b, preferred_element_type=f32)


def _dot_nt(a, b):
    return lax.dot_general(a, b, (((1,), (1,)), ((), ())), preferred_element_type=f32)


def _dot_exact(a, b):
    return jnp.dot(a, b, preferred_element_type=f32, precision=lax.Precision.HIGHEST)


def _sigmoid(x):
    return 1.0 / (1.0 + jnp.exp(-x))


def _log_sigmoid(x):
    return jnp.minimum(x, 0.0) - jnp.log(1.0 + jnp.exp(-jnp.abs(x)))


def _ada_kernel(c_ref, w_ref, b_ref, o_ref):
    c = c_ref[...]
    sc = (c * _sigmoid(c)).astype(bf16)
    o_ref[0] = _dot(sc, w_ref[0].astype(bf16)) + b_ref[0]


def _ada_mods(c_all, ada_w, ada_b):
    depth, d, e = ada_w.shape
    bc = c_all.shape[0]
    tn = 1536
    return pl.pallas_call(
        _ada_kernel,
        grid=(depth, e // tn),
        in_specs=[
            pl.BlockSpec((bc, d), lambda l, j: (0, 0)),
            pl.BlockSpec((1, d, tn), lambda l, j: (l, 0, j)),
            pl.BlockSpec((1, 1, tn), lambda l, j: (l, 0, j)),
        ],
        out_specs=pl.BlockSpec((1, bc, tn), lambda l, j: (l, 0, j)),
        out_shape=jax.ShapeDtypeStruct((depth, bc, e), f32),
        compiler_params=_cparams(("parallel", "parallel")),
        name="ada_mods",
    )(c_all, ada_w, ada_b.reshape(depth, 1, e))


def _mod_spec(mod, tm):
    d = mod.shape[-1]
    if mod.shape[1] == 1:
        return pl.BlockSpec((1, 1, d), lambda g, i: (g, 0, 0))
    return pl.BlockSpec((1, tm, d), lambda g, i: (g, i, 0))


def _tok_spec(tm, d):
    return pl.BlockSpec((1, tm, d), lambda g, i: (g, i, 0))


def _full_spec(shape):
    nd = len(shape)
    return pl.BlockSpec(shape, lambda g, i: (0,) * nd)


def _route(h2, rw_ref, rwh_ref, rb_ref):
    hi = h2.astype(bf16)
    lo = (h2 - hi.astype(f32)).astype(bf16)
    p = _dot(hi, rw_ref[...]) + _dot(lo, rwh_ref[...])
    pt = p.T
    logits = pt[0:N_EXPERTS] + pt[N_EXPERTS:2 * N_EXPERTS]
    sc = _sigmoid(logits)
    sel = sc + rb_ref[...]
    a = [sel[PER_GROUP * j:PER_GROUP * (j + 1)] for j in range(PER_GROUP)]
    s = [sc[PER_GROUP * j:PER_GROUP * (j + 1)] for j in range(PER_GROUP)]
    hi01, lo01 = jnp.maximum(a[0], a[1]), jnp.minimum(a[0], a[1])
    hi23, lo23 = jnp.maximum(a[2], a[3]), jnp.minimum(a[2], a[3])
    gs = jnp.maximum(hi01, hi23) + jnp.maximum(jnp.minimum(hi01, hi23), jnp.maximum(lo01, lo23))
    best = gs[0:1]
    grp = jnp.zeros_like(best)
    for g in range(1, N_GROUPS):
        better = gs[g:g + 1] > best
        grp = jnp.where(better, float(g), grp)
        best = jnp.where(better, gs[g:g + 1], best)
    mv, sv = [], []
    for j in range(PER_GROUP):
        m_j = a[j][0:1]
        s_j = s[j][0:1]
        for g in range(1, N_GROUPS):
            m_j = jnp.where(grp == float(g), a[j][g:g + 1], m_j)
            s_j = jnp.where(grp == float(g), s[j][g:g + 1], s_j)
        mv.append(m_j)
        sv.append(s_j)

    def first_argmax(vals):
        bv, bi = vals[0], jnp.zeros_like(vals[0])
        for j in range(1, PER_GROUP):
            better = vals[j] > bv
            bi = jnp.where(better, float(j), bi)
            bv = jnp.where(better, vals[j], bv)
        return bi

    i1 = first_argmax(mv)
    i2 = first_argmax([jnp.where(i1 == float(j), NEG_INF, mv[j]) for j in range(PER_GROUP)])
    w1, w2 = sv[0], sv[0]
    for j in range(1, PER_GROUP):
        w1 = jnp.where(i1 == float(j), sv[j], w1)
        w2 = jnp.where(i2 == float(j), sv[j], w2)
    tot = w1 + w2
    e1 = grp * float(PER_GROUP) + i1
    e2 = grp * float(PER_GROUP) + i2
    z = jnp.zeros_like(e1)
    return jnp.concatenate([e1, e2, w1 / tot, w2 / tot, z, z, z, z], axis=0)


N_POST_IN = 9


def _residual_ln_route(y, post_refs, x1_ref, h2_ref, route_ref):
    x_ref, gate_ref, lng_ref, lnb_ref, sc_ref, sh_ref, rw_ref, rwh_ref, rb_ref = post_refs
    r = ALPHA * x_ref[0] + gate_ref[0] * y
    mu = jnp.mean(r, axis=-1, keepdims=True)
    cen = r - mu
    var = jnp.mean(cen * cen, axis=-1, keepdims=True)
    x1 = cen * lax.rsqrt(var + EPS) * lng_ref[...] + lnb_ref[...]
    x1_ref[0] = x1
    h2 = x1 * (1.0 + sc_ref[0]) + sh_ref[0]
    h2_ref[...] = h2
    route_ref[0] = _route(h2, rw_ref, rwh_ref, rb_ref)


def _post_call(kernel_fn, name, mixer_args, mixer_specs, x, gate, ln_g, ln_b, scale2, shift2, rw, rwh, rb, tm,
               h2_rows, h2_row0, h2_prev):
    g, t, d = x.shape
    steps = t // tm
    blk0 = h2_row0 // tm
    in_specs = list(mixer_specs) + [
        _tok_spec(tm, d), _mod_spec(gate, tm), _full_spec((1, d)), _full_spec((1, d)),
        _mod_spec(scale2, tm), _mod_spec(shift2, tm),
        _full_spec((d, LANES)), _full_spec((d, LANES)), _full_spec((N_EXPERTS, 1)),
    ]
    args = list(mixer_args) + [x, gate, ln_g, ln_b, scale2, shift2, rw, rwh, rb]
    aliases = {}
    if h2_prev is not None:
        aliases = {len(args): 1}
        in_specs.append(pl.BlockSpec(memory_space=pl.ANY))
        args.append(h2_prev)
    return pl.pallas_call(
        kernel_fn,
        grid=(g, steps),
        in_specs=in_specs,
        out_specs=[_tok_spec(tm, d),
                   pl.BlockSpec((tm, d), lambda gi, i: (blk0 + gi * steps + i, 0)),
                   pl.BlockSpec((1, 8, tm), lambda gi, i: (gi, 0, i))],
        out_shape=[jax.ShapeDtypeStruct((g, t, d), f32),
                   jax.ShapeDtypeStruct((h2_rows, d), f32),
                   jax.ShapeDtypeStruct((g, 8, t), f32)],
        input_output_aliases=aliases,
        compiler_params=_cparams(("parallel", "parallel")),
        name=name,
    )(*args)


def _mlstm_in_kernel(x_ref, sc_ref, sh_ref, w_ref, bg_ref, qkv_ref, o_ref, g_ref):
    h = x_ref[0] * (1.0 + sc_ref[0]) + sh_ref[0]
    z = _dot(h.astype(bf16), w_ref[...])
    hqk2 = 2 * HEADS * A_DQK
    hv = HEADS * A_DV
    qkv_ref[0] = z[:, :hqk2 + hv].astype(bf16)
    o_ref[0] = z[:, hqk2 + hv:hqk2 + 2 * hv]
    g_ref[0] = z[:, hqk2 + 2 * hv:] + bg_ref[...]


def _mlstm_in(x, scale, shift, w_pad, bg_pad, tm):
    g, t, d = x.shape
    n_qkv = 2 * HEADS * A_DQK + HEADS * A_DV
    hv = HEADS * A_DV
    return pl.pallas_call(
        _mlstm_in_kernel,
        grid=(g, t // tm),
        in_specs=[_tok_spec(tm, d), _mod_spec(scale, tm), _mod_spec(shift, tm),
                  _full_spec(w_pad.shape), _full_spec((1, LANES))],
        out_specs=[_tok_spec(tm, n_qkv), _tok_spec(tm, hv), _tok_spec(tm, LANES)],
        out_shape=[jax.ShapeDtypeStruct((g, t, n_qkv), bf16),
                   jax.ShapeDtypeStruct((g, t, hv), f32),
                   jax.ShapeDtypeStruct((g, t, LANES), f32)],
        compiler_params=_cparams(("parallel", "parallel")),
        name="mlstm_in",
    )(x, scale, shift, w_pad, bg_pad)


def _mlstm_out_kernel(u_ref, w_ref, *refs):
    y = _dot(u_ref[0], w_ref[...])
    _residual_ln_route(y, refs[:N_POST_IN], *refs[-3:])


def _mlstm_out(u, w_out, tm, **post):
    return _post_call(_mlstm_out_kernel, "mlstm_out", [u, w_out],
                      [_tok_spec(tm, u.shape[-1]), _full_spec(w_out.shape)], tm=tm, **post)


def _head_norm_gate(hh, ng, o):
    mu = jnp.mean(hh, axis=-1, keepdims=True)
    cen = hh - mu
    var = jnp.mean(cen * cen, axis=-1, keepdims=True)
    return (cen * lax.rsqrt(var + EPS) * ng) * _sigmoid(o)


def _mlstm_prompt_kernel(qkv_ref, kt_ref, o_ref, gcol_ref, grow_ref, ng_ref,
                         u_ref, c_out, n_out, m_out, c_s, n_s, m_s):
    c = pl.program_id(1)
    nc = pl.num_programs(1)
    L = qkv_ref.shape[1]
    hqk = HEADS * A_DQK

    @pl.when(c == 0)
    def _():
        c_s[...] = jnp.zeros_like(c_s)
        n_s[...] = jnp.zeros_like(n_s)
        m_s[...] = jnp.zeros_like(m_s)

    r_i = lax.broadcasted_iota(jnp.int32, (L, L), 0)
    c_i = lax.broadcasted_iota(jnp.int32, (L, L), 1)
    causal = c_i <= r_i
    tri = causal.astype(f32)
    tri_t = (r_i <= c_i).astype(f32)

    gcol = gcol_ref[0]
    grow = grow_ref[0]
    b_col = _dot_exact(tri, _log_sigmoid(gcol))
    b_row = _dot_exact(_log_sigmoid(grow[HEADS:2 * HEADS]), tri_t)
    li_row = grow[0:HEADS]

    qkv = qkv_ref[0]
    kt = kt_ref[0]
    o_all = o_ref[0]
    ng = ng_ref[...]
    c_old, n_old, m_old = c_s[...], n_s[...], m_s[...]
    hs = range(HEADS)
    q = [qkv[:, A_DQK * h:A_DQK * (h + 1)] for h in hs]
    v = [qkv[:, 2 * hqk + A_DV * h:2 * hqk + A_DV * (h + 1)] for h in hs]
    kth = [kt[A_DQK * h:A_DQK * (h + 1), :] for h in hs]
    s_qk = [_dot(q[h], kth[h]) for h in hs]
    bc = [b_col[:, HEADS + h:HEADS + h + 1] for h in hs]
    br = [b_row[h:h + 1] for h in hs]
    lir = [li_row[h:h + 1] for h in hs]
    m_prev = [m_old[h:h + 1, 0:1] for h in hs]
    d = [jnp.where(causal, bc[h] - br[h] + lir[h], NEG_INF) for h in hs]
    a = [bc[h] + m_prev[h] for h in hs]
    d_max = [jnp.max(d[h], axis=-1, keepdims=True) for h in hs]
    m_t = [jnp.maximum(a[h], d_max[h]) for h in hs]
    p = [s_qk[h] * jnp.exp(d[h] - m_t[h]) for h in hs]
    inter = [jnp.exp(a[h] - m_t[h]) for h in hs]
    qc = [_dot(q[h], jnp.concatenate([c_old[h], n_old[h]], axis=1).astype(bf16)) for h in hs]
    p_sum = [jnp.sum(p[h], axis=-1, keepdims=True) for h in hs]
    num = [_dot(p[h].astype(bf16), v[h]) + inter[h] * qc[h][:, :A_DV] for h in hs]
    den = [p_sum[h] + inter[h] * qc[h][:, A_DV:A_DV + 1] for h in hs]
    hh = [num[h] / jnp.maximum(jnp.abs(den[h]), jnp.exp(-m_t[h])) for h in hs]
    mu = [jnp.mean(hh[h], axis=-1, keepdims=True) for h in hs]
    cen = [hh[h] - mu[h] for h in hs]
    var = [jnp.mean(cen[h] * cen[h], axis=-1, keepdims=True) for h in hs]
    u_parts = [((cen[h] * lax.rsqrt(var[h] + EPS) * ng[:, A_DV * h:A_DV * (h + 1)])
                * _sigmoid(o_all[:, A_DV * h:A_DV * (h + 1)])).astype(bf16) for h in hs]

    b_end = [br[h][:, L - 1:L] for h in hs]
    g_row = [b_end[h] - br[h] + lir[h] for h in hs]
    g_max = [jnp.max(g_row[h], axis=-1, keepdims=True) for h in hs]
    m_new = [jnp.maximum(b_end[h] + m_prev[h], g_max[h]) for h in hs]
    decay = [jnp.exp(b_end[h] + m_prev[h] - m_new[h]) for h in hs]
    kw = [kth[h].astype(f32) * jnp.exp(g_row[h] - m_new[h]) for h in hs]
    kw_sum = [jnp.sum(kw[h], axis=-1, keepdims=True) for h in hs]
    c_parts = [decay[h] * c_old[h] + _dot(kw[h].astype(bf16), v[h]) for h in hs]
    n_parts = [decay[h] * n_old[h] + kw_sum[h] for h in hs]
    m_parts = [jnp.broadcast_to(m_new[h], (1, LANES)) for h in hs]

    u_ref[0] = jnp.concatenate(u_parts, axis=1)
    c_s[...] = jnp.stack(c_parts)
    n_s[...] = jnp.stack(n_parts)
    m_s[...] = jnp.concatenate(m_parts, axis=0)

    @pl.when(c == nc - 1)
    def _():
        c_out[0] = c_s[...]
        n_out[0] = n_s[...]
        m_out[0] = m_s[...]


def _mlstm_prompt(qkv, kt, o, gcol, grow, ng):
    b, t, _ = qkv.shape
    L = MLSTM_CHUNK
    hv = HEADS * A_DV
    return pl.pallas_call(
        _mlstm_prompt_kernel,
        grid=(b, t // L),
        in_specs=[
            pl.BlockSpec((1, L, qkv.shape[-1]), lambda g, c: (g, c, 0)),
            pl.BlockSpec((1, HEADS * A_DQK, L), lambda g, c: (g, 0, c)),
            pl.BlockSpec((1, L, hv), lambda g, c: (g, c, 0)),
            pl.BlockSpec((1, L, LANES), lambda g, c: (g, c, 0)),
            pl.BlockSpec((1, 2 * HEADS, L), lambda g, c: (g, 0, c)),
            pl.BlockSpec((1, hv), lambda g, c: (0, 0)),
        ],
        out_specs=[
            pl.BlockSpec((1, L, hv), lambda g, c: (g, c, 0)),
            pl.BlockSpec((1, HEADS, A_DQK, A_DV), lambda g, c: (g, 0, 0, 0)),
            pl.BlockSpec((1, HEADS, A_DQK, LANES), lambda g, c: (g, 0, 0, 0)),
            pl.BlockSpec((1, HEADS, LANES), lambda g, c: (g, 0, 0)),
        ],
        out_shape=[
            jax.ShapeDtypeStruct((b, t, hv), bf16),
            jax.ShapeDtypeStruct((b, HEADS, A_DQK, A_DV), f32),
            jax.ShapeDtypeStruct((b, HEADS, A_DQK, LANES), f32),
            jax.ShapeDtypeStruct((b, HEADS, LANES), f32),
        ],
        scratch_shapes=[
            pltpu.VMEM((HEADS, A_DQK, A_DV), f32),
            pltpu.VMEM((HEADS, A_DQK, LANES), f32),
            pltpu.VMEM((HEADS, LANES), f32),
        ],
        compiler_params=_cparams(("parallel", "arbitrary")),
        name="mlstm_prompt",
    )(qkv, kt, o, gcol, grow, ng)


def _mlstm_sample_kernel(qkv_ref, kt_ref, o_ref, gcol_ref, grow_ref, ng_ref, c0_ref, n0_ref, nrows_ref,
                         m0_ref, mcol_ref, mrow_ref, *refs, ts):
    u_ref, c_out, n_out, m_out = refs[-4:]
    R = qkv_ref.shape[1]
    G = R // ts
    shift = ts.bit_length() - 1
    hqk = HEADS * A_DQK

    r_i = lax.broadcasted_iota(jnp.int32, (R, R), 0)
    c_i = lax.broadcasted_iota(jnp.int32, (R, R), 1)
    same = (r_i >> shift) == (c_i >> shift)
    valid = same & (c_i <= r_i)
    tri = valid.astype(f32)
    tri_t = (same & (r_i <= c_i)).astype(f32)
    same_f = same.astype(f32)
    row_seq = lax.broadcasted_iota(jnp.int32, (R, 1), 0) >> shift
    g_i = lax.broadcasted_iota(jnp.int32, (G, R), 0)
    s_i = lax.broadcasted_iota(jnp.int32, (G, R), 1)
    bmask = (s_i >> shift) == g_i
    lastmask = s_i == g_i * ts + (ts - 1)
    g3 = lax.broadcasted_iota(jnp.int32, (G, A_DQK, R), 0)
    s3 = lax.broadcasted_iota(jnp.int32, (G, A_DQK, R), 2)
    bmask3 = (s3 >> shift) == g3
    lane_i = lax.broadcasted_iota(jnp.int32, (G, LANES), 1)

    gcol = gcol_ref[0]
    grow = grow_ref[0]
    lf_row = _log_sigmoid(grow[HEADS:2 * HEADS])
    b_col = _dot_exact(tri, _log_sigmoid(gcol))
    b_row = _dot_exact(lf_row, tri_t)
    bend_row = _dot_exact(lf_row, same_f)
    li_row = grow[0:HEADS]
    mcol = mcol_ref[0]
    mrow = mrow_ref[0]
    m0 = m0_ref[...]

    m_acc = jnp.zeros((G, LANES), f32)
    n_parts = []
    for h in range(HEADS):
        q_h = qkv_ref[0, :, A_DQK * h:A_DQK * (h + 1)]
        k_h = qkv_ref[0, :, hqk + A_DQK * h:hqk + A_DQK * (h + 1)]
        v_h = qkv_ref[0, :, 2 * hqk + A_DV * h:2 * hqk + A_DV * (h + 1)]
        kt_h = kt_ref[0, A_DQK * h:A_DQK * (h + 1), :]
        s_qk = _dot(q_h, kt_h)
        bc = b_col[:, HEADS + h:HEADS + h + 1]
        br = b_row[h:h + 1]
        lir = li_row[h:h + 1]
        d = jnp.where(valid, bc - br + lir, NEG_INF)
        a = bc + mcol[:, h:h + 1]
        m_t = jnp.maximum(a, jnp.max(d, axis=-1, keepdims=True))
        p = s_qk * jnp.exp(d - m_t)
        inter = jnp.exp(a - m_t)
        c_all = jnp.concatenate([c0_ref[0, g, h] for g in range(G)], axis=1).astype(bf16)
        qc = _dot(q_h, c_all)
        inter_c = jnp.zeros((R, A_DV), f32)
        for g in range(G):
            inter_c = jnp.where(row_seq == g, qc[:, A_DV * g:A_DV * (g + 1)], inter_c)
        qn = jnp.sum(q_h.astype(f32) * nrows_ref[0, :, A_DQK * h:A_DQK * (h + 1)], axis=-1, keepdims=True)
        num = _dot(p.astype(bf16), v_h) + inter * inter_c
        den = jnp.sum(p, axis=-1, keepdims=True) + inter * qn
        hh = num / jnp.maximum(jnp.abs(den), jnp.exp(-m_t))
        u = _head_norm_gate(hh, ng_ref[:, A_DV * h:A_DV * (h + 1)], o_ref[0, :, A_DV * h:A_DV * (h + 1)])
        u_ref[0, :, A_DV * h:A_DV * (h + 1)] = u.astype(bf16)

        g_row = bend_row[h:h + 1] - br + lir
        gmax = jnp.max(jnp.where(bmask, g_row, NEG_INF), axis=-1, keepdims=True)
        bend_b = jnp.sum(jnp.where(lastmask, br, 0.0), axis=-1, keepdims=True)
        m0_h = m0[:, h:h + 1]
        mnew_b = jnp.maximum(bend_b + m0_h, gmax)
        mnew_row = jnp.sum(jnp.where(bmask, mnew_b, 0.0), axis=0, keepdims=True)
        w_row = jnp.exp(g_row - mnew_row)
        decay_b = jnp.exp(bend_b + m0_h - mnew_b)
        kw = kt_h.astype(f32) * w_row
        kw3 = jnp.where(bmask3, jnp.broadcast_to(kw[None], (G, A_DQK, R)), 0.0)
        upd = _dot(kw3.reshape(G * A_DQK, R).astype(bf16), v_h)
        for g in range(G):
            c_out[0, g, h] = decay_b[g:g + 1, :] * c0_ref[0, g, h] + upd[A_DQK * g:A_DQK * (g + 1)]
        wm = jnp.where(bmask, w_row, 0.0)
        n_parts.append(decay_b * n0_ref[:, A_DQK * h:A_DQK * (h + 1)] + _dot(wm.astype(bf16), k_h))
        m_acc = jnp.where(lane_i == h, mnew_b, m_acc)

    n_out[...] = jnp.concatenate(n_parts, axis=1)
    m_out[...] = m_acc[:, :HEADS]


def _mlstm_sample(qkv, kt, o, gcol, grow, ng, c0_all, layer, c_prev, n0, nrows, m0, mcol, mrow, ts):
    _, t, _ = qkv.shape
    n_layers, bs = c0_all.shape[:2]
    G = SAMPLE_GROUP
    R = G * ts
    hv = HEADS * A_DV
    hqk = HEADS * A_DQK
    args = [qkv, kt, o, gcol, grow, ng, c0_all, n0, nrows, m0, mcol, mrow]
    extra_specs, aliases = [], {}
    if c_prev is not None:
        aliases = {len(args): 1}
        extra_specs = [pl.BlockSpec(memory_space=pl.ANY)]
        args.append(c_prev)
    return pl.pallas_call(
        functools.partial(_mlstm_sample_kernel, ts=ts),
        grid=(t // R,),
        input_output_aliases=aliases,
        in_specs=[
            pl.BlockSpec((1, R, qkv.shape[-1]), lambda i: (0, i, 0)),
            pl.BlockSpec((1, hqk, R), lambda i: (0, 0, i)),
            pl.BlockSpec((1, R, hv), lambda i: (0, i, 0)),
            pl.BlockSpec((1, R, LANES), lambda i: (0, i, 0)),
            pl.BlockSpec((1, 2 * HEADS, R), lambda i: (0, 0, i)),
            pl.BlockSpec((1, hv), lambda i: (0, 0)),
            pl.BlockSpec((1, G, HEADS, A_DQK, A_DV), lambda i: (layer, i, 0, 0, 0)),
            pl.BlockSpec((G, hqk), lambda i: (i, 0)),
            pl.BlockSpec((1, R, hqk), lambda i: (0, i, 0)),
            pl.BlockSpec((G, HEADS), lambda i: (i, 0)),
            pl.BlockSpec((1, R, LANES), lambda i: (0, i, 0)),
            pl.BlockSpec((1, HEADS, R), lambda i: (0, 0, i)),
        ] + extra_specs,
        out_specs=[
            pl.BlockSpec((1, R, hv), lambda i: (0, i, 0)),
            pl.BlockSpec((1, G, HEADS, A_DQK, A_DV), lambda i: (layer, i, 0, 0, 0)),
            pl.BlockSpec((G, hqk), lambda i: (i, 0)),
            pl.BlockSpec((G, HEADS), lambda i: (i, 0)),
        ],
        out_shape=[
            jax.ShapeDtypeStruct((1, t, hv), bf16),
            jax.ShapeDtypeStruct((n_layers, bs, HEADS, A_DQK, A_DV), f32),
            jax.ShapeDtypeStruct((bs, hqk), f32),
            jax.ShapeDtypeStruct((bs, HEADS), f32),
        ],
        compiler_params=_cparams(("parallel",)),
        name="mlstm_sample",
    )(*args)


def _expert_kernel(be_ref, nu_ref, first_ref, x_ref, wg_ref, wu_ref, wd_ref, *refs, blk0):
    y_ref, wg_s, wu_s, wd_s = refs[-4:]
    i = blk0 + pl.program_id(0)

    @pl.when((first_ref[i] == 1) | (pl.program_id(0) == 0))
    def _():
        wg_s[...] = wg_ref[0, 0].astype(bf16)
        wu_s[...] = wu_ref[0, 0].astype(bf16)
        wd_s[...] = wd_ref[0, 0].astype(bf16)

    @pl.when(i < nu_ref[0])
    def _():
        x = x_ref[...].astype(bf16)
        g = _dot(x, wg_s[...])
        u = _dot(x, wu_s[...])
        hid = (g * _sigmoid(g)) * u
        y_ref[...] = _dot(hid.astype(bf16), wd_s[...])

    @pl.when(i >= nu_ref[0])
    def _():
        y_ref[...] = jnp.zeros_like(y_ref)


def _experts(x_chunk, blk0, p_total, y_prev, block_e, n_used, first, wg, wu, wd, layer):
    pc, d = x_chunk.shape
    de = wg.shape[-1]
    bm = MOE_BLOCK
    w_map = lambda i, be, nu, fi: (layer, be[blk0 + i], 0, 0)
    args = [block_e, n_used, first, x_chunk, wg, wu, wd]
    in_specs = [
        pl.BlockSpec((bm, d), lambda i, be, nu, fi: (i, 0)),
        pl.BlockSpec((1, 1, d, de), w_map),
        pl.BlockSpec((1, 1, d, de), w_map),
        pl.BlockSpec((1, 1, de, d), w_map),
    ]
    aliases = {}
    if y_prev is not None:
        aliases = {len(args): 0}
        in_specs.append(pl.BlockSpec(memory_space=pl.ANY))
        args.append(y_prev)
    return pl.pallas_call(
        functools.partial(_expert_kernel, blk0=blk0),
        grid_spec=pltpu.PrefetchScalarGridSpec(
            num_scalar_prefetch=3,
            grid=(pc // bm,),
            in_specs=in_specs,
            out_specs=pl.BlockSpec((bm, d), lambda i, be, nu, fi: (blk0 + i, 0)),
            scratch_shapes=[pltpu.VMEM((d, de), bf16), pltpu.VMEM((d, de), bf16), pltpu.VMEM((de, d), bf16)],
        ),
        out_shape=jax.ShapeDtypeStruct((p_total, d), f32),
        input_output_aliases=aliases,
        compiler_params=_cparams(("arbitrary",)),
        name="moe_experts",
    )(*args)


def _combine_kernel(x_ref, ya_ref, yb_ref, w_ref, gate_ref, lng_ref, lnb_ref, o_ref):
    w = w_ref[0]
    y = w[:, 0:1] * ya_ref[0] + w[:, 1:2] * yb_ref[0]
    r = ALPHA * x_ref[0] + gate_ref[0] * y
    mu = jnp.mean(r, axis=-1, keepdims=True)
    cen = r - mu
    var = jnp.mean(cen * cen, axis=-1, keepdims=True)
    o_ref[0] = cen * lax.rsqrt(var + EPS) * lng_ref[...] + lnb_ref[...]


def _combine(x1, y2, row0, wexp, gate, ln_g, ln_b, tm):
    g, t, d = x1.shape
    steps = t // tm
    blk0 = row0 // tm
    return pl.pallas_call(
        _combine_kernel,
        grid=(g, steps),
        in_specs=[_tok_spec(tm, d),
                  pl.BlockSpec((1, tm, d), lambda gi, i: (0, blk0 + gi * steps + i, 0)),
                  pl.BlockSpec((1, tm, d), lambda gi, i: (1, blk0 + gi * steps + i, 0)),
                  _tok_spec(tm, LANES), _mod_spec(gate, tm), _full_spec((1, d)), _full_spec((1, d))],
        out_specs=_tok_spec(tm, d),
        out_shape=jax.ShapeDtypeStruct((g, t, d), f32),
        compiler_params=_cparams(("parallel", "parallel")),
        name="moe_combine",
    )(x1, y2, y2, wexp, gate, ln_g, ln_b)


def _moe(groups, h2_all, ln_g, ln_b, wg, wu, wd, layer, tm):
    bm = MOE_BLOCK
    d = groups[0][0].shape[-1]
    sizes = [x1.shape[0] * x1.shape[1] for x1, _, _ in groups]
    n = sum(sizes)
    nk = 2 * n
    expert = jnp.concatenate([jnp.transpose(r[:, 0:2, :], (0, 2, 1)).reshape(-1, 2) for _, r, _ in groups],
                             axis=0).astype(jnp.int32)
    e_flat = expert.reshape(nk)
    onehot = (e_flat[:, None] == jnp.arange(N_EXPERTS, dtype=jnp.int32)[None, :]).astype(jnp.int32)
    csum = jnp.cumsum(onehot, axis=0)
    counts = csum[-1]
    rank = jnp.sum(csum * onehot, axis=1) - 1
    padded = (counts + bm - 1) // bm * bm
    pends = jnp.cumsum(padded)
    pstarts = pends - padded
    dest = jnp.sum(onehot * pstarts[None, :], axis=1) + rank
    n_blocks = nk // bm + N_EXPERTS
    n_used = (pends[-1] // bm).astype(jnp.int32)
    blk = jnp.minimum(jnp.arange(n_blocks, dtype=jnp.int32), n_used - 1)
    block_e = jnp.minimum(jnp.sum((pends[None, :] <= (blk * bm)[:, None]).astype(jnp.int32), axis=1),
                          N_EXPERTS - 1).astype(jnp.int32)
    first = jnp.concatenate([jnp.ones((1,), jnp.int32), (block_e[1:] != block_e[:-1]).astype(jnp.int32)])
    tok = jnp.arange(nk, dtype=jnp.int32) // 2
    tok_pad = jnp.zeros((n_blocks * bm,), jnp.int32).at[dest].set(tok, unique_indices=True, mode="promise_in_bounds")
    chunk_blocks = n_blocks // MOE_CHUNKS
    y = None
    for c in range(MOE_CHUNKS):
        blk0 = c * chunk_blocks
        x_c = jnp.take(h2_all, tok_pad[blk0 * bm:(blk0 + chunk_blocks) * bm], axis=0, mode="clip")
        y = _experts(x_c, blk0, n_blocks * bm, y, block_e, n_used.reshape(1), first, wg, wu, wd, layer)
    y2 = jnp.take(y, dest.reshape(n, 2).T.reshape(nk), axis=0, mode="clip").reshape(2, n, d)
    outs, off = [], 0
    for (x1, route, gate2), sz in zip(groups, sizes):
        wexp = jnp.pad(jnp.transpose(route[:, 2:4, :], (0, 2, 1)), ((0, 0), (0, 0), (0, LANES - 2)))
        outs.append(_combine(x1, y2, off, wexp, gate2, ln_g, ln_b, tm))
        off += sz
    return outs


def _kv_kernel(x_ref, w_ref, g_ref, cos_ref, sin_ref, lat_ref, kr_ref, latb_ref, krb_ref):
    kva = _dot(x_ref[0].astype(bf16), w_ref[...])
    latp = kva[:, :B_KV_RANK]
    lat = latp * lax.rsqrt(jnp.mean(latp * latp, axis=-1, keepdims=True) + EPS) * g_ref[...]
    kr = kva[:, B_KV_RANK:B_KV_RANK + B_D_ROPE] * cos_ref[0] \
        + kva[:, B_KV_RANK + LANES:B_KV_RANK + LANES + B_D_ROPE] * sin_ref[0]
    lat_ref[0] = lat
    kr_ref[0] = kr
    latb_ref[0] = lat.astype(bf16)
    krb_ref[0] = kr.astype(bf16)


def _shared_kv(x, w_pad, g, cos, sin, tm):
    gg, t, d = x.shape
    rope_spec = pl.BlockSpec((1, tm, B_D_ROPE), lambda gi, i: (0, i, 0)) if cos.shape[0] == 1 else \
        _tok_spec(tm, B_D_ROPE)
    return pl.pallas_call(
        _kv_kernel,
        grid=(gg, t // tm),
        in_specs=[_tok_spec(tm, d), _full_spec(w_pad.shape), _full_spec((1, B_KV_RANK)), rope_spec, rope_spec],
        out_specs=[_tok_spec(tm, B_KV_RANK), _tok_spec(tm, B_D_ROPE),
                   _tok_spec(tm, B_KV_RANK), _tok_spec(tm, B_D_ROPE)],
        out_shape=[jax.ShapeDtypeStruct((gg, t, B_KV_RANK), f32), jax.ShapeDtypeStruct((gg, t, B_D_ROPE), f32),
                   jax.ShapeDtypeStruct((gg, t, B_KV_RANK), bf16), jax.ShapeDtypeStruct((gg, t, B_D_ROPE), bf16)],
        compiler_params=_cparams(("parallel", "parallel")),
        name="shared_kv",
    )(x, w_pad, g, cos, sin)


def _mla_q_kernel(x_ref, sc_ref, sh_ref, wdq_ref, qg_ref, wuq_ref, wuk_ref, cos_ref, sin_ref, ql_ref, qr_ref):
    h = x_ref[0] * (1.0 + sc_ref[0]) + sh_ref[0]
    cq = _dot(h.astype(bf16), wdq_ref[...])
    cq = cq * lax.rsqrt(jnp.mean(cq * cq, axis=-1, keepdims=True) + EPS) * qg_ref[...]
    q = _dot(cq.astype(bf16), wuq_ref[...])
    n_nope = HEADS * B_D_NOPE
    n_rope = HEADS * B_D_ROPE
    rot = q[:, n_nope:n_nope + n_rope] * cos_ref[0] + q[:, n_nope + n_rope:] * sin_ref[0]
    for hd in range(HEADS):
        qn = q[:, B_D_NOPE * hd:B_D_NOPE * (hd + 1)].astype(bf16)
        ql_ref[0, hd] = _dot(qn, wuk_ref[hd]).astype(bf16)
        qr_ref[0, hd] = rot[:, B_D_ROPE * hd:B_D_ROPE * (hd + 1)].astype(bf16)


def _mla_q(x, scale, shift, wdq, qg, wuq, wuk, cos8, sin8, tm):
    g, t, d = x.shape
    n_rope = HEADS * B_D_ROPE
    rope_spec = pl.BlockSpec((1, tm, n_rope), lambda gi, i: (0, i, 0)) if cos8.shape[0] == 1 else \
        _tok_spec(tm, n_rope)
    return pl.pallas_call(
        _mla_q_kernel,
        grid=(g, t // tm),
        in_specs=[_tok_spec(tm, d), _mod_spec(scale, tm), _mod_spec(shift, tm),
                  _full_spec(wdq.shape), _full_spec(qg.shape), _full_spec(wuq.shape), _full_spec(wuk.shape),
                  rope_spec, rope_spec],
        out_specs=[pl.BlockSpec((1, HEADS, tm, B_KV_RANK), lambda gi, i: (gi, 0, i, 0)),
                   pl.BlockSpec((1, HEADS, tm, B_D_ROPE), lambda gi, i: (gi, 0, i, 0))],
        out_shape=[jax.ShapeDtypeStruct((g, HEADS, t, B_KV_RANK), bf16),
                   jax.ShapeDtypeStruct((g, HEADS, t, B_D_ROPE), bf16)],
        compiler_params=_cparams(("parallel", "parallel")),
        name="mla_q",
    )(x, scale, shift, wdq, qg, wuq, wuk, cos8, sin8)


def _softmax_step(s, v, m_s, l_s, acc_s, slot):
    rows, n = s.shape
    m_old = m_s[slot]
    if n % LANES == 0:
        chunks = [s[:, LANES * c:LANES * (c + 1)] for c in range(n // LANES)]
        m_new = jnp.maximum(m_old, jnp.max(functools.reduce(jnp.maximum, chunks), axis=-1, keepdims=True))
        ps = [jnp.exp2((ch - m_new) * SOFTMAX_EXP2_SCALE) for ch in chunks]
        psum = functools.reduce(jnp.add, ps)
        p = jnp.concatenate(ps, axis=1)
    else:
        m_new = jnp.maximum(m_old, jnp.max(s, axis=-1, keepdims=True))
        p = jnp.exp2((s - m_new[:, 0:1]) * SOFTMAX_EXP2_SCALE)
        lane = lax.broadcasted_iota(jnp.int32, (rows, LANES), 1)
        psum = jnp.where(lane == 0, jnp.sum(p, axis=-1, keepdims=True), 0.0)
    alpha = jnp.exp2((m_old - m_new) * SOFTMAX_EXP2_SCALE)
    l_s[slot] = alpha * l_s[slot] + psum
    alpha_v = jnp.concatenate([alpha] * (v.shape[1] // LANES), axis=1)
    acc_s[slot] = alpha_v * acc_s[slot] + _dot(p.astype(bf16), v)
    m_s[slot] = m_new


def _flash_kernel(ql_ref, qr_ref, lat_ref, kr_ref, o_ref, m_s, l_s, acc_s, *, tq):
    i = pl.program_id(1)
    nh = FLASH_HEADS
    r_i = lax.broadcasted_iota(jnp.int32, (tq, tq), 0)
    c_i = lax.broadcasted_iota(jnp.int32, (tq, tq), 1)
    causal = c_i <= r_i

    def group_body(hg, carry):
        m_s[...] = jnp.full_like(m_s, NEG_INF)
        l_s[...] = jnp.zeros_like(l_s)
        acc_s[...] = jnp.zeros_like(acc_s)

        def block(j, masked):
            start = pl.multiple_of(j * tq, tq)
            k_lat = lat_ref[0, pl.ds(start, tq), :]
            k_r = kr_ref[0, pl.ds(start, tq), :]
            for hs in range(nh):
                h = hg * nh + hs
                s = _dot_nt(ql_ref[0, h], k_lat) + _dot_nt(qr_ref[0, h], k_r)
                if masked:
                    s = jnp.where(causal, s, NEG_INF)
                _softmax_step(s, k_lat, m_s, l_s, acc_s, hs)

        def kv_body(j, c2):
            block(j, False)
            return c2

        lax.fori_loop(0, i, kv_body, 0)
        block(i, True)
        for hs in range(nh):
            l = jnp.sum(l_s[hs], axis=-1, keepdims=True)
            o_ref[0, hg * nh + hs] = (acc_s[hs] / l).astype(bf16)
        return carry

    lax.fori_loop(0, HEADS // nh, group_body, 0)


def _flash(ql, qr, latb, krb, tq):
    b, _, t, _ = ql.shape
    nh = FLASH_HEADS
    return pl.pallas_call(
        functools.partial(_flash_kernel, tq=tq),
        grid=(b, t // tq),
        in_specs=[
            pl.BlockSpec((1, HEADS, tq, B_KV_RANK), lambda g, i: (g, 0, i, 0)),
            pl.BlockSpec((1, HEADS, tq, B_D_ROPE), lambda g, i: (g, 0, i, 0)),
            pl.BlockSpec((1, t, B_KV_RANK), lambda g, i: (g, 0, 0)),
            pl.BlockSpec((1, t, B_D_ROPE), lambda g, i: (g, 0, 0)),
        ],
        out_specs=pl.BlockSpec((1, HEADS, tq, B_KV_RANK), lambda g, i: (g, 0, i, 0)),
        out_shape=jax.ShapeDtypeStruct((b, HEADS, t, B_KV_RANK), bf16),
        scratch_shapes=[pltpu.VMEM((nh, tq, LANES), f32), pltpu.VMEM((nh, tq, LANES), f32),
                        pltpu.VMEM((nh, tq, B_KV_RANK), f32)],
        compiler_params=_cparams(("parallel", "arbitrary")),
        name="mla_flash",
    )(ql, qr, latb, krb)


def _paged_kernel(pt_ref, ql_ref, qr_ref, *rest, ts, new_pad):
    np_ = PAGES_PER_STEP
    nc = PAGED_CHAINS
    per = np_ // nc
    lat_pages = rest[:np_]
    krt_pages = rest[np_:2 * np_]
    nl_ref, nk_ref, o_ref, m_s, l_s, acc_s = rest[2 * np_:]
    s_id = pl.program_id(1)
    ql = ql_ref[0]
    qr = qr_ref[0]

    @pl.when(s_id == 0)
    def _():
        m_s[...] = jnp.full_like(m_s, NEG_INF)
        l_s[...] = jnp.zeros_like(l_s)
        acc_s[...] = jnp.zeros_like(acc_s)

    for c in range(nc):
        k_lat = jnp.concatenate([p[0] for p in lat_pages[c * per:(c + 1) * per]], axis=0).astype(bf16)
        k_rt = jnp.concatenate([p[0] for p in krt_pages[c * per:(c + 1) * per]], axis=1).astype(bf16)
        _softmax_step(_dot_nt(ql, k_lat) + _dot(qr, k_rt), k_lat, m_s, l_s, acc_s, c)

    @pl.when(s_id == pl.num_programs(1) - 1)
    def _():
        rows = ql.shape[0]
        n_lat = nl_ref[0]
        s_new = _dot_nt(ql, n_lat) + _dot_nt(qr, nk_ref[0])
        t_row = lax.broadcasted_iota(jnp.int32, (rows, new_pad), 0) & (ts - 1)
        c_new = lax.broadcasted_iota(jnp.int32, (rows, new_pad), 1)
        _softmax_step(jnp.where(c_new <= t_row, s_new, NEG_INF), n_lat, m_s, l_s, acc_s, 0)
        m = functools.reduce(jnp.maximum, [m_s[c] for c in range(nc)])
        nv = B_KV_RANK // LANES
        l = jnp.zeros_like(m)
        acc = jnp.zeros_like(acc_s[0])
        for c in range(nc):
            a_c = jnp.exp2((m_s[c] - m) * SOFTMAX_EXP2_SCALE)
            l = l + a_c * l_s[c]
            acc = acc + jnp.concatenate([a_c] * nv, axis=1) * acc_s[c]
        o_ref[0] = (acc / jnp.sum(l, axis=-1, keepdims=True)).astype(bf16)


def _paged_attention(page_table, ql, qr, cache_lat, cache_krt, new_lat, new_kr, ts):
    bs, rows, _ = ql.shape
    n_pages = page_table.shape[1]
    np_ = PAGES_PER_STEP
    new_pad = new_lat.shape[1]
    page_map = [functools.partial(lambda b, s, pt, r: (pt[b, s * np_ + r], 0, 0), r=r) for r in range(np_)]
    lat_specs = [pl.BlockSpec((1, PAGE, B_KV_RANK), page_map[r]) for r in range(np_)]
    kr_specs = [pl.BlockSpec((1, B_D_ROPE, PAGE), page_map[r]) for r in range(np_)]
    return pl.pallas_call(
        functools.partial(_paged_kernel, ts=ts, new_pad=new_pad),
        grid_spec=pltpu.PrefetchScalarGridSpec(
            num_scalar_prefetch=1,
            grid=(bs, n_pages // np_),
            in_specs=[pl.BlockSpec((1, rows, B_KV_RANK), lambda b, s, pt: (b, 0, 0)),
                      pl.BlockSpec((1, rows, B_D_ROPE), lambda b, s, pt: (b, 0, 0))]
            + lat_specs + kr_specs
            + [pl.BlockSpec((1, new_pad, B_KV_RANK), lambda b, s, pt: (b, 0, 0)),
               pl.BlockSpec((1, new_pad, B_D_ROPE), lambda b, s, pt: (b, 0, 0))],
            out_specs=pl.BlockSpec((1, rows, B_KV_RANK), lambda b, s, pt: (b, 0, 0)),
            scratch_shapes=[pltpu.VMEM((PAGED_CHAINS, rows, LANES), f32), pltpu.VMEM((PAGED_CHAINS, rows, LANES), f32),
                            pltpu.VMEM((PAGED_CHAINS, rows, B_KV_RANK), f32)],
        ),
        out_shape=jax.ShapeDtypeStruct((bs, rows, B_KV_RANK), bf16),
        compiler_params=_cparams(("parallel", "arbitrary")),
        name="mla_paged",
    )(page_table, ql, qr, *([cache_lat] * np_), *([cache_krt] * np_), new_lat, new_kr)


def _mla_out_kernel(ol_ref, wuv_ref, wo_ref, *refs):
    o = jnp.concatenate([_dot(ol_ref[0, hd], wuv_ref[hd]) for hd in range(HEADS)], axis=1)
    y = _dot(o.astype(bf16), wo_ref[...])
    _residual_ln_route(y, refs[:N_POST_IN], *refs[-3:])


def _mla_out(ol, wuv, wo, tm, **post):
    return _post_call(_mla_out_kernel, "mla_out", [ol, wuv, wo],
                      [pl.BlockSpec((1, HEADS, tm, B_KV_RANK), lambda gi, i: (gi, 0, i, 0)),
                       _full_spec(wuv.shape), _full_spec(wo.shape)], tm=tm, **post)


def _rope_tables(pos):
    half = B_D_ROPE // 2
    inv = jnp.power(ROPE_THETA, -jnp.arange(half, dtype=f32) / half)
    ang = pos.astype(f32)[:, None] * inv[None, :]
    cos, sin = jnp.cos(ang), jnp.sin(ang)
    return jnp.concatenate([cos, cos], axis=-1), jnp.concatenate([-sin, sin], axis=-1)


def _swap_halves(w, width):
    lead = w.shape[:-1]
    w2 = w.reshape(lead + (-1, 2, width // 2))
    return w2[..., ::-1, :].reshape(w.shape)


def kernel(x_prompt, x_sample, cache_latent, cache_krope, page_table, state_C, state_n, state_m, c_prompt, c_sample, ada_w, ada_b, ln_g, ln_b, a_w_in, a_b_gates, a_norm_g, a_w_out, b_w_kv_a, b_kv_norm_g, b_w_uk, b_w_uv, b_w_dq, b_q_norm_g, b_w_uq, b_w_o, router_w, router_b, e_w_gate, e_w_up, e_w_down):
    bp, tp, d = x_prompt.shape
    bs, ts, _ = x_sample.shape
    hqk = HEADS * A_DQK
    hv = HEADS * A_DV

    w_in = a_w_in.at[:, :, hqk:2 * hqk].multiply(A_DQK ** -0.5)
    w_in = jnp.pad(w_in, ((0, 0), (0, 0), (0, LANES - 2 * HEADS))).astype(bf16)
    bg = jnp.pad(a_b_gates, ((0, 0), (0, LANES - 2 * HEADS))).reshape(N_A_LAYERS, 1, LANES)
    w_out = a_w_out.astype(bf16)
    ng = a_norm_g.reshape(N_A_LAYERS, 1, hv)
    perm = jnp.array([(r % N_GROUPS) * PER_GROUP + r // N_GROUPS for r in range(N_EXPERTS)], jnp.int32)
    rw_perm = router_w[:, perm]
    rw_hi = rw_perm.astype(bf16)
    rw_lo = (rw_perm - rw_hi.astype(f32)).astype(bf16)
    rw = jnp.pad(jnp.concatenate([rw_hi, rw_lo], axis=1), ((0, 0), (0, LANES - 2 * N_EXPERTS)))
    rwh = jnp.pad(rw_hi, ((0, 0), (0, LANES - N_EXPERTS)))
    rb = router_b[perm].reshape(N_EXPERTS, 1)
    w_lat = b_w_kv_a[:, :B_KV_RANK]
    w_kr = b_w_kv_a[:, B_KV_RANK:]
    zpad = jnp.zeros((d, LANES - B_D_ROPE), f32)
    w_kv = jnp.concatenate([w_lat, w_kr, zpad, _swap_halves(w_kr, B_D_ROPE), zpad], axis=1).astype(bf16)
    kvg = b_kv_norm_g.reshape(1, B_KV_RANK)
    wdq = b_w_dq.astype(bf16)
    qg = b_q_norm_g.reshape(-1, 1, b_q_norm_g.shape[-1])
    uq = b_w_uq.reshape(b_w_uq.shape[0], b_w_uq.shape[1], HEADS, B_D_NOPE + B_D_ROPE)
    uq_nope = uq[..., :B_D_NOPE].reshape(uq.shape[0], uq.shape[1], HEADS * B_D_NOPE)
    uq_rope = uq[..., B_D_NOPE:].reshape(uq.shape[0], uq.shape[1], HEADS * B_D_ROPE)
    wuq = jnp.concatenate([uq_nope, uq_rope, _swap_halves(uq_rope, B_D_ROPE)], axis=-1).astype(bf16)
    wuk = jnp.transpose(b_w_uk, (1, 2, 0)).astype(bf16)
    wuv = jnp.transpose(b_w_uv, (1, 0, 2)).astype(bf16)
    wo = b_w_o.astype(bf16)

    mods = _ada_mods(jnp.concatenate([c_prompt, c_sample], axis=0), ada_w, ada_b)

    cos_p, sin_p = _rope_tables(jnp.arange(tp, dtype=jnp.int32))
    past_len = page_table.shape[1] * PAGE
    cos_s, sin_s = _rope_tables(jnp.tile(past_len + jnp.arange(ts, dtype=jnp.int32), bs))

    tm = 512
    cache_krt = jnp.swapaxes(cache_krope, 1, 2)
    streams = []
    for x0, mods_g, cos, sin, sample in ((x_prompt, mods[:, :bp], cos_p, sin_p, False),
                                         (x_sample.reshape(1, bs * ts, d), mods[:, bp:], cos_s, sin_s, True)):
        streams.append(dict(x=x0, mods=mods_g, sample=sample, cos1=cos[None], sin1=sin[None],
                            cos8=jnp.tile(cos, (1, HEADS))[None], sin8=jnp.tile(sin, (1, HEADS))[None],
                            new_c=[], new_n=[], new_m=[]))

    for l in range(DEPTH):
        lg = ln_g[l].reshape(2, 1, d)
        lb = ln_b[l].reshape(2, 1, d)
        moe_in = []
        h2_all, row0 = None, 0
        n_all = sum(st["x"].shape[0] * st["x"].shape[1] for st in streams)
        for st in streams:
            x, sample = st["x"], st["sample"]
            cos8, sin8 = st["cos8"], st["sin8"]
            latb, krb = st.get("latb"), st.get("krb")
            new_c, new_n, new_m = st["new_c"], st["new_n"], st["new_m"]
            m6 = st["mods"][l].reshape(st["mods"].shape[1], 6, d)
            if sample:
                parts = [jnp.repeat(m6[:, i], ts, axis=0)[None] for i in range(6)]
            else:
                parts = [m6[:, i][:, None, :] for i in range(6)]
            shift1, scale1, gate1, shift2, scale2, gate2 = parts
            post = dict(x=x, gate=gate1, ln_g=lg[0], ln_b=lb[0], scale2=scale2, shift2=shift2, rw=rw, rwh=rwh, rb=rb,
                        h2_rows=n_all, h2_row0=row0, h2_prev=h2_all)
            if l < N_A_LAYERS:
                qkv, o, gates = _mlstm_in(x, scale1, shift1, w_in[l], bg[l], tm)
                kt = jnp.swapaxes(qkv[:, :, hqk:2 * hqk], 1, 2)
                grow = jnp.swapaxes(gates[:, :, :2 * HEADS], 1, 2)
                if sample:
                    n0 = state_n[l].reshape(bs, hqk)
                    m0 = state_m[l]
                    nrows = jnp.repeat(n0, ts, axis=0)[None]
                    m_tok = jnp.repeat(m0, ts, axis=0)
                    mcol = jnp.pad(m_tok, ((0, 0), (0, LANES - HEADS)))[None]
                    mrow = m_tok.T[None]
                    u, c_stack, n_new, m_new = _mlstm_sample(qkv, kt, o, gates, grow, ng[l], state_C, l,
                                                             st.get("c_stack"), n0, nrows, m0, mcol, mrow, ts)
                    st["c_stack"] = c_stack
                    n_new = n_new.reshape(bs, HEADS, A_DQK)
                else:
                    u, c_new, n_rep, m_rep = _mlstm_prompt(qkv, kt, o, gates, grow, ng[l])
                    n_new = n_rep[..., 0]
                    m_new = m_rep[..., 0]
                    new_c.append(c_new)
                new_n.append(n_new)
                new_m.append(m_new)
                x1, h2_all, route = _mlstm_out(u, w_out[l], tm, **post)
            else:
                j = l - N_A_LAYERS
                ql, qr = _mla_q(x, scale1, shift1, wdq[j], qg[j], wuq[j], wuk, cos8, sin8, tm)
                if sample:
                    ql_s = jnp.transpose(ql.reshape(HEADS, bs, ts, B_KV_RANK), (1, 0, 2, 3)).reshape(
                        bs, HEADS * ts, B_KV_RANK)
                    qr_s = jnp.transpose(qr.reshape(HEADS, bs, ts, B_D_ROPE), (1, 0, 2, 3)).reshape(
                        bs, HEADS * ts, B_D_ROPE)
                    new_pad = 16
                    nl = jnp.pad(latb.reshape(bs, ts, B_KV_RANK), ((0, 0), (0, new_pad - ts), (0, 0)))
                    nk = jnp.pad(krb.reshape(bs, ts, B_D_ROPE), ((0, 0), (0, new_pad - ts), (0, 0)))
                    ol = _paged_attention(page_table, ql_s, qr_s, cache_latent, cache_krt, nl, nk, ts)
                    ol = jnp.transpose(ol.reshape(bs, HEADS, ts, B_KV_RANK), (1, 0, 2, 3)).reshape(
                        1, HEADS, bs * ts, B_KV_RANK)
                else:
                    ol = _flash(ql, qr, latb, krb, 512)
                x1, h2_all, route = _mla_out(ol, wuv, wo[j], tm, **post)
            moe_in.append((x1, route, gate2))
            row0 += x.shape[0] * x.shape[1]
        for st, x_new in zip(streams, _moe(moe_in, h2_all, lg[1], lb[1], e_w_gate, e_w_up, e_w_down, l, tm)):
            st["x"] = x_new
            if l == N_A_LAYERS - 1:
                st["lat"], st["kr"], st["latb"], st["krb"] = _shared_kv(x_new, w_kv, kvg, st["cos1"], st["sin1"], tm)

    sp, ss = streams
    return (sp["x"], ss["x"].reshape(bs, ts, d),
            jnp.stack(sp["new_c"]), jnp.stack(sp["new_n"]), jnp.stack(sp["new_m"]), sp["lat"], sp["kr"],
            ss["c_stack"], jnp.stack(ss["new_n"]), jnp.stack(ss["new_m"]),
            ss["lat"].reshape(bs, ts, B_KV_RANK), ss["kr"].reshape(bs, ts, B_D_ROPE))
```

```python
import functools

import jax
import jax.numpy as jnp
from jax import lax
from jax.experimental import pallas as pl
from jax.experimental.pallas import tpu as pltpu

f32 = jnp.float32
bf16 = jnp.bfloat16

DEPTH = 4
N_A_LAYERS = 2
HEADS = 8
A_DQK = 64
A_DV = 128
B_D_NOPE = 128
B_D_ROPE = 64
B_KV_RANK = 256
ROPE_THETA = 10000.0
ATTN_SCALE = (B_D_NOPE + B_D_ROPE) ** -0.5
N_EXPERTS = 16
N_GROUPS = 4
PER_GROUP = 4
PAGE = 128
ALPHA = (2 * DEPTH) ** 0.25
EPS = 1e-6
NEG_INF = float("-inf")
LOG2E = 1.4426950408889634
SOFTMAX_EXP2_SCALE = ATTN_SCALE * LOG2E

VMEM_LIMIT_BYTES = 56 * 1024 * 1024
LANES = 128
MLSTM_CHUNK = 128
SAMPLE_GROUP = 16
MOE_BLOCK = 256
GATHER_PARTS = 2
PAGES_PER_STEP = 16
PAGED_CHAINS = 2
FLASH_HEADS = 4


def _cparams(sem):
    return pltpu.CompilerParams(dimension_semantics=sem, vmem_limit_bytes=VMEM_LIMIT_BYTES)


def _dot(a, b):
    return jnp.dot(a, b, preferred_element_type=f32)


def _dot_nt(a, b):
    return lax.dot_general(a, b, (((1,), (1,)), ((), ())), preferred_element_type=f32)


def _dot_exact(a, b):
    return jnp.dot(a, b, preferred_element_type=f32, precision=lax.Precision.HIGHEST)


def _sigmoid(x):
    return 1.0 / (1.0 + jnp.exp(-x))


def _log_sigmoid(x):
    return jnp.minimum(x, 0.0) - jnp.log(1.0 + jnp.exp(-jnp.abs(x)))


def _ada_kernel(c_ref, w_ref, b_ref, o_ref):
    c = c_ref[...]
    sc = (c * _sigmoid(c)).astype(bf16)
    o_ref[0] = _dot(sc, w_ref[0].astype(bf16)) + b_ref[0]


def _ada_mods(c_all, ada_w, ada_b):
    depth, d, e = ada_w.shape
    bc = c_all.shape[0]
    tn = 1536
    return pl.pallas_call(
        _ada_kernel,
        grid=(depth, e // tn),
        in_specs=[
            pl.BlockSpec((bc, d), lambda l, j: (0, 0)),
            pl.BlockSpec((1, d, tn), lambda l, j: (l, 0, j)),
            pl.BlockSpec((1, 1, tn), lambda l, j: (l, 0, j)),
        ],
        out_specs=pl.BlockSpec((1, bc, tn), lambda l, j: (l, 0, j)),
        out_shape=jax.ShapeDtypeStruct((depth, bc, e), f32),
        compiler_params=_cparams(("parallel", "parallel")),
        name="ada_mods",
    )(c_all, ada_w, ada_b.reshape(depth, 1, e))


def _mod_spec(mod, tm):
    d = mod.shape[-1]
    if mod.shape[1] == 1:
        return pl.BlockSpec((1, 1, d), lambda g, i: (g, 0, 0))
    return pl.BlockSpec((1, tm, d), lambda g, i: (g, i, 0))


def _tok_spec(tm, d):
    return pl.BlockSpec((1, tm, d), lambda g, i: (g, i, 0))


def _full_spec(shape):
    nd = len(shape)
    return pl.BlockSpec(shape, lambda g, i: (0,) * nd)


def _route(h2, rw_ref, rwh_ref, rb_ref):
    hi = h2.astype(bf16)
    lo = (h2 - hi.astype(f32)).astype(bf16)
    p = _dot(hi, rw_ref[...]) + _dot(lo, rwh_ref[...])
    pt = p.T
    logits = pt[0:N_EXPERTS] + pt[N_EXPERTS:2 * N_EXPERTS]
    sc = _sigmoid(logits)
    sel = sc + rb_ref[...]
    a = [sel[PER_GROUP * j:PER_GROUP * (j + 1)] for j in range(PER_GROUP)]
    s = [sc[PER_GROUP * j:PER_GROUP * (j + 1)] for j in range(PER_GROUP)]
    hi01, lo01 = jnp.maximum(a[0], a[1]), jnp.minimum(a[0], a[1])
    hi23, lo23 = jnp.maximum(a[2], a[3]), jnp.minimum(a[2], a[3])
    gs = jnp.maximum(hi01, hi23) + jnp.maximum(jnp.minimum(hi01, hi23), jnp.maximum(lo01, lo23))
    best = gs[0:1]
    grp = jnp.zeros_like(best)
    for g in range(1, N_GROUPS):
        better = gs[g:g + 1] > best
        grp = jnp.where(better, float(g), grp)
        best = jnp.where(better, gs[g:g + 1], best)
    mv, sv = [], []
    for j in range(PER_GROUP):
        m_j = a[j][0:1]
        s_j = s[j][0:1]
        for g in range(1, N_GROUPS):
            m_j = jnp.where(grp == float(g), a[j][g:g + 1], m_j)
            s_j = jnp.where(grp == float(g), s[j][g:g + 1], s_j)
        mv.append(m_j)
        sv.append(s_j)

    def first_argmax(vals):
        bv, bi = vals[0], jnp.zeros_like(vals[0])
        for j in range(1, PER_GROUP):
            better = vals[j] > bv
            bi = jnp.where(better, float(j), bi)
            bv = jnp.where(better, vals[j], bv)
        return bi

    i1 = first_argmax(mv)
    i2 = first_argmax([jnp.where(i1 == float(j), NEG_INF, mv[j]) for j in range(PER_GROUP)])
    w1, w2 = sv[0], sv[0]
    for j in range(1, PER_GROUP):
        w1 = jnp.where(i1 == float(j), sv[j], w1)
        w2 = jnp.where(i2 == float(j), sv[j], w2)
    tot = w1 + w2
    e1 = grp * float(PER_GROUP) + i1
    e2 = grp * float(PER_GROUP) + i2
    z = jnp.zeros_like(e1)
    return jnp.concatenate([e1, e2, w1 / tot, w2 / tot, z, z, z, z], axis=0)


N_POST_IN = 9


def _residual_ln_route(y, post_refs, x1_ref, h2_ref, route_ref):
    x_ref, gate_ref, lng_ref, lnb_ref, sc_ref, sh_ref, rw_ref, rwh_ref, rb_ref = post_refs
    r = ALPHA * x_ref[0] + gate_ref[0] * y
    mu = jnp.mean(r, axis=-1, keepdims=True)
    cen = r - mu
    var = jnp.mean(cen * cen, axis=-1, keepdims=True)
    x1 = cen * lax.rsqrt(var + EPS) * lng_ref[...] + lnb_ref[...]
    x1_ref[0] = x1
    h2 = x1 * (1.0 + sc_ref[0]) + sh_ref[0]
    h2_ref[...] = h2
    route_ref[0] = _route(h2, rw_ref, rwh_ref, rb_ref)


def _post_call(kernel_fn, name, mixer_args, mixer_specs, x, gate, ln_g, ln_b, scale2, shift2, rw, rwh, rb, tm,
               h2_rows, h2_row0, h2_prev):
    g, t, d = x.shape
    steps = t // tm
    blk0 = h2_row0 // tm
    in_specs = list(mixer_specs) + [
        _tok_spec(tm, d), _mod_spec(gate, tm), _full_spec((1, d)), _full_spec((1, d)),
        _mod_spec(scale2, tm), _mod_spec(shift2, tm),
        _full_spec((d, LANES)), _full_spec((d, LANES)), _full_spec((N_EXPERTS, 1)),
    ]
    args = list(mixer_args) + [x, gate, ln_g, ln_b, scale2, shift2, rw, rwh, rb]
    aliases = {}
    if h2_prev is not None:
        aliases = {len(args): 1}
        in_specs.append(pl.BlockSpec(memory_space=pl.ANY))
        args.append(h2_prev)
    return pl.pallas_call(
        kernel_fn,
        grid=(g, steps),
        in_specs=in_specs,
        out_specs=[_tok_spec(tm, d),
                   pl.BlockSpec((tm, d), lambda gi, i: (blk0 + gi * steps + i, 0)),
                   pl.BlockSpec((1, 8, tm), lambda gi, i: (gi, 0, i))],
        out_shape=[jax.ShapeDtypeStruct((g, t, d), f32),
                   jax.ShapeDtypeStruct((h2_rows, d), f32),
                   jax.ShapeDtypeStruct((g, 8, t), f32)],
        input_output_aliases=aliases,
        compiler_params=_cparams(("parallel", "parallel")),
        name=name,
    )(*args)


def _mlstm_in_kernel(x_ref, sc_ref, sh_ref, w_ref, bg_ref, qkv_ref, o_ref, g_ref):
    h = x_ref[0] * (1.0 + sc_ref[0]) + sh_ref[0]
    z = _dot(h.astype(bf16), w_ref[...])
    hqk2 = 2 * HEADS * A_DQK
    hv = HEADS * A_DV
    qkv_ref[0] = z[:, :hqk2 + hv].astype(bf16)
    o_ref[0] = z[:, hqk2 + hv:hqk2 + 2 * hv]
    g_ref[0] = z[:, hqk2 + 2 * hv:] + bg_ref[...]


def _mlstm_in(x, scale, shift, w_pad, bg_pad, tm):
    g, t, d = x.shape
    n_qkv = 2 * HEADS * A_DQK + HEADS * A_DV
    hv = HEADS * A_DV
    return pl.pallas_call(
        _mlstm_in_kernel,
        grid=(g, t // tm),
        in_specs=[_tok_spec(tm, d), _mod_spec(scale, tm), _mod_spec(shift, tm),
                  _full_spec(w_pad.shape), _full_spec((1, LANES))],
        out_specs=[_tok_spec(tm, n_qkv), _tok_spec(tm, hv), _tok_spec(tm, LANES)],
        out_shape=[jax.ShapeDtypeStruct((g, t, n_qkv), bf16),
                   jax.ShapeDtypeStruct((g, t, hv), f32),
                   jax.ShapeDtypeStruct((g, t, LANES), f32)],
        compiler_params=_cparams(("parallel", "parallel")),
        name="mlstm_in",
    )(x, scale, shift, w_pad, bg_pad)


def _mlstm_out_kernel(u_ref, w_ref, *refs):
    y = _dot(u_ref[0], w_ref[...])
    _residual_ln_route(y, refs[:N_POST_IN], *refs[-3:])


def _mlstm_out(u, w_out, tm, **post):
    return _post_call(_mlstm_out_kernel, "mlstm_out", [u, w_out],
                      [_tok_spec(tm, u.shape[-1]), _full_spec(w_out.shape)], tm=tm, **post)


def _head_norm_gate(hh, ng, o):
    mu = jnp.mean(hh, axis=-1, keepdims=True)
    cen = hh - mu
    var = jnp.mean(cen * cen, axis=-1, keepdims=True)
    return (cen * lax.rsqrt(var + EPS) * ng) * _sigmoid(o)


def _mlstm_prompt_kernel(qkv_ref, kt_ref, o_ref, gcol_ref, grow_ref, ng_ref,
                         u_ref, c_out, n_out, m_out, c_s, n_s, m_s):
    c = pl.program_id(1)
    nc = pl.num_programs(1)
    L = qkv_ref.shape[1]
    hqk = HEADS * A_DQK

    @pl.when(c == 0)
    def _():
        c_s[...] = jnp.zeros_like(c_s)
        n_s[...] = jnp.zeros_like(n_s)
        m_s[...] = jnp.zeros_like(m_s)

    r_i = lax.broadcasted_iota(jnp.int32, (L, L), 0)
    c_i = lax.broadcasted_iota(jnp.int32, (L, L), 1)
    causal = c_i <= r_i
    tri = causal.astype(f32)
    tri_t = (r_i <= c_i).astype(f32)

    gcol = gcol_ref[0]
    grow = grow_ref[0]
    b_col = _dot_exact(tri, _log_sigmoid(gcol))
    b_row = _dot_exact(_log_sigmoid(grow[HEADS:2 * HEADS]), tri_t)
    li_row = grow[0:HEADS]

    qkv = qkv_ref[0]
    kt = kt_ref[0]
    o_all = o_ref[0]
    ng = ng_ref[...]
    c_old, n_old, m_old = c_s[...], n_s[...], m_s[...]
    hs = range(HEADS)
    q = [qkv[:, A_DQK * h:A_DQK * (h + 1)] for h in hs]
    v = [qkv[:, 2 * hqk + A_DV * h:2 * hqk + A_DV * (h + 1)] for h in hs]
    kth = [kt[A_DQK * h:A_DQK * (h + 1), :] for h in hs]
    s_qk = [_dot(q[h], kth[h]) for h in hs]
    bc = [b_col[:, HEADS + h:HEADS + h + 1] for h in hs]
    br = [b_row[h:h + 1] for h in hs]
    lir = [li_row[h:h + 1] for h in hs]
    m_prev = [m_old[h:h + 1, 0:1] for h in hs]
    d = [jnp.where(causal, bc[h] - br[h] + lir[h], NEG_INF) for h in hs]
    a = [bc[h] + m_prev[h] for h in hs]
    d_max = [jnp.max(d[h], axis=-1, keepdims=True) for h in hs]
    m_t = [jnp.maximum(a[h], d_max[h]) for h in hs]
    p = [s_qk[h] * jnp.exp(d[h] - m_t[h]) for h in hs]
    inter = [jnp.exp(a[h] - m_t[h]) for h in hs]
    qc = [_dot(q[h], jnp.concatenate([c_old[h], n_old[h]], axis=1).astype(bf16)) for h in hs]
    p_sum = [jnp.sum(p[h], axis=-1, keepdims=True) for h in hs]
    num = [_dot(p[h].astype(bf16), v[h]) + inter[h] * qc[h][:, :A_DV] for h in hs]
    den = [p_sum[h] + inter[h] * qc[h][:, A_DV:A_DV + 1] for h in hs]
    hh = [num[h] / jnp.maximum(jnp.abs(den[h]), jnp.exp(-m_t[h])) for h in hs]
    mu = [jnp.mean(hh[h], axis=-1, keepdims=True) for h in hs]
    cen = [hh[h] - mu[h] for h in hs]
    var = [jnp.mean(cen[h] * cen[h], axis=-1, keepdims=True) for h in hs]
    u_parts = [((cen[h] * lax.rsqrt(var[h] + EPS) * ng[:, A_DV * h:A_DV * (h + 1)])
                * _sigmoid(o_all[:, A_DV * h:A_DV * (h + 1)])).astype(bf16) for h in hs]

    b_end = [br[h][:, L - 1:L] for h in hs]
    g_row = [b_end[h] - br[h] + lir[h] for h in hs]
    g_max = [jnp.max(g_row[h], axis=-1, keepdims=True) for h in hs]
    m_new = [jnp.maximum(b_end[h] + m_prev[h], g_max[h]) for h in hs]
    decay = [jnp.exp(b_end[h] + m_prev[h] - m_new[h]) for h in hs]
    kw = [kth[h].astype(f32) * jnp.exp(g_row[h] - m_new[h]) for h in hs]
    kw_sum = [jnp.sum(kw[h], axis=-1, keepdims=True) for h in hs]
    c_parts = [decay[h] * c_old[h] + _dot(kw[h].astype(bf16), v[h]) for h in hs]
    n_parts = [decay[h] * n_old[h] + kw_sum[h] for h in hs]
    m_parts = [jnp.broadcast_to(m_new[h], (1, LANES)) for h in hs]

    u_ref[0] = jnp.concatenate(u_parts, axis=1)
    c_s[...] = jnp.stack(c_parts)
    n_s[...] = jnp.stack(n_parts)
    m_s[...] = jnp.concatenate(m_parts, axis=0)

    @pl.when(c == nc - 1)
    def _():
        c_out[0] = c_s[...]
        n_out[0] = n_s[...]
        m_out[0] = m_s[...]


def _mlstm_prompt(qkv, kt, o, gcol, grow, ng):
    b, t, _ = qkv.shape
    L = MLSTM_CHUNK
    hv = HEADS * A_DV
    return pl.pallas_call(
        _mlstm_prompt_kernel,
        grid=(b, t // L),
        in_specs=[
            pl.BlockSpec((1, L, qkv.shape[-1]), lambda g, c: (g, c, 0)),
            pl.BlockSpec((1, HEADS * A_DQK, L), lambda g, c: (g, 0, c)),
            pl.BlockSpec((1, L, hv), lambda g, c: (g, c, 0)),
            pl.BlockSpec((1, L, LANES), lambda g, c: (g, c, 0)),
            pl.BlockSpec((1, 2 * HEADS, L), lambda g, c: (g, 0, c)),
            pl.BlockSpec((1, hv), lambda g, c: (0, 0)),
        ],
        out_specs=[
            pl.BlockSpec((1, L, hv), lambda g, c: (g, c, 0)),
            pl.BlockSpec((1, HEADS, A_DQK, A_DV), lambda g, c: (g, 0, 0, 0)),
            pl.BlockSpec((1, HEADS, A_DQK, LANES), lambda g, c: (g, 0, 0, 0)),
            pl.BlockSpec((1, HEADS, LANES), lambda g, c: (g, 0, 0)),
        ],
        out_shape=[
            jax.ShapeDtypeStruct((b, t, hv), bf16),
            jax.ShapeDtypeStruct((b, HEADS, A_DQK, A_DV), f32),
            jax.ShapeDtypeStruct((b, HEADS, A_DQK, LANES), f32),
            jax.ShapeDtypeStruct((b, HEADS, LANES), f32),
        ],
        scratch_shapes=[
            pltpu.VMEM((HEADS, A_DQK, A_DV), f32),
            pltpu.VMEM((HEADS, A_DQK, LANES), f32),
            pltpu.VMEM((HEADS, LANES), f32),
        ],
        compiler_params=_cparams(("parallel", "arbitrary")),
        name="mlstm_prompt",
    )(qkv, kt, o, gcol, grow, ng)


def _mlstm_sample_kernel(qkv_ref, kt_ref, o_ref, gcol_ref, grow_ref, ng_ref, c0_ref, n0_ref, nrows_ref,
                         m0_ref, mcol_ref, mrow_ref, *refs, ts):
    u_ref, c_out, n_out, m_out = refs[-4:]
    R = qkv_ref.shape[1]
    G = R // ts
    shift = ts.bit_length() - 1
    hqk = HEADS * A_DQK

    r_i = lax.broadcasted_iota(jnp.int32, (R, R), 0)
    c_i = lax.broadcasted_iota(jnp.int32, (R, R), 1)
    same = (r_i >> shift) == (c_i >> shift)
    valid = same & (c_i <= r_i)
    tri = valid.astype(f32)
    tri_t = (same & (r_i <= c_i)).astype(f32)
    same_f = same.astype(f32)
    row_seq = lax.broadcasted_iota(jnp.int32, (R, 1), 0) >> shift
    g_i = lax.broadcasted_iota(jnp.int32, (G, R), 0)
    s_i = lax.broadcasted_iota(jnp.int32, (G, R), 1)
    bmask = (s_i >> shift) == g_i
    lastmask = s_i == g_i * ts + (ts - 1)
    g3 = lax.broadcasted_iota(jnp.int32, (G, A_DQK, R), 0)
    s3 = lax.broadcasted_iota(jnp.int32, (G, A_DQK, R), 2)
    bmask3 = (s3 >> shift) == g3
    lane_i = lax.broadcasted_iota(jnp.int32, (G, LANES), 1)

    gcol = gcol_ref[0]
    grow = grow_ref[0]
    lf_row = _log_sigmoid(grow[HEADS:2 * HEADS])
    b_col = _dot_exact(tri, _log_sigmoid(gcol))
    b_row = _dot_exact(lf_row, tri_t)
    bend_row = _dot_exact(lf_row, same_f)
    li_row = grow[0:HEADS]
    mcol = mcol_ref[0]
    mrow = mrow_ref[0]
    m0 = m0_ref[...]

    m_acc = jnp.zeros((G, LANES), f32)
    n_parts = []
    for h in range(HEADS):
        q_h = qkv_ref[0, :, A_DQK * h:A_DQK * (h + 1)]
        k_h = qkv_ref[0, :, hqk + A_DQK * h:hqk + A_DQK * (h + 1)]
        v_h = qkv_ref[0, :, 2 * hqk + A_DV * h:2 * hqk + A_DV * (h + 1)]
        kt_h = kt_ref[0, A_DQK * h:A_DQK * (h + 1), :]
        s_qk = _dot(q_h, kt_h)
        bc = b_col[:, HEADS + h:HEADS + h + 1]
        br = b_row[h:h + 1]
        lir = li_row[h:h + 1]
        d = jnp.where(valid, bc - br + lir, NEG_INF)
        a = bc + mcol[:, h:h + 1]
        m_t = jnp.maximum(a, jnp.max(d, axis=-1, keepdims=True))
        p = s_qk * jnp.exp(d - m_t)
        inter = jnp.exp(a - m_t)
        c_all = jnp.concatenate([c0_ref[0, g, h] for g in range(G)], axis=1).astype(bf16)
        qc = _dot(q_h, c_all)
        inter_c = jnp.zeros((R, A_DV), f32)
        for g in range(G):
            inter_c = jnp.where(row_seq == g, qc[:, A_DV * g:A_DV * (g + 1)], inter_c)
        qn = jnp.sum(q_h.astype(f32) * nrows_ref[0, :, A_DQK * h:A_DQK * (h + 1)], axis=-1, keepdims=True)
        num = _dot(p.astype(bf16), v_h) + inter * inter_c
        den = jnp.sum(p, axis=-1, keepdims=True) + inter * qn
        hh = num / jnp.maximum(jnp.abs(den), jnp.exp(-m_t))
        u = _head_norm_gate(hh, ng_ref[:, A_DV * h:A_DV * (h + 1)], o_ref[0, :, A_DV * h:A_DV * (h + 1)])
        u_ref[0, :, A_DV * h:A_DV * (h + 1)] = u.astype(bf16)

        g_row = bend_row[h:h + 1] - br + lir
        gmax = jnp.max(jnp.where(bmask, g_row, NEG_INF), axis=-1, keepdims=True)
        bend_b = jnp.sum(jnp.where(lastmask, br, 0.0), axis=-1, keepdims=True)
        m0_h = m0[:, h:h + 1]
        mnew_b = jnp.maximum(bend_b + m0_h, gmax)
        mnew_row = jnp.sum(jnp.where(bmask, mnew_b, 0.0), axis=0, keepdims=True)
        w_row = jnp.exp(g_row - mnew_row)
        decay_b = jnp.exp(bend_b + m0_h - mnew_b)
        kw = kt_h.astype(f32) * w_row
        kw3 = jnp.where(bmask3, jnp.broadcast_to(kw[None], (G, A_DQK, R)), 0.0)
        upd = _dot(kw3.reshape(G * A_DQK, R).astype(bf16), v_h)
        for g in range(G):
            c_out[0, g, h] = decay_b[g:g + 1, :] * c0_ref[0, g, h] + upd[A_DQK * g:A_DQK * (g + 1)]
        wm = jnp.where(bmask, w_row, 0.0)
        n_parts.append(decay_b * n0_ref[:, A_DQK * h:A_DQK * (h + 1)] + _dot(wm.astype(bf16), k_h))
        m_acc = jnp.where(lane_i == h, mnew_b, m_acc)

    n_out[...] = jnp.concatenate(n_parts, axis=1)
    m_out[...] = m_acc[:, :HEADS]


def _mlstm_sample(qkv, kt, o, gcol, grow, ng, c0_all, layer, c_prev, n0, nrows, m0, mcol, mrow, ts):
    _, t, _ = qkv.shape
    n_layers, bs = c0_all.shape[:2]
    G = SAMPLE_GROUP
    R = G * ts
    hv = HEADS * A_DV
    hqk = HEADS * A_DQK
    args = [qkv, kt, o, gcol, grow, ng, c0_all, n0, nrows, m0, mcol, mrow]
    extra_specs, aliases = [], {}
    if c_prev is not None:
        aliases = {len(args): 1}
        extra_specs = [pl.BlockSpec(memory_space=pl.ANY)]
        args.append(c_prev)
    return pl.pallas_call(
        functools.partial(_mlstm_sample_kernel, ts=ts),
        grid=(t // R,),
        input_output_aliases=aliases,
        in_specs=[
            pl.BlockSpec((1, R, qkv.shape[-1]), lambda i: (0, i, 0)),
            pl.BlockSpec((1, hqk, R), lambda i: (0, 0, i)),
            pl.BlockSpec((1, R, hv), lambda i: (0, i, 0)),
            pl.BlockSpec((1, R, LANES), lambda i: (0, i, 0)),
            pl.BlockSpec((1, 2 * HEADS, R), lambda i: (0, 0, i)),
            pl.BlockSpec((1, hv), lambda i: (0, 0)),
            pl.BlockSpec((1, G, HEADS, A_DQK, A_DV), lambda i: (layer, i, 0, 0, 0)),
            pl.BlockSpec((G, hqk), lambda i: (i, 0)),
            pl.BlockSpec((1, R, hqk), lambda i: (0, i, 0)),
            pl.BlockSpec((G, HEADS), lambda i: (i, 0)),
            pl.BlockSpec((1, R, LANES), lambda i: (0, i, 0)),
            pl.BlockSpec((1, HEADS, R), lambda i: (0, 0, i)),
        ] + extra_specs,
        out_specs=[
            pl.BlockSpec((1, R, hv), lambda i: (0, i, 0)),
            pl.BlockSpec((1, G, HEADS, A_DQK, A_DV), lambda i: (layer, i, 0, 0, 0)),
            pl.BlockSpec((G, hqk), lambda i: (i, 0)),
            pl.BlockSpec((G, HEADS), lambda i: (i, 0)),
        ],
        out_shape=[
            jax.ShapeDtypeStruct((1, t, hv), bf16),
            jax.ShapeDtypeStruct((n_layers, bs, HEADS, A_DQK, A_DV), f32),
            jax.ShapeDtypeStruct((bs, hqk), f32),
            jax.ShapeDtypeStruct((bs, HEADS), f32),
        ],
        compiler_params=_cparams(("parallel",)),
        name="mlstm_sample",
    )(*args)


def _expert_kernel(be_ref, nu_ref, first_ref, *refs, n_parts):
    x_refs = refs[:n_parts]
    wg_ref, wu_ref, wd_ref, y_ref, wg_s, wu_s, wd_s = refs[n_parts:]
    i = pl.program_id(0)
    part_blocks = pl.num_programs(0) // n_parts

    @pl.when(first_ref[i] == 1)
    def _():
        wg_s[...] = wg_ref[0, 0].astype(bf16)
        wu_s[...] = wu_ref[0, 0].astype(bf16)
        wd_s[...] = wd_ref[0, 0].astype(bf16)

    @pl.when(i < nu_ref[0])
    def _():
        x = x_refs[n_parts - 1][...]
        for part in range(n_parts - 2, -1, -1):
            x = jnp.where(i < (part + 1) * part_blocks, x_refs[part][...], x)
        x = x.astype(bf16)
        g = _dot(x, wg_s[...])
        u = _dot(x, wu_s[...])
        hid = (g * _sigmoid(g)) * u
        y_ref[...] = _dot(hid.astype(bf16), wd_s[...])

    @pl.when(i >= nu_ref[0])
    def _():
        y_ref[...] = jnp.zeros_like(y_ref)


def _experts(x_parts, block_e, n_used, first, wg, wu, wd, layer):
    n_parts = len(x_parts)
    pp, d = x_parts[0].shape
    de = wg.shape[-1]
    bm = MOE_BLOCK
    pb = pp // bm
    w_map = lambda i, be, nu, fi: (layer, be[i], 0, 0)
    part_specs = [pl.BlockSpec((bm, d), functools.partial(
        lambda i, be, nu, fi, part: (jnp.clip(i - part * pb, 0, pb - 1), 0), part=part)) for part in range(n_parts)]
    return pl.pallas_call(
        functools.partial(_expert_kernel, n_parts=n_parts),
        grid_spec=pltpu.PrefetchScalarGridSpec(
            num_scalar_prefetch=3,
            grid=(n_parts * pb,),
            in_specs=part_specs + [
                pl.BlockSpec((1, 1, d, de), w_map),
                pl.BlockSpec((1, 1, d, de), w_map),
                pl.BlockSpec((1, 1, de, d), w_map),
            ],
            out_specs=pl.BlockSpec((bm, d), lambda i, be, nu, fi: (i, 0)),
            scratch_shapes=[pltpu.VMEM((d, de), bf16), pltpu.VMEM((d, de), bf16), pltpu.VMEM((de, d), bf16)],
        ),
        out_shape=jax.ShapeDtypeStruct((n_parts * pp, d), f32),
        compiler_params=_cparams(("arbitrary",)),
        name="moe_experts",
    )(block_e, n_used, first, *x_parts, wg, wu, wd)


def _combine_kernel(x_ref, ya_ref, yb_ref, w_ref, gate_ref, lng_ref, lnb_ref, o_ref):
    w = w_ref[0]
    y = w[:, 0:1] * ya_ref[...] + w[:, 1:2] * yb_ref[...]
    r = ALPHA * x_ref[0] + gate_ref[0] * y
    mu = jnp.mean(r, axis=-1, keepdims=True)
    cen = r - mu
    var = jnp.mean(cen * cen, axis=-1, keepdims=True)
    o_ref[0] = cen * lax.rsqrt(var + EPS) * lng_ref[...] + lnb_ref[...]


def _combine(x1, ya, yb, row0, wexp, gate, ln_g, ln_b, tm):
    g, t, d = x1.shape
    steps = t // tm
    blk0 = row0 // tm
    y_spec = pl.BlockSpec((tm, d), lambda gi, i: (blk0 + gi * steps + i, 0))
    return pl.pallas_call(
        _combine_kernel,
        grid=(g, steps),
        in_specs=[_tok_spec(tm, d), y_spec, y_spec,
                  _tok_spec(tm, LANES), _mod_spec(gate, tm), _full_spec((1, d)), _full_spec((1, d))],
        out_specs=_tok_spec(tm, d),
        out_shape=jax.ShapeDtypeStruct((g, t, d), f32),
        compiler_params=_cparams(("parallel", "parallel")),
        name="moe_combine",
    )(x1, ya, yb, wexp, gate, ln_g, ln_b)


def _moe(groups, h2_all, ln_g, ln_b, wg, wu, wd, layer, tm):
    bm = MOE_BLOCK
    d = groups[0][0].shape[-1]
    sizes = [x1.shape[0] * x1.shape[1] for x1, _, _ in groups]
    n = sum(sizes)
    nk = 2 * n
    expert = jnp.concatenate([jnp.transpose(r[:, 0:2, :], (0, 2, 1)).reshape(-1, 2) for _, r, _ in groups],
                             axis=0).astype(jnp.int32)
    e_flat = expert.reshape(nk)
    onehot = (e_flat[:, None] == jnp.arange(N_EXPERTS, dtype=jnp.int32)[None, :]).astype(jnp.int32)
    csum = jnp.cumsum(onehot, axis=0)
    counts = csum[-1]
    rank = jnp.sum(csum * onehot, axis=1) - 1
    padded = (counts + bm - 1) // bm * bm
    pends = jnp.cumsum(padded)
    pstarts = pends - padded
    dest = jnp.sum(onehot * pstarts[None, :], axis=1) + rank
    n_blocks = nk // bm + N_EXPERTS
    n_used = (pends[-1] // bm).astype(jnp.int32)
    blk = jnp.minimum(jnp.arange(n_blocks, dtype=jnp.int32), n_used - 1)
    block_e = jnp.minimum(jnp.sum((pends[None, :] <= (blk * bm)[:, None]).astype(jnp.int32), axis=1),
                          N_EXPERTS - 1).astype(jnp.int32)
    first = jnp.concatenate([jnp.ones((1,), jnp.int32), (block_e[1:] != block_e[:-1]).astype(jnp.int32)])
    tok = jnp.arange(nk, dtype=jnp.int32) // 2
    tok_pad = jnp.zeros((n_blocks * bm,), jnp.int32).at[dest].set(tok, unique_indices=True, mode="promise_in_bounds")
    part_rows = n_blocks // GATHER_PARTS * bm
    x_parts = [jnp.take(h2_all, tok_pad[c * part_rows:(c + 1) * part_rows], axis=0, mode="clip")
               for c in range(GATHER_PARTS)]
    y = _experts(x_parts, block_e, n_used.reshape(1), first, wg, wu, wd, layer)
    dest2 = dest.reshape(n, 2)
    ya = jnp.take(y, dest2[:, 0], axis=0, mode="clip")
    yb = jnp.take(y, dest2[:, 1], axis=0, mode="clip")
    outs, off = [], 0
    for (x1, route, gate2), sz in zip(groups, sizes):
        wexp = jnp.pad(jnp.transpose(route[:, 2:4, :], (0, 2, 1)), ((0, 0), (0, 0), (0, LANES - 2)))
        outs.append(_combine(x1, ya, yb, off, wexp, gate2, ln_g, ln_b, tm))
        off += sz
    return outs


def _kv_kernel(x_ref, w_ref, g_ref, cos_ref, sin_ref, lat_ref, kr_ref, latb_ref, krb_ref):
    kva = _dot(x_ref[0].astype(bf16), w_ref[...])
    latp = kva[:, :B_KV_RANK]
    lat = latp * lax.rsqrt(jnp.mean(latp * latp, axis=-1, keepdims=True) + EPS) * g_ref[...]
    kr = kva[:, B_KV_RANK:B_KV_RANK + B_D_ROPE] * cos_ref[0] \
        + kva[:, B_KV_RANK + LANES:B_KV_RANK + LANES + B_D_ROPE] * sin_ref[0]
    lat_ref[0] = lat
    kr_ref[0] = kr
    latb_ref[0] = lat.astype(bf16)
    krb_ref[0] = kr.astype(bf16)


def _shared_kv(x, w_pad, g, cos, sin, tm):
    gg, t, d = x.shape
    rope_spec = pl.BlockSpec((1, tm, B_D_ROPE), lambda gi, i: (0, i, 0)) if cos.shape[0] == 1 else \
        _tok_spec(tm, B_D_ROPE)
    return pl.pallas_call(
        _kv_kernel,
        grid=(gg, t // tm),
        in_specs=[_tok_spec(tm, d), _full_spec(w_pad.shape), _full_spec((1, B_KV_RANK)), rope_spec, rope_spec],
        out_specs=[_tok_spec(tm, B_KV_RANK), _tok_spec(tm, B_D_ROPE),
                   _tok_spec(tm, B_KV_RANK), _tok_spec(tm, B_D_ROPE)],
        out_shape=[jax.ShapeDtypeStruct((gg, t, B_KV_RANK), f32), jax.ShapeDtypeStruct((gg, t, B_D_ROPE), f32),
                   jax.ShapeDtypeStruct((gg, t, B_KV_RANK), bf16), jax.ShapeDtypeStruct((gg, t, B_D_ROPE), bf16)],
        compiler_params=_cparams(("parallel", "parallel")),
        name="shared_kv",
    )(x, w_pad, g, cos, sin)


def _mla_q_kernel(x_ref, sc_ref, sh_ref, wdq_ref, qg_ref, wuq_ref, wuk_ref, cos_ref, sin_ref, ql_ref, qr_ref):
    h = x_ref[0] * (1.0 + sc_ref[0]) + sh_ref[0]
    cq = _dot(h.astype(bf16), wdq_ref[...])
    cq = cq * lax.rsqrt(jnp.mean(cq * cq, axis=-1, keepdims=True) + EPS) * qg_ref[...]
    q = _dot(cq.astype(bf16), wuq_ref[...])
    n_nope = HEADS * B_D_NOPE
    n_rope = HEADS * B_D_ROPE
    rot = q[:, n_nope:n_nope + n_rope] * cos_ref[0] + q[:, n_nope + n_rope:] * sin_ref[0]
    for hd in range(HEADS):
        qn = q[:, B_D_NOPE * hd:B_D_NOPE * (hd + 1)].astype(bf16)
        ql_ref[0, hd] = _dot(qn, wuk_ref[hd]).astype(bf16)
        qr_ref[0, hd] = rot[:, B_D_ROPE * hd:B_D_ROPE * (hd + 1)].astype(bf16)


def _mla_q(x, scale, shift, wdq, qg, wuq, wuk, cos8, sin8, tm):
    g, t, d = x.shape
    n_rope = HEADS * B_D_ROPE
    rope_spec = pl.BlockSpec((1, tm, n_rope), lambda gi, i: (0, i, 0)) if cos8.shape[0] == 1 else \
        _tok_spec(tm, n_rope)
    return pl.pallas_call(
        _mla_q_kernel,
        grid=(g, t // tm),
        in_specs=[_tok_spec(tm, d), _mod_spec(scale, tm), _mod_spec(shift, tm),
                  _full_spec(wdq.shape), _full_spec(qg.shape), _full_spec(wuq.shape), _full_spec(wuk.shape),
                  rope_spec, rope_spec],
        out_specs=[pl.BlockSpec((1, HEADS, tm, B_KV_RANK), lambda gi, i: (gi, 0, i, 0)),
                   pl.BlockSpec((1, HEADS, tm, B_D_ROPE), lambda gi, i: (gi, 0, i, 0))],
        out_shape=[jax.ShapeDtypeStruct((g, HEADS, t, B_KV_RANK), bf16),
                   jax.ShapeDtypeStruct((g, HEADS, t, B_D_ROPE), bf16)],
        compiler_params=_cparams(("parallel", "parallel")),
        name="mla_q",
    )(x, scale, shift, wdq, qg, wuq, wuk, cos8, sin8)


def _softmax_step(s, v, m_s, l_s, acc_s, slot):
    rows, n = s.shape
    m_old = m_s[slot]
    if n % LANES == 0:
        _softmax_steps([s], [v], m_s, l_s, acc_s, [slot])
        return
    m_new = jnp.maximum(m_old, jnp.max(s, axis=-1, keepdims=True))
    p = jnp.exp2((s - m_new[:, 0:1]) * SOFTMAX_EXP2_SCALE)
    lane = lax.broadcasted_iota(jnp.int32, (rows, LANES), 1)
    psum = jnp.where(lane == 0, jnp.sum(p, axis=-1, keepdims=True), 0.0)
    alpha = jnp.exp2((m_old - m_new) * SOFTMAX_EXP2_SCALE)
    l_s[slot] = alpha * l_s[slot] + psum
    alpha_v = jnp.concatenate([alpha] * (v.shape[1] // LANES), axis=1)
    acc_s[slot] = alpha_v * acc_s[slot] + _dot(p.astype(bf16), v)
    m_s[slot] = m_new


def _softmax_steps(s_list, v_list, m_s, l_s, acc_s, slots):
    ids = range(len(s_list))
    nv = v_list[0].shape[1] // LANES
    m_old = [m_s[slot] for slot in slots]
    chunks = [[s[:, LANES * c:LANES * (c + 1)] for c in range(s.shape[1] // LANES)] for s in s_list]
    c_max = [functools.reduce(jnp.maximum, chunks[k]) for k in ids]
    r_max = [jnp.max(c_max[k], axis=-1, keepdims=True) for k in ids]
    m_new = [jnp.maximum(m_old[k], r_max[k]) for k in ids]
    ps = [[jnp.exp2((ch - m_new[k]) * SOFTMAX_EXP2_SCALE) for ch in chunks[k]] for k in ids]
    alpha = [jnp.exp2((m_old[k] - m_new[k]) * SOFTMAX_EXP2_SCALE) for k in ids]
    pv = [_dot(jnp.concatenate(ps[k], axis=1).astype(bf16), v_list[k]) for k in ids]
    for k, slot in enumerate(slots):
        l_s[slot] = alpha[k] * l_s[slot] + functools.reduce(jnp.add, ps[k])
        acc_s[slot] = jnp.concatenate([alpha[k]] * nv, axis=1) * acc_s[slot] + pv[k]
        m_s[slot] = m_new[k]


def _flash_kernel(ql_ref, qr_ref, lat_ref, kr_ref, o_ref, m_s, l_s, acc_s, *, tq):
    i = pl.program_id(1)
    nh = FLASH_HEADS
    r_i = lax.broadcasted_iota(jnp.int32, (tq, tq), 0)
    c_i = lax.broadcasted_iota(jnp.int32, (tq, tq), 1)
    causal = c_i <= r_i

    def group_body(hg, carry):
        m_s[...] = jnp.full_like(m_s, NEG_INF)
        l_s[...] = jnp.zeros_like(l_s)
        acc_s[...] = jnp.zeros_like(acc_s)

        def block(j, masked):
            start = pl.multiple_of(j * tq, tq)
            k_lat = lat_ref[0, pl.ds(start, tq), :]
            k_r = kr_ref[0, pl.ds(start, tq), :]
            s_list = []
            for hs in range(nh):
                h = hg * nh + hs
                s = _dot_nt(ql_ref[0, h], k_lat) + _dot_nt(qr_ref[0, h], k_r)
                s_list.append(jnp.where(causal, s, NEG_INF) if masked else s)
            _softmax_steps(s_list, [k_lat] * nh, m_s, l_s, acc_s, list(range(nh)))

        def kv_body(j, c2):
            block(j, False)
            return c2

        lax.fori_loop(0, i, kv_body, 0)
        block(i, True)
        for hs in range(nh):
            l = jnp.sum(l_s[hs], axis=-1, keepdims=True)
            o_ref[0, hg * nh + hs] = (acc_s[hs] / l).astype(bf16)
        return carry

    lax.fori_loop(0, HEADS // nh, group_body, 0)


def _flash(ql, qr, latb, krb, tq):
    b, _, t, _ = ql.shape
    nh = FLASH_HEADS
    return pl.pallas_call(
        functools.partial(_flash_kernel, tq=tq),
        grid=(b, t // tq),
        in_specs=[
            pl.BlockSpec((1, HEADS, tq, B_KV_RANK), lambda g, i: (g, 0, i, 0)),
            pl.BlockSpec((1, HEADS, tq, B_D_ROPE), lambda g, i: (g, 0, i, 0)),
            pl.BlockSpec((1, t, B_KV_RANK), lambda g, i: (g, 0, 0)),
            pl.BlockSpec((1, t, B_D_ROPE), lambda g, i: (g, 0, 0)),
        ],
        out_specs=pl.BlockSpec((1, HEADS, tq, B_KV_RANK), lambda g, i: (g, 0, i, 0)),
        out_shape=jax.ShapeDtypeStruct((b, HEADS, t, B_KV_RANK), bf16),
        scratch_shapes=[pltpu.VMEM((nh, tq, LANES), f32), pltpu.VMEM((nh, tq, LANES), f32),
                        pltpu.VMEM((nh, tq, B_KV_RANK), f32)],
        compiler_params=_cparams(("parallel", "arbitrary")),
        name="mla_flash",
    )(ql, qr, latb, krb)


def _paged_kernel(pt_ref, ql_ref, qr_ref, *rest, ts, new_pad):
    np_ = PAGES_PER_STEP
    nc = PAGED_CHAINS
    per = np_ // nc
    lat_pages = rest[:np_]
    krt_pages = rest[np_:2 * np_]
    nl_ref, nk_ref, o_ref, m_s, l_s, acc_s = rest[2 * np_:]
    s_id = pl.program_id(1)
    ql = ql_ref[0]
    qr = qr_ref[0]

    @pl.when(s_id == 0)
    def _():
        m_s[...] = jnp.full_like(m_s, NEG_INF)
        l_s[...] = jnp.zeros_like(l_s)
        acc_s[...] = jnp.zeros_like(acc_s)

    k_lats = [jnp.concatenate([p[0] for p in lat_pages[c * per:(c + 1) * per]], axis=0).astype(bf16)
              for c in range(nc)]
    k_rts = [jnp.concatenate([p[0] for p in krt_pages[c * per:(c + 1) * per]], axis=1).astype(bf16)
             for c in range(nc)]
    _softmax_steps([_dot_nt(ql, k_lats[c]) + _dot(qr, k_rts[c]) for c in range(nc)], k_lats,
                   m_s, l_s, acc_s, list(range(nc)))

    @pl.when(s_id == pl.num_programs(1) - 1)
    def _():
        rows = ql.shape[0]
        n_lat = nl_ref[0]
        s_new = _dot_nt(ql, n_lat) + _dot_nt(qr, nk_ref[0])
        t_row = lax.broadcasted_iota(jnp.int32, (rows, new_pad), 0) & (ts - 1)
        c_new = lax.broadcasted_iota(jnp.int32, (rows, new_pad), 1)
        _softmax_step(jnp.where(c_new <= t_row, s_new, NEG_INF), n_lat, m_s, l_s, acc_s, 0)
        m = functools.reduce(jnp.maximum, [m_s[c] for c in range(nc)])
        nv = B_KV_RANK // LANES
        l = jnp.zeros_like(m)
        acc = jnp.zeros_like(acc_s[0])
        for c in range(nc):
            a_c = jnp.exp2((m_s[c] - m) * SOFTMAX_EXP2_SCALE)
            l = l + a_c * l_s[c]
            acc = acc + jnp.concatenate([a_c] * nv, axis=1) * acc_s[c]
        o_ref[0] = (acc / jnp.sum(l, axis=-1, keepdims=True)).astype(bf16)


def _paged_attention(page_table, ql, qr, cache_lat, cache_krt, new_lat, new_kr, ts):
    bs, rows, _ = ql.shape
    n_pages = page_table.shape[1]
    np_ = PAGES_PER_STEP
    new_pad = new_lat.shape[1]
    page_map = [functools.partial(lambda b, s, pt, r: (pt[b, s * np_ + r], 0, 0), r=r) for r in range(np_)]
    lat_specs = [pl.BlockSpec((1, PAGE, B_KV_RANK), page_map[r]) for r in range(np_)]
    kr_specs = [pl.BlockSpec((1, B_D_ROPE, PAGE), page_map[r]) for r in range(np_)]
    return pl.pallas_call(
        functools.partial(_paged_kernel, ts=ts, new_pad=new_pad),
        grid_spec=pltpu.PrefetchScalarGridSpec(
            num_scalar_prefetch=1,
            grid=(bs, n_pages // np_),
            in_specs=[pl.BlockSpec((1, rows, B_KV_RANK), lambda b, s, pt: (b, 0, 0)),
                      pl.BlockSpec((1, rows, B_D_ROPE), lambda b, s, pt: (b, 0, 0))]
            + lat_specs + kr_specs
            + [pl.BlockSpec((1, new_pad, B_KV_RANK), lambda b, s, pt: (b, 0, 0)),
               pl.BlockSpec((1, new_pad, B_D_ROPE), lambda b, s, pt: (b, 0, 0))],
            out_specs=pl.BlockSpec((1, rows, B_KV_RANK), lambda b, s, pt: (b, 0, 0)),
            scratch_shapes=[pltpu.VMEM((PAGED_CHAINS, rows, LANES), f32), pltpu.VMEM((PAGED_CHAINS, rows, LANES), f32),
                            pltpu.VMEM((PAGED_CHAINS, rows, B_KV_RANK), f32)],
        ),
        out_shape=jax.ShapeDtypeStruct((bs, rows, B_KV_RANK), bf16),
        compiler_params=_cparams(("parallel", "arbitrary")),
        name="mla_paged",
    )(page_table, ql, qr, *([cache_lat] * np_), *([cache_krt] * np_), new_lat, new_kr)


def _mla_out_kernel(ol_ref, wuv_ref, wo_ref, *refs):
    o = jnp.concatenate([_dot(ol_ref[0, hd], wuv_ref[hd]) for hd in range(HEADS)], axis=1)
    y = _dot(o.astype(bf16), wo_ref[...])
    _residual_ln_route(y, refs[:N_POST_IN], *refs[-3:])


def _mla_out(ol, wuv, wo, tm, **post):
    return _post_call(_mla_out_kernel, "mla_out", [ol, wuv, wo],
                      [pl.BlockSpec((1, HEADS, tm, B_KV_RANK), lambda gi, i: (gi, 0, i, 0)),
                       _full_spec(wuv.shape), _full_spec(wo.shape)], tm=tm, **post)


def _rope_tables(pos):
    half = B_D_ROPE // 2
    inv = jnp.power(ROPE_THETA, -jnp.arange(half, dtype=f32) / half)
    ang = pos.astype(f32)[:, None] * inv[None, :]
    cos, sin = jnp.cos(ang), jnp.sin(ang)
    return jnp.concatenate([cos, cos], axis=-1), jnp.concatenate([-sin, sin], axis=-1)


def _swap_halves(w, width):
    lead = w.shape[:-1]
    w2 = w.reshape(lead + (-1, 2, width // 2))
    return w2[..., ::-1, :].reshape(w.shape)


def kernel(x_prompt, x_sample, cache_latent, cache_krope, page_table, state_C, state_n, state_m, c_prompt, c_sample, ada_w, ada_b, ln_g, ln_b, a_w_in, a_b_gates, a_norm_g, a_w_out, b_w_kv_a, b_kv_norm_g, b_w_uk, b_w_uv, b_w_dq, b_q_norm_g, b_w_uq, b_w_o, router_w, router_b, e_w_gate, e_w_up, e_w_down):
    bp, tp, d = x_prompt.shape
    bs, ts, _ = x_sample.shape
    hqk = HEADS * A_DQK
    hv = HEADS * A_DV

    w_in = a_w_in.at[:, :, hqk:2 * hqk].multiply(A_DQK ** -0.5)
    w_in = jnp.pad(w_in, ((0, 0), (0, 0), (0, LANES - 2 * HEADS))).astype(bf16)
    bg = jnp.pad(a_b_gates, ((0, 0), (0, LANES - 2 * HEADS))).reshape(N_A_LAYERS, 1, LANES)
    w_out = a_w_out.astype(bf16)
    ng = a_norm_g.reshape(N_A_LAYERS, 1, hv)
    perm = jnp.array([(r % N_GROUPS) * PER_GROUP + r // N_GROUPS for r in range(N_EXPERTS)], jnp.int32)
    rw_perm = router_w[:, perm]
    rw_hi = rw_perm.astype(bf16)
    rw_lo = (rw_perm - rw_hi.astype(f32)).astype(bf16)
    rw = jnp.pad(jnp.concatenate([rw_hi, rw_lo], axis=1), ((0, 0), (0, LANES - 2 * N_EXPERTS)))
    rwh = jnp.pad(rw_hi, ((0, 0), (0, LANES - N_EXPERTS)))
    rb = router_b[perm].reshape(N_EXPERTS, 1)
    w_lat = b_w_kv_a[:, :B_KV_RANK]
    w_kr = b_w_kv_a[:, B_KV_RANK:]
    zpad = jnp.zeros((d, LANES - B_D_ROPE), f32)
    w_kv = jnp.concatenate([w_lat, w_kr, zpad, _swap_halves(w_kr, B_D_ROPE), zpad], axis=1).astype(bf16)
    kvg = b_kv_norm_g.reshape(1, B_KV_RANK)
    wdq = b_w_dq.astype(bf16)
    qg = b_q_norm_g.reshape(-1, 1, b_q_norm_g.shape[-1])
    uq = b_w_uq.reshape(b_w_uq.shape[0], b_w_uq.shape[1], HEADS, B_D_NOPE + B_D_ROPE)
    uq_nope = uq[..., :B_D_NOPE].reshape(uq.shape[0], uq.shape[1], HEADS * B_D_NOPE)
    uq_rope = uq[..., B_D_NOPE:].reshape(uq.shape[0], uq.shape[1], HEADS * B_D_ROPE)
    wuq = jnp.concatenate([uq_nope, uq_rope, _swap_halves(uq_rope, B_D_ROPE)], axis=-1).astype(bf16)
    wuk = jnp.transpose(b_w_uk, (1, 2, 0)).astype(bf16)
    wuv = jnp.transpose(b_w_uv, (1, 0, 2)).astype(bf16)
    wo = b_w_o.astype(bf16)

    mods = _ada_mods(jnp.concatenate([c_prompt, c_sample], axis=0), ada_w, ada_b)

    cos_p, sin_p = _rope_tables(jnp.arange(tp, dtype=jnp.int32))
    past_len = page_table.shape[1] * PAGE
    cos_s, sin_s = _rope_tables(jnp.tile(past_len + jnp.arange(ts, dtype=jnp.int32), bs))

    tm = 512
    cache_krt = jnp.swapaxes(cache_krope, 1, 2)
    streams = []
    for x0, mods_g, cos, sin, sample in ((x_prompt, mods[:, :bp], cos_p, sin_p, False),
                                         (x_sample.reshape(1, bs * ts, d), mods[:, bp:], cos_s, sin_s, True)):
        streams.append(dict(x=x0, mods=mods_g, sample=sample, cos1=cos[None], sin1=sin[None],
                            cos8=jnp.tile(cos, (1, HEADS))[None], sin8=jnp.tile(sin, (1, HEADS))[None],
                            new_c=[], new_n=[], new_m=[]))

    for l in range(DEPTH):
        lg = ln_g[l].reshape(2, 1, d)
        lb = ln_b[l].reshape(2, 1, d)
        moe_in = []
        h2_all, row0 = None, 0
        n_all = sum(st["x"].shape[0] * st["x"].shape[1] for st in streams)
        for st in streams:
            x, sample = st["x"], st["sample"]
            cos8, sin8 = st["cos8"], st["sin8"]
            latb, krb = st.get("latb"), st.get("krb")
            new_c, new_n, new_m = st["new_c"], st["new_n"], st["new_m"]
            m6 = st["mods"][l].reshape(st["mods"].shape[1], 6, d)
            if sample:
                parts = [jnp.repeat(m6[:, i], ts, axis=0)[None] for i in range(6)]
            else:
                parts = [m6[:, i][:, None, :] for i in range(6)]
            shift1, scale1, gate1, shift2, scale2, gate2 = parts
            post = dict(x=x, gate=gate1, ln_g=lg[0], ln_b=lb[0], scale2=scale2, shift2=shift2, rw=rw, rwh=rwh, rb=rb,
                        h2_rows=n_all, h2_row0=row0, h2_prev=h2_all)
            if l < N_A_LAYERS:
                qkv, o, gates = _mlstm_in(x, scale1, shift1, w_in[l], bg[l], tm)
                kt = jnp.swapaxes(qkv[:, :, hqk:2 * hqk], 1, 2)
                grow = jnp.swapaxes(gates[:, :, :2 * HEADS], 1, 2)
                if sample:
                    n0 = state_n[l].reshape(bs, hqk)
                    m0 = state_m[l]
                    nrows = jnp.repeat(n0, ts, axis=0)[None]
                    m_tok = jnp.repeat(m0, ts, axis=0)
                    mcol = jnp.pad(m_tok, ((0, 0), (0, LANES - HEADS)))[None]
                    mrow = m_tok.T[None]
                    u, c_stack, n_new, m_new = _mlstm_sample(qkv, kt, o, gates, grow, ng[l], state_C, l,
                                                             st.get("c_stack"), n0, nrows, m0, mcol, mrow, ts)
                    st["c_stack"] = c_stack
                    n_new = n_new.reshape(bs, HEADS, A_DQK)
                else:
                    u, c_new, n_rep, m_rep = _mlstm_prompt(qkv, kt, o, gates, grow, ng[l])
                    n_new = n_rep[..., 0]
                    m_new = m_rep[..., 0]
                    new_c.append(c_new)
                new_n.append(n_new)
                new_m.append(m_new)
                x1, h2_all, route = _mlstm_out(u, w_out[l], tm, **post)
            else:
                j = l - N_A_LAYERS
                ql, qr = _mla_q(x, scale1, shift1, wdq[j], qg[j], wuq[j], wuk, cos8, sin8, tm)
                if sample:
                    ql_s = jnp.transpose(ql.reshape(HEADS, bs, ts, B_KV_RANK), (1, 0, 2, 3)).reshape(
                        bs, HEADS * ts, B_KV_RANK)
                    qr_s = jnp.transpose(qr.reshape(HEADS, bs, ts, B_D_ROPE), (1, 0, 2, 3)).reshape(
                        bs, HEADS * ts, B_D_ROPE)
                    new_pad = 16
                    nl = jnp.pad(latb.reshape(bs, ts, B_KV_RANK), ((0, 0), (0, new_pad - ts), (0, 0)))
                    nk = jnp.pad(krb.reshape(bs, ts, B_D_ROPE), ((0, 0), (0, new_pad - ts), (0, 0)))
                    ol = _paged_attention(page_table, ql_s, qr_s, cache_latent, cache_krt, nl, nk, ts)
                    ol = jnp.transpose(ol.reshape(bs, HEADS, ts, B_KV_RANK), (1, 0, 2, 3)).reshape(
                        1, HEADS, bs * ts, B_KV_RANK)
                else:
                    ol = _flash(ql, qr, latb, krb, 512)
                x1, h2_all, route = _mla_out(ol, wuv, wo[j], tm, **post)
            moe_in.append((x1, route, gate2))
            row0 += x.shape[0] * x.shape[1]
        for st, x_new in zip(streams, _moe(moe_in, h2_all, lg[1], lb[1], e_w_gate, e_w_up, e_w_down, l, tm)):
            st["x"] = x_new
            if l == N_A_LAYERS - 1:
                st["lat"], st["kr"], st["latb"], st["krb"] = _shared_kv(x_new, w_kv, kvg, st["cos1"], st["sin1"], tm)

    sp, ss = streams
    return (sp["x"], ss["x"].reshape(bs, ts, d),
            jnp.stack(sp["new_c"]), jnp.stack(sp["new_n"]), jnp.stack(sp["new_m"]), sp["lat"], sp["kr"],
            ss["c_stack"], jnp.stack(ss["new_n"]), jnp.stack(ss["new_m"]),
            ss["lat"].reshape(bs, ts, B_KV_RANK), ss["kr"].reshape(bs, ts, B_D_ROPE))
```

```python
import functools

import jax
import jax.numpy as jnp
from jax import lax
from jax.experimental import pallas as pl
from jax.experimental.pallas import tpu as pltpu

f32 = jnp.float32
bf16 = jnp.bfloat16

DEPTH = 4
N_A_LAYERS = 2
HEADS = 8
A_DQK = 64
A_DV = 128
B_D_NOPE = 128
B_D_ROPE = 64
B_KV_RANK = 256
ROPE_THETA = 10000.0
ATTN_SCALE = (B_D_NOPE + B_D_ROPE) ** -0.5
N_EXPERTS = 16
N_GROUPS = 4
PER_GROUP = 4
PAGE = 128
ALPHA = (2 * DEPTH) ** 0.25
EPS = 1e-6
NEG_INF = float("-inf")
LOG2E = 1.4426950408889634
SOFTMAX_EXP2_SCALE = ATTN_SCALE * LOG2E

VMEM_LIMIT_BYTES = 56 * 1024 * 1024
LANES = 128
MLSTM_CHUNK = 128
SAMPLE_GROUP = 16
MOE_BLOCK = 512
GATHER_PARTS = 4
PAGES_PER_STEP = 16
PAGED_CHAINS = 2
FLASH_HEADS = 8


def _cparams(sem):
    return pltpu.CompilerParams(dimension_semantics=sem, vmem_limit_bytes=VMEM_LIMIT_BYTES)


def _dot(a, b):
    return jnp.dot(a, b, preferred_element_type=f32)


def _dot_nt(a, b):
    return lax.dot_general(a, b, (((1,), (1,)), ((), ())), preferred_element_type=f32)


def _dot_exact(a, b):
    return jnp.dot(a, b, preferred_element_type=f32, precision=lax.Precision.HIGHEST)


def _sigmoid(x):
    return 1.0 / (1.0 + jnp.exp(-x))


def _log_sigmoid(x):
    return jnp.minimum(x, 0.0) - jnp.log(1.0 + jnp.exp(-jnp.abs(x)))


def _ada_kernel(c_ref, w_ref, b_ref, o_ref):
    c = c_ref[...]
    sc = (c * _sigmoid(c)).astype(bf16)
    o_ref[0] = _dot(sc, w_ref[0].astype(bf16)) + b_ref[0]


def _ada_mods(c_all, ada_w, ada_b):
    depth, d, e = ada_w.shape
    bc = c_all.shape[0]
    tn = 1536
    return pl.pallas_call(
        _ada_kernel,
        grid=(depth, e // tn),
        in_specs=[
            pl.BlockSpec((bc, d), lambda l, j: (0, 0)),
            pl.BlockSpec((1, d, tn), lambda l, j: (l, 0, j)),
            pl.BlockSpec((1, 1, tn), lambda l, j: (l, 0, j)),
        ],
        out_specs=pl.BlockSpec((1, bc, tn), lambda l, j: (l, 0, j)),
        out_shape=jax.ShapeDtypeStruct((depth, bc, e), f32),
        compiler_params=_cparams(("parallel", "parallel")),
        name="ada_mods",
    )(c_all, ada_w, ada_b.reshape(depth, 1, e))


N_MODS = 6


def _mod_spec(mod, tm):
    mods, layer, chunk = mod
    d = mods.shape[-1] // N_MODS
    if mods.shape[2] == 1:
        return pl.BlockSpec((1, 1, 1, d), lambda g, i: (layer, g, 0, chunk))
    return pl.BlockSpec((1, 1, tm, d), lambda g, i: (layer, g, i, chunk))


def _tok_spec(tm, d):
    return pl.BlockSpec((1, tm, d), lambda g, i: (g, i, 0))


def _full_spec(shape):
    nd = len(shape)
    return pl.BlockSpec(shape, lambda g, i: (0,) * nd)


def _route(h2, rw_ref, rwh_ref, rb_ref):
    hi = h2.astype(bf16)
    lo = (h2 - hi.astype(f32)).astype(bf16)
    p = _dot(hi, rw_ref[...]) + _dot(lo, rwh_ref[...])
    pt = p.T
    logits = pt[0:N_EXPERTS] + pt[N_EXPERTS:2 * N_EXPERTS]
    sc = _sigmoid(logits)
    sel = sc + rb_ref[...]
    a = [sel[PER_GROUP * j:PER_GROUP * (j + 1)] for j in range(PER_GROUP)]
    s = [sc[PER_GROUP * j:PER_GROUP * (j + 1)] for j in range(PER_GROUP)]
    hi01, lo01 = jnp.maximum(a[0], a[1]), jnp.minimum(a[0], a[1])
    hi23, lo23 = jnp.maximum(a[2], a[3]), jnp.minimum(a[2], a[3])
    gs = jnp.maximum(hi01, hi23) + jnp.maximum(jnp.minimum(hi01, hi23), jnp.maximum(lo01, lo23))
    best = gs[0:1]
    grp = jnp.zeros_like(best)
    for g in range(1, N_GROUPS):
        better = gs[g:g + 1] > best
        grp = jnp.where(better, float(g), grp)
        best = jnp.where(better, gs[g:g + 1], best)
    mv, sv = [], []
    for j in range(PER_GROUP):
        m_j = a[j][0:1]
        s_j = s[j][0:1]
        for g in range(1, N_GROUPS):
            m_j = jnp.where(grp == float(g), a[j][g:g + 1], m_j)
            s_j = jnp.where(grp == float(g), s[j][g:g + 1], s_j)
        mv.append(m_j)
        sv.append(s_j)

    def first_argmax(vals):
        bv, bi = vals[0], jnp.zeros_like(vals[0])
        for j in range(1, PER_GROUP):
            better = vals[j] > bv
            bi = jnp.where(better, float(j), bi)
            bv = jnp.where(better, vals[j], bv)
        return bi

    i1 = first_argmax(mv)
    i2 = first_argmax([jnp.where(i1 == float(j), NEG_INF, mv[j]) for j in range(PER_GROUP)])
    w1, w2 = sv[0], sv[0]
    for j in range(1, PER_GROUP):
        w1 = jnp.where(i1 == float(j), sv[j], w1)
        w2 = jnp.where(i2 == float(j), sv[j], w2)
    tot = w1 + w2
    e1 = grp * float(PER_GROUP) + i1
    e2 = grp * float(PER_GROUP) + i2
    rows = jnp.concatenate([e1, e2, w1 / tot, w2 / tot, jnp.zeros((LANES - 4, e1.shape[1]), f32)], axis=0)
    return rows.T


N_POST_IN = 9


def _residual_ln_route(y, post_refs, x1_ref, h2_ref, route_ref):
    x_ref, gate_ref, lng_ref, lnb_ref, sc_ref, sh_ref, rw_ref, rwh_ref, rb_ref = post_refs
    r = ALPHA * x_ref[0] + gate_ref[0, 0] * y
    mu = jnp.mean(r, axis=-1, keepdims=True)
    cen = r - mu
    var = jnp.mean(cen * cen, axis=-1, keepdims=True)
    x1 = cen * lax.rsqrt(var + EPS) * lng_ref[...] + lnb_ref[...]
    x1_ref[0] = x1
    h2 = x1 * (1.0 + sc_ref[0, 0]) + sh_ref[0, 0]
    h2_ref[...] = h2
    route_ref[0] = _route(h2, rw_ref, rwh_ref, rb_ref)


def _post_call(kernel_fn, name, mixer_args, mixer_specs, x, gate, ln_g, ln_b, scale2, shift2, rw, rwh, rb, tm,
               h2_rows, h2_row0, h2_prev):
    g, t, d = x.shape
    steps = t // tm
    blk0 = h2_row0 // tm
    in_specs = list(mixer_specs) + [
        _tok_spec(tm, d), _mod_spec(gate, tm), _full_spec((1, d)), _full_spec((1, d)),
        _mod_spec(scale2, tm), _mod_spec(shift2, tm),
        _full_spec((d, LANES)), _full_spec((d, LANES)), _full_spec((N_EXPERTS, 1)),
    ]
    args = list(mixer_args) + [x, gate[0], ln_g, ln_b, scale2[0], shift2[0], rw, rwh, rb]
    aliases = {}
    if h2_prev is not None:
        aliases = {len(args): 1}
        in_specs.append(pl.BlockSpec(memory_space=pl.ANY))
        args.append(h2_prev)
    return pl.pallas_call(
        kernel_fn,
        grid=(g, steps),
        in_specs=in_specs,
        out_specs=[_tok_spec(tm, d),
                   pl.BlockSpec((tm, d), lambda gi, i: (blk0 + gi * steps + i, 0)),
                   _tok_spec(tm, LANES)],
        out_shape=[jax.ShapeDtypeStruct((g, t, d), f32),
                   jax.ShapeDtypeStruct((h2_rows, d), f32),
                   jax.ShapeDtypeStruct((g, t, LANES), f32)],
        input_output_aliases=aliases,
        compiler_params=_cparams(("parallel", "parallel")),
        name=name,
    )(*args)


def _mlstm_in_kernel(x_ref, sc_ref, sh_ref, w_ref, bg_ref, qkv_ref, o_ref, g_ref):
    h = x_ref[0] * (1.0 + sc_ref[0, 0]) + sh_ref[0, 0]
    z = _dot(h.astype(bf16), w_ref[...])
    hqk2 = 2 * HEADS * A_DQK
    hv = HEADS * A_DV
    qkv_ref[0] = z[:, :hqk2 + hv].astype(bf16)
    o_ref[0] = z[:, hqk2 + hv:hqk2 + 2 * hv]
    g_ref[0] = z[:, hqk2 + 2 * hv:] + bg_ref[...]


def _mlstm_in(x, scale, shift, w_pad, bg_pad, tm):
    g, t, d = x.shape
    n_qkv = 2 * HEADS * A_DQK + HEADS * A_DV
    hv = HEADS * A_DV
    return pl.pallas_call(
        _mlstm_in_kernel,
        grid=(g, t // tm),
        in_specs=[_tok_spec(tm, d), _mod_spec(scale, tm), _mod_spec(shift, tm),
                  _full_spec(w_pad.shape), _full_spec((1, LANES))],
        out_specs=[_tok_spec(tm, n_qkv), _tok_spec(tm, hv), _tok_spec(tm, LANES)],
        out_shape=[jax.ShapeDtypeStruct((g, t, n_qkv), bf16),
                   jax.ShapeDtypeStruct((g, t, hv), f32),
                   jax.ShapeDtypeStruct((g, t, LANES), f32)],
        compiler_params=_cparams(("parallel", "parallel")),
        name="mlstm_in",
    )(x, scale[0], shift[0], w_pad, bg_pad)


def _mlstm_out_kernel(u_ref, w_ref, *refs):
    y = _dot(u_ref[0], w_ref[...])
    _residual_ln_route(y, refs[:N_POST_IN], *refs[-3:])


def _mlstm_out(u, w_out, tm, **post):
    return _post_call(_mlstm_out_kernel, "mlstm_out", [u, w_out],
                      [_tok_spec(tm, u.shape[-1]), _full_spec(w_out.shape)], tm=tm, **post)


def _head_norm_gate(hh, ng, o):
    mu = jnp.mean(hh, axis=-1, keepdims=True)
    cen = hh - mu
    var = jnp.mean(cen * cen, axis=-1, keepdims=True)
    return (cen * lax.rsqrt(var + EPS) * ng) * _sigmoid(o)


def _mlstm_prompt_kernel(qkv_ref, kt_ref, o_ref, gcol_ref, grow_ref, ng_ref,
                         u_ref, c_out, n_out, m_out, c_s, n_s, m_s):
    c = pl.program_id(1)
    nc = pl.num_programs(1)
    L = qkv_ref.shape[1]
    hqk = HEADS * A_DQK

    @pl.when(c == 0)
    def _():
        c_s[...] = jnp.zeros_like(c_s)
        n_s[...] = jnp.zeros_like(n_s)
        m_s[...] = jnp.zeros_like(m_s)

    r_i = lax.broadcasted_iota(jnp.int32, (L, L), 0)
    c_i = lax.broadcasted_iota(jnp.int32, (L, L), 1)
    causal = c_i <= r_i
    tri = causal.astype(f32)
    tri_t = (r_i <= c_i).astype(f32)

    gcol = gcol_ref[0]
    grow = grow_ref[0]
    b_col = _dot_exact(tri, _log_sigmoid(gcol))
    b_row = _dot_exact(_log_sigmoid(grow[HEADS:2 * HEADS]), tri_t)
    li_row = grow[0:HEADS]

    qkv = qkv_ref[0]
    kt = kt_ref[0]
    o_all = o_ref[0]
    ng = ng_ref[...]
    c_old, n_old, m_old = c_s[...], n_s[...], m_s[...]
    hs = range(HEADS)
    q = [qkv[:, A_DQK * h:A_DQK * (h + 1)] for h in hs]
    v = [qkv[:, 2 * hqk + A_DV * h:2 * hqk + A_DV * (h + 1)] for h in hs]
    kth = [kt[A_DQK * h:A_DQK * (h + 1), :] for h in hs]
    s_qk = [_dot(q[h], kth[h]) for h in hs]
    bc = [b_col[:, HEADS + h:HEADS + h + 1] for h in hs]
    br = [b_row[h:h + 1] for h in hs]
    lir = [li_row[h:h + 1] for h in hs]
    m_prev = [m_old[h:h + 1, 0:1] for h in hs]
    d = [jnp.where(causal, bc[h] - br[h] + lir[h], NEG_INF) for h in hs]
    a = [bc[h] + m_prev[h] for h in hs]
    d_max = [jnp.max(d[h], axis=-1, keepdims=True) for h in hs]
    m_t = [jnp.maximum(a[h], d_max[h]) for h in hs]
    p = [s_qk[h] * jnp.exp(d[h] - m_t[h]) for h in hs]
    inter = [jnp.exp(a[h] - m_t[h]) for h in hs]
    qc = [_dot(q[h], jnp.concatenate([c_old[h], n_old[h]], axis=1).astype(bf16)) for h in hs]
    p_sum = [jnp.sum(p[h], axis=-1, keepdims=True) for h in hs]
    num = [_dot(p[h].astype(bf16), v[h]) + inter[h] * qc[h][:, :A_DV] for h in hs]
    den = [p_sum[h] + inter[h] * qc[h][:, A_DV:A_DV + 1] for h in hs]
    hh = [num[h] / jnp.maximum(jnp.abs(den[h]), jnp.exp(-m_t[h])) for h in hs]
    mu = [jnp.mean(hh[h], axis=-1, keepdims=True) for h in hs]
    cen = [hh[h] - mu[h] for h in hs]
    var = [jnp.mean(cen[h] * cen[h], axis=-1, keepdims=True) for h in hs]
    u_parts = [((cen[h] * lax.rsqrt(var[h] + EPS) * ng[:, A_DV * h:A_DV * (h + 1)])
                * _sigmoid(o_all[:, A_DV * h:A_DV * (h + 1)])).astype(bf16) for h in hs]

    b_end = [br[h][:, L - 1:L] for h in hs]
    g_row = [b_end[h] - br[h] + lir[h] for h in hs]
    g_max = [jnp.max(g_row[h], axis=-1, keepdims=True) for h in hs]
    m_new = [jnp.maximum(b_end[h] + m_prev[h], g_max[h]) for h in hs]
    decay = [jnp.exp(b_end[h] + m_prev[h] - m_new[h]) for h in hs]
    kw = [kth[h].astype(f32) * jnp.exp(g_row[h] - m_new[h]) for h in hs]
    kw_sum = [jnp.sum(kw[h], axis=-1, keepdims=True) for h in hs]
    c_parts = [decay[h] * c_old[h] + _dot(kw[h].astype(bf16), v[h]) for h in hs]
    n_parts = [decay[h] * n_old[h] + kw_sum[h] for h in hs]
    m_parts = [jnp.broadcast_to(m_new[h], (1, LANES)) for h in hs]

    u_ref[0] = jnp.concatenate(u_parts, axis=1)
    c_s[...] = jnp.stack(c_parts)
    n_s[...] = jnp.stack(n_parts)
    m_s[...] = jnp.concatenate(m_parts, axis=0)

    @pl.when(c == nc - 1)
    def _():
        c_out[0] = c_s[...]
        n_out[0] = n_s[...]
        m_out[0] = m_s[...]


def _mlstm_prompt(qkv, kt, o, gcol, grow, ng):
    b, t, _ = qkv.shape
    L = MLSTM_CHUNK
    hv = HEADS * A_DV
    return pl.pallas_call(
        _mlstm_prompt_kernel,
        grid=(b, t // L),
        in_specs=[
            pl.BlockSpec((1, L, qkv.shape[-1]), lambda g, c: (g, c, 0)),
            pl.BlockSpec((1, HEADS * A_DQK, L), lambda g, c: (g, 0, c)),
            pl.BlockSpec((1, L, hv), lambda g, c: (g, c, 0)),
            pl.BlockSpec((1, L, LANES), lambda g, c: (g, c, 0)),
            pl.BlockSpec((1, 2 * HEADS, L), lambda g, c: (g, 0, c)),
            pl.BlockSpec((1, hv), lambda g, c: (0, 0)),
        ],
        out_specs=[
            pl.BlockSpec((1, L, hv), lambda g, c: (g, c, 0)),
            pl.BlockSpec((1, HEADS, A_DQK, A_DV), lambda g, c: (g, 0, 0, 0)),
            pl.BlockSpec((1, HEADS, A_DQK, LANES), lambda g, c: (g, 0, 0, 0)),
            pl.BlockSpec((1, HEADS, LANES), lambda g, c: (g, 0, 0)),
        ],
        out_shape=[
            jax.ShapeDtypeStruct((b, t, hv), bf16),
            jax.ShapeDtypeStruct((b, HEADS, A_DQK, A_DV), f32),
            jax.ShapeDtypeStruct((b, HEADS, A_DQK, LANES), f32),
            jax.ShapeDtypeStruct((b, HEADS, LANES), f32),
        ],
        scratch_shapes=[
            pltpu.VMEM((HEADS, A_DQK, A_DV), f32),
            pltpu.VMEM((HEADS, A_DQK, LANES), f32),
            pltpu.VMEM((HEADS, LANES), f32),
        ],
        compiler_params=_cparams(("parallel", "arbitrary")),
        name="mlstm_prompt",
    )(qkv, kt, o, gcol, grow, ng)


def _mlstm_sample_kernel(qkv_ref, kt_ref, o_ref, gcol_ref, grow_ref, ng_ref, c0_ref, n0_ref, nrows_ref,
                         m0_ref, mcol_ref, mrow_ref, *refs, ts):
    u_ref, c_out, n_out, m_out = refs[-4:]
    R = qkv_ref.shape[1]
    G = R // ts
    shift = ts.bit_length() - 1
    hqk = HEADS * A_DQK

    r_i = lax.broadcasted_iota(jnp.int32, (R, R), 0)
    c_i = lax.broadcasted_iota(jnp.int32, (R, R), 1)
    same = (r_i >> shift) == (c_i >> shift)
    valid = same & (c_i <= r_i)
    tri = valid.astype(f32)
    tri_t = (same & (r_i <= c_i)).astype(f32)
    same_f = same.astype(f32)
    row_seq = lax.broadcasted_iota(jnp.int32, (R, 1), 0) >> shift
    g_i = lax.broadcasted_iota(jnp.int32, (G, R), 0)
    s_i = lax.broadcasted_iota(jnp.int32, (G, R), 1)
    bmask = (s_i >> shift) == g_i
    lastmask = s_i == g_i * ts + (ts - 1)
    g3 = lax.broadcasted_iota(jnp.int32, (G, A_DQK, R), 0)
    s3 = lax.broadcasted_iota(jnp.int32, (G, A_DQK, R), 2)
    bmask3 = (s3 >> shift) == g3
    lane_i = lax.broadcasted_iota(jnp.int32, (G, LANES), 1)

    gcol = gcol_ref[0]
    grow = grow_ref[0]
    lf_row = _log_sigmoid(grow[HEADS:2 * HEADS])
    b_col = _dot_exact(tri, _log_sigmoid(gcol))
    b_row = _dot_exact(lf_row, tri_t)
    bend_row = _dot_exact(lf_row, same_f)
    li_row = grow[0:HEADS]
    mcol = mcol_ref[0]
    mrow = mrow_ref[0]
    m0 = m0_ref[...]

    m_acc = jnp.zeros((G, LANES), f32)
    n_parts = []
    for h in range(HEADS):
        q_h = qkv_ref[0, :, A_DQK * h:A_DQK * (h + 1)]
        k_h = qkv_ref[0, :, hqk + A_DQK * h:hqk + A_DQK * (h + 1)]
        v_h = qkv_ref[0, :, 2 * hqk + A_DV * h:2 * hqk + A_DV * (h + 1)]
        kt_h = kt_ref[0, A_DQK * h:A_DQK * (h + 1), :]
        s_qk = _dot(q_h, kt_h)
        bc = b_col[:, HEADS + h:HEADS + h + 1]
        br = b_row[h:h + 1]
        lir = li_row[h:h + 1]
        d = jnp.where(valid, bc - br + lir, NEG_INF)
        a = bc + mcol[:, h:h + 1]
        m_t = jnp.maximum(a, jnp.max(d, axis=-1, keepdims=True))
        p = s_qk * jnp.exp(d - m_t)
        inter = jnp.exp(a - m_t)
        c_all = jnp.concatenate([c0_ref[0, g, h] for g in range(G)], axis=1).astype(bf16)
        qc = _dot(q_h, c_all)
        inter_c = jnp.zeros((R, A_DV), f32)
        for g in range(G):
            inter_c = jnp.where(row_seq == g, qc[:, A_DV * g:A_DV * (g + 1)], inter_c)
        qn = jnp.sum(q_h.astype(f32) * nrows_ref[0, :, A_DQK * h:A_DQK * (h + 1)], axis=-1, keepdims=True)
        num = _dot(p.astype(bf16), v_h) + inter * inter_c
        den = jnp.sum(p, axis=-1, keepdims=True) + inter * qn
        hh = num / jnp.maximum(jnp.abs(den), jnp.exp(-m_t))
        u = _head_norm_gate(hh, ng_ref[:, A_DV * h:A_DV * (h + 1)], o_ref[0, :, A_DV * h:A_DV * (h + 1)])
        u_ref[0, :, A_DV * h:A_DV * (h + 1)] = u.astype(bf16)

        g_row = bend_row[h:h + 1] - br + lir
        gmax = jnp.max(jnp.where(bmask, g_row, NEG_INF), axis=-1, keepdims=True)
        bend_b = jnp.sum(jnp.where(lastmask, br, 0.0), axis=-1, keepdims=True)
        m0_h = m0[:, h:h + 1]
        mnew_b = jnp.maximum(bend_b + m0_h, gmax)
        mnew_row = jnp.sum(jnp.where(bmask, mnew_b, 0.0), axis=0, keepdims=True)
        w_row = jnp.exp(g_row - mnew_row)
        decay_b = jnp.exp(bend_b + m0_h - mnew_b)
        kw = kt_h.astype(f32) * w_row
        kw3 = jnp.where(bmask3, jnp.broadcast_to(kw[None], (G, A_DQK, R)), 0.0)
        upd = _dot(kw3.reshape(G * A_DQK, R).astype(bf16), v_h)
        for g in range(G):
            c_out[0, g, h] = decay_b[g:g + 1, :] * c0_ref[0, g, h] + upd[A_DQK * g:A_DQK * (g + 1)]
        wm = jnp.where(bmask, w_row, 0.0)
        n_parts.append(decay_b * n0_ref[:, A_DQK * h:A_DQK * (h + 1)] + _dot(wm.astype(bf16), k_h))
        m_acc = jnp.where(lane_i == h, mnew_b, m_acc)

    n_out[...] = jnp.concatenate(n_parts, axis=1)
    m_out[...] = m_acc[:, :HEADS]


def _mlstm_sample(qkv, kt, o, gcol, grow, ng, c0_all, layer, c_prev, n0, nrows, m0, mcol, mrow, ts):
    _, t, _ = qkv.shape
    n_layers, bs = c0_all.shape[:2]
    G = SAMPLE_GROUP
    R = G * ts
    hv = HEADS * A_DV
    hqk = HEADS * A_DQK
    args = [qkv, kt, o, gcol, grow, ng, c0_all, n0, nrows, m0, mcol, mrow]
    extra_specs, aliases = [], {}
    if c_prev is not None:
        aliases = {len(args): 1}
        extra_specs = [pl.BlockSpec(memory_space=pl.ANY)]
        args.append(c_prev)
    return pl.pallas_call(
        functools.partial(_mlstm_sample_kernel, ts=ts),
        grid=(t // R,),
        input_output_aliases=aliases,
        in_specs=[
            pl.BlockSpec((1, R, qkv.shape[-1]), lambda i: (0, i, 0)),
            pl.BlockSpec((1, hqk, R), lambda i: (0, 0, i)),
            pl.BlockSpec((1, R, hv), lambda i: (0, i, 0)),
            pl.BlockSpec((1, R, LANES), lambda i: (0, i, 0)),
            pl.BlockSpec((1, 2 * HEADS, R), lambda i: (0, 0, i)),
            pl.BlockSpec((1, hv), lambda i: (0, 0)),
            pl.BlockSpec((1, G, HEADS, A_DQK, A_DV), lambda i: (layer, i, 0, 0, 0)),
            pl.BlockSpec((G, hqk), lambda i: (i, 0)),
            pl.BlockSpec((1, R, hqk), lambda i: (0, i, 0)),
            pl.BlockSpec((G, HEADS), lambda i: (i, 0)),
            pl.BlockSpec((1, R, LANES), lambda i: (0, i, 0)),
            pl.BlockSpec((1, HEADS, R), lambda i: (0, 0, i)),
        ] + extra_specs,
        out_specs=[
            pl.BlockSpec((1, R, hv), lambda i: (0, i, 0)),
            pl.BlockSpec((1, G, HEADS, A_DQK, A_DV), lambda i: (layer, i, 0, 0, 0)),
            pl.BlockSpec((G, hqk), lambda i: (i, 0)),
            pl.BlockSpec((G, HEADS), lambda i: (i, 0)),
        ],
        out_shape=[
            jax.ShapeDtypeStruct((1, t, hv), bf16),
            jax.ShapeDtypeStruct((n_layers, bs, HEADS, A_DQK, A_DV), f32),
            jax.ShapeDtypeStruct((bs, hqk), f32),
            jax.ShapeDtypeStruct((bs, HEADS), f32),
        ],
        compiler_params=_cparams(("parallel",)),
        name="mlstm_sample",
    )(*args)


def _expert_kernel(be_ref, nu_ref, first_ref, *refs, n_parts):
    x_refs = refs[:n_parts]
    wg_ref, wu_ref, wd_ref, y_ref, wg_s, wu_s, wd_s = refs[n_parts:]
    i = pl.program_id(0)
    part_blocks = pl.num_programs(0) // n_parts

    @pl.when(first_ref[i] == 1)
    def _():
        wg_s[...] = wg_ref[0, 0].astype(bf16)
        wu_s[...] = wu_ref[0, 0].astype(bf16)
        wd_s[...] = wd_ref[0, 0].astype(bf16)

    @pl.when(i < nu_ref[0])
    def _():
        x = x_refs[n_parts - 1][...]
        for part in range(n_parts - 2, -1, -1):
            x = jnp.where(i < (part + 1) * part_blocks, x_refs[part][...], x)
        x = x.astype(bf16)
        g = _dot(x, wg_s[...])
        u = _dot(x, wu_s[...])
        hid = (g * _sigmoid(g)) * u
        y_ref[...] = _dot(hid.astype(bf16), wd_s[...])

    @pl.when(i >= nu_ref[0])
    def _():
        y_ref[...] = jnp.zeros_like(y_ref)


def _experts(x_parts, block_e, n_used, first, wg, wu, wd, layer):
    n_parts = len(x_parts)
    pp, d = x_parts[0].shape
    de = wg.shape[-1]
    bm = MOE_BLOCK
    pb = pp // bm
    w_map = lambda i, be, nu, fi: (layer, be[i], 0, 0)
    part_specs = [pl.BlockSpec((bm, d), functools.partial(
        lambda i, be, nu, fi, part: (jnp.clip(i - part * pb, 0, pb - 1), 0), part=part)) for part in range(n_parts)]
    return pl.pallas_call(
        functools.partial(_expert_kernel, n_parts=n_parts),
        grid_spec=pltpu.PrefetchScalarGridSpec(
            num_scalar_prefetch=3,
            grid=(n_parts * pb,),
            in_specs=part_specs + [
                pl.BlockSpec((1, 1, d, de), w_map),
                pl.BlockSpec((1, 1, d, de), w_map),
                pl.BlockSpec((1, 1, de, d), w_map),
            ],
            out_specs=pl.BlockSpec((bm, d), lambda i, be, nu, fi: (i, 0)),
            scratch_shapes=[pltpu.VMEM((d, de), bf16), pltpu.VMEM((d, de), bf16), pltpu.VMEM((de, d), bf16)],
        ),
        out_shape=jax.ShapeDtypeStruct((n_parts * pp, d), f32),
        compiler_params=_cparams(("arbitrary",)),
        name="moe_experts",
    )(block_e, n_used, first, *x_parts, wg, wu, wd)


def _combine_kernel(x_ref, ya_ref, yb_ref, w_ref, gate_ref, lng_ref, lnb_ref, o_ref):
    w = w_ref[0]
    y = w[:, 2:3] * ya_ref[...] + w[:, 3:4] * yb_ref[...]
    r = ALPHA * x_ref[0] + gate_ref[0, 0] * y
    mu = jnp.mean(r, axis=-1, keepdims=True)
    cen = r - mu
    var = jnp.mean(cen * cen, axis=-1, keepdims=True)
    o_ref[0] = cen * lax.rsqrt(var + EPS) * lng_ref[...] + lnb_ref[...]


def _combine(x1, ya, yb, row0, wexp, gate, ln_g, ln_b, tm):
    g, t, d = x1.shape
    steps = t // tm
    blk0 = row0 // tm
    y_spec = pl.BlockSpec((tm, d), lambda gi, i: (blk0 + gi * steps + i, 0))
    return pl.pallas_call(
        _combine_kernel,
        grid=(g, steps),
        in_specs=[_tok_spec(tm, d), y_spec, y_spec,
                  _tok_spec(tm, LANES), _mod_spec(gate, tm), _full_spec((1, d)), _full_spec((1, d))],
        out_specs=_tok_spec(tm, d),
        out_shape=jax.ShapeDtypeStruct((g, t, d), f32),
        compiler_params=_cparams(("parallel", "parallel")),
        name="moe_combine",
    )(x1, ya, yb, wexp, gate[0], ln_g, ln_b)


def _moe(groups, h2_all, ln_g, ln_b, wg, wu, wd, layer, tm):
    bm = MOE_BLOCK
    d = groups[0][0].shape[-1]
    sizes = [x1.shape[0] * x1.shape[1] for x1, _, _ in groups]
    n = sum(sizes)
    nk = 2 * n
    expert = jnp.concatenate([r[:, :, 0:2].reshape(-1, 2) for _, r, _ in groups], axis=0).astype(jnp.int32)
    e_flat = expert.reshape(nk)
    onehot = (e_flat[:, None] == jnp.arange(N_EXPERTS, dtype=jnp.int32)[None, :]).astype(jnp.int32)
    csum = jnp.cumsum(onehot, axis=0)
    counts = csum[-1]
    rank = jnp.sum(csum * onehot, axis=1) - 1
    padded = (counts + bm - 1) // bm * bm
    pends = jnp.cumsum(padded)
    pstarts = pends - padded
    dest = jnp.sum(onehot * pstarts[None, :], axis=1) + rank
    n_blocks = nk // bm + N_EXPERTS
    n_used = (pends[-1] // bm).astype(jnp.int32)
    blk = jnp.minimum(jnp.arange(n_blocks, dtype=jnp.int32), n_used - 1)
    block_e = jnp.minimum(jnp.sum((pends[None, :] <= (blk * bm)[:, None]).astype(jnp.int32), axis=1),
                          N_EXPERTS - 1).astype(jnp.int32)
    first = jnp.concatenate([jnp.ones((1,), jnp.int32), (block_e[1:] != block_e[:-1]).astype(jnp.int32)])
    tok = jnp.arange(nk, dtype=jnp.int32) // 2
    tok_pad = jnp.zeros((n_blocks * bm,), jnp.int32).at[dest].set(tok, unique_indices=True, mode="promise_in_bounds")
    part_rows = n_blocks // GATHER_PARTS * bm
    x_parts = [jnp.take(h2_all, tok_pad[c * part_rows:(c + 1) * part_rows], axis=0, mode="clip")
               for c in range(GATHER_PARTS)]
    y = _experts(x_parts, block_e, n_used.reshape(1), first, wg, wu, wd, layer)
    dest2 = dest.reshape(n, 2)
    ya = jnp.take(y, dest2[:, 0], axis=0, mode="clip")
    yb = jnp.take(y, dest2[:, 1], axis=0, mode="clip")
    outs, off = [], 0
    for (x1, route, gate2), sz in zip(groups, sizes):
        outs.append(_combine(x1, ya, yb, off, route, gate2, ln_g, ln_b, tm))
        off += sz
    return outs


def _kv_kernel(x_ref, w_ref, g_ref, cos_ref, sin_ref, lat_ref, kr_ref, latb_ref, krb_ref):
    kva = _dot(x_ref[0].astype(bf16), w_ref[...])
    latp = kva[:, :B_KV_RANK]
    lat = latp * lax.rsqrt(jnp.mean(latp * latp, axis=-1, keepdims=True) + EPS) * g_ref[...]
    kr = kva[:, B_KV_RANK:B_KV_RANK + B_D_ROPE] * cos_ref[0] \
        + kva[:, B_KV_RANK + LANES:B_KV_RANK + LANES + B_D_ROPE] * sin_ref[0]
    lat_ref[0] = lat
    kr_ref[0] = kr
    latb_ref[0] = lat.astype(bf16)
    krb_ref[0] = kr.astype(bf16)


def _shared_kv(x, w_pad, g, cos, sin, tm):
    gg, t, d = x.shape
    rope_spec = pl.BlockSpec((1, tm, B_D_ROPE), lambda gi, i: (0, i, 0)) if cos.shape[0] == 1 else \
        _tok_spec(tm, B_D_ROPE)
    return pl.pallas_call(
        _kv_kernel,
        grid=(gg, t // tm),
        in_specs=[_tok_spec(tm, d), _full_spec(w_pad.shape), _full_spec((1, B_KV_RANK)), rope_spec, rope_spec],
        out_specs=[_tok_spec(tm, B_KV_RANK), _tok_spec(tm, B_D_ROPE),
                   _tok_spec(tm, B_KV_RANK), _tok_spec(tm, B_D_ROPE)],
        out_shape=[jax.ShapeDtypeStruct((gg, t, B_KV_RANK), f32), jax.ShapeDtypeStruct((gg, t, B_D_ROPE), f32),
                   jax.ShapeDtypeStruct((gg, t, B_KV_RANK), bf16), jax.ShapeDtypeStruct((gg, t, B_D_ROPE), bf16)],
        compiler_params=_cparams(("parallel", "parallel")),
        name="shared_kv",
    )(x, w_pad, g, cos, sin)


def _mla_q_kernel(x_ref, sc_ref, sh_ref, wdq_ref, qg_ref, wuq_ref, wuk_ref, cos_ref, sin_ref, ql_ref, qr_ref):
    h = x_ref[0] * (1.0 + sc_ref[0, 0]) + sh_ref[0, 0]
    cq = _dot(h.astype(bf16), wdq_ref[...])
    cq = cq * lax.rsqrt(jnp.mean(cq * cq, axis=-1, keepdims=True) + EPS) * qg_ref[...]
    q = _dot(cq.astype(bf16), wuq_ref[...])
    n_nope = HEADS * B_D_NOPE
    n_rope = HEADS * B_D_ROPE
    rot = q[:, n_nope:n_nope + n_rope] * cos_ref[0] + q[:, n_nope + n_rope:] * sin_ref[0]
    for hd in range(HEADS):
        qn = q[:, B_D_NOPE * hd:B_D_NOPE * (hd + 1)].astype(bf16)
        ql_ref[0, hd] = _dot(qn, wuk_ref[hd]).astype(bf16)
        qr_ref[0, hd] = rot[:, B_D_ROPE * hd:B_D_ROPE * (hd + 1)].astype(bf16)


def _mla_q(x, scale, shift, wdq, qg, wuq, wuk, cos8, sin8, tm):
    g, t, d = x.shape
    n_rope = HEADS * B_D_ROPE
    rope_spec = pl.BlockSpec((1, tm, n_rope), lambda gi, i: (0, i, 0)) if cos8.shape[0] == 1 else \
        _tok_spec(tm, n_rope)
    return pl.pallas_call(
        _mla_q_kernel,
        grid=(g, t // tm),
        in_specs=[_tok_spec(tm, d), _mod_spec(scale, tm), _mod_spec(shift, tm),
                  _full_spec(wdq.shape), _full_spec(qg.shape), _full_spec(wuq.shape), _full_spec(wuk.shape),
                  rope_spec, rope_spec],
        out_specs=[pl.BlockSpec((1, HEADS, tm, B_KV_RANK), lambda gi, i: (gi, 0, i, 0)),
                   pl.BlockSpec((1, HEADS, tm, B_D_ROPE), lambda gi, i: (gi, 0, i, 0))],
        out_shape=[jax.ShapeDtypeStruct((g, HEADS, t, B_KV_RANK), bf16),
                   jax.ShapeDtypeStruct((g, HEADS, t, B_D_ROPE), bf16)],
        compiler_params=_cparams(("parallel", "parallel")),
        name="mla_q",
    )(x, scale[0], shift[0], wdq, qg, wuq, wuk, cos8, sin8)


def _softmax_step(s, v, m_s, l_s, acc_s, slot):
    rows, n = s.shape
    m_old = m_s[slot]
    if n % LANES == 0:
        _softmax_steps([s], [v], m_s, l_s, acc_s, [slot])
        return
    m_new = jnp.maximum(m_old, jnp.max(s, axis=-1, keepdims=True))
    p = jnp.exp2((s - m_new[:, 0:1]) * SOFTMAX_EXP2_SCALE)
    lane = lax.broadcasted_iota(jnp.int32, (rows, LANES), 1)
    psum = jnp.where(lane == 0, jnp.sum(p, axis=-1, keepdims=True), 0.0)
    alpha = jnp.exp2((m_old - m_new) * SOFTMAX_EXP2_SCALE)
    l_s[slot] = alpha * l_s[slot] + psum
    alpha_v = jnp.concatenate([alpha] * (v.shape[1] // LANES), axis=1)
    acc_s[slot] = alpha_v * acc_s[slot] + _dot(p.astype(bf16), v)
    m_s[slot] = m_new


def _softmax_steps(s_list, v_list, m_s, l_s, acc_s, slots):
    ids = range(len(s_list))
    nv = v_list[0].shape[1] // LANES
    m_old = [m_s[slot] for slot in slots]
    chunks = [[s[:, LANES * c:LANES * (c + 1)] for c in range(s.shape[1] // LANES)] for s in s_list]
    c_max = [functools.reduce(jnp.maximum, chunks[k]) for k in ids]
    r_max = [jnp.max(c_max[k], axis=-1, keepdims=True) for k in ids]
    m_new = [jnp.maximum(m_old[k], r_max[k]) for k in ids]
    ps = [[jnp.exp2((ch - m_new[k]) * SOFTMAX_EXP2_SCALE) for ch in chunks[k]] for k in ids]
    alpha = [jnp.exp2((m_old[k] - m_new[k]) * SOFTMAX_EXP2_SCALE) for k in ids]
    pv = [_dot(jnp.concatenate(ps[k], axis=1).astype(bf16), v_list[k]) for k in ids]
    for k, slot in enumerate(slots):
        l_s[slot] = alpha[k] * l_s[slot] + functools.reduce(jnp.add, ps[k])
        acc_s[slot] = jnp.concatenate([alpha[k]] * nv, axis=1) * acc_s[slot] + pv[k]
        m_s[slot] = m_new[k]


def _flash_kernel(ql_ref, qr_ref, lat_ref, kr_ref, o_ref, m_s, l_s, acc_s, *, tq):
    i = pl.program_id(1)
    nh = FLASH_HEADS
    r_i = lax.broadcasted_iota(jnp.int32, (tq, tq), 0)
    c_i = lax.broadcasted_iota(jnp.int32, (tq, tq), 1)
    causal = c_i <= r_i

    def group_body(hg, carry):
        m_s[...] = jnp.full_like(m_s, NEG_INF)
        l_s[...] = jnp.zeros_like(l_s)
        acc_s[...] = jnp.zeros_like(acc_s)

        def block(j, masked):
            start = pl.multiple_of(j * tq, tq)
            k_lat = lat_ref[0, pl.ds(start, tq), :]
            k_r = kr_ref[0, pl.ds(start, tq), :]
            s_list = []
            for hs in range(nh):
                h = hg * nh + hs
                s = _dot_nt(ql_ref[0, h], k_lat) + _dot_nt(qr_ref[0, h], k_r)
                s_list.append(jnp.where(causal, s, NEG_INF) if masked else s)
            _softmax_steps(s_list, [k_lat] * nh, m_s, l_s, acc_s, list(range(nh)))

        def kv_body(j, c2):
            block(j, False)
            return c2

        lax.fori_loop(0, i, kv_body, 0)
        block(i, True)
        for hs in range(nh):
            l = jnp.sum(l_s[hs], axis=-1, keepdims=True)
            o_ref[0, hg * nh + hs] = (acc_s[hs] / l).astype(bf16)
        return carry

    lax.fori_loop(0, HEADS // nh, group_body, 0)


def _flash(ql, qr, latb, krb, tq):
    b, _, t, _ = ql.shape
    nh = FLASH_HEADS
    return pl.pallas_call(
        functools.partial(_flash_kernel, tq=tq),
        grid=(b, t // tq),
        in_specs=[
            pl.BlockSpec((1, HEADS, tq, B_KV_RANK), lambda g, i: (g, 0, i, 0)),
            pl.BlockSpec((1, HEADS, tq, B_D_ROPE), lambda g, i: (g, 0, i, 0)),
            pl.BlockSpec((1, t, B_KV_RANK), lambda g, i: (g, 0, 0)),
            pl.BlockSpec((1, t, B_D_ROPE), lambda g, i: (g, 0, 0)),
        ],
        out_specs=pl.BlockSpec((1, HEADS, tq, B_KV_RANK), lambda g, i: (g, 0, i, 0)),
        out_shape=jax.ShapeDtypeStruct((b, HEADS, t, B_KV_RANK), bf16),
        scratch_shapes=[pltpu.VMEM((nh, tq, LANES), f32), pltpu.VMEM((nh, tq, LANES), f32),
                        pltpu.VMEM((nh, tq, B_KV_RANK), f32)],
        compiler_params=_cparams(("parallel", "arbitrary")),
        name="mla_flash",
    )(ql, qr, latb, krb)


def _paged_kernel(pt_ref, ql_ref, qr_ref, *rest, ts, new_pad):
    np_ = PAGES_PER_STEP
    nc = PAGED_CHAINS
    per = np_ // nc
    lat_pages = rest[:np_]
    krt_pages = rest[np_:2 * np_]
    nl_ref, nk_ref, o_ref, m_s, l_s, acc_s = rest[2 * np_:]
    s_id = pl.program_id(1)
    ql = ql_ref[0]
    qr = qr_ref[0]

    @pl.when(s_id == 0)
    def _():
        m_s[...] = jnp.full_like(m_s, NEG_INF)
        l_s[...] = jnp.zeros_like(l_s)
        acc_s[...] = jnp.zeros_like(acc_s)

    k_lats = [jnp.concatenate([p[0] for p in lat_pages[c * per:(c + 1) * per]], axis=0).astype(bf16)
              for c in range(nc)]
    k_rts = [jnp.concatenate([p[0] for p in krt_pages[c * per:(c + 1) * per]], axis=1).astype(bf16)
             for c in range(nc)]
    _softmax_steps([_dot_nt(ql, k_lats[c]) + _dot(qr, k_rts[c]) for c in range(nc)], k_lats,
                   m_s, l_s, acc_s, list(range(nc)))

    @pl.when(s_id == pl.num_programs(1) - 1)
    def _():
        rows = ql.shape[0]
        n_lat = nl_ref[0]
        s_new = _dot_nt(ql, n_lat) + _dot_nt(qr, nk_ref[0])
        t_row = lax.broadcasted_iota(jnp.int32, (rows, new_pad), 0) & (ts - 1)
        c_new = lax.broadcasted_iota(jnp.int32, (rows, new_pad), 1)
        _softmax_step(jnp.where(c_new <= t_row, s_new, NEG_INF), n_lat, m_s, l_s, acc_s, 0)
        m = functools.reduce(jnp.maximum, [m_s[c] for c in range(nc)])
        nv = B_KV_RANK // LANES
        l = jnp.zeros_like(m)
        acc = jnp.zeros_like(acc_s[0])
        for c in range(nc):
            a_c = jnp.exp2((m_s[c] - m) * SOFTMAX_EXP2_SCALE)
            l = l + a_c * l_s[c]
            acc = acc + jnp.concatenate([a_c] * nv, axis=1) * acc_s[c]
        o_ref[0] = (acc / jnp.sum(l, axis=-1, keepdims=True)).astype(bf16)


def _paged_attention(page_table, ql, qr, cache_lat, cache_krt, new_lat, new_kr, ts):
    bs, rows, _ = ql.shape
    n_pages = page_table.shape[1]
    np_ = PAGES_PER_STEP
    new_pad = new_lat.shape[1]
    page_map = [functools.partial(lambda b, s, pt, r: (pt[b, s * np_ + r], 0, 0), r=r) for r in range(np_)]
    lat_specs = [pl.BlockSpec((1, PAGE, B_KV_RANK), page_map[r]) for r in range(np_)]
    kr_specs = [pl.BlockSpec((1, B_D_ROPE, PAGE), page_map[r]) for r in range(np_)]
    return pl.pallas_call(
        functools.partial(_paged_kernel, ts=ts, new_pad=new_pad),
        grid_spec=pltpu.PrefetchScalarGridSpec(
            num_scalar_prefetch=1,
            grid=(bs, n_pages // np_),
            in_specs=[pl.BlockSpec((1, rows, B_KV_RANK), lambda b, s, pt: (b, 0, 0)),
                      pl.BlockSpec((1, rows, B_D_ROPE), lambda b, s, pt: (b, 0, 0))]
            + lat_specs + kr_specs
            + [pl.BlockSpec((1, new_pad, B_KV_RANK), lambda b, s, pt: (b, 0, 0)),
               pl.BlockSpec((1, new_pad, B_D_ROPE), lambda b, s, pt: (b, 0, 0))],
            out_specs=pl.BlockSpec((1, rows, B_KV_RANK), lambda b, s, pt: (b, 0, 0)),
            scratch_shapes=[pltpu.VMEM((PAGED_CHAINS, rows, LANES), f32), pltpu.VMEM((PAGED_CHAINS, rows, LANES), f32),
                            pltpu.VMEM((PAGED_CHAINS, rows, B_KV_RANK), f32)],
        ),
        out_shape=jax.ShapeDtypeStruct((bs, rows, B_KV_RANK), bf16),
        compiler_params=_cparams(("parallel", "arbitrary")),
        name="mla_paged",
    )(page_table, ql, qr, *([cache_lat] * np_), *([cache_krt] * np_), new_lat, new_kr)


def _mla_out_kernel(ol_ref, wuv_ref, wo_ref, *refs):
    o = jnp.concatenate([_dot(ol_ref[0, hd], wuv_ref[hd]) for hd in range(HEADS)], axis=1)
    y = _dot(o.astype(bf16), wo_ref[...])
    _residual_ln_route(y, refs[:N_POST_IN], *refs[-3:])


def _mla_out(ol, wuv, wo, tm, **post):
    return _post_call(_mla_out_kernel, "mla_out", [ol, wuv, wo],
                      [pl.BlockSpec((1, HEADS, tm, B_KV_RANK), lambda gi, i: (gi, 0, i, 0)),
                       _full_spec(wuv.shape), _full_spec(wo.shape)], tm=tm, **post)


def _rope_tables(pos):
    half = B_D_ROPE // 2
    inv = jnp.power(ROPE_THETA, -jnp.arange(half, dtype=f32) / half)
    ang = pos.astype(f32)[:, None] * inv[None, :]
    cos, sin = jnp.cos(ang), jnp.sin(ang)
    return jnp.concatenate([cos, cos], axis=-1), jnp.concatenate([-sin, sin], axis=-1)


def _swap_halves(w, width):
    lead = w.shape[:-1]
    w2 = w.reshape(lead + (-1, 2, width // 2))
    return w2[..., ::-1, :].reshape(w.shape)


def kernel(x_prompt, x_sample, cache_latent, cache_krope, page_table, state_C, state_n, state_m, c_prompt, c_sample, ada_w, ada_b, ln_g, ln_b, a_w_in, a_b_gates, a_norm_g, a_w_out, b_w_kv_a, b_kv_norm_g, b_w_uk, b_w_uv, b_w_dq, b_q_norm_g, b_w_uq, b_w_o, router_w, router_b, e_w_gate, e_w_up, e_w_down):
    bp, tp, d = x_prompt.shape
    bs, ts, _ = x_sample.shape
    hqk = HEADS * A_DQK
    hv = HEADS * A_DV

    w_in = a_w_in.at[:, :, hqk:2 * hqk].multiply(A_DQK ** -0.5)
    w_in = jnp.pad(w_in, ((0, 0), (0, 0), (0, LANES - 2 * HEADS))).astype(bf16)
    bg = jnp.pad(a_b_gates, ((0, 0), (0, LANES - 2 * HEADS))).reshape(N_A_LAYERS, 1, LANES)
    w_out = a_w_out.astype(bf16)
    ng = a_norm_g.reshape(N_A_LAYERS, 1, hv)
    perm = jnp.array([(r % N_GROUPS) * PER_GROUP + r // N_GROUPS for r in range(N_EXPERTS)], jnp.int32)
    rw_perm = router_w[:, perm]
    rw_hi = rw_perm.astype(bf16)
    rw_lo = (rw_perm - rw_hi.astype(f32)).astype(bf16)
    rw = jnp.pad(jnp.concatenate([rw_hi, rw_lo], axis=1), ((0, 0), (0, LANES - 2 * N_EXPERTS)))
    rwh = jnp.pad(rw_hi, ((0, 0), (0, LANES - N_EXPERTS)))
    rb = router_b[perm].reshape(N_EXPERTS, 1)
    w_lat = b_w_kv_a[:, :B_KV_RANK]
    w_kr = b_w_kv_a[:, B_KV_RANK:]
    zpad = jnp.zeros((d, LANES - B_D_ROPE), f32)
    w_kv = jnp.concatenate([w_lat, w_kr, zpad, _swap_halves(w_kr, B_D_ROPE), zpad], axis=1).astype(bf16)
    kvg = b_kv_norm_g.reshape(1, B_KV_RANK)
    wdq = b_w_dq.astype(bf16)
    qg = b_q_norm_g.reshape(-1, 1, b_q_norm_g.shape[-1])
    uq = b_w_uq.reshape(b_w_uq.shape[0], b_w_uq.shape[1], HEADS, B_D_NOPE + B_D_ROPE)
    uq_nope = uq[..., :B_D_NOPE].reshape(uq.shape[0], uq.shape[1], HEADS * B_D_NOPE)
    uq_rope = uq[..., B_D_NOPE:].reshape(uq.shape[0], uq.shape[1], HEADS * B_D_ROPE)
    wuq = jnp.concatenate([uq_nope, uq_rope, _swap_halves(uq_rope, B_D_ROPE)], axis=-1).astype(bf16)
    wuk = jnp.transpose(b_w_uk, (1, 2, 0)).astype(bf16)
    wuv = jnp.transpose(b_w_uv, (1, 0, 2)).astype(bf16)
    wo = b_w_o.astype(bf16)

    mods = _ada_mods(jnp.concatenate([c_prompt, c_sample], axis=0), ada_w, ada_b)

    cos_p, sin_p = _rope_tables(jnp.arange(tp, dtype=jnp.int32))
    past_len = page_table.shape[1] * PAGE
    cos_s, sin_s = _rope_tables(jnp.tile(past_len + jnp.arange(ts, dtype=jnp.int32), bs))

    tm = 512
    cache_krt = jnp.swapaxes(cache_krope, 1, 2)
    streams = []
    mods_p = mods[:, :bp].reshape(DEPTH, bp, 1, N_MODS * d)
    mods_s = jnp.repeat(mods[:, bp:], ts, axis=1).reshape(DEPTH, 1, bs * ts, N_MODS * d)
    for x0, mods_g, cos, sin, sample in ((x_prompt, mods_p, cos_p, sin_p, False),
                                         (x_sample.reshape(1, bs * ts, d), mods_s, cos_s, sin_s, True)):
        streams.append(dict(x=x0, mods=mods_g, sample=sample, cos1=cos[None], sin1=sin[None],
                            cos8=jnp.tile(cos, (1, HEADS))[None], sin8=jnp.tile(sin, (1, HEADS))[None],
                            new_c=[], new_n=[], new_m=[]))

    for l in range(DEPTH):
        lg = ln_g[l].reshape(2, 1, d)
        lb = ln_b[l].reshape(2, 1, d)
        moe_in = []
        h2_all, row0 = None, 0
        n_all = sum(st["x"].shape[0] * st["x"].shape[1] for st in streams)
        for st in streams:
            x, sample = st["x"], st["sample"]
            cos8, sin8 = st["cos8"], st["sin8"]
            latb, krb = st.get("latb"), st.get("krb")
            new_c, new_n, new_m = st["new_c"], st["new_n"], st["new_m"]
            shift1, scale1, gate1, shift2, scale2, gate2 = [(st["mods"], l, i) for i in range(N_MODS)]
            post = dict(x=x, gate=gate1, ln_g=lg[0], ln_b=lb[0], scale2=scale2, shift2=shift2, rw=rw, rwh=rwh, rb=rb,
                        h2_rows=n_all, h2_row0=row0, h2_prev=h2_all)
            if l < N_A_LAYERS:
                qkv, o, gates = _mlstm_in(x, scale1, shift1, w_in[l], bg[l], tm)
                kt = jnp.swapaxes(qkv[:, :, hqk:2 * hqk], 1, 2)
                grow = jnp.swapaxes(gates[:, :, :2 * HEADS], 1, 2)
                if sample:
                    n0 = state_n[l].reshape(bs, hqk)
                    m0 = state_m[l]
                    nrows = jnp.repeat(n0, ts, axis=0)[None]
                    m_tok = jnp.repeat(m0, ts, axis=0)
                    mcol = jnp.pad(m_tok, ((0, 0), (0, LANES - HEADS)))[None]
                    mrow = m_tok.T[None]
                    u, c_stack, n_new, m_new = _mlstm_sample(qkv, kt, o, gates, grow, ng[l], state_C, l,
                                                             st.get("c_stack"), n0, nrows, m0, mcol, mrow, ts)
                    st["c_stack"] = c_stack
                    n_new = n_new.reshape(bs, HEADS, A_DQK)
                else:
                    u, c_new, n_rep, m_rep = _mlstm_prompt(qkv, kt, o, gates, grow, ng[l])
                    n_new = n_rep[..., 0]
                    m_new = m_rep[..., 0]
                    new_c.append(c_new)
                new_n.append(n_new)
                new_m.append(m_new)
                x1, h2_all, route = _mlstm_out(u, w_out[l], tm, **post)
            else:
                j = l - N_A_LAYERS
                ql, qr = _mla_q(x, scale1, shift1, wdq[j], qg[j], wuq[j], wuk, cos8, sin8, tm)
                if sample:
                    ql_s = jnp.transpose(ql.reshape(HEADS, bs, ts, B_KV_RANK), (1, 0, 2, 3)).reshape(
                        bs, HEADS * ts, B_KV_RANK)
                    qr_s = jnp.transpose(qr.reshape(HEADS, bs, ts, B_D_ROPE), (1, 0, 2, 3)).reshape(
                        bs, HEADS * ts, B_D_ROPE)
                    new_pad = 16
                    nl = jnp.pad(latb.reshape(bs, ts, B_KV_RANK), ((0, 0), (0, new_pad - ts), (0, 0)))
                    nk = jnp.pad(krb.reshape(bs, ts, B_D_ROPE), ((0, 0), (0, new_pad - ts), (0, 0)))
                    ol = _paged_attention(page_table, ql_s, qr_s, cache_latent, cache_krt, nl, nk, ts)
                    ol = jnp.transpose(ol.reshape(bs, HEADS, ts, B_KV_RANK), (1, 0, 2, 3)).reshape(
                        1, HEADS, bs * ts, B_KV_RANK)
                else:
                    ol = _flash(ql, qr, latb, krb, 512)
                x1, h2_all, route = _mla_out(ol, wuv, wo[j], tm, **post)
            moe_in.append((x1, route, gate2))
            row0 += x.shape[0] * x.shape[1]
        for st, x_new in zip(streams, _moe(moe_in, h2_all, lg[1], lb[1], e_w_gate, e_w_up, e_w_down, l, tm)):
            st["x"] = x_new
            if l == N_A_LAYERS - 1:
                st["lat"], st["kr"], st["latb"], st["krb"] = _shared_kv(x_new, w_kv, kvg, st["cos1"], st["sin1"], tm)

    sp, ss = streams
    return (sp["x"], ss["x"].reshape(bs, ts, d),
            jnp.stack(sp["new_c"]), jnp.stack(sp["new_n"]), jnp.stack(sp["new_m"]), sp["lat"], sp["kr"],
            ss["c_stack"], jnp.stack(ss["new_n"]), jnp.stack(ss["new_m"]),
            ss["lat"].reshape(bs, ts, B_KV_RANK), ss["kr"].reshape(bs, ts, B_D_ROPE))
```

```python
import functools

import jax
import jax.numpy as jnp
from jax import lax
from jax.experimental import pallas as pl
from jax.experimental.pallas import tpu as pltpu

f32 = jnp.float32
bf16 = jnp.bfloat16

DEPTH = 4
N_A_LAYERS = 2
HEADS = 8
A_DQK = 64
A_DV = 128
B_D_NOPE = 128
B_D_ROPE = 64
B_KV_RANK = 256
ROPE_THETA = 10000.0
ATTN_SCALE = (B_D_NOPE + B_D_ROPE) ** -0.5
N_EXPERTS = 16
N_GROUPS = 4
PER_GROUP = 4
PAGE = 128
ALPHA = (2 * DEPTH) ** 0.25
EPS = 1e-6
NEG_INF = float("-inf")
LOG2E = 1.4426950408889634
SOFTMAX_EXP2_SCALE = ATTN_SCALE * LOG2E

VMEM_LIMIT_BYTES = 56 * 1024 * 1024
LANES = 128
MLSTM_CHUNK = 128
SAMPLE_GROUP = 16
MOE_BLOCK = 512
GATHER_PARTS = 2
PAGES_PER_STEP = 16
PAGED_CHAINS = 2
FLASH_HEADS = 8


def _cparams(sem):
    return pltpu.CompilerParams(dimension_semantics=sem, vmem_limit_bytes=VMEM_LIMIT_BYTES)


def _dot(a, b):
    return jnp.dot(a, b, preferred_element_type=f32)


def _dot_nt(a, b):
    return lax.dot_general(a, b, (((1,), (1,)), ((), ())), preferred_element_type=f32)


def _dot_exact(a, b):
    return jnp.dot(a, b, preferred_element_type=f32, precision=lax.Precision.HIGHEST)


def _sigmoid(x):
    return 1.0 / (1.0 + jnp.exp(-x))


def _log_sigmoid(x):
    return jnp.minimum(x, 0.0) - jnp.log(1.0 + jnp.exp(-jnp.abs(x)))


def _ada_kernel(c_ref, w_ref, b_ref, o_ref):
    c = c_ref[...]
    sc = (c * _sigmoid(c)).astype(bf16)
    o_ref[0] = _dot(sc, w_ref[0].astype(bf16)) + b_ref[0]


def _ada_mods(c_all, ada_w, ada_b):
    depth, d, e = ada_w.shape
    bc = c_all.shape[0]
    tn = 1536
    return pl.pallas_call(
        _ada_kernel,
        grid=(depth, e // tn),
        in_specs=[
            pl.BlockSpec((bc, d), lambda l, j: (0, 0)),
            pl.BlockSpec((1, d, tn), lambda l, j: (l, 0, j)),
            pl.BlockSpec((1, 1, tn), lambda l, j: (l, 0, j)),
        ],
        out_specs=pl.BlockSpec((1, bc, tn), lambda l, j: (l, 0, j)),
        out_shape=jax.ShapeDtypeStruct((depth, bc, e), f32),
        compiler_params=_cparams(("parallel", "parallel")),
        name="ada_mods",
    )(c_all, ada_w, ada_b.reshape(depth, 1, e))


N_MODS = 6


def _mod_spec(mod, tm):
    mods, layer, chunk = mod
    d = mods.shape[-1] // N_MODS
    if mods.shape[2] == 1:
        return pl.BlockSpec((1, 1, 1, d), lambda g, i: (layer, g, 0, chunk))
    return pl.BlockSpec((1, 1, tm, d), lambda g, i: (layer, g, i, chunk))


def _tok_spec(tm, d):
    return pl.BlockSpec((1, tm, d), lambda g, i: (g, i, 0))


def _full_spec(shape):
    nd = len(shape)
    return pl.BlockSpec(shape, lambda g, i: (0,) * nd)


def _route(h2, rw_ref, rwh_ref, rb_ref):
    hi = h2.astype(bf16)
    lo = (h2 - hi.astype(f32)).astype(bf16)
    p = _dot(hi, rw_ref[...]) + _dot(lo, rwh_ref[...])
    pt = p.T
    logits = pt[0:N_EXPERTS] + pt[N_EXPERTS:2 * N_EXPERTS]
    sc = _sigmoid(logits)
    sel = sc + rb_ref[...]
    a = [sel[PER_GROUP * j:PER_GROUP * (j + 1)] for j in range(PER_GROUP)]
    s = [sc[PER_GROUP * j:PER_GROUP * (j + 1)] for j in range(PER_GROUP)]
    hi01, lo01 = jnp.maximum(a[0], a[1]), jnp.minimum(a[0], a[1])
    hi23, lo23 = jnp.maximum(a[2], a[3]), jnp.minimum(a[2], a[3])
    gs = jnp.maximum(hi01, hi23) + jnp.maximum(jnp.minimum(hi01, hi23), jnp.maximum(lo01, lo23))
    best = gs[0:1]
    grp = jnp.zeros_like(best)
    for g in range(1, N_GROUPS):
        better = gs[g:g + 1] > best
        grp = jnp.where(better, float(g), grp)
        best = jnp.where(better, gs[g:g + 1], best)
    mv, sv = [], []
    for j in range(PER_GROUP):
        m_j = a[j][0:1]
        s_j = s[j][0:1]
        for g in range(1, N_GROUPS):
            m_j = jnp.where(grp == float(g), a[j][g:g + 1], m_j)
            s_j = jnp.where(grp == float(g), s[j][g:g + 1], s_j)
        mv.append(m_j)
        sv.append(s_j)

    def first_argmax(vals):
        bv, bi = vals[0], jnp.zeros_like(vals[0])
        for j in range(1, PER_GROUP):
            better = vals[j] > bv
            bi = jnp.where(better, float(j), bi)
            bv = jnp.where(better, vals[j], bv)
        return bi

    i1 = first_argmax(mv)
    i2 = first_argmax([jnp.where(i1 == float(j), NEG_INF, mv[j]) for j in range(PER_GROUP)])
    w1, w2 = sv[0], sv[0]
    for j in range(1, PER_GROUP):
        w1 = jnp.where(i1 == float(j), sv[j], w1)
        w2 = jnp.where(i2 == float(j), sv[j], w2)
    tot = w1 + w2
    e1 = grp * float(PER_GROUP) + i1
    e2 = grp * float(PER_GROUP) + i2
    rows = jnp.concatenate([e1, e2, w1 / tot, w2 / tot, jnp.zeros((LANES - 4, e1.shape[1]), f32)], axis=0)
    return rows.T


N_POST_IN = 9


def _residual_ln_route(y, post_refs, x1_ref, h2_ref, route_ref):
    x_ref, gate_ref, lng_ref, lnb_ref, sc_ref, sh_ref, rw_ref, rwh_ref, rb_ref = post_refs
    r = ALPHA * x_ref[0] + gate_ref[0, 0] * y
    mu = jnp.mean(r, axis=-1, keepdims=True)
    cen = r - mu
    var = jnp.mean(cen * cen, axis=-1, keepdims=True)
    x1 = cen * lax.rsqrt(var + EPS) * lng_ref[...] + lnb_ref[...]
    x1_ref[0] = x1
    h2 = x1 * (1.0 + sc_ref[0, 0]) + sh_ref[0, 0]
    h2_ref[...] = h2
    route_ref[0] = _route(h2, rw_ref, rwh_ref, rb_ref)


def _post_call(kernel_fn, name, mixer_args, mixer_specs, x, gate, ln_g, ln_b, scale2, shift2, rw, rwh, rb, tm,
               h2_rows, h2_row0, h2_prev):
    g, t, d = x.shape
    steps = t // tm
    blk0 = h2_row0 // tm
    in_specs = list(mixer_specs) + [
        _tok_spec(tm, d), _mod_spec(gate, tm), _full_spec((1, d)), _full_spec((1, d)),
        _mod_spec(scale2, tm), _mod_spec(shift2, tm),
        _full_spec((d, LANES)), _full_spec((d, LANES)), _full_spec((N_EXPERTS, 1)),
    ]
    args = list(mixer_args) + [x, gate[0], ln_g, ln_b, scale2[0], shift2[0], rw, rwh, rb]
    aliases = {}
    if h2_prev is not None:
        aliases = {len(args): 1}
        in_specs.append(pl.BlockSpec(memory_space=pl.ANY))
        args.append(h2_prev)
    return pl.pallas_call(
        kernel_fn,
        grid=(g, steps),
        in_specs=in_specs,
        out_specs=[_tok_spec(tm, d),
                   pl.BlockSpec((tm, d), lambda gi, i: (blk0 + gi * steps + i, 0)),
                   _tok_spec(tm, LANES)],
        out_shape=[jax.ShapeDtypeStruct((g, t, d), f32),
                   jax.ShapeDtypeStruct((h2_rows, d), f32),
                   jax.ShapeDtypeStruct((g, t, LANES), f32)],
        input_output_aliases=aliases,
        compiler_params=_cparams(("parallel", "parallel")),
        name=name,
    )(*args)


def _mlstm_in_kernel(x_ref, sc_ref, sh_ref, w_ref, bg_ref, qkv_ref, o_ref, g_ref):
    h = x_ref[0] * (1.0 + sc_ref[0, 0]) + sh_ref[0, 0]
    z = _dot(h.astype(bf16), w_ref[...])
    hqk2 = 2 * HEADS * A_DQK
    hv = HEADS * A_DV
    qkv_ref[0] = z[:, :hqk2 + hv].astype(bf16)
    o_ref[0] = z[:, hqk2 + hv:hqk2 + 2 * hv]
    g_ref[0] = z[:, hqk2 + 2 * hv:] + bg_ref[...]


def _mlstm_in(x, scale, shift, w_pad, bg_pad, tm):
    g, t, d = x.shape
    n_qkv = 2 * HEADS * A_DQK + HEADS * A_DV
    hv = HEADS * A_DV
    return pl.pallas_call(
        _mlstm_in_kernel,
        grid=(g, t // tm),
        in_specs=[_tok_spec(tm, d), _mod_spec(scale, tm), _mod_spec(shift, tm),
                  _full_spec(w_pad.shape), _full_spec((1, LANES))],
        out_specs=[_tok_spec(tm, n_qkv), _tok_spec(tm, hv), _tok_spec(tm, LANES)],
        out_shape=[jax.ShapeDtypeStruct((g, t, n_qkv), bf16),
                   jax.ShapeDtypeStruct((g, t, hv), f32),
                   jax.ShapeDtypeStruct((g, t, LANES), f32)],
        compiler_params=_cparams(("parallel", "parallel")),
        name="mlstm_in",
    )(x, scale[0], shift[0], w_pad, bg_pad)


def _mlstm_out_kernel(u_ref, w_ref, *refs):
    y = _dot(u_ref[0], w_ref[...])
    _residual_ln_route(y, refs[:N_POST_IN], *refs[-3:])


def _mlstm_out(u, w_out, tm, **post):
    return _post_call(_mlstm_out_kernel, "mlstm_out", [u, w_out],
                      [_tok_spec(tm, u.shape[-1]), _full_spec(w_out.shape)], tm=tm, **post)


def _head_norm_gate(hh, ng, o):
    mu = jnp.mean(hh, axis=-1, keepdims=True)
    cen = hh - mu
    var = jnp.mean(cen * cen, axis=-1, keepdims=True)
    return (cen * lax.rsqrt(var + EPS) * ng) * _sigmoid(o)


def _mlstm_prompt_kernel(qkv_ref, kt_ref, o_ref, gcol_ref, grow_ref, ng_ref,
                         u_ref, c_out, n_out, m_out, c_s, n_s, m_s):
    c = pl.program_id(1)
    nc = pl.num_programs(1)
    L = qkv_ref.shape[1]
    hqk = HEADS * A_DQK

    @pl.when(c == 0)
    def _():
        c_s[...] = jnp.zeros_like(c_s)
        n_s[...] = jnp.zeros_like(n_s)
        m_s[...] = jnp.zeros_like(m_s)

    r_i = lax.broadcasted_iota(jnp.int32, (L, L), 0)
    c_i = lax.broadcasted_iota(jnp.int32, (L, L), 1)
    causal = c_i <= r_i
    tri = causal.astype(f32)
    tri_t = (r_i <= c_i).astype(f32)

    gcol = gcol_ref[0]
    grow = grow_ref[0]
    b_col = _dot_exact(tri, _log_sigmoid(gcol))
    b_row = _dot_exact(_log_sigmoid(grow[HEADS:2 * HEADS]), tri_t)
    li_row = grow[0:HEADS]

    qkv = qkv_ref[0]
    kt = kt_ref[0]
    o_all = o_ref[0]
    ng = ng_ref[...]
    c_old, n_old, m_old = c_s[...], n_s[...], m_s[...]
    hs = range(HEADS)
    q = [qkv[:, A_DQK * h:A_DQK * (h + 1)] for h in hs]
    v = [qkv[:, 2 * hqk + A_DV * h:2 * hqk + A_DV * (h + 1)] for h in hs]
    kth = [kt[A_DQK * h:A_DQK * (h + 1), :] for h in hs]
    s_qk = [_dot(q[h], kth[h]) for h in hs]
    bc = [b_col[:, HEADS + h:HEADS + h + 1] for h in hs]
    br = [b_row[h:h + 1] for h in hs]
    lir = [li_row[h:h + 1] for h in hs]
    m_prev = [m_old[h:h + 1, 0:1] for h in hs]
    d = [jnp.where(causal, bc[h] - br[h] + lir[h], NEG_INF) for h in hs]
    a = [bc[h] + m_prev[h] for h in hs]
    d_max = [jnp.max(d[h], axis=-1, keepdims=True) for h in hs]
    m_t = [jnp.maximum(a[h], d_max[h]) for h in hs]
    p = [s_qk[h] * jnp.exp(d[h] - m_t[h]) for h in hs]
    inter = [jnp.exp(a[h] - m_t[h]) for h in hs]
    qc = [_dot(q[h], jnp.concatenate([c_old[h], n_old[h]], axis=1).astype(bf16)) for h in hs]
    p_sum = [jnp.sum(p[h], axis=-1, keepdims=True) for h in hs]
    num = [_dot(p[h].astype(bf16), v[h]) + inter[h] * qc[h][:, :A_DV] for h in hs]
    den = [p_sum[h] + inter[h] * qc[h][:, A_DV:A_DV + 1] for h in hs]
    hh = [num[h] / jnp.maximum(jnp.abs(den[h]), jnp.exp(-m_t[h])) for h in hs]
    mu = [jnp.mean(hh[h], axis=-1, keepdims=True) for h in hs]
    cen = [hh[h] - mu[h] for h in hs]
    var = [jnp.mean(cen[h] * cen[h], axis=-1, keepdims=True) for h in hs]
    u_parts = [((cen[h] * lax.rsqrt(var[h] + EPS) * ng[:, A_DV * h:A_DV * (h + 1)])
                * _sigmoid(o_all[:, A_DV * h:A_DV * (h + 1)])).astype(bf16) for h in hs]

    b_end = [br[h][:, L - 1:L] for h in hs]
    g_row = [b_end[h] - br[h] + lir[h] for h in hs]
    g_max = [jnp.max(g_row[h], axis=-1, keepdims=True) for h in hs]
    m_new = [jnp.maximum(b_end[h] + m_prev[h], g_max[h]) for h in hs]
    decay = [jnp.exp(b_end[h] + m_prev[h] - m_new[h]) for h in hs]
    kw = [kth[h].astype(f32) * jnp.exp(g_row[h] - m_new[h]) for h in hs]
    kw_sum = [jnp.sum(kw[h], axis=-1, keepdims=True) for h in hs]
    c_parts = [decay[h] * c_old[h] + _dot(kw[h].astype(bf16), v[h]) for h in hs]
    n_parts = [decay[h] * n_old[h] + kw_sum[h] for h in hs]
    m_parts = [jnp.broadcast_to(m_new[h], (1, LANES)) for h in hs]

    u_ref[0] = jnp.concatenate(u_parts, axis=1)
    c_s[...] = jnp.stack(c_parts)
    n_s[...] = jnp.stack(n_parts)
    m_s[...] = jnp.concatenate(m_parts, axis=0)

    @pl.when(c == nc - 1)
    def _():
        c_out[0] = c_s[...]
        n_out[0] = n_s[...]
        m_out[0] = m_s[...]


def _mlstm_prompt(qkv, kt, o, gcol, grow, ng):
    b, t, _ = qkv.shape
    L = MLSTM_CHUNK
    hv = HEADS * A_DV
    return pl.pallas_call(
        _mlstm_prompt_kernel,
        grid=(b, t // L),
        in_specs=[
            pl.BlockSpec((1, L, qkv.shape[-1]), lambda g, c: (g, c, 0)),
            pl.BlockSpec((1, HEADS * A_DQK, L), lambda g, c: (g, 0, c)),
            pl.BlockSpec((1, L, hv), lambda g, c: (g, c, 0)),
            pl.BlockSpec((1, L, LANES), lambda g, c: (g, c, 0)),
            pl.BlockSpec((1, 2 * HEADS, L), lambda g, c: (g, 0, c)),
            pl.BlockSpec((1, hv), lambda g, c: (0, 0)),
        ],
        out_specs=[
            pl.BlockSpec((1, L, hv), lambda g, c: (g, c, 0)),
            pl.BlockSpec((1, HEADS, A_DQK, A_DV), lambda g, c: (g, 0, 0, 0)),
            pl.BlockSpec((1, HEADS, A_DQK, LANES), lambda g, c: (g, 0, 0, 0)),
            pl.BlockSpec((1, HEADS, LANES), lambda g, c: (g, 0, 0)),
        ],
        out_shape=[
            jax.ShapeDtypeStruct((b, t, hv), bf16),
            jax.ShapeDtypeStruct((b, HEADS, A_DQK, A_DV), f32),
            jax.ShapeDtypeStruct((b, HEADS, A_DQK, LANES), f32),
            jax.ShapeDtypeStruct((b, HEADS, LANES), f32),
        ],
        scratch_shapes=[
            pltpu.VMEM((HEADS, A_DQK, A_DV), f32),
            pltpu.VMEM((HEADS, A_DQK, LANES), f32),
            pltpu.VMEM((HEADS, LANES), f32),
        ],
        compiler_params=_cparams(("parallel", "arbitrary")),
        name="mlstm_prompt",
    )(qkv, kt, o, gcol, grow, ng)


def _mlstm_sample_kernel(qkv_ref, kt_ref, o_ref, gcol_ref, grow_ref, ng_ref, c0_ref, n0_ref, nrows_ref,
                         m0_ref, mcol_ref, mrow_ref, *refs, ts):
    u_ref, c_out, n_out, m_out = refs[-4:]
    R = qkv_ref.shape[1]
    G = R // ts
    shift = ts.bit_length() - 1
    hqk = HEADS * A_DQK

    r_i = lax.broadcasted_iota(jnp.int32, (R, R), 0)
    c_i = lax.broadcasted_iota(jnp.int32, (R, R), 1)
    same = (r_i >> shift) == (c_i >> shift)
    valid = same & (c_i <= r_i)
    tri = valid.astype(f32)
    tri_t = (same & (r_i <= c_i)).astype(f32)
    same_f = same.astype(f32)
    row_seq = lax.broadcasted_iota(jnp.int32, (R, 1), 0) >> shift
    g_i = lax.broadcasted_iota(jnp.int32, (G, R), 0)
    s_i = lax.broadcasted_iota(jnp.int32, (G, R), 1)
    bmask = (s_i >> shift) == g_i
    lastmask = s_i == g_i * ts + (ts - 1)
    g3 = lax.broadcasted_iota(jnp.int32, (G, A_DQK, R), 0)
    s3 = lax.broadcasted_iota(jnp.int32, (G, A_DQK, R), 2)
    bmask3 = (s3 >> shift) == g3
    lane_i = lax.broadcasted_iota(jnp.int32, (G, LANES), 1)

    gcol = gcol_ref[0]
    grow = grow_ref[0]
    lf_row = _log_sigmoid(grow[HEADS:2 * HEADS])
    b_col = _dot_exact(tri, _log_sigmoid(gcol))
    b_row = _dot_exact(lf_row, tri_t)
    bend_row = _dot_exact(lf_row, same_f)
    li_row = grow[0:HEADS]
    mcol = mcol_ref[0]
    mrow = mrow_ref[0]
    m0 = m0_ref[...]

    m_acc = jnp.zeros((G, LANES), f32)
    n_parts = []
    for h in range(HEADS):
        q_h = qkv_ref[0, :, A_DQK * h:A_DQK * (h + 1)]
        k_h = qkv_ref[0, :, hqk + A_DQK * h:hqk + A_DQK * (h + 1)]
        v_h = qkv_ref[0, :, 2 * hqk + A_DV * h:2 * hqk + A_DV * (h + 1)]
        kt_h = kt_ref[0, A_DQK * h:A_DQK * (h + 1), :]
        s_qk = _dot(q_h, kt_h)
        bc = b_col[:, HEADS + h:HEADS + h + 1]
        br = b_row[h:h + 1]
        lir = li_row[h:h + 1]
        d = jnp.where(valid, bc - br + lir, NEG_INF)
        a = bc + mcol[:, h:h + 1]
        m_t = jnp.maximum(a, jnp.max(d, axis=-1, keepdims=True))
        p = s_qk * jnp.exp(d - m_t)
        inter = jnp.exp(a - m_t)
        c_all = jnp.concatenate([c0_ref[0, g, h] for g in range(G)], axis=1).astype(bf16)
        qc = _dot(q_h, c_all)
        inter_c = jnp.zeros((R, A_DV), f32)
        for g in range(G):
            inter_c = jnp.where(row_seq == g, qc[:, A_DV * g:A_DV * (g + 1)], inter_c)
        qn = jnp.sum(q_h.astype(f32) * nrows_ref[0, :, A_DQK * h:A_DQK * (h + 1)], axis=-1, keepdims=True)
        num = _dot(p.astype(bf16), v_h) + inter * inter_c
        den = jnp.sum(p, axis=-1, keepdims=True) + inter * qn
        hh = num / jnp.maximum(jnp.abs(den), jnp.exp(-m_t))
        u = _head_norm_gate(hh, ng_ref[:, A_DV * h:A_DV * (h + 1)], o_ref[0, :, A_DV * h:A_DV * (h + 1)])
        u_ref[0, :, A_DV * h:A_DV * (h + 1)] = u.astype(bf16)

        g_row = bend_row[h:h + 1] - br + lir
        gmax = jnp.max(jnp.where(bmask, g_row, NEG_INF), axis=-1, keepdims=True)
        bend_b = jnp.sum(jnp.where(lastmask, br, 0.0), axis=-1, keepdims=True)
        m0_h = m0[:, h:h + 1]
        mnew_b = jnp.maximum(bend_b + m0_h, gmax)
        mnew_row = jnp.sum(jnp.where(bmask, mnew_b, 0.0), axis=0, keepdims=True)
        w_row = jnp.exp(g_row - mnew_row)
        decay_b = jnp.exp(bend_b + m0_h - mnew_b)
        kw = kt_h.astype(f32) * w_row
        kw3 = jnp.where(bmask3, jnp.broadcast_to(kw[None], (G, A_DQK, R)), 0.0)
        upd = _dot(kw3.reshape(G * A_DQK, R).astype(bf16), v_h)
        for g in range(G):
            c_out[0, g, h] = decay_b[g:g + 1, :] * c0_ref[0, g, h] + upd[A_DQK * g:A_DQK * (g + 1)]
        wm = jnp.where(bmask, w_row, 0.0)
        n_parts.append(decay_b * n0_ref[:, A_DQK * h:A_DQK * (h + 1)] + _dot(wm.astype(bf16), k_h))
        m_acc = jnp.where(lane_i == h, mnew_b, m_acc)

    n_out[...] = jnp.concatenate(n_parts, axis=1)
    m_out[...] = m_acc[:, :HEADS]


def _mlstm_sample(qkv, kt, o, gcol, grow, ng, c0_all, layer, c_prev, n0, nrows, m0, mcol, mrow, ts):
    _, t, _ = qkv.shape
    n_layers, bs = c0_all.shape[:2]
    G = SAMPLE_GROUP
    R = G * ts
    hv = HEADS * A_DV
    hqk = HEADS * A_DQK
    args = [qkv, kt, o, gcol, grow, ng, c0_all, n0, nrows, m0, mcol, mrow]
    extra_specs, aliases = [], {}
    if c_prev is not None:
        aliases = {len(args): 1}
        extra_specs = [pl.BlockSpec(memory_space=pl.ANY)]
        args.append(c_prev)
    return pl.pallas_call(
        functools.partial(_mlstm_sample_kernel, ts=ts),
        grid=(t // R,),
        input_output_aliases=aliases,
        in_specs=[
            pl.BlockSpec((1, R, qkv.shape[-1]), lambda i: (0, i, 0)),
            pl.BlockSpec((1, hqk, R), lambda i: (0, 0, i)),
            pl.BlockSpec((1, R, hv), lambda i: (0, i, 0)),
            pl.BlockSpec((1, R, LANES), lambda i: (0, i, 0)),
            pl.BlockSpec((1, 2 * HEADS, R), lambda i: (0, 0, i)),
            pl.BlockSpec((1, hv), lambda i: (0, 0)),
            pl.BlockSpec((1, G, HEADS, A_DQK, A_DV), lambda i: (layer, i, 0, 0, 0)),
            pl.BlockSpec((G, hqk), lambda i: (i, 0)),
            pl.BlockSpec((1, R, hqk), lambda i: (0, i, 0)),
            pl.BlockSpec((G, HEADS), lambda i: (i, 0)),
            pl.BlockSpec((1, R, LANES), lambda i: (0, i, 0)),
            pl.BlockSpec((1, HEADS, R), lambda i: (0, 0, i)),
        ] + extra_specs,
        out_specs=[
            pl.BlockSpec((1, R, hv), lambda i: (0, i, 0)),
            pl.BlockSpec((1, G, HEADS, A_DQK, A_DV), lambda i: (layer, i, 0, 0, 0)),
            pl.BlockSpec((G, hqk), lambda i: (i, 0)),
            pl.BlockSpec((G, HEADS), lambda i: (i, 0)),
        ],
        out_shape=[
            jax.ShapeDtypeStruct((1, t, hv), bf16),
            jax.ShapeDtypeStruct((n_layers, bs, HEADS, A_DQK, A_DV), f32),
            jax.ShapeDtypeStruct((bs, hqk), f32),
            jax.ShapeDtypeStruct((bs, HEADS), f32),
        ],
        compiler_params=_cparams(("parallel",)),
        name="mlstm_sample",
    )(*args)


def _expert_kernel(be_ref, nu_ref, first_ref, *refs, n_parts):
    x_refs = refs[:n_parts]
    wg_ref, wu_ref, wd_ref, y_ref, wg_s, wu_s, wd_s = refs[n_parts:]
    i = pl.program_id(0)
    part_blocks = pl.num_programs(0) // n_parts

    @pl.when(first_ref[i] == 1)
    def _():
        wg_s[...] = wg_ref[0, 0].astype(bf16)
        wu_s[...] = wu_ref[0, 0].astype(bf16)
        wd_s[...] = wd_ref[0, 0].astype(bf16)

    @pl.when(i < nu_ref[0])
    def _():
        x = x_refs[n_parts - 1][...]
        for part in range(n_parts - 2, -1, -1):
            x = jnp.where(i < (part + 1) * part_blocks, x_refs[part][...], x)
        x = x.astype(bf16)
        g = _dot(x, wg_s[...])
        u = _dot(x, wu_s[...])
        hid = (g * _sigmoid(g)) * u
        y_ref[...] = _dot(hid.astype(bf16), wd_s[...])

    @pl.when(i >= nu_ref[0])
    def _():
        y_ref[...] = jnp.zeros_like(y_ref)


def _experts(x_parts, block_e, n_used, first, wg, wu, wd, layer):
    n_parts = len(x_parts)
    pp, d = x_parts[0].shape
    de = wg.shape[-1]
    bm = MOE_BLOCK
    pb = pp // bm
    w_map = lambda i, be, nu, fi: (layer, be[i], 0, 0)
    part_specs = [pl.BlockSpec((bm, d), functools.partial(
        lambda i, be, nu, fi, part: (jnp.clip(i - part * pb, 0, pb - 1), 0), part=part)) for part in range(n_parts)]
    return pl.pallas_call(
        functools.partial(_expert_kernel, n_parts=n_parts),
        grid_spec=pltpu.PrefetchScalarGridSpec(
            num_scalar_prefetch=3,
            grid=(n_parts * pb,),
            in_specs=part_specs + [
                pl.BlockSpec((1, 1, d, de), w_map),
                pl.BlockSpec((1, 1, d, de), w_map),
                pl.BlockSpec((1, 1, de, d), w_map),
            ],
            out_specs=pl.BlockSpec((bm, d), lambda i, be, nu, fi: (i, 0)),
            scratch_shapes=[pltpu.VMEM((d, de), bf16), pltpu.VMEM((d, de), bf16), pltpu.VMEM((de, d), bf16)],
        ),
        out_shape=jax.ShapeDtypeStruct((n_parts * pp, d), f32),
        compiler_params=_cparams(("arbitrary",)),
        name="moe_experts",
    )(block_e, n_used, first, *x_parts, wg, wu, wd)


def _combine_kernel(x_ref, ya_ref, yb_ref, w_ref, gate_ref, lng_ref, lnb_ref, o_ref):
    w = w_ref[0]
    y = w[:, 2:3] * ya_ref[...] + w[:, 3:4] * yb_ref[...]
    r = ALPHA * x_ref[0] + gate_ref[0, 0] * y
    mu = jnp.mean(r, axis=-1, keepdims=True)
    cen = r - mu
    var = jnp.mean(cen * cen, axis=-1, keepdims=True)
    o_ref[0] = cen * lax.rsqrt(var + EPS) * lng_ref[...] + lnb_ref[...]


def _combine(x1, ya, yb, row0, wexp, gate, ln_g, ln_b, tm):
    g, t, d = x1.shape
    steps = t // tm
    blk0 = row0 // tm
    y_spec = pl.BlockSpec((tm, d), lambda gi, i: (blk0 + gi * steps + i, 0))
    return pl.pallas_call(
        _combine_kernel,
        grid=(g, steps),
        in_specs=[_tok_spec(tm, d), y_spec, y_spec,
                  _tok_spec(tm, LANES), _mod_spec(gate, tm), _full_spec((1, d)), _full_spec((1, d))],
        out_specs=_tok_spec(tm, d),
        out_shape=jax.ShapeDtypeStruct((g, t, d), f32),
        compiler_params=_cparams(("parallel", "parallel")),
        name="moe_combine",
    )(x1, ya, yb, wexp, gate[0], ln_g, ln_b)


def _moe(groups, h2_all, ln_g, ln_b, wg, wu, wd, layer, tm):
    bm = MOE_BLOCK
    d = groups[0][0].shape[-1]
    sizes = [x1.shape[0] * x1.shape[1] for x1, _, _ in groups]
    n = sum(sizes)
    nk = 2 * n
    expert = jnp.concatenate([r[:, :, 0:2].reshape(-1, 2) for _, r, _ in groups], axis=0).astype(jnp.int32)
    e_flat = expert.reshape(nk)
    onehot = (e_flat[:, None] == jnp.arange(N_EXPERTS, dtype=jnp.int32)[None, :]).astype(jnp.int32)
    csum = jnp.cumsum(onehot, axis=0)
    counts = csum[-1]
    rank = jnp.sum(csum * onehot, axis=1) - 1
    padded = (counts + bm - 1) // bm * bm
    pends = jnp.cumsum(padded)
    pstarts = pends - padded
    dest = jnp.sum(onehot * pstarts[None, :], axis=1) + rank
    n_blocks = nk // bm + N_EXPERTS
    n_used = (pends[-1] // bm).astype(jnp.int32)
    blk = jnp.minimum(jnp.arange(n_blocks, dtype=jnp.int32), n_used - 1)
    block_e = jnp.minimum(jnp.sum((pends[None, :] <= (blk * bm)[:, None]).astype(jnp.int32), axis=1),
                          N_EXPERTS - 1).astype(jnp.int32)
    first = jnp.concatenate([jnp.ones((1,), jnp.int32), (block_e[1:] != block_e[:-1]).astype(jnp.int32)])
    tok = jnp.arange(nk, dtype=jnp.int32) // 2
    tok_pad = jnp.zeros((n_blocks * bm,), jnp.int32).at[dest].set(tok, unique_indices=True, mode="promise_in_bounds")
    part_rows = n_blocks // GATHER_PARTS * bm
    x_parts = [jnp.take(h2_all, tok_pad[c * part_rows:(c + 1) * part_rows], axis=0, mode="clip")
               for c in range(GATHER_PARTS)]
    y = _experts(x_parts, block_e, n_used.reshape(1), first, wg, wu, wd, layer)
    dest2 = dest.reshape(n, 2)
    ya = jnp.take(y, dest2[:, 0], axis=0, mode="clip")
    yb = jnp.take(y, dest2[:, 1], axis=0, mode="clip")
    outs, off = [], 0
    for (x1, route, gate2), sz in zip(groups, sizes):
        outs.append(_combine(x1, ya, yb, off, route, gate2, ln_g, ln_b, tm))
        off += sz
    return outs


def _kv_kernel(x_ref, w_ref, g_ref, cos_ref, sin_ref, lat_ref, kr_ref, latb_ref, krb_ref):
    kva = _dot(x_ref[0].astype(bf16), w_ref[...])
    latp = kva[:, :B_KV_RANK]
    lat = latp * lax.rsqrt(jnp.mean(latp * latp, axis=-1, keepdims=True) + EPS) * g_ref[...]
    kr = kva[:, B_KV_RANK:B_KV_RANK + B_D_ROPE] * cos_ref[0] \
        + kva[:, B_KV_RANK + LANES:B_KV_RANK + LANES + B_D_ROPE] * sin_ref[0]
    lat_ref[0] = lat
    kr_ref[0] = kr
    latb_ref[0] = lat.astype(bf16)
    krb_ref[0] = kr.astype(bf16)


def _shared_kv(x, w_pad, g, cos, sin, tm):
    gg, t, d = x.shape
    rope_spec = pl.BlockSpec((1, tm, B_D_ROPE), lambda gi, i: (0, i, 0)) if cos.shape[0] == 1 else \
        _tok_spec(tm, B_D_ROPE)
    return pl.pallas_call(
        _kv_kernel,
        grid=(gg, t // tm),
        in_specs=[_tok_spec(tm, d), _full_spec(w_pad.shape), _full_spec((1, B_KV_RANK)), rope_spec, rope_spec],
        out_specs=[_tok_spec(tm, B_KV_RANK), _tok_spec(tm, B_D_ROPE),
                   _tok_spec(tm, B_KV_RANK), _tok_spec(tm, B_D_ROPE)],
        out_shape=[jax.ShapeDtypeStruct((gg, t, B_KV_RANK), f32), jax.ShapeDtypeStruct((gg, t, B_D_ROPE), f32),
                   jax.ShapeDtypeStruct((gg, t, B_KV_RANK), bf16), jax.ShapeDtypeStruct((gg, t, B_D_ROPE), bf16)],
        compiler_params=_cparams(("parallel", "parallel")),
        name="shared_kv",
    )(x, w_pad, g, cos, sin)


def _mla_q_kernel(x_ref, sc_ref, sh_ref, wdq_ref, qg_ref, wuq_ref, wuk_ref, cos_ref, sin_ref, ql_ref, qr_ref):
    h = x_ref[0] * (1.0 + sc_ref[0, 0]) + sh_ref[0, 0]
    cq = _dot(h.astype(bf16), wdq_ref[...])
    cq = cq * lax.rsqrt(jnp.mean(cq * cq, axis=-1, keepdims=True) + EPS) * qg_ref[...]
    q = _dot(cq.astype(bf16), wuq_ref[...])
    n_nope = HEADS * B_D_NOPE
    n_rope = HEADS * B_D_ROPE
    rot = q[:, n_nope:n_nope + n_rope] * cos_ref[0] + q[:, n_nope + n_rope:] * sin_ref[0]
    for hd in range(HEADS):
        qn = q[:, B_D_NOPE * hd:B_D_NOPE * (hd + 1)].astype(bf16)
        ql_ref[0, hd] = _dot(qn, wuk_ref[hd]).astype(bf16)
        qr_ref[0, hd] = rot[:, B_D_ROPE * hd:B_D_ROPE * (hd + 1)].astype(bf16)


def _mla_q(x, scale, shift, wdq, qg, wuq, wuk, cos8, sin8, tm):
    g, t, d = x.shape
    n_rope = HEADS * B_D_ROPE
    rope_spec = pl.BlockSpec((1, tm, n_rope), lambda gi, i: (0, i, 0)) if cos8.shape[0] == 1 else \
        _tok_spec(tm, n_rope)
    return pl.pallas_call(
        _mla_q_kernel,
        grid=(g, t // tm),
        in_specs=[_tok_spec(tm, d), _mod_spec(scale, tm), _mod_spec(shift, tm),
                  _full_spec(wdq.shape), _full_spec(qg.shape), _full_spec(wuq.shape), _full_spec(wuk.shape),
                  rope_spec, rope_spec],
        out_specs=[pl.BlockSpec((1, HEADS, tm, B_KV_RANK), lambda gi, i: (gi, 0, i, 0)),
                   pl.BlockSpec((1, HEADS, tm, B_D_ROPE), lambda gi, i: (gi, 0, i, 0))],
        out_shape=[jax.ShapeDtypeStruct((g, HEADS, t, B_KV_RANK), bf16),
                   jax.ShapeDtypeStruct((g, HEADS, t, B_D_ROPE), bf16)],
        compiler_params=_cparams(("parallel", "parallel")),
        name="mla_q",
    )(x, scale[0], shift[0], wdq, qg, wuq, wuk, cos8, sin8)


def _softmax_step(s, v, m_s, l_s, acc_s, slot):
    rows, n = s.shape
    m_old = m_s[slot]
    if n % LANES == 0:
        _softmax_steps([s], [v], m_s, l_s, acc_s, [slot])
        return
    m_new = jnp.maximum(m_old, jnp.max(s, axis=-1, keepdims=True))
    p = jnp.exp2((s - m_new[:, 0:1]) * SOFTMAX_EXP2_SCALE)
    lane = lax.broadcasted_iota(jnp.int32, (rows, LANES), 1)
    psum = jnp.where(lane == 0, jnp.sum(p, axis=-1, keepdims=True), 0.0)
    alpha = jnp.exp2((m_old - m_new) * SOFTMAX_EXP2_SCALE)
    l_s[slot] = alpha * l_s[slot] + psum
    alpha_v = jnp.concatenate([alpha] * (v.shape[1] // LANES), axis=1)
    acc_s[slot] = alpha_v * acc_s[slot] + _dot(p.astype(bf16), v)
    m_s[slot] = m_new


def _softmax_steps(s_list, v_list, m_s, l_s, acc_s, slots):
    ids = range(len(s_list))
    nv = v_list[0].shape[1] // LANES
    m_old = [m_s[slot] for slot in slots]
    chunks = [[s[:, LANES * c:LANES * (c + 1)] for c in range(s.shape[1] // LANES)] for s in s_list]
    c_max = [functools.reduce(jnp.maximum, chunks[k]) for k in ids]
    r_max = [jnp.max(c_max[k], axis=-1, keepdims=True) for k in ids]
    m_new = [jnp.maximum(m_old[k], r_max[k]) for k in ids]
    ps = [[jnp.exp2((ch - m_new[k]) * SOFTMAX_EXP2_SCALE) for ch in chunks[k]] for k in ids]
    alpha = [jnp.exp2((m_old[k] - m_new[k]) * SOFTMAX_EXP2_SCALE) for k in ids]
    pv = [_dot(jnp.concatenate(ps[k], axis=1).astype(bf16), v_list[k]) for k in ids]
    for k, slot in enumerate(slots):
        l_s[slot] = alpha[k] * l_s[slot] + functools.reduce(jnp.add, ps[k])
        acc_s[slot] = jnp.concatenate([alpha[k]] * nv, axis=1) * acc_s[slot] + pv[k]
        m_s[slot] = m_new[k]


def _flash_kernel(ql_ref, qr_ref, lat_ref, kr_ref, o_ref, m_s, l_s, acc_s, *, tq):
    i = pl.program_id(1)
    nh = FLASH_HEADS
    r_i = lax.broadcasted_iota(jnp.int32, (tq, tq), 0)
    c_i = lax.broadcasted_iota(jnp.int32, (tq, tq), 1)
    causal = c_i <= r_i

    def group_body(hg, carry):
        m_s[...] = jnp.full_like(m_s, NEG_INF)
        l_s[...] = jnp.zeros_like(l_s)
        acc_s[...] = jnp.zeros_like(acc_s)

        def block(j, masked):
            start = pl.multiple_of(j * tq, tq)
            k_lat = lat_ref[0, pl.ds(start, tq), :]
            k_r = kr_ref[0, pl.ds(start, tq), :]
            s_list = []
            for hs in range(nh):
                h = hg * nh + hs
                s = _dot_nt(ql_ref[0, h], k_lat) + _dot_nt(qr_ref[0, h], k_r)
                s_list.append(jnp.where(causal, s, NEG_INF) if masked else s)
            _softmax_steps(s_list, [k_lat] * nh, m_s, l_s, acc_s, list(range(nh)))

        def kv_body(j, c2):
            block(j, False)
            return c2

        lax.fori_loop(0, i, kv_body, 0)
        block(i, True)
        for hs in range(nh):
            l = jnp.sum(l_s[hs], axis=-1, keepdims=True)
            o_ref[0, hg * nh + hs] = (acc_s[hs] / l).astype(bf16)
        return carry

    lax.fori_loop(0, HEADS // nh, group_body, 0)


def _flash(ql, qr, latb, krb, tq):
    b, _, t, _ = ql.shape
    nh = FLASH_HEADS
    return pl.pallas_call(
        functools.partial(_flash_kernel, tq=tq),
        grid=(b, t // tq),
        in_specs=[
            pl.BlockSpec((1, HEADS, tq, B_KV_RANK), lambda g, i: (g, 0, i, 0)),
            pl.BlockSpec((1, HEADS, tq, B_D_ROPE), lambda g, i: (g, 0, i, 0)),
            pl.BlockSpec((1, t, B_KV_RANK), lambda g, i: (g, 0, 0)),
            pl.BlockSpec((1, t, B_D_ROPE), lambda g, i: (g, 0, 0)),
        ],
        out_specs=pl.BlockSpec((1, HEADS, tq, B_KV_RANK), lambda g, i: (g, 0, i, 0)),
        out_shape=jax.ShapeDtypeStruct((b, HEADS, t, B_KV_RANK), bf16),
        scratch_shapes=[pltpu.VMEM((nh, tq, LANES), f32), pltpu.VMEM((nh, tq, LANES), f32),
                        pltpu.VMEM((nh, tq, B_KV_RANK), f32)],
        compiler_params=_cparams(("parallel", "arbitrary")),
        name="mla_flash",
    )(ql, qr, latb, krb)


def _paged_kernel(pt_ref, ql_ref, qr_ref, *rest, ts, new_pad):
    np_ = PAGES_PER_STEP
    nc = PAGED_CHAINS
    per = np_ // nc
    lat_pages = rest[:np_]
    krt_pages = rest[np_:2 * np_]
    nl_ref, nk_ref, o_ref, m_s, l_s, acc_s = rest[2 * np_:]
    s_id = pl.program_id(1)
    ql = ql_ref[0]
    qr = qr_ref[0]

    @pl.when(s_id == 0)
    def _():
        m_s[...] = jnp.full_like(m_s, NEG_INF)
        l_s[...] = jnp.zeros_like(l_s)
        acc_s[...] = jnp.zeros_like(acc_s)

    k_lats = [jnp.concatenate([p[0] for p in lat_pages[c * per:(c + 1) * per]], axis=0).astype(bf16)
              for c in range(nc)]
    k_rts = [jnp.concatenate([p[0] for p in krt_pages[c * per:(c + 1) * per]], axis=1).astype(bf16)
             for c in range(nc)]
    _softmax_steps([_dot_nt(ql, k_lats[c]) + _dot(qr, k_rts[c]) for c in range(nc)], k_lats,
                   m_s, l_s, acc_s, list(range(nc)))

    @pl.when(s_id == pl.num_programs(1) - 1)
    def _():
        rows = ql.shape[0]
        n_lat = nl_ref[0]
        s_new = _dot_nt(ql, n_lat) + _dot_nt(qr, nk_ref[0])
        t_row = lax.broadcasted_iota(jnp.int32, (rows, new_pad), 0) & (ts - 1)
        c_new = lax.broadcasted_iota(jnp.int32, (rows, new_pad), 1)
        _softmax_step(jnp.where(c_new <= t_row, s_new, NEG_INF), n_lat, m_s, l_s, acc_s, 0)
        m = functools.reduce(jnp.maximum, [m_s[c] for c in range(nc)])
        nv = B_KV_RANK // LANES
        l = jnp.zeros_like(m)
        acc = jnp.zeros_like(acc_s[0])
        for c in range(nc):
            a_c = jnp.exp2((m_s[c] - m) * SOFTMAX_EXP2_SCALE)
            l = l + a_c * l_s[c]
            acc = acc + jnp.concatenate([a_c] * nv, axis=1) * acc_s[c]
        o_ref[0] = (acc / jnp.sum(l, axis=-1, keepdims=True)).astype(bf16)


def _paged_attention(page_table, ql, qr, cache_lat, cache_krt, new_lat, new_kr, ts):
    bs, rows, _ = ql.shape
    n_pages = page_table.shape[1]
    np_ = PAGES_PER_STEP
    new_pad = new_lat.shape[1]
    page_map = [functools.partial(lambda b, s, pt, r: (pt[b, s * np_ + r], 0, 0), r=r) for r in range(np_)]
    lat_specs = [pl.BlockSpec((1, PAGE, B_KV_RANK), page_map[r]) for r in range(np_)]
    kr_specs = [pl.BlockSpec((1, B_D_ROPE, PAGE), page_map[r]) for r in range(np_)]
    return pl.pallas_call(
        functools.partial(_paged_kernel, ts=ts, new_pad=new_pad),
        grid_spec=pltpu.PrefetchScalarGridSpec(
            num_scalar_prefetch=1,
            grid=(bs, n_pages // np_),
            in_specs=[pl.BlockSpec((1, rows, B_KV_RANK), lambda b, s, pt: (b, 0, 0)),
                      pl.BlockSpec((1, rows, B_D_ROPE), lambda b, s, pt: (b, 0, 0))]
            + lat_specs + kr_specs
            + [pl.BlockSpec((1, new_pad, B_KV_RANK), lambda b, s, pt: (b, 0, 0)),
               pl.BlockSpec((1, new_pad, B_D_ROPE), lambda b, s, pt: (b, 0, 0))],
            out_specs=pl.BlockSpec((1, rows, B_KV_RANK), lambda b, s, pt: (b, 0, 0)),
            scratch_shapes=[pltpu.VMEM((PAGED_CHAINS, rows, LANES), f32), pltpu.VMEM((PAGED_CHAINS, rows, LANES), f32),
                            pltpu.VMEM((PAGED_CHAINS, rows, B_KV_RANK), f32)],
        ),
        out_shape=jax.ShapeDtypeStruct((bs, rows, B_KV_RANK), bf16),
        compiler_params=_cparams(("parallel", "arbitrary")),
        name="mla_paged",
    )(page_table, ql, qr, *([cache_lat] * np_), *([cache_krt] * np_), new_lat, new_kr)


def _mla_out_kernel(ol_ref, wuv_ref, wo_ref, *refs):
    o = jnp.concatenate([_dot(ol_ref[0, hd], wuv_ref[hd]) for hd in range(HEADS)], axis=1)
    y = _dot(o.astype(bf16), wo_ref[...])
    _residual_ln_route(y, refs[:N_POST_IN], *refs[-3:])


def _mla_out(ol, wuv, wo, tm, **post):
    return _post_call(_mla_out_kernel, "mla_out", [ol, wuv, wo],
                      [pl.BlockSpec((1, HEADS, tm, B_KV_RANK), lambda gi, i: (gi, 0, i, 0)),
                       _full_spec(wuv.shape), _full_spec(wo.shape)], tm=tm, **post)


def _rope_tables(pos):
    half = B_D_ROPE // 2
    inv = jnp.power(ROPE_THETA, -jnp.arange(half, dtype=f32) / half)
    ang = pos.astype(f32)[:, None] * inv[None, :]
    cos, sin = jnp.cos(ang), jnp.sin(ang)
    return jnp.concatenate([cos, cos], axis=-1), jnp.concatenate([-sin, sin], axis=-1)


def _swap_halves(w, width):
    lead = w.shape[:-1]
    w2 = w.reshape(lead + (-1, 2, width // 2))
    return w2[..., ::-1, :].reshape(w.shape)


def kernel(x_prompt, x_sample, cache_latent, cache_krope, page_table, state_C, state_n, state_m, c_prompt, c_sample, ada_w, ada_b, ln_g, ln_b, a_w_in, a_b_gates, a_norm_g, a_w_out, b_w_kv_a, b_kv_norm_g, b_w_uk, b_w_uv, b_w_dq, b_q_norm_g, b_w_uq, b_w_o, router_w, router_b, e_w_gate, e_w_up, e_w_down):
    bp, tp, d = x_prompt.shape
    bs, ts, _ = x_sample.shape
    hqk = HEADS * A_DQK
    hv = HEADS * A_DV

    w_in = a_w_in.at[:, :, hqk:2 * hqk].multiply(A_DQK ** -0.5)
    w_in = jnp.pad(w_in, ((0, 0), (0, 0), (0, LANES - 2 * HEADS))).astype(bf16)
    bg = jnp.pad(a_b_gates, ((0, 0), (0, LANES - 2 * HEADS))).reshape(N_A_LAYERS, 1, LANES)
    w_out = a_w_out.astype(bf16)
    ng = a_norm_g.reshape(N_A_LAYERS, 1, hv)
    perm = jnp.array([(r % N_GROUPS) * PER_GROUP + r // N_GROUPS for r in range(N_EXPERTS)], jnp.int32)
    rw_perm = router_w[:, perm]
    rw_hi = rw_perm.astype(bf16)
    rw_lo = (rw_perm - rw_hi.astype(f32)).astype(bf16)
    rw = jnp.pad(jnp.concatenate([rw_hi, rw_lo], axis=1), ((0, 0), (0, LANES - 2 * N_EXPERTS)))
    rwh = jnp.pad(rw_hi, ((0, 0), (0, LANES - N_EXPERTS)))
    rb = router_b[perm].reshape(N_EXPERTS, 1)
    w_lat = b_w_kv_a[:, :B_KV_RANK]
    w_kr = b_w_kv_a[:, B_KV_RANK:]
    zpad = jnp.zeros((d, LANES - B_D_ROPE), f32)
    w_kv = jnp.concatenate([w_lat, w_kr, zpad, _swap_halves(w_kr, B_D_ROPE), zpad], axis=1).astype(bf16)
    kvg = b_kv_norm_g.reshape(1, B_KV_RANK)
    wdq = b_w_dq.astype(bf16)
    qg = b_q_norm_g.reshape(-1, 1, b_q_norm_g.shape[-1])
    uq = b_w_uq.reshape(b_w_uq.shape[0], b_w_uq.shape[1], HEADS, B_D_NOPE + B_D_ROPE)
    uq_nope = uq[..., :B_D_NOPE].reshape(uq.shape[0], uq.shape[1], HEADS * B_D_NOPE)
    uq_rope = uq[..., B_D_NOPE:].reshape(uq.shape[0], uq.shape[1], HEADS * B_D_ROPE)
    wuq = jnp.concatenate([uq_nope, uq_rope, _swap_halves(uq_rope, B_D_ROPE)], axis=-1).astype(bf16)
    wuk = jnp.transpose(b_w_uk, (1, 2, 0)).astype(bf16)
    wuv = jnp.transpose(b_w_uv, (1, 0, 2)).astype(bf16)
    wo = b_w_o.astype(bf16)

    mods = _ada_mods(jnp.concatenate([c_prompt, c_sample], axis=0), ada_w, ada_b)

    cos_p, sin_p = _rope_tables(jnp.arange(tp, dtype=jnp.int32))
    past_len = page_table.shape[1] * PAGE
    cos_s, sin_s = _rope_tables(jnp.tile(past_len + jnp.arange(ts, dtype=jnp.int32), bs))

    tm = 512
    cache_krt = jnp.swapaxes(cache_krope, 1, 2)
    streams = []
    mods_p = mods[:, :bp].reshape(DEPTH, bp, 1, N_MODS * d)
    mods_s = jnp.repeat(mods[:, bp:], ts, axis=1).reshape(DEPTH, 1, bs * ts, N_MODS * d)
    for x0, mods_g, cos, sin, sample in ((x_prompt, mods_p, cos_p, sin_p, False),
                                         (x_sample.reshape(1, bs * ts, d), mods_s, cos_s, sin_s, True)):
        streams.append(dict(x=x0, mods=mods_g, sample=sample, cos1=cos[None], sin1=sin[None],
                            cos8=jnp.tile(cos, (1, HEADS))[None], sin8=jnp.tile(sin, (1, HEADS))[None],
                            new_c=[], new_n=[], new_m=[]))

    for l in range(DEPTH):
        lg = ln_g[l].reshape(2, 1, d)
        lb = ln_b[l].reshape(2, 1, d)
        moe_in = []
        for st in streams:
            x, sample = st["x"], st["sample"]
            h2_all, row0, n_all = None, 0, st["x"].shape[0] * st["x"].shape[1]
            cos8, sin8 = st["cos8"], st["sin8"]
            latb, krb = st.get("latb"), st.get("krb")
            new_c, new_n, new_m = st["new_c"], st["new_n"], st["new_m"]
            shift1, scale1, gate1, shift2, scale2, gate2 = [(st["mods"], l, i) for i in range(N_MODS)]
            post = dict(x=x, gate=gate1, ln_g=lg[0], ln_b=lb[0], scale2=scale2, shift2=shift2, rw=rw, rwh=rwh, rb=rb,
                        h2_rows=n_all, h2_row0=row0, h2_prev=h2_all)
            if l < N_A_LAYERS:
                qkv, o, gates = _mlstm_in(x, scale1, shift1, w_in[l], bg[l], tm)
                kt = jnp.swapaxes(qkv[:, :, hqk:2 * hqk], 1, 2)
                grow = jnp.swapaxes(gates[:, :, :2 * HEADS], 1, 2)
                if sample:
                    n0 = state_n[l].reshape(bs, hqk)
                    m0 = state_m[l]
                    nrows = jnp.repeat(n0, ts, axis=0)[None]
                    m_tok = jnp.repeat(m0, ts, axis=0)
                    mcol = jnp.pad(m_tok, ((0, 0), (0, LANES - HEADS)))[None]
                    mrow = m_tok.T[None]
                    u, c_stack, n_new, m_new = _mlstm_sample(qkv, kt, o, gates, grow, ng[l], state_C, l,
                                                             st.get("c_stack"), n0, nrows, m0, mcol, mrow, ts)
                    st["c_stack"] = c_stack
                    n_new = n_new.reshape(bs, HEADS, A_DQK)
                else:
                    u, c_new, n_rep, m_rep = _mlstm_prompt(qkv, kt, o, gates, grow, ng[l])
                    n_new = n_rep[..., 0]
                    m_new = m_rep[..., 0]
                    new_c.append(c_new)
                new_n.append(n_new)
                new_m.append(m_new)
                x1, h2_all, route = _mlstm_out(u, w_out[l], tm, **post)
            else:
                j = l - N_A_LAYERS
                ql, qr = _mla_q(x, scale1, shift1, wdq[j], qg[j], wuq[j], wuk, cos8, sin8, tm)
                if sample:
                    ql_s = jnp.transpose(ql.reshape(HEADS, bs, ts, B_KV_RANK), (1, 0, 2, 3)).reshape(
                        bs, HEADS * ts, B_KV_RANK)
                    qr_s = jnp.transpose(qr.reshape(HEADS, bs, ts, B_D_ROPE), (1, 0, 2, 3)).reshape(
                        bs, HEADS * ts, B_D_ROPE)
                    new_pad = 16
                    nl = jnp.pad(latb.reshape(bs, ts, B_KV_RANK), ((0, 0), (0, new_pad - ts), (0, 0)))
                    nk = jnp.pad(krb.reshape(bs, ts, B_D_ROPE), ((0, 0), (0, new_pad - ts), (0, 0)))
                    ol = _paged_attention(page_table, ql_s, qr_s, cache_latent, cache_krt, nl, nk, ts)
                    ol = jnp.transpose(ol.reshape(bs, HEADS, ts, B_KV_RANK), (1, 0, 2, 3)).reshape(
                        1, HEADS, bs * ts, B_KV_RANK)
                else:
                    ol = _flash(ql, qr, latb, krb, 512)
                x1, h2_all, route = _mla_out(ol, wuv, wo[j], tm, **post)
            moe_in.append(([(x1, route, gate2)], h2_all))
        for st, (group, h2_g) in zip(streams, moe_in):
            x_new, = _moe(group, h2_g, lg[1], lb[1], e_w_gate, e_w_up, e_w_down, l, tm)
            st["x"] = x_new
            if l == N_A_LAYERS - 1:
                st["lat"], st["kr"], st["latb"], st["krb"] = _shared_kv(x_new, w_kv, kvg, st["cos1"], st["sin1"], tm)

    sp, ss = streams
    return (sp["x"], ss["x"].reshape(bs, ts, d),
            jnp.stack(sp["new_c"]), jnp.stack(sp["new_n"]), jnp.stack(sp["new_m"]), sp["lat"], sp["kr"],
            ss["c_stack"], jnp.stack(ss["new_n"]), jnp.stack(ss["new_m"]),
            ss["lat"].reshape(bs, ts, B_KV_RANK), ss["kr"].reshape(bs, ts, B_D_ROPE))
```

```python
import functools

import jax
import jax.numpy as jnp
from jax import lax
from jax.experimental import pallas as pl
from jax.experimental.pallas import tpu as pltpu

f32 = jnp.float32
bf16 = jnp.bfloat16

DEPTH = 4
N_A_LAYERS = 2
HEADS = 8
A_DQK = 64
A_DV = 128
B_D_NOPE = 128
B_D_ROPE = 64
B_KV_RANK = 256
ROPE_THETA = 10000.0
ATTN_SCALE = (B_D_NOPE + B_D_ROPE) ** -0.5
N_EXPERTS = 16
N_GROUPS = 4
PER_GROUP = 4
PAGE = 128
ALPHA = (2 * DEPTH) ** 0.25
EPS = 1e-6
NEG_INF = float("-inf")
LOG2E = 1.4426950408889634
SOFTMAX_EXP2_SCALE = ATTN_SCALE * LOG2E

VMEM_LIMIT_BYTES = 56 * 1024 * 1024
LANES = 128
MLSTM_CHUNK = 128
SAMPLE_GROUP = 16
MOE_BLOCK = 256
GATHER_PARTS = 2
PAGES_PER_STEP = 16
PAGED_CHAINS = 2
FLASH_HEADS = 8


def _cparams(sem):
    return pltpu.CompilerParams(dimension_semantics=sem, vmem_limit_bytes=VMEM_LIMIT_BYTES)


def _dot(a, b):
    return jnp.dot(a, b, preferred_element_type=f32)


def _dot_nt(a, b):
    return lax.dot_general(a, b, (((1,), (1,)), ((), ())), preferred_element_type=f32)


def _dot_exact(a, b):
    return jnp.dot(a, b, preferred_element_type=f32, precision=lax.Precision.HIGHEST)


def _sigmoid(x):
    return 1.0 / (1.0 + jnp.exp(-x))


def _log_sigmoid(x):
    return jnp.minimum(x, 0.0) - jnp.log(1.0 + jnp.exp(-jnp.abs(x)))


def _ada_kernel(c_ref, w_ref, b_ref, o_ref):
    c = c_ref[...]
    sc = (c * _sigmoid(c)).astype(bf16)
    o_ref[0] = _dot(sc, w_ref[0].astype(bf16)) + b_ref[0]


def _ada_mods(c_all, ada_w, ada_b):
    depth, d, e = ada_w.shape
    bc = c_all.shape[0]
    tn = 1536
    return pl.pallas_call(
        _ada_kernel,
        grid=(depth, e // tn),
        in_specs=[
            pl.BlockSpec((bc, d), lambda l, j: (0, 0)),
            pl.BlockSpec((1, d, tn), lambda l, j: (l, 0, j)),
            pl.BlockSpec((1, 1, tn), lambda l, j: (l, 0, j)),
        ],
        out_specs=pl.BlockSpec((1, bc, tn), lambda l, j: (l, 0, j)),
        out_shape=jax.ShapeDtypeStruct((depth, bc, e), f32),
        compiler_params=_cparams(("parallel", "parallel")),
        name="ada_mods",
    )(c_all, ada_w, ada_b.reshape(depth, 1, e))


N_MODS = 6


def _mod_spec(mod, tm):
    mods, layer, chunk = mod
    d = mods.shape[-1] // N_MODS
    if mods.shape[2] == 1:
        return pl.BlockSpec((1, 1, 1, d), lambda g, i: (layer, g, 0, chunk))
    return pl.BlockSpec((1, 1, tm, d), lambda g, i: (layer, g, i, chunk))


def _tok_spec(tm, d):
    return pl.BlockSpec((1, tm, d), lambda g, i: (g, i, 0))


def _full_spec(shape):
    nd = len(shape)
    return pl.BlockSpec(shape, lambda g, i: (0,) * nd)


def _route(h2, rw_ref, rwh_ref, rb_ref):
    hi = h2.astype(bf16)
    lo = (h2 - hi.astype(f32)).astype(bf16)
    p = _dot(hi, rw_ref[...]) + _dot(lo, rwh_ref[...])
    pt = p.T
    logits = pt[0:N_EXPERTS] + pt[N_EXPERTS:2 * N_EXPERTS]
    sc = _sigmoid(logits)
    sel = sc + rb_ref[...]
    a = [sel[PER_GROUP * j:PER_GROUP * (j + 1)] for j in range(PER_GROUP)]
    s = [sc[PER_GROUP * j:PER_GROUP * (j + 1)] for j in range(PER_GROUP)]
    hi01, lo01 = jnp.maximum(a[0], a[1]), jnp.minimum(a[0], a[1])
    hi23, lo23 = jnp.maximum(a[2], a[3]), jnp.minimum(a[2], a[3])
    gs = jnp.maximum(hi01, hi23) + jnp.maximum(jnp.minimum(hi01, hi23), jnp.maximum(lo01, lo23))
    best = gs[0:1]
    grp = jnp.zeros_like(best)
    for g in range(1, N_GROUPS):
        better = gs[g:g + 1] > best
        grp = jnp.where(better, float(g), grp)
        best = jnp.where(better, gs[g:g + 1], best)
    mv, sv = [], []
    for j in range(PER_GROUP):
        m_j = a[j][0:1]
        s_j = s[j][0:1]
        for g in range(1, N_GROUPS):
            m_j = jnp.where(grp == float(g), a[j][g:g + 1], m_j)
            s_j = jnp.where(grp == float(g), s[j][g:g + 1], s_j)
        mv.append(m_j)
        sv.append(s_j)

    def first_argmax(vals):
        bv, bi = vals[0], jnp.zeros_like(vals[0])
        for j in range(1, PER_GROUP):
            better = vals[j] > bv
            bi = jnp.where(better, float(j), bi)
            bv = jnp.where(better, vals[j], bv)
        return bi

    i1 = first_argmax(mv)
    i2 = first_argmax([jnp.where(i1 == float(j), NEG_INF, mv[j]) for j in range(PER_GROUP)])
    w1, w2 = sv[0], sv[0]
    for j in range(1, PER_GROUP):
        w1 = jnp.where(i1 == float(j), sv[j], w1)
        w2 = jnp.where(i2 == float(j), sv[j], w2)
    tot = w1 + w2
    e1 = grp * float(PER_GROUP) + i1
    e2 = grp * float(PER_GROUP) + i2
    rows = jnp.concatenate([e1, e2, w1 / tot, w2 / tot, jnp.zeros((LANES - 4, e1.shape[1]), f32)], axis=0)
    return rows.T


N_POST_IN = 9


def _residual_ln_route(y, post_refs, x1_ref, h2_ref, route_ref):
    x_ref, gate_ref, lng_ref, lnb_ref, sc_ref, sh_ref, rw_ref, rwh_ref, rb_ref = post_refs
    r = ALPHA * x_ref[0] + gate_ref[0, 0] * y
    mu = jnp.mean(r, axis=-1, keepdims=True)
    cen = r - mu
    var = jnp.mean(cen * cen, axis=-1, keepdims=True)
    x1 = cen * lax.rsqrt(var + EPS) * lng_ref[...] + lnb_ref[...]
    x1_ref[0] = x1
    h2 = x1 * (1.0 + sc_ref[0, 0]) + sh_ref[0, 0]
    h2_ref[...] = h2
    route_ref[0] = _route(h2, rw_ref, rwh_ref, rb_ref)


def _post_call(kernel_fn, name, mixer_args, mixer_specs, x, gate, ln_g, ln_b, scale2, shift2, rw, rwh, rb, tm,
               h2_rows, h2_row0, h2_prev):
    g, t, d = x.shape
    steps = t // tm
    blk0 = h2_row0 // tm
    in_specs = list(mixer_specs) + [
        _tok_spec(tm, d), _mod_spec(gate, tm), _full_spec((1, d)), _full_spec((1, d)),
        _mod_spec(scale2, tm), _mod_spec(shift2, tm),
        _full_spec((d, LANES)), _full_spec((d, LANES)), _full_spec((N_EXPERTS, 1)),
    ]
    args = list(mixer_args) + [x, gate[0], ln_g, ln_b, scale2[0], shift2[0], rw, rwh, rb]
    aliases = {}
    if h2_prev is not None:
        aliases = {len(args): 1}
        in_specs.append(pl.BlockSpec(memory_space=pl.ANY))
        args.append(h2_prev)
    return pl.pallas_call(
        kernel_fn,
        grid=(g, steps),
        in_specs=in_specs,
        out_specs=[_tok_spec(tm, d),
                   pl.BlockSpec((tm, d), lambda gi, i: (blk0 + gi * steps + i, 0)),
                   _tok_spec(tm, LANES)],
        out_shape=[jax.ShapeDtypeStruct((g, t, d), f32),
                   jax.ShapeDtypeStruct((h2_rows, d), f32),
                   jax.ShapeDtypeStruct((g, t, LANES), f32)],
        input_output_aliases=aliases,
        compiler_params=_cparams(("parallel", "parallel")),
        name=name,
    )(*args)


def _mlstm_in_kernel(x_ref, sc_ref, sh_ref, w_ref, bg_ref, qkv_ref, o_ref, g_ref):
    h = x_ref[0] * (1.0 + sc_ref[0, 0]) + sh_ref[0, 0]
    z = _dot(h.astype(bf16), w_ref[...])
    hqk2 = 2 * HEADS * A_DQK
    hv = HEADS * A_DV
    qkv_ref[0] = z[:, :hqk2 + hv].astype(bf16)
    o_ref[0] = z[:, hqk2 + hv:hqk2 + 2 * hv]
    g_ref[0] = z[:, hqk2 + 2 * hv:] + bg_ref[...]


def _mlstm_in(x, scale, shift, w_pad, bg_pad, tm):
    g, t, d = x.shape
    n_qkv = 2 * HEADS * A_DQK + HEADS * A_DV
    hv = HEADS * A_DV
    return pl.pallas_call(
        _mlstm_in_kernel,
        grid=(g, t // tm),
        in_specs=[_tok_spec(tm, d), _mod_spec(scale, tm), _mod_spec(shift, tm),
                  _full_spec(w_pad.shape), _full_spec((1, LANES))],
        out_specs=[_tok_spec(tm, n_qkv), _tok_spec(tm, hv), _tok_spec(tm, LANES)],
        out_shape=[jax.ShapeDtypeStruct((g, t, n_qkv), bf16),
                   jax.ShapeDtypeStruct((g, t, hv), f32),
                   jax.ShapeDtypeStruct((g, t, LANES), f32)],
        compiler_params=_cparams(("parallel", "parallel")),
        name="mlstm_in",
    )(x, scale[0], shift[0], w_pad, bg_pad)


def _mlstm_out_kernel(u_ref, w_ref, *refs):
    y = _dot(u_ref[0], w_ref[...])
    _residual_ln_route(y, refs[:N_POST_IN], *refs[-3:])


def _mlstm_out(u, w_out, tm, **post):
    return _post_call(_mlstm_out_kernel, "mlstm_out", [u, w_out],
                      [_tok_spec(tm, u.shape[-1]), _full_spec(w_out.shape)], tm=tm, **post)


def _head_norm_gate(hh, ng, o):
    mu = jnp.mean(hh, axis=-1, keepdims=True)
    cen = hh - mu
    var = jnp.mean(cen * cen, axis=-1, keepdims=True)
    return (cen * lax.rsqrt(var + EPS) * ng) * _sigmoid(o)


def _mlstm_prompt_kernel(qkv_ref, kt_ref, o_ref, gcol_ref, grow_ref, ng_ref,
                         u_ref, c_out, n_out, m_out, c_s, n_s, m_s):
    c = pl.program_id(1)
    nc = pl.num_programs(1)
    L = qkv_ref.shape[1]
    hqk = HEADS * A_DQK

    @pl.when(c == 0)
    def _():
        c_s[...] = jnp.zeros_like(c_s)
        n_s[...] = jnp.zeros_like(n_s)
        m_s[...] = jnp.zeros_like(m_s)

    r_i = lax.broadcasted_iota(jnp.int32, (L, L), 0)
    c_i = lax.broadcasted_iota(jnp.int32, (L, L), 1)
    causal = c_i <= r_i
    tri = causal.astype(f32)
    tri_t = (r_i <= c_i).astype(f32)

    gcol = gcol_ref[0]
    grow = grow_ref[0]
    b_col = _dot_exact(tri, _log_sigmoid(gcol))
    b_row = _dot_exact(_log_sigmoid(grow[HEADS:2 * HEADS]), tri_t)
    li_row = grow[0:HEADS]

    qkv = qkv_ref[0]
    kt = kt_ref[0]
    o_all = o_ref[0]
    ng = ng_ref[...]
    c_old, n_old, m_old = c_s[...], n_s[...], m_s[...]
    hs = range(HEADS)
    q = [qkv[:, A_DQK * h:A_DQK * (h + 1)] for h in hs]
    v = [qkv[:, 2 * hqk + A_DV * h:2 * hqk + A_DV * (h + 1)] for h in hs]
    kth = [kt[A_DQK * h:A_DQK * (h + 1), :] for h in hs]
    s_qk = [_dot(q[h], kth[h]) for h in hs]
    bc = [b_col[:, HEADS + h:HEADS + h + 1] for h in hs]
    br = [b_row[h:h + 1] for h in hs]
    lir = [li_row[h:h + 1] for h in hs]
    m_prev = [m_old[h:h + 1, 0:1] for h in hs]
    d = [jnp.where(causal, bc[h] - br[h] + lir[h], NEG_INF) for h in hs]
    a = [bc[h] + m_prev[h] for h in hs]
    d_max = [jnp.max(d[h], axis=-1, keepdims=True) for h in hs]
    m_t = [jnp.maximum(a[h], d_max[h]) for h in hs]
    p = [s_qk[h] * jnp.exp(d[h] - m_t[h]) for h in hs]
    inter = [jnp.exp(a[h] - m_t[h]) for h in hs]
    qc = [_dot(q[h], jnp.concatenate([c_old[h], n_old[h]], axis=1).astype(bf16)) for h in hs]
    p_sum = [jnp.sum(p[h], axis=-1, keepdims=True) for h in hs]
    num = [_dot(p[h].astype(bf16), v[h]) + inter[h] * qc[h][:, :A_DV] for h in hs]
    den = [p_sum[h] + inter[h] * qc[h][:, A_DV:A_DV + 1] for h in hs]
    hh = [num[h] / jnp.maximum(jnp.abs(den[h]), jnp.exp(-m_t[h])) for h in hs]
    mu = [jnp.mean(hh[h], axis=-1, keepdims=True) for h in hs]
    cen = [hh[h] - mu[h] for h in hs]
    var = [jnp.mean(cen[h] * cen[h], axis=-1, keepdims=True) for h in hs]
    u_parts = [((cen[h] * lax.rsqrt(var[h] + EPS) * ng[:, A_DV * h:A_DV * (h + 1)])
                * _sigmoid(o_all[:, A_DV * h:A_DV * (h + 1)])).astype(bf16) for h in hs]

    b_end = [br[h][:, L - 1:L] for h in hs]
    g_row = [b_end[h] - br[h] + lir[h] for h in hs]
    g_max = [jnp.max(g_row[h], axis=-1, keepdims=True) for h in hs]
    m_new = [jnp.maximum(b_end[h] + m_prev[h], g_max[h]) for h in hs]
    decay = [jnp.exp(b_end[h] + m_prev[h] - m_new[h]) for h in hs]
    kw = [kth[h].astype(f32) * jnp.exp(g_row[h] - m_new[h]) for h in hs]
    kw_sum = [jnp.sum(kw[h], axis=-1, keepdims=True) for h in hs]
    c_parts = [decay[h] * c_old[h] + _dot(kw[h].astype(bf16), v[h]) for h in hs]
    n_parts = [decay[h] * n_old[h] + kw_sum[h] for h in hs]
    m_parts = [jnp.broadcast_to(m_new[h], (1, LANES)) for h in hs]

    u_ref[0] = jnp.concatenate(u_parts, axis=1)
    c_s[...] = jnp.stack(c_parts)
    n_s[...] = jnp.stack(n_parts)
    m_s[...] = jnp.concatenate(m_parts, axis=0)

    @pl.when(c == nc - 1)
    def _():
        c_out[0] = c_s[...]
        n_out[0] = n_s[...]
        m_out[0] = m_s[...]


def _mlstm_prompt(qkv, kt, o, gcol, grow, ng):
    b, t, _ = qkv.shape
    L = MLSTM_CHUNK
    hv = HEADS * A_DV
    return pl.pallas_call(
        _mlstm_prompt_kernel,
        grid=(b, t // L),
        in_specs=[
            pl.BlockSpec((1, L, qkv.shape[-1]), lambda g, c: (g, c, 0)),
            pl.BlockSpec((1, HEADS * A_DQK, L), lambda g, c: (g, 0, c)),
            pl.BlockSpec((1, L, hv), lambda g, c: (g, c, 0)),
            pl.BlockSpec((1, L, LANES), lambda g, c: (g, c, 0)),
            pl.BlockSpec((1, 2 * HEADS, L), lambda g, c: (g, 0, c)),
            pl.BlockSpec((1, hv), lambda g, c: (0, 0)),
        ],
        out_specs=[
            pl.BlockSpec((1, L, hv), lambda g, c: (g, c, 0)),
            pl.BlockSpec((1, HEADS, A_DQK, A_DV), lambda g, c: (g, 0, 0, 0)),
            pl.BlockSpec((1, HEADS, A_DQK, LANES), lambda g, c: (g, 0, 0, 0)),
            pl.BlockSpec((1, HEADS, LANES), lambda g, c: (g, 0, 0)),
        ],
        out_shape=[
            jax.ShapeDtypeStruct((b, t, hv), bf16),
            jax.ShapeDtypeStruct((b, HEADS, A_DQK, A_DV), f32),
            jax.ShapeDtypeStruct((b, HEADS, A_DQK, LANES), f32),
            jax.ShapeDtypeStruct((b, HEADS, LANES), f32),
        ],
        scratch_shapes=[
            pltpu.VMEM((HEADS, A_DQK, A_DV), f32),
            pltpu.VMEM((HEADS, A_DQK, LANES), f32),
            pltpu.VMEM((HEADS, LANES), f32),
        ],
        compiler_params=_cparams(("parallel", "arbitrary")),
        name="mlstm_prompt",
    )(qkv, kt, o, gcol, grow, ng)


def _mlstm_sample_kernel(qkv_ref, kt_ref, o_ref, gcol_ref, grow_ref, ng_ref, c0_ref, n0_ref, nrows_ref,
                         m0_ref, mcol_ref, mrow_ref, *refs, ts):
    u_ref, c_out, n_out, m_out = refs[-4:]
    R = qkv_ref.shape[1]
    G = R // ts
    shift = ts.bit_length() - 1
    hqk = HEADS * A_DQK

    r_i = lax.broadcasted_iota(jnp.int32, (R, R), 0)
    c_i = lax.broadcasted_iota(jnp.int32, (R, R), 1)
    same = (r_i >> shift) == (c_i >> shift)
    valid = same & (c_i <= r_i)
    tri = valid.astype(f32)
    tri_t = (same & (r_i <= c_i)).astype(f32)
    same_f = same.astype(f32)
    row_seq = lax.broadcasted_iota(jnp.int32, (R, 1), 0) >> shift
    g_i = lax.broadcasted_iota(jnp.int32, (G, R), 0)
    s_i = lax.broadcasted_iota(jnp.int32, (G, R), 1)
    bmask = (s_i >> shift) == g_i
    lastmask = s_i == g_i * ts + (ts - 1)
    g3 = lax.broadcasted_iota(jnp.int32, (G, A_DQK, R), 0)
    s3 = lax.broadcasted_iota(jnp.int32, (G, A_DQK, R), 2)
    bmask3 = (s3 >> shift) == g3
    lane_i = lax.broadcasted_iota(jnp.int32, (G, LANES), 1)

    gcol = gcol_ref[0]
    grow = grow_ref[0]
    lf_row = _log_sigmoid(grow[HEADS:2 * HEADS])
    b_col = _dot_exact(tri, _log_sigmoid(gcol))
    b_row = _dot_exact(lf_row, tri_t)
    bend_row = _dot_exact(lf_row, same_f)
    li_row = grow[0:HEADS]
    mcol = mcol_ref[0]
    mrow = mrow_ref[0]
    m0 = m0_ref[...]

    m_acc = jnp.zeros((G, LANES), f32)
    n_parts = []
    for h in range(HEADS):
        q_h = qkv_ref[0, :, A_DQK * h:A_DQK * (h + 1)]
        k_h = qkv_ref[0, :, hqk + A_DQK * h:hqk + A_DQK * (h + 1)]
        v_h = qkv_ref[0, :, 2 * hqk + A_DV * h:2 * hqk + A_DV * (h + 1)]
        kt_h = kt_ref[0, A_DQK * h:A_DQK * (h + 1), :]
        s_qk = _dot(q_h, kt_h)
        bc = b_col[:, HEADS + h:HEADS + h + 1]
        br = b_row[h:h + 1]
        lir = li_row[h:h + 1]
        d = jnp.where(valid, bc - br + lir, NEG_INF)
        a = bc + mcol[:, h:h + 1]
        m_t = jnp.maximum(a, jnp.max(d, axis=-1, keepdims=True))
        p = s_qk * jnp.exp(d - m_t)
        inter = jnp.exp(a - m_t)
        c_all = jnp.concatenate([c0_ref[0, g, h] for g in range(G)], axis=1).astype(bf16)
        qc = _dot(q_h, c_all)
        inter_c = jnp.zeros((R, A_DV), f32)
        for g in range(G):
            inter_c = jnp.where(row_seq == g, qc[:, A_DV * g:A_DV * (g + 1)], inter_c)
        qn = jnp.sum(q_h.astype(f32) * nrows_ref[0, :, A_DQK * h:A_DQK * (h + 1)], axis=-1, keepdims=True)
        num = _dot(p.astype(bf16), v_h) + inter * inter_c
        den = jnp.sum(p, axis=-1, keepdims=True) + inter * qn
        hh = num / jnp.maximum(jnp.abs(den), jnp.exp(-m_t))
        u = _head_norm_gate(hh, ng_ref[:, A_DV * h:A_DV * (h + 1)], o_ref[0, :, A_DV * h:A_DV * (h + 1)])
        u_ref[0, :, A_DV * h:A_DV * (h + 1)] = u.astype(bf16)

        g_row = bend_row[h:h + 1] - br + lir
        gmax = jnp.max(jnp.where(bmask, g_row, NEG_INF), axis=-1, keepdims=True)
        bend_b = jnp.sum(jnp.where(lastmask, br, 0.0), axis=-1, keepdims=True)
        m0_h = m0[:, h:h + 1]
        mnew_b = jnp.maximum(bend_b + m0_h, gmax)
        mnew_row = jnp.sum(jnp.where(bmask, mnew_b, 0.0), axis=0, keepdims=True)
        w_row = jnp.exp(g_row - mnew_row)
        decay_b = jnp.exp(bend_b + m0_h - mnew_b)
        kw = kt_h.astype(f32) * w_row
        kw3 = jnp.where(bmask3, jnp.broadcast_to(kw[None], (G, A_DQK, R)), 0.0)
        upd = _dot(kw3.reshape(G * A_DQK, R).astype(bf16), v_h)
        for g in range(G):
            c_out[0, g, h] = decay_b[g:g + 1, :] * c0_ref[0, g, h] + upd[A_DQK * g:A_DQK * (g + 1)]
        wm = jnp.where(bmask, w_row, 0.0)
        n_parts.append(decay_b * n0_ref[:, A_DQK * h:A_DQK * (h + 1)] + _dot(wm.astype(bf16), k_h))
        m_acc = jnp.where(lane_i == h, mnew_b, m_acc)

    n_out[...] = jnp.concatenate(n_parts, axis=1)
    m_out[...] = m_acc[:, :HEADS]


def _mlstm_sample(qkv, kt, o, gcol, grow, ng, c0_all, layer, c_prev, n0, nrows, m0, mcol, mrow, ts):
    _, t, _ = qkv.shape
    n_layers, bs = c0_all.shape[:2]
    G = SAMPLE_GROUP
    R = G * ts
    hv = HEADS * A_DV
    hqk = HEADS * A_DQK
    args = [qkv, kt, o, gcol, grow, ng, c0_all, n0, nrows, m0, mcol, mrow]
    extra_specs, aliases = [], {}
    if c_prev is not None:
        aliases = {len(args): 1}
        extra_specs = [pl.BlockSpec(memory_space=pl.ANY)]
        args.append(c_prev)
    return pl.pallas_call(
        functools.partial(_mlstm_sample_kernel, ts=ts),
        grid=(t // R,),
        input_output_aliases=aliases,
        in_specs=[
            pl.BlockSpec((1, R, qkv.shape[-1]), lambda i: (0, i, 0)),
            pl.BlockSpec((1, hqk, R), lambda i: (0, 0, i)),
            pl.BlockSpec((1, R, hv), lambda i: (0, i, 0)),
            pl.BlockSpec((1, R, LANES), lambda i: (0, i, 0)),
            pl.BlockSpec((1, 2 * HEADS, R), lambda i: (0, 0, i)),
            pl.BlockSpec((1, hv), lambda i: (0, 0)),
            pl.BlockSpec((1, G, HEADS, A_DQK, A_DV), lambda i: (layer, i, 0, 0, 0)),
            pl.BlockSpec((G, hqk), lambda i: (i, 0)),
            pl.BlockSpec((1, R, hqk), lambda i: (0, i, 0)),
            pl.BlockSpec((G, HEADS), lambda i: (i, 0)),
            pl.BlockSpec((1, R, LANES), lambda i: (0, i, 0)),
            pl.BlockSpec((1, HEADS, R), lambda i: (0, 0, i)),
        ] + extra_specs,
        out_specs=[
            pl.BlockSpec((1, R, hv), lambda i: (0, i, 0)),
            pl.BlockSpec((1, G, HEADS, A_DQK, A_DV), lambda i: (layer, i, 0, 0, 0)),
            pl.BlockSpec((G, hqk), lambda i: (i, 0)),
            pl.BlockSpec((G, HEADS), lambda i: (i, 0)),
        ],
        out_shape=[
            jax.ShapeDtypeStruct((1, t, hv), bf16),
            jax.ShapeDtypeStruct((n_layers, bs, HEADS, A_DQK, A_DV), f32),
            jax.ShapeDtypeStruct((bs, hqk), f32),
            jax.ShapeDtypeStruct((bs, HEADS), f32),
        ],
        compiler_params=_cparams(("parallel",)),
        name="mlstm_sample",
    )(*args)


def _expert_kernel(be_ref, nu_ref, first_ref, *refs, n_parts):
    x_refs = refs[:n_parts]
    wg_ref, wu_ref, wd_ref, y_ref, wg_s, wu_s, wd_s = refs[n_parts:]
    i = pl.program_id(0)
    part_blocks = pl.num_programs(0) // n_parts

    @pl.when(first_ref[i] == 1)
    def _():
        wg_s[...] = wg_ref[0, 0].astype(bf16)
        wu_s[...] = wu_ref[0, 0].astype(bf16)
        wd_s[...] = wd_ref[0, 0].astype(bf16)

    @pl.when(i < nu_ref[0])
    def _():
        x = x_refs[n_parts - 1][...]
        for part in range(n_parts - 2, -1, -1):
            x = jnp.where(i < (part + 1) * part_blocks, x_refs[part][...], x)
        x = x.astype(bf16)
        g = _dot(x, wg_s[...])
        u = _dot(x, wu_s[...])
        hid = (g * _sigmoid(g)) * u
        y_ref[...] = _dot(hid.astype(bf16), wd_s[...])

    @pl.when(i >= nu_ref[0])
    def _():
        y_ref[...] = jnp.zeros_like(y_ref)


def _experts(x_parts, block_e, n_used, first, wg, wu, wd, layer):
    n_parts = len(x_parts)
    pp, d = x_parts[0].shape
    de = wg.shape[-1]
    bm = MOE_BLOCK
    pb = pp // bm
    w_map = lambda i, be, nu, fi: (layer, be[i], 0, 0)
    part_specs = [pl.BlockSpec((bm, d), functools.partial(
        lambda i, be, nu, fi, part: (jnp.clip(i - part * pb, 0, pb - 1), 0), part=part)) for part in range(n_parts)]
    return pl.pallas_call(
        functools.partial(_expert_kernel, n_parts=n_parts),
        grid_spec=pltpu.PrefetchScalarGridSpec(
            num_scalar_prefetch=3,
            grid=(n_parts * pb,),
            in_specs=part_specs + [
                pl.BlockSpec((1, 1, d, de), w_map),
                pl.BlockSpec((1, 1, d, de), w_map),
                pl.BlockSpec((1, 1, de, d), w_map),
            ],
            out_specs=pl.BlockSpec((bm, d), lambda i, be, nu, fi: (i, 0)),
            scratch_shapes=[pltpu.VMEM((d, de), bf16), pltpu.VMEM((d, de), bf16), pltpu.VMEM((de, d), bf16)],
        ),
        out_shape=jax.ShapeDtypeStruct((n_parts * pp, d), f32),
        compiler_params=_cparams(("arbitrary",)),
        name="moe_experts",
    )(block_e, n_used, first, *x_parts, wg, wu, wd)


def _combine_kernel(x_ref, ya_ref, yb_ref, w_ref, gate_ref, lng_ref, lnb_ref, o_ref):
    w = w_ref[0]
    y = w[:, 2:3] * ya_ref[...] + w[:, 3:4] * yb_ref[...]
    r = ALPHA * x_ref[0] + gate_ref[0, 0] * y
    mu = jnp.mean(r, axis=-1, keepdims=True)
    cen = r - mu
    var = jnp.mean(cen * cen, axis=-1, keepdims=True)
    o_ref[0] = cen * lax.rsqrt(var + EPS) * lng_ref[...] + lnb_ref[...]


def _combine(x1, ya, yb, row0, wexp, gate, ln_g, ln_b, tm):
    g, t, d = x1.shape
    steps = t // tm
    blk0 = row0 // tm
    y_spec = pl.BlockSpec((tm, d), lambda gi, i: (blk0 + gi * steps + i, 0))
    return pl.pallas_call(
        _combine_kernel,
        grid=(g, steps),
        in_specs=[_tok_spec(tm, d), y_spec, y_spec,
                  _tok_spec(tm, LANES), _mod_spec(gate, tm), _full_spec((1, d)), _full_spec((1, d))],
        out_specs=_tok_spec(tm, d),
        out_shape=jax.ShapeDtypeStruct((g, t, d), f32),
        compiler_params=_cparams(("parallel", "parallel")),
        name="moe_combine",
    )(x1, ya, yb, wexp, gate[0], ln_g, ln_b)


def _moe(groups, h2_all, ln_g, ln_b, wg, wu, wd, layer, tm):
    bm = MOE_BLOCK
    d = groups[0][0].shape[-1]
    sizes = [x1.shape[0] * x1.shape[1] for x1, _, _ in groups]
    n = sum(sizes)
    nk = 2 * n
    expert = jnp.concatenate([r[:, :, 0:2].reshape(-1, 2) for _, r, _ in groups], axis=0).astype(jnp.int32)
    e_flat = expert.reshape(nk)
    onehot = (e_flat[:, None] == jnp.arange(N_EXPERTS, dtype=jnp.int32)[None, :]).astype(jnp.int32)
    csum = jnp.cumsum(onehot, axis=0)
    counts = csum[-1]
    rank = jnp.sum(csum * onehot, axis=1) - 1
    padded = (counts + bm - 1) // bm * bm
    pends = jnp.cumsum(padded)
    pstarts = pends - padded
    dest = jnp.sum(onehot * pstarts[None, :], axis=1) + rank
    n_blocks = nk // bm + N_EXPERTS
    n_used = (pends[-1] // bm).astype(jnp.int32)
    blk = jnp.minimum(jnp.arange(n_blocks, dtype=jnp.int32), n_used - 1)
    block_e = jnp.minimum(jnp.sum((pends[None, :] <= (blk * bm)[:, None]).astype(jnp.int32), axis=1),
                          N_EXPERTS - 1).astype(jnp.int32)
    first = jnp.concatenate([jnp.ones((1,), jnp.int32), (block_e[1:] != block_e[:-1]).astype(jnp.int32)])
    tok = jnp.arange(nk, dtype=jnp.int32) // 2
    tok_pad = jnp.zeros((n_blocks * bm,), jnp.int32).at[dest].set(tok, unique_indices=True, mode="promise_in_bounds")
    part_rows = n_blocks // GATHER_PARTS * bm
    x_parts = [jnp.take(h2_all, tok_pad[c * part_rows:(c + 1) * part_rows], axis=0, mode="clip")
               for c in range(GATHER_PARTS)]
    y = _experts(x_parts, block_e, n_used.reshape(1), first, wg, wu, wd, layer)
    dest2 = dest.reshape(n, 2)
    ya = jnp.take(y, dest2[:, 0], axis=0, mode="clip")
    yb = jnp.take(y, dest2[:, 1], axis=0, mode="clip")
    outs, off = [], 0
    for (x1, route, gate2), sz in zip(groups, sizes):
        outs.append(_combine(x1, ya, yb, off, route, gate2, ln_g, ln_b, tm))
        off += sz
    return outs


def _kv_kernel(x_ref, w_ref, g_ref, cos_ref, sin_ref, lat_ref, kr_ref, latb_ref, krb_ref):
    kva = _dot(x_ref[0].astype(bf16), w_ref[...])
    latp = kva[:, :B_KV_RANK]
    lat = latp * lax.rsqrt(jnp.mean(latp * latp, axis=-1, keepdims=True) + EPS) * g_ref[...]
    kr = kva[:, B_KV_RANK:B_KV_RANK + B_D_ROPE] * cos_ref[0] \
        + kva[:, B_KV_RANK + LANES:B_KV_RANK + LANES + B_D_ROPE] * sin_ref[0]
    lat_ref[0] = lat
    kr_ref[0] = kr
    latb_ref[0] = lat.astype(bf16)
    krb_ref[0] = kr.astype(bf16)


def _shared_kv(x, w_pad, g, cos, sin, tm):
    gg, t, d = x.shape
    rope_spec = pl.BlockSpec((1, tm, B_D_ROPE), lambda gi, i: (0, i, 0)) if cos.shape[0] == 1 else \
        _tok_spec(tm, B_D_ROPE)
    return pl.pallas_call(
        _kv_kernel,
        grid=(gg, t // tm),
        in_specs=[_tok_spec(tm, d), _full_spec(w_pad.shape), _full_spec((1, B_KV_RANK)), rope_spec, rope_spec],
        out_specs=[_tok_spec(tm, B_KV_RANK), _tok_spec(tm, B_D_ROPE),
                   _tok_spec(tm, B_KV_RANK), _tok_spec(tm, B_D_ROPE)],
        out_shape=[jax.ShapeDtypeStruct((gg, t, B_KV_RANK), f32), jax.ShapeDtypeStruct((gg, t, B_D_ROPE), f32),
                   jax.ShapeDtypeStruct((gg, t, B_KV_RANK), bf16), jax.ShapeDtypeStruct((gg, t, B_D_ROPE), bf16)],
        compiler_params=_cparams(("parallel", "parallel")),
        name="shared_kv",
    )(x, w_pad, g, cos, sin)


def _mla_q_kernel(x_ref, sc_ref, sh_ref, wdq_ref, qg_ref, wuq_ref, wuk_ref, cos_ref, sin_ref, ql_ref, qr_ref):
    h = x_ref[0] * (1.0 + sc_ref[0, 0]) + sh_ref[0, 0]
    cq = _dot(h.astype(bf16), wdq_ref[...])
    cq = cq * lax.rsqrt(jnp.mean(cq * cq, axis=-1, keepdims=True) + EPS) * qg_ref[...]
    q = _dot(cq.astype(bf16), wuq_ref[...])
    n_nope = HEADS * B_D_NOPE
    n_rope = HEADS * B_D_ROPE
    rot = q[:, n_nope:n_nope + n_rope] * cos_ref[0] + q[:, n_nope + n_rope:] * sin_ref[0]
    for hd in range(HEADS):
        qn = q[:, B_D_NOPE * hd:B_D_NOPE * (hd + 1)].astype(bf16)
        ql_ref[0, hd] = _dot(qn, wuk_ref[hd]).astype(bf16)
        qr_ref[0, hd] = rot[:, B_D_ROPE * hd:B_D_ROPE * (hd + 1)].astype(bf16)


def _mla_q(x, scale, shift, wdq, qg, wuq, wuk, cos8, sin8, tm):
    g, t, d = x.shape
    n_rope = HEADS * B_D_ROPE
    rope_spec = pl.BlockSpec((1, tm, n_rope), lambda gi, i: (0, i, 0)) if cos8.shape[0] == 1 else \
        _tok_spec(tm, n_rope)
    return pl.pallas_call(
        _mla_q_kernel,
        grid=(g, t // tm),
        in_specs=[_tok_spec(tm, d), _mod_spec(scale, tm), _mod_spec(shift, tm),
                  _full_spec(wdq.shape), _full_spec(qg.shape), _full_spec(wuq.shape), _full_spec(wuk.shape),
                  rope_spec, rope_spec],
        out_specs=[pl.BlockSpec((1, HEADS, tm, B_KV_RANK), lambda gi, i: (gi, 0, i, 0)),
                   pl.BlockSpec((1, HEADS, tm, B_D_ROPE), lambda gi, i: (gi, 0, i, 0))],
        out_shape=[jax.ShapeDtypeStruct((g, HEADS, t, B_KV_RANK), bf16),
                   jax.ShapeDtypeStruct((g, HEADS, t, B_D_ROPE), bf16)],
        compiler_params=_cparams(("parallel", "parallel")),
        name="mla_q",
    )(x, scale[0], shift[0], wdq, qg, wuq, wuk, cos8, sin8)


def _softmax_step(s, v, m_s, l_s, acc_s, slot):
    rows, n = s.shape
    m_old = m_s[slot]
    if n % LANES == 0:
        _softmax_steps([s], [v], m_s, l_s, acc_s, [slot])
        return
    m_new = jnp.maximum(m_old, jnp.max(s, axis=-1, keepdims=True))
    p = jnp.exp2((s - m_new[:, 0:1]) * SOFTMAX_EXP2_SCALE)
    lane = lax.broadcasted_iota(jnp.int32, (rows, LANES), 1)
    psum = jnp.where(lane == 0, jnp.sum(p, axis=-1, keepdims=True), 0.0)
    alpha = jnp.exp2((m_old - m_new) * SOFTMAX_EXP2_SCALE)
    l_s[slot] = alpha * l_s[slot] + psum
    alpha_v = jnp.concatenate([alpha] * (v.shape[1] // LANES), axis=1)
    acc_s[slot] = alpha_v * acc_s[slot] + _dot(p.astype(bf16), v)
    m_s[slot] = m_new


def _softmax_steps(s_list, v_list, m_s, l_s, acc_s, slots):
    ids = range(len(s_list))
    nv = v_list[0].shape[1] // LANES
    m_old = [m_s[slot] for slot in slots]
    chunks = [[s[:, LANES * c:LANES * (c + 1)] for c in range(s.shape[1] // LANES)] for s in s_list]
    c_max = [functools.reduce(jnp.maximum, chunks[k]) for k in ids]
    r_max = [jnp.max(c_max[k], axis=-1, keepdims=True) for k in ids]
    m_new = [jnp.maximum(m_old[k], r_max[k]) for k in ids]
    ps = [[jnp.exp2((ch - m_new[k]) * SOFTMAX_EXP2_SCALE) for ch in chunks[k]] for k in ids]
    alpha = [jnp.exp2((m_old[k] - m_new[k]) * SOFTMAX_EXP2_SCALE) for k in ids]
    pv = [_dot(jnp.concatenate(ps[k], axis=1).astype(bf16), v_list[k]) for k in ids]
    for k, slot in enumerate(slots):
        l_s[slot] = alpha[k] * l_s[slot] + functools.reduce(jnp.add, ps[k])
        acc_s[slot] = jnp.concatenate([alpha[k]] * nv, axis=1) * acc_s[slot] + pv[k]
        m_s[slot] = m_new[k]


def _flash_kernel(ql_ref, qr_ref, lat_ref, kr_ref, o_ref, m_s, l_s, acc_s, *, tq):
    i = pl.program_id(1)
    nh = FLASH_HEADS
    r_i = lax.broadcasted_iota(jnp.int32, (tq, tq), 0)
    c_i = lax.broadcasted_iota(jnp.int32, (tq, tq), 1)
    causal = c_i <= r_i

    def group_body(hg, carry):
        m_s[...] = jnp.full_like(m_s, NEG_INF)
        l_s[...] = jnp.zeros_like(l_s)
        acc_s[...] = jnp.zeros_like(acc_s)

        def block(j, masked):
            start = pl.multiple_of(j * tq, tq)
            k_lat = lat_ref[0, pl.ds(start, tq), :]
            k_r = kr_ref[0, pl.ds(start, tq), :]
            s_list = []
            for hs in range(nh):
                h = hg * nh + hs
                s = _dot_nt(ql_ref[0, h], k_lat) + _dot_nt(qr_ref[0, h], k_r)
                s_list.append(jnp.where(causal, s, NEG_INF) if masked else s)
            _softmax_steps(s_list, [k_lat] * nh, m_s, l_s, acc_s, list(range(nh)))

        def kv_body(j, c2):
            block(j, False)
            return c2

        lax.fori_loop(0, i, kv_body, 0)
        block(i, True)
        for hs in range(nh):
            l = jnp.sum(l_s[hs], axis=-1, keepdims=True)
            o_ref[0, hg * nh + hs] = (acc_s[hs] / l).astype(bf16)
        return carry

    lax.fori_loop(0, HEADS // nh, group_body, 0)


def _flash(ql, qr, latb, krb, tq):
    b, _, t, _ = ql.shape
    nh = FLASH_HEADS
    return pl.pallas_call(
        functools.partial(_flash_kernel, tq=tq),
        grid=(b, t // tq),
        in_specs=[
            pl.BlockSpec((1, HEADS, tq, B_KV_RANK), lambda g, i: (g, 0, i, 0)),
            pl.BlockSpec((1, HEADS, tq, B_D_ROPE), lambda g, i: (g, 0, i, 0)),
            pl.BlockSpec((1, t, B_KV_RANK), lambda g, i: (g, 0, 0)),
            pl.BlockSpec((1, t, B_D_ROPE), lambda g, i: (g, 0, 0)),
        ],
        out_specs=pl.BlockSpec((1, HEADS, tq, B_KV_RANK), lambda g, i: (g, 0, i, 0)),
        out_shape=jax.ShapeDtypeStruct((b, HEADS, t, B_KV_RANK), bf16),
        scratch_shapes=[pltpu.VMEM((nh, tq, LANES), f32), pltpu.VMEM((nh, tq, LANES), f32),
                        pltpu.VMEM((nh, tq, B_KV_RANK), f32)],
        compiler_params=_cparams(("parallel", "arbitrary")),
        name="mla_flash",
    )(ql, qr, latb, krb)


def _paged_kernel(pt_ref, ql_ref, qr_ref, *rest, ts, new_pad):
    np_ = PAGES_PER_STEP
    nc = PAGED_CHAINS
    per = np_ // nc
    lat_pages = rest[:np_]
    krt_pages = rest[np_:2 * np_]
    nl_ref, nk_ref, o_ref, m_s, l_s, acc_s = rest[2 * np_:]
    s_id = pl.program_id(1)
    ql = ql_ref[0]
    qr = qr_ref[0]

    @pl.when(s_id == 0)
    def _():
        m_s[...] = jnp.full_like(m_s, NEG_INF)
        l_s[...] = jnp.zeros_like(l_s)
        acc_s[...] = jnp.zeros_like(acc_s)

    k_lats = [jnp.concatenate([p[0] for p in lat_pages[c * per:(c + 1) * per]], axis=0).astype(bf16)
              for c in range(nc)]
    k_rts = [jnp.concatenate([p[0] for p in krt_pages[c * per:(c + 1) * per]], axis=1).astype(bf16)
             for c in range(nc)]
    _softmax_steps([_dot_nt(ql, k_lats[c]) + _dot(qr, k_rts[c]) for c in range(nc)], k_lats,
                   m_s, l_s, acc_s, list(range(nc)))

    @pl.when(s_id == pl.num_programs(1) - 1)
    def _():
        rows = ql.shape[0]
        n_lat = nl_ref[0]
        s_new = _dot_nt(ql, n_lat) + _dot_nt(qr, nk_ref[0])
        t_row = lax.broadcasted_iota(jnp.int32, (rows, new_pad), 0) & (ts - 1)
        c_new = lax.broadcasted_iota(jnp.int32, (rows, new_pad), 1)
        _softmax_step(jnp.where(c_new <= t_row, s_new, NEG_INF), n_lat, m_s, l_s, acc_s, 0)
        m = functools.reduce(jnp.maximum, [m_s[c] for c in range(nc)])
        nv = B_KV_RANK // LANES
        l = jnp.zeros_like(m)
        acc = jnp.zeros_like(acc_s[0])
        for c in range(nc):
            a_c = jnp.exp2((m_s[c] - m) * SOFTMAX_EXP2_SCALE)
            l = l + a_c * l_s[c]
            acc = acc + jnp.concatenate([a_c] * nv, axis=1) * acc_s[c]
        o_ref[0] = (acc / jnp.sum(l, axis=-1, keepdims=True)).astype(bf16)


def _paged_attention(page_table, ql, qr, cache_lat, cache_krt, new_lat, new_kr, ts):
    bs, rows, _ = ql.shape
    n_pages = page_table.shape[1]
    np_ = PAGES_PER_STEP
    new_pad = new_lat.shape[1]
    page_map = [functools.partial(lambda b, s, pt, r: (pt[b, s * np_ + r], 0, 0), r=r) for r in range(np_)]
    lat_specs = [pl.BlockSpec((1, PAGE, B_KV_RANK), page_map[r]) for r in range(np_)]
    kr_specs = [pl.BlockSpec((1, B_D_ROPE, PAGE), page_map[r]) for r in range(np_)]
    return pl.pallas_call(
        functools.partial(_paged_kernel, ts=ts, new_pad=new_pad),
        grid_spec=pltpu.PrefetchScalarGridSpec(
            num_scalar_prefetch=1,
            grid=(bs, n_pages // np_),
            in_specs=[pl.BlockSpec((1, rows, B_KV_RANK), lambda b, s, pt: (b, 0, 0)),
                      pl.BlockSpec((1, rows, B_D_ROPE), lambda b, s, pt: (b, 0, 0))]
            + lat_specs + kr_specs
            + [pl.BlockSpec((1, new_pad, B_KV_RANK), lambda b, s, pt: (b, 0, 0)),
               pl.BlockSpec((1, new_pad, B_D_ROPE), lambda b, s, pt: (b, 0, 0))],
            out_specs=pl.BlockSpec((1, rows, B_KV_RANK), lambda b, s, pt: (b, 0, 0)),
            scratch_shapes=[pltpu.VMEM((PAGED_CHAINS, rows, LANES), f32), pltpu.VMEM((PAGED_CHAINS, rows, LANES), f32),
                            pltpu.VMEM((PAGED_CHAINS, rows, B_KV_RANK), f32)],
        ),
        out_shape=jax.ShapeDtypeStruct((bs, rows, B_KV_RANK), bf16),
        compiler_params=_cparams(("parallel", "arbitrary")),
        name="mla_paged",
    )(page_table, ql, qr, *([cache_lat] * np_), *([cache_krt] * np_), new_lat, new_kr)


def _mla_out_kernel(ol_ref, wuv_ref, wo_ref, *refs):
    o = jnp.concatenate([_dot(ol_ref[0, hd], wuv_ref[hd]) for hd in range(HEADS)], axis=1)
    y = _dot(o.astype(bf16), wo_ref[...])
    _residual_ln_route(y, refs[:N_POST_IN], *refs[-3:])


def _mla_out(ol, wuv, wo, tm, **post):
    return _post_call(_mla_out_kernel, "mla_out", [ol, wuv, wo],
                      [pl.BlockSpec((1, HEADS, tm, B_KV_RANK), lambda gi, i: (gi, 0, i, 0)),
                       _full_spec(wuv.shape), _full_spec(wo.shape)], tm=tm, **post)


def _rope_tables(pos):
    half = B_D_ROPE // 2
    inv = jnp.power(ROPE_THETA, -jnp.arange(half, dtype=f32) / half)
    ang = pos.astype(f32)[:, None] * inv[None, :]
    cos, sin = jnp.cos(ang), jnp.sin(ang)
    return jnp.concatenate([cos, cos], axis=-1), jnp.concatenate([-sin, sin], axis=-1)


def _swap_halves(w, width):
    lead = w.shape[:-1]
    w2 = w.reshape(lead + (-1, 2, width // 2))
    return w2[..., ::-1, :].reshape(w.shape)


def kernel(x_prompt, x_sample, cache_latent, cache_krope, page_table, state_C, state_n, state_m, c_prompt, c_sample, ada_w, ada_b, ln_g, ln_b, a_w_in, a_b_gates, a_norm_g, a_w_out, b_w_kv_a, b_kv_norm_g, b_w_uk, b_w_uv, b_w_dq, b_q_norm_g, b_w_uq, b_w_o, router_w, router_b, e_w_gate, e_w_up, e_w_down):
    bp, tp, d = x_prompt.shape
    bs, ts, _ = x_sample.shape
    hqk = HEADS * A_DQK
    hv = HEADS * A_DV

    w_in = a_w_in.at[:, :, hqk:2 * hqk].multiply(A_DQK ** -0.5)
    w_in = jnp.pad(w_in, ((0, 0), (0, 0), (0, LANES - 2 * HEADS))).astype(bf16)
    bg = jnp.pad(a_b_gates, ((0, 0), (0, LANES - 2 * HEADS))).reshape(N_A_LAYERS, 1, LANES)
    w_out = a_w_out.astype(bf16)
    ng = a_norm_g.reshape(N_A_LAYERS, 1, hv)
    perm = jnp.array([(r % N_GROUPS) * PER_GROUP + r // N_GROUPS for r in range(N_EXPERTS)], jnp.int32)
    rw_perm = router_w[:, perm]
    rw_hi = rw_perm.astype(bf16)
    rw_lo = (rw_perm - rw_hi.astype(f32)).astype(bf16)
    rw = jnp.pad(jnp.concatenate([rw_hi, rw_lo], axis=1), ((0, 0), (0, LANES - 2 * N_EXPERTS)))
    rwh = jnp.pad(rw_hi, ((0, 0), (0, LANES - N_EXPERTS)))
    rb = router_b[perm].reshape(N_EXPERTS, 1)
    w_lat = b_w_kv_a[:, :B_KV_RANK]
    w_kr = b_w_kv_a[:, B_KV_RANK:]
    zpad = jnp.zeros((d, LANES - B_D_ROPE), f32)
    w_kv = jnp.concatenate([w_lat, w_kr, zpad, _swap_halves(w_kr, B_D_ROPE), zpad], axis=1).astype(bf16)
    kvg = b_kv_norm_g.reshape(1, B_KV_RANK)
    wdq = b_w_dq.astype(bf16)
    qg = b_q_norm_g.reshape(-1, 1, b_q_norm_g.shape[-1])
    uq = b_w_uq.reshape(b_w_uq.shape[0], b_w_uq.shape[1], HEADS, B_D_NOPE + B_D_ROPE)
    uq_nope = uq[..., :B_D_NOPE].reshape(uq.shape[0], uq.shape[1], HEADS * B_D_NOPE)
    uq_rope = uq[..., B_D_NOPE:].reshape(uq.shape[0], uq.shape[1], HEADS * B_D_ROPE)
    wuq = jnp.concatenate([uq_nope, uq_rope, _swap_halves(uq_rope, B_D_ROPE)], axis=-1).astype(bf16)
    wuk = jnp.transpose(b_w_uk, (1, 2, 0)).astype(bf16)
    wuv = jnp.transpose(b_w_uv, (1, 0, 2)).astype(bf16)
    wo = b_w_o.astype(bf16)

    mods = _ada_mods(jnp.concatenate([c_prompt, c_sample], axis=0), ada_w, ada_b)

    cos_p, sin_p = _rope_tables(jnp.arange(tp, dtype=jnp.int32))
    past_len = page_table.shape[1] * PAGE
    cos_s, sin_s = _rope_tables(jnp.tile(past_len + jnp.arange(ts, dtype=jnp.int32), bs))

    tm = 512
    cache_krt = jnp.swapaxes(cache_krope, 1, 2)
    streams = []
    mods_p = mods[:, :bp].reshape(DEPTH, bp, 1, N_MODS * d)
    mods_s = jnp.repeat(mods[:, bp:], ts, axis=1).reshape(DEPTH, 1, bs * ts, N_MODS * d)
    for x0, mods_g, cos, sin, sample in ((x_prompt, mods_p, cos_p, sin_p, False),
                                         (x_sample.reshape(1, bs * ts, d), mods_s, cos_s, sin_s, True)):
        streams.append(dict(x=x0, mods=mods_g, sample=sample, cos1=cos[None], sin1=sin[None],
                            cos8=jnp.tile(cos, (1, HEADS))[None], sin8=jnp.tile(sin, (1, HEADS))[None],
                            new_c=[], new_n=[], new_m=[]))

    for l in range(DEPTH):
        lg = ln_g[l].reshape(2, 1, d)
        lb = ln_b[l].reshape(2, 1, d)
        moe_in = []
        h2_all, row0 = None, 0
        n_all = sum(st["x"].shape[0] * st["x"].shape[1] for st in streams)
        for st in streams:
            x, sample = st["x"], st["sample"]
            cos8, sin8 = st["cos8"], st["sin8"]
            latb, krb = st.get("latb"), st.get("krb")
            new_c, new_n, new_m = st["new_c"], st["new_n"], st["new_m"]
            shift1, scale1, gate1, shift2, scale2, gate2 = [(st["mods"], l, i) for i in range(N_MODS)]
            post = dict(x=x, gate=gate1, ln_g=lg[0], ln_b=lb[0], scale2=scale2, shift2=shift2, rw=rw, rwh=rwh, rb=rb,
                        h2_rows=n_all, h2_row0=row0, h2_prev=h2_all)
            if l < N_A_LAYERS:
                qkv, o, gates = _mlstm_in(x, scale1, shift1, w_in[l], bg[l], tm)
                kt = jnp.swapaxes(qkv[:, :, hqk:2 * hqk], 1, 2)
                grow = jnp.swapaxes(gates[:, :, :2 * HEADS], 1, 2)
                if sample:
                    n0 = state_n[l].reshape(bs, hqk)
                    m0 = state_m[l]
                    nrows = jnp.repeat(n0, ts, axis=0)[None]
                    m_tok = jnp.repeat(m0, ts, axis=0)
                    mcol = jnp.pad(m_tok, ((0, 0), (0, LANES - HEADS)))[None]
                    mrow = m_tok.T[None]
                    u, c_stack, n_new, m_new = _mlstm_sample(qkv, kt, o, gates, grow, ng[l], state_C, l,
                                                             st.get("c_stack"), n0, nrows, m0, mcol, mrow, ts)
                    st["c_stack"] = c_stack
                    n_new = n_new.reshape(bs, HEADS, A_DQK)
                else:
                    u, c_new, n_rep, m_rep = _mlstm_prompt(qkv, kt, o, gates, grow, ng[l])
                    n_new = n_rep[..., 0]
                    m_new = m_rep[..., 0]
                    new_c.append(c_new)
                new_n.append(n_new)
                new_m.append(m_new)
                x1, h2_all, route = _mlstm_out(u, w_out[l], tm, **post)
            else:
                j = l - N_A_LAYERS
                ql, qr = _mla_q(x, scale1, shift1, wdq[j], qg[j], wuq[j], wuk, cos8, sin8, tm)
                if sample:
                    ql_s = jnp.transpose(ql.reshape(HEADS, bs, ts, B_KV_RANK), (1, 0, 2, 3)).reshape(
                        bs, HEADS * ts, B_KV_RANK)
                    qr_s = jnp.transpose(qr.reshape(HEADS, bs, ts, B_D_ROPE), (1, 0, 2, 3)).reshape(
                        bs, HEADS * ts, B_D_ROPE)
                    new_pad = 16
                    nl = jnp.pad(latb.reshape(bs, ts, B_KV_RANK), ((0, 0), (0, new_pad - ts), (0, 0)))
                    nk = jnp.pad(krb.reshape(bs, ts, B_D_ROPE), ((0, 0), (0, new_pad - ts), (0, 0)))
                    ol = _paged_attention(page_table, ql_s, qr_s, cache_latent, cache_krt, nl, nk, ts)
                    ol = jnp.transpose(ol.reshape(bs, HEADS, ts, B_KV_RANK), (1, 0, 2, 3)).reshape(
                        1, HEADS, bs * ts, B_KV_RANK)
                else:
                    ol = _flash(ql, qr, latb, krb, 512)
                x1, h2_all, route = _mla_out(ol, wuv, wo[j], tm, **post)
            moe_in.append((x1, route, gate2))
            row0 += x.shape[0] * x.shape[1]
        for st, x_new in zip(streams, _moe(moe_in, h2_all, lg[1], lb[1], e_w_gate, e_w_up, e_w_down, l, tm)):
            st["x"] = x_new
            if l == N_A_LAYERS - 1:
                st["lat"], st["kr"], st["latb"], st["krb"] = _shared_kv(x_new, w_kv, kvg, st["cos1"], st["sin1"], tm)

    sp, ss = streams
    return (sp["x"], ss["x"].reshape(bs, ts, d),
            jnp.stack(sp["new_c"]), jnp.stack(sp["new_n"]), jnp.stack(sp["new_m"]), sp["lat"], sp["kr"],
            ss["c_stack"], jnp.stack(ss["new_n"]), jnp.stack(ss["new_m"]),
            ss["lat"].reshape(bs, ts, B_KV_RANK), ss["kr"].reshape(bs, ts, B_D_ROPE))
```

```python
import functools

import jax
import jax.numpy as jnp
from jax import lax
from jax.experimental import pallas as pl
from jax.experimental.pallas import tpu as pltpu

f32 = jnp.float32
bf16 = jnp.bfloat16

DEPTH = 4
N_A_LAYERS = 2
HEADS = 8
A_DQK = 64
A_DV = 128
B_D_NOPE = 128
B_D_ROPE = 64
B_KV_RANK = 256
ROPE_THETA = 10000.0
ATTN_SCALE = (B_D_NOPE + B_D_ROPE) ** -0.5
N_EXPERTS = 16
N_GROUPS = 4
PER_GROUP = 4
PAGE = 128
ALPHA = (2 * DEPTH) ** 0.25
EPS = 1e-6
NEG_INF = float("-inf")
LOG2E = 1.4426950408889634
SOFTMAX_EXP2_SCALE = ATTN_SCALE * LOG2E

VMEM_LIMIT_BYTES = 56 * 1024 * 1024
LANES = 128
MLSTM_CHUNK = 128
SAMPLE_GROUP = 16
MOE_BLOCK = 256
GATHER_PARTS = 2
PAGES_PER_STEP = 16
PAGED_CHAINS = 2
FLASH_HEADS = 8


def _cparams(sem):
    return pltpu.CompilerParams(dimension_semantics=sem, vmem_limit_bytes=VMEM_LIMIT_BYTES)


def _dot(a, b):
    return jnp.dot(a, b, preferred_element_type=f32)


def _dot_nt(a, b):
    return lax.dot_general(a, b, (((1,), (1,)), ((), ())), preferred_element_type=f32)


def _dot_exact(a, b):
    return jnp.dot(a, b, preferred_element_type=f32, precision=lax.Precision.HIGHEST)


def _sigmoid(x):
    return 1.0 / (1.0 + jnp.exp(-x))


def _log_sigmoid(x):
    return jnp.minimum(x, 0.0) - jnp.log(1.0 + jnp.exp(-jnp.abs(x)))


def _ada_kernel(c_ref, w_ref, b_ref, o_ref):
    c = c_ref[...]
    sc = (c * _sigmoid(c)).astype(bf16)
    o_ref[0] = _dot(sc, w_ref[0].astype(bf16)) + b_ref[0]


def _ada_mods(c_all, ada_w, ada_b):
    depth, d, e = ada_w.shape
    bc = c_all.shape[0]
    tn = 1536
    return pl.pallas_call(
        _ada_kernel,
        grid=(depth, e // tn),
        in_specs=[
            pl.BlockSpec((bc, d), lambda l, j: (0, 0)),
            pl.BlockSpec((1, d, tn), lambda l, j: (l, 0, j)),
            pl.BlockSpec((1, 1, tn), lambda l, j: (l, 0, j)),
        ],
        out_specs=pl.BlockSpec((1, bc, tn), lambda l, j: (l, 0, j)),
        out_shape=jax.ShapeDtypeStruct((depth, bc, e), f32),
        compiler_params=_cparams(("parallel", "parallel")),
        name="ada_mods",
    )(c_all, ada_w, ada_b.reshape(depth, 1, e))


N_MODS = 6


def _mod_spec(mod, tm):
    mods, layer, chunk = mod
    d = mods.shape[-1] // N_MODS
    if mods.shape[2] == 1:
        return pl.BlockSpec((1, 1, 1, d), lambda g, i: (layer, g, 0, chunk))
    return pl.BlockSpec((1, 1, tm, d), lambda g, i: (layer, g, i, chunk))


def _tok_spec(tm, d):
    return pl.BlockSpec((1, tm, d), lambda g, i: (g, i, 0))


def _full_spec(shape):
    nd = len(shape)
    return pl.BlockSpec(shape, lambda g, i: (0,) * nd)


def _route(h2, rw_ref, rwh_ref, rb_ref):
    hi = h2.astype(bf16)
    lo = (h2 - hi.astype(f32)).astype(bf16)
    p = _dot(hi, rw_ref[...]) + _dot(lo, rwh_ref[...])
    pt = p.T
    logits = pt[0:N_EXPERTS] + pt[N_EXPERTS:2 * N_EXPERTS]
    sc = _sigmoid(logits)
    sel = sc + rb_ref[...]
    a = [sel[PER_GROUP * j:PER_GROUP * (j + 1)] for j in range(PER_GROUP)]
    s = [sc[PER_GROUP * j:PER_GROUP * (j + 1)] for j in range(PER_GROUP)]
    hi01, lo01 = jnp.maximum(a[0], a[1]), jnp.minimum(a[0], a[1])
    hi23, lo23 = jnp.maximum(a[2], a[3]), jnp.minimum(a[2], a[3])
    gs = jnp.maximum(hi01, hi23) + jnp.maximum(jnp.minimum(hi01, hi23), jnp.maximum(lo01, lo23))
    best = gs[0:1]
    grp = jnp.zeros_like(best)
    for g in range(1, N_GROUPS):
        better = gs[g:g + 1] > best
        grp = jnp.where(better, float(g), grp)
        best = jnp.where(better, gs[g:g + 1], best)
    mv, sv = [], []
    for j in range(PER_GROUP):
        m_j = a[j][0:1]
        s_j = s[j][0:1]
        for g in range(1, N_GROUPS):
            m_j = jnp.where(grp == float(g), a[j][g:g + 1], m_j)
            s_j = jnp.where(grp == float(g), s[j][g:g + 1], s_j)
        mv.append(m_j)
        sv.append(s_j)

    def first_argmax(vals):
        bv, bi = vals[0], jnp.zeros_like(vals[0])
        for j in range(1, PER_GROUP):
            better = vals[j] > bv
            bi = jnp.where(better, float(j), bi)
            bv = jnp.where(better, vals[j], bv)
        return bi

    i1 = first_argmax(mv)
    i2 = first_argmax([jnp.where(i1 == float(j), NEG_INF, mv[j]) for j in range(PER_GROUP)])
    w1, w2 = sv[0], sv[0]
    for j in range(1, PER_GROUP):
        w1 = jnp.where(i1 == float(j), sv[j], w1)
        w2 = jnp.where(i2 == float(j), sv[j], w2)
    tot = w1 + w2
    e1 = grp * float(PER_GROUP) + i1
    e2 = grp * float(PER_GROUP) + i2
    rows = jnp.concatenate([e1, e2, w1 / tot, w2 / tot, jnp.zeros((LANES - 4, e1.shape[1]), f32)], axis=0)
    return rows.T


N_POST_IN = 9


def _residual_ln_route(y, post_refs, x1_ref, h2_ref, route_ref):
    x_ref, gate_ref, lng_ref, lnb_ref, sc_ref, sh_ref, rw_ref, rwh_ref, rb_ref = post_refs
    r = ALPHA * x_ref[0] + gate_ref[0, 0] * y
    mu = jnp.mean(r, axis=-1, keepdims=True)
    cen = r - mu
    var = jnp.mean(cen * cen, axis=-1, keepdims=True)
    x1 = cen * lax.rsqrt(var + EPS) * lng_ref[...] + lnb_ref[...]
    x1_ref[0] = x1
    h2 = x1 * (1.0 + sc_ref[0, 0]) + sh_ref[0, 0]
    h2_ref[...] = h2
    route_ref[0] = _route(h2, rw_ref, rwh_ref, rb_ref)


def _post_call(kernel_fn, name, mixer_args, mixer_specs, x, gate, ln_g, ln_b, scale2, shift2, rw, rwh, rb, tm,
               h2_rows, h2_row0, h2_prev):
    g, t, d = x.shape
    steps = t // tm
    blk0 = h2_row0 // tm
    in_specs = list(mixer_specs) + [
        _tok_spec(tm, d), _mod_spec(gate, tm), _full_spec((1, d)), _full_spec((1, d)),
        _mod_spec(scale2, tm), _mod_spec(shift2, tm),
        _full_spec((d, LANES)), _full_spec((d, LANES)), _full_spec((N_EXPERTS, 1)),
    ]
    args = list(mixer_args) + [x, gate[0], ln_g, ln_b, scale2[0], shift2[0], rw, rwh, rb]
    aliases = {}
    if h2_prev is not None:
        aliases = {len(args): 1}
        in_specs.append(pl.BlockSpec(memory_space=pl.ANY))
        args.append(h2_prev)
    return pl.pallas_call(
        kernel_fn,
        grid=(g, steps),
        in_specs=in_specs,
        out_specs=[_tok_spec(tm, d),
                   pl.BlockSpec((tm, d), lambda gi, i: (blk0 + gi * steps + i, 0)),
                   _tok_spec(tm, LANES)],
        out_shape=[jax.ShapeDtypeStruct((g, t, d), f32),
                   jax.ShapeDtypeStruct((h2_rows, d), f32),
                   jax.ShapeDtypeStruct((g, t, LANES), f32)],
        input_output_aliases=aliases,
        compiler_params=_cparams(("parallel", "parallel")),
        name=name,
    )(*args)


def _mlstm_in_kernel(x_ref, sc_ref, sh_ref, w_ref, bg_ref, qkv_ref, o_ref, g_ref):
    h = x_ref[0] * (1.0 + sc_ref[0, 0]) + sh_ref[0, 0]
    z = _dot(h.astype(bf16), w_ref[...])
    hqk2 = 2 * HEADS * A_DQK
    hv = HEADS * A_DV
    qkv_ref[0] = z[:, :hqk2 + hv].astype(bf16)
    o_ref[0] = z[:, hqk2 + hv:hqk2 + 2 * hv]
    g_ref[0] = z[:, hqk2 + 2 * hv:] + bg_ref[...]


def _mlstm_in(x, scale, shift, w_pad, bg_pad, tm):
    g, t, d = x.shape
    n_qkv = 2 * HEADS * A_DQK + HEADS * A_DV
    hv = HEADS * A_DV
    return pl.pallas_call(
        _mlstm_in_kernel,
        grid=(g, t // tm),
        in_specs=[_tok_spec(tm, d), _mod_spec(scale, tm), _mod_spec(shift, tm),
                  _full_spec(w_pad.shape), _full_spec((1, LANES))],
        out_specs=[_tok_spec(tm, n_qkv), _tok_spec(tm, hv), _tok_spec(tm, LANES)],
        out_shape=[jax.ShapeDtypeStruct((g, t, n_qkv), bf16),
                   jax.ShapeDtypeStruct((g, t, hv), f32),
                   jax.ShapeDtypeStruct((g, t, LANES), f32)],
        compiler_params=_cparams(("parallel", "parallel")),
        name="mlstm_in",
    )(x, scale[0], shift[0], w_pad, bg_pad)


def _mlstm_out_kernel(u_ref, w_ref, *refs):
    y = _dot(u_ref[0], w_ref[...])
    _residual_ln_route(y, refs[:N_POST_IN], *refs[-3:])


def _mlstm_out(u, w_out, tm, **post):
    return _post_call(_mlstm_out_kernel, "mlstm_out", [u, w_out],
                      [_tok_spec(tm, u.shape[-1]), _full_spec(w_out.shape)], tm=tm, **post)


def _head_norm_gate(hh, ng, o):
    mu = jnp.mean(hh, axis=-1, keepdims=True)
    cen = hh - mu
    var = jnp.mean(cen * cen, axis=-1, keepdims=True)
    return (cen * lax.rsqrt(var + EPS) * ng) * _sigmoid(o)


def _row_sum_replicated(x):
    ones = jnp.ones((LANES, LANES), bf16)
    hi = x.astype(bf16)
    lo = (x - hi.astype(f32)).astype(bf16)
    return _dot(hi, ones) + _dot(lo, ones)


def _mlstm_prompt_kernel(qkv_ref, kt_ref, o_ref, gcol_ref, grow_ref, ng_ref,
                         u_ref, c_out, n_out, m_out, c_s, n_s, m_s):
    c = pl.program_id(1)
    nc = pl.num_programs(1)
    L = qkv_ref.shape[1]
    hqk = HEADS * A_DQK

    @pl.when(c == 0)
    def _():
        c_s[...] = jnp.zeros_like(c_s)
        n_s[...] = jnp.zeros_like(n_s)
        m_s[...] = jnp.zeros_like(m_s)

    r_i = lax.broadcasted_iota(jnp.int32, (L, L), 0)
    c_i = lax.broadcasted_iota(jnp.int32, (L, L), 1)
    causal = c_i <= r_i
    tri = causal.astype(f32)
    tri_t = (r_i <= c_i).astype(f32)

    gcol = gcol_ref[0]
    grow = grow_ref[0]
    b_col = _dot_exact(tri, _log_sigmoid(gcol))
    b_row = _dot_exact(_log_sigmoid(grow[HEADS:2 * HEADS]), tri_t)
    li_row = grow[0:HEADS]

    qkv = qkv_ref[0]
    kt = kt_ref[0]
    o_all = o_ref[0]
    ng = ng_ref[...]
    c_old, n_old, m_old = c_s[...], n_s[...], m_s[...]
    hs = range(HEADS)
    q = [qkv[:, A_DQK * h:A_DQK * (h + 1)] for h in hs]
    v = [qkv[:, 2 * hqk + A_DV * h:2 * hqk + A_DV * (h + 1)] for h in hs]
    kth = [kt[A_DQK * h:A_DQK * (h + 1), :] for h in hs]
    s_qk = [_dot(q[h], kth[h]) for h in hs]
    r_e = lax.broadcasted_iota(jnp.int32, (LANES, HEADS * LANES), 0)
    c_e = lax.broadcasted_iota(jnp.int32, (LANES, HEADS * LANES), 1)
    sel = (r_e == HEADS + (c_e >> 7)).astype(bf16)
    b_hi = b_col.astype(bf16)
    b_r1 = b_col - b_hi.astype(f32)
    b_mid = b_r1.astype(bf16)
    b_lo = (b_r1 - b_mid.astype(f32)).astype(bf16)
    b_rep = _dot(b_hi, sel) + _dot(b_mid, sel) + _dot(b_lo, sel)
    bc = [b_rep[:, LANES * h:LANES * (h + 1)] for h in hs]
    br = [b_row[h:h + 1] for h in hs]
    lir = [li_row[h:h + 1] for h in hs]
    m_prev = [m_old[h:h + 1, 0:1] for h in hs]
    d = [jnp.where(causal, bc[h] - br[h] + lir[h], NEG_INF) for h in hs]
    a = [bc[h] + m_prev[h] for h in hs]
    d_max = [jnp.max(d[h], axis=-1, keepdims=True) for h in hs]
    m_t = [jnp.maximum(a[h], d_max[h]) for h in hs]
    p = [s_qk[h] * jnp.exp(d[h] - m_t[h]) for h in hs]
    inter = [jnp.exp(a[h] - m_t[h]) for h in hs]
    qc = [_dot(q[h], jnp.concatenate([c_old[h], n_old[h]], axis=1).astype(bf16)) for h in hs]
    p_sum = [_row_sum_replicated(p[h]) for h in hs]
    num = [_dot(p[h].astype(bf16), v[h]) + inter[h] * qc[h][:, :A_DV] for h in hs]
    den = [p_sum[h] + inter[h] * qc[h][:, A_DV:] for h in hs]
    hh = [num[h] / jnp.maximum(jnp.abs(den[h]), jnp.exp(-m_t[h])) for h in hs]
    mu = [_row_sum_replicated(hh[h]) * (1.0 / A_DV) for h in hs]
    cen = [hh[h] - mu[h] for h in hs]
    var = [_row_sum_replicated(cen[h] * cen[h]) * (1.0 / A_DV) for h in hs]
    u_parts = [((cen[h] * lax.rsqrt(var[h] + EPS) * ng[:, A_DV * h:A_DV * (h + 1)])
                * _sigmoid(o_all[:, A_DV * h:A_DV * (h + 1)])).astype(bf16) for h in hs]

    b_end = [br[h][:, L - 1:L] for h in hs]
    g_row = [b_end[h] - br[h] + lir[h] for h in hs]
    g_max = [jnp.max(g_row[h], axis=-1, keepdims=True) for h in hs]
    m_new = [jnp.maximum(b_end[h] + m_prev[h], g_max[h]) for h in hs]
    decay = [jnp.exp(b_end[h] + m_prev[h] - m_new[h]) for h in hs]
    kw = [kth[h].astype(f32) * jnp.exp(g_row[h] - m_new[h]) for h in hs]
    kw_sum = [_row_sum_replicated(kw[h]) for h in hs]
    c_parts = [decay[h] * c_old[h] + _dot(kw[h].astype(bf16), v[h]) for h in hs]
    n_parts = [decay[h] * n_old[h] + kw_sum[h] for h in hs]
    m_parts = [jnp.broadcast_to(m_new[h], (1, LANES)) for h in hs]

    u_ref[0] = jnp.concatenate(u_parts, axis=1)
    c_s[...] = jnp.stack(c_parts)
    n_s[...] = jnp.stack(n_parts)
    m_s[...] = jnp.concatenate(m_parts, axis=0)

    @pl.when(c == nc - 1)
    def _():
        c_out[0] = c_s[...]
        n_out[0] = n_s[...]
        m_out[0] = m_s[...]


def _mlstm_prompt(qkv, kt, o, gcol, grow, ng):
    b, t, _ = qkv.shape
    L = MLSTM_CHUNK
    assert L == LANES and t % L == 0
    hv = HEADS * A_DV
    return pl.pallas_call(
        _mlstm_prompt_kernel,
        grid=(b, t // L),
        in_specs=[
            pl.BlockSpec((1, L, qkv.shape[-1]), lambda g, c: (g, c, 0)),
            pl.BlockSpec((1, HEADS * A_DQK, L), lambda g, c: (g, 0, c)),
            pl.BlockSpec((1, L, hv), lambda g, c: (g, c, 0)),
            pl.BlockSpec((1, L, LANES), lambda g, c: (g, c, 0)),
            pl.BlockSpec((1, 2 * HEADS, L), lambda g, c: (g, 0, c)),
            pl.BlockSpec((1, hv), lambda g, c: (0, 0)),
        ],
        out_specs=[
            pl.BlockSpec((1, L, hv), lambda g, c: (g, c, 0)),
            pl.BlockSpec((1, HEADS, A_DQK, A_DV), lambda g, c: (g, 0, 0, 0)),
            pl.BlockSpec((1, HEADS, A_DQK, LANES), lambda g, c: (g, 0, 0, 0)),
            pl.BlockSpec((1, HEADS, LANES), lambda g, c: (g, 0, 0)),
        ],
        out_shape=[
            jax.ShapeDtypeStruct((b, t, hv), bf16),
            jax.ShapeDtypeStruct((b, HEADS, A_DQK, A_DV), f32),
            jax.ShapeDtypeStruct((b, HEADS, A_DQK, LANES), f32),
            jax.ShapeDtypeStruct((b, HEADS, LANES), f32),
        ],
        scratch_shapes=[
            pltpu.VMEM((HEADS, A_DQK, A_DV), f32),
            pltpu.VMEM((HEADS, A_DQK, LANES), f32),
            pltpu.VMEM((HEADS, LANES), f32),
        ],
        compiler_params=_cparams(("parallel", "arbitrary")),
        name="mlstm_prompt",
    )(qkv, kt, o, gcol, grow, ng)


def _mlstm_sample_kernel(qkv_ref, kt_ref, o_ref, gcol_ref, grow_ref, ng_ref, c0_ref, n0_ref, nrows_ref,
                         m0_ref, mcol_ref, mrow_ref, *refs, ts):
    u_ref, c_out, n_out, m_out = refs[-4:]
    R = qkv_ref.shape[1]
    G = R // ts
    shift = ts.bit_length() - 1
    hqk = HEADS * A_DQK

    r_i = lax.broadcasted_iota(jnp.int32, (R, R), 0)
    c_i = lax.broadcasted_iota(jnp.int32, (R, R), 1)
    same = (r_i >> shift) == (c_i >> shift)
    valid = same & (c_i <= r_i)
    tri = valid.astype(f32)
    tri_t = (same & (r_i <= c_i)).astype(f32)
    same_f = same.astype(f32)
    row_seq = lax.broadcasted_iota(jnp.int32, (R, 1), 0) >> shift
    g_i = lax.broadcasted_iota(jnp.int32, (G, R), 0)
    s_i = lax.broadcasted_iota(jnp.int32, (G, R), 1)
    bmask = (s_i >> shift) == g_i
    lastmask = s_i == g_i * ts + (ts - 1)
    g3 = lax.broadcasted_iota(jnp.int32, (G, A_DQK, R), 0)
    s3 = lax.broadcasted_iota(jnp.int32, (G, A_DQK, R), 2)
    bmask3 = (s3 >> shift) == g3
    lane_i = lax.broadcasted_iota(jnp.int32, (G, LANES), 1)

    gcol = gcol_ref[0]
    grow = grow_ref[0]
    lf_row = _log_sigmoid(grow[HEADS:2 * HEADS])
    b_col = _dot_exact(tri, _log_sigmoid(gcol))
    b_row = _dot_exact(lf_row, tri_t)
    bend_row = _dot_exact(lf_row, same_f)
    li_row = grow[0:HEADS]
    mcol = mcol_ref[0]
    mrow = mrow_ref[0]
    m0 = m0_ref[...]

    m_acc = jnp.zeros((G, LANES), f32)
    n_parts = []
    for h in range(HEADS):
        q_h = qkv_ref[0, :, A_DQK * h:A_DQK * (h + 1)]
        k_h = qkv_ref[0, :, hqk + A_DQK * h:hqk + A_DQK * (h + 1)]
        v_h = qkv_ref[0, :, 2 * hqk + A_DV * h:2 * hqk + A_DV * (h + 1)]
        kt_h = kt_ref[0, A_DQK * h:A_DQK * (h + 1), :]
        s_qk = _dot(q_h, kt_h)
        bc = b_col[:, HEADS + h:HEADS + h + 1]
        br = b_row[h:h + 1]
        lir = li_row[h:h + 1]
        d = jnp.where(valid, bc - br + lir, NEG_INF)
        a = bc + mcol[:, h:h + 1]
        m_t = jnp.maximum(a, jnp.max(d, axis=-1, keepdims=True))
        p = s_qk * jnp.exp(d - m_t)
        inter = jnp.exp(a - m_t)
        c_all = jnp.concatenate([c0_ref[0, g, h] for g in range(G)], axis=1).astype(bf16)
        qc = _dot(q_h, c_all)
        inter_c = jnp.zeros((R, A_DV), f32)
        for g in range(G):
            inter_c = jnp.where(row_seq == g, qc[:, A_DV * g:A_DV * (g + 1)], inter_c)
        qn = jnp.sum(q_h.astype(f32) * nrows_ref[0, :, A_DQK * h:A_DQK * (h + 1)], axis=-1, keepdims=True)
        num = _dot(p.astype(bf16), v_h) + inter * inter_c
        den = jnp.sum(p, axis=-1, keepdims=True) + inter * qn
        hh = num / jnp.maximum(jnp.abs(den), jnp.exp(-m_t))
        u = _head_norm_gate(hh, ng_ref[:, A_DV * h:A_DV * (h + 1)], o_ref[0, :, A_DV * h:A_DV * (h + 1)])
        u_ref[0, :, A_DV * h:A_DV * (h + 1)] = u.astype(bf16)

        g_row = bend_row[h:h + 1] - br + lir
        gmax = jnp.max(jnp.where(bmask, g_row, NEG_INF), axis=-1, keepdims=True)
        bend_b = jnp.sum(jnp.where(lastmask, br, 0.0), axis=-1, keepdims=True)
        m0_h = m0[:, h:h + 1]
        mnew_b = jnp.maximum(bend_b + m0_h, gmax)
        mnew_row = jnp.sum(jnp.where(bmask, mnew_b, 0.0), axis=0, keepdims=True)
        w_row = jnp.exp(g_row - mnew_row)
        decay_b = jnp.exp(bend_b + m0_h - mnew_b)
        kw = kt_h.astype(f32) * w_row
        kw3 = jnp.where(bmask3, jnp.broadcast_to(kw[None], (G, A_DQK, R)), 0.0)
        upd = _dot(kw3.reshape(G * A_DQK, R).astype(bf16), v_h)
        for g in range(G):
            c_out[0, g, h] = decay_b[g:g + 1, :] * c0_ref[0, g, h] + upd[A_DQK * g:A_DQK * (g + 1)]
        wm = jnp.where(bmask, w_row, 0.0)
        n_parts.append(decay_b * n0_ref[:, A_DQK * h:A_DQK * (h + 1)] + _dot(wm.astype(bf16), k_h))
        m_acc = jnp.where(lane_i == h, mnew_b, m_acc)

    n_out[...] = jnp.concatenate(n_parts, axis=1)
    m_out[...] = m_acc[:, :HEADS]


def _mlstm_sample(qkv, kt, o, gcol, grow, ng, c0_all, layer, c_prev, n0, nrows, m0, mcol, mrow, ts):
    _, t, _ = qkv.shape
    n_layers, bs = c0_all.shape[:2]
    G = SAMPLE_GROUP
    R = G * ts
    hv = HEADS * A_DV
    hqk = HEADS * A_DQK
    args = [qkv, kt, o, gcol, grow, ng, c0_all, n0, nrows, m0, mcol, mrow]
    extra_specs, aliases = [], {}
    if c_prev is not None:
        aliases = {len(args): 1}
        extra_specs = [pl.BlockSpec(memory_space=pl.ANY)]
        args.append(c_prev)
    return pl.pallas_call(
        functools.partial(_mlstm_sample_kernel, ts=ts),
        grid=(t // R,),
        input_output_aliases=aliases,
        in_specs=[
            pl.BlockSpec((1, R, qkv.shape[-1]), lambda i: (0, i, 0)),
            pl.BlockSpec((1, hqk, R), lambda i: (0, 0, i)),
            pl.BlockSpec((1, R, hv), lambda i: (0, i, 0)),
            pl.BlockSpec((1, R, LANES), lambda i: (0, i, 0)),
            pl.BlockSpec((1, 2 * HEADS, R), lambda i: (0, 0, i)),
            pl.BlockSpec((1, hv), lambda i: (0, 0)),
            pl.BlockSpec((1, G, HEADS, A_DQK, A_DV), lambda i: (layer, i, 0, 0, 0)),
            pl.BlockSpec((G, hqk), lambda i: (i, 0)),
            pl.BlockSpec((1, R, hqk), lambda i: (0, i, 0)),
            pl.BlockSpec((G, HEADS), lambda i: (i, 0)),
            pl.BlockSpec((1, R, LANES), lambda i: (0, i, 0)),
            pl.BlockSpec((1, HEADS, R), lambda i: (0, 0, i)),
        ] + extra_specs,
        out_specs=[
            pl.BlockSpec((1, R, hv), lambda i: (0, i, 0)),
            pl.BlockSpec((1, G, HEADS, A_DQK, A_DV), lambda i: (layer, i, 0, 0, 0)),
            pl.BlockSpec((G, hqk), lambda i: (i, 0)),
            pl.BlockSpec((G, HEADS), lambda i: (i, 0)),
        ],
        out_shape=[
            jax.ShapeDtypeStruct((1, t, hv), bf16),
            jax.ShapeDtypeStruct((n_layers, bs, HEADS, A_DQK, A_DV), f32),
            jax.ShapeDtypeStruct((bs, hqk), f32),
            jax.ShapeDtypeStruct((bs, HEADS), f32),
        ],
        compiler_params=_cparams(("parallel",)),
        name="mlstm_sample",
    )(*args)


def _expert_kernel(be_ref, nu_ref, first_ref, *refs, n_parts):
    x_refs = refs[:n_parts]
    wg_ref, wu_ref, wd_ref, y_ref, wg_s, wu_s, wd_s = refs[n_parts:]
    i = pl.program_id(0)
    part_blocks = pl.num_programs(0) // n_parts

    @pl.when(first_ref[i] == 1)
    def _():
        wg_s[...] = wg_ref[0, 0].astype(bf16)
        wu_s[...] = wu_ref[0, 0].astype(bf16)
        wd_s[...] = wd_ref[0, 0].astype(bf16)

    @pl.when(i < nu_ref[0])
    def _():
        x = x_refs[n_parts - 1][...]
        for part in range(n_parts - 2, -1, -1):
            x = jnp.where(i < (part + 1) * part_blocks, x_refs[part][...], x)
        x = x.astype(bf16)
        g = _dot(x, wg_s[...])
        u = _dot(x, wu_s[...])
        hid = (g * _sigmoid(g)) * u
        y_ref[...] = _dot(hid.astype(bf16), wd_s[...])

    @pl.when(i >= nu_ref[0])
    def _():
        y_ref[...] = jnp.zeros_like(y_ref)


def _experts(x_parts, block_e, n_used, first, wg, wu, wd, layer):
    n_parts = len(x_parts)
    pp, d = x_parts[0].shape
    de = wg.shape[-1]
    bm = MOE_BLOCK
    pb = pp // bm
    w_map = lambda i, be, nu, fi: (layer, be[i], 0, 0)
    part_specs = [pl.BlockSpec((bm, d), functools.partial(
        lambda i, be, nu, fi, part: (jnp.clip(i - part * pb, 0, pb - 1), 0), part=part)) for part in range(n_parts)]
    return pl.pallas_call(
        functools.partial(_expert_kernel, n_parts=n_parts),
        grid_spec=pltpu.PrefetchScalarGridSpec(
            num_scalar_prefetch=3,
            grid=(n_parts * pb,),
            in_specs=part_specs + [
                pl.BlockSpec((1, 1, d, de), w_map),
                pl.BlockSpec((1, 1, d, de), w_map),
                pl.BlockSpec((1, 1, de, d), w_map),
            ],
            out_specs=pl.BlockSpec((bm, d), lambda i, be, nu, fi: (i, 0)),
            scratch_shapes=[pltpu.VMEM((d, de), bf16), pltpu.VMEM((d, de), bf16), pltpu.VMEM((de, d), bf16)],
        ),
        out_shape=jax.ShapeDtypeStruct((n_parts * pp, d), f32),
        compiler_params=_cparams(("arbitrary",)),
        name="moe_experts",
    )(block_e, n_used, first, *x_parts, wg, wu, wd)


def _combine_kernel(x_ref, ya_ref, yb_ref, w_ref, gate_ref, lng_ref, lnb_ref, o_ref):
    w = w_ref[0]
    y = w[:, 2:3] * ya_ref[...] + w[:, 3:4] * yb_ref[...]
    r = ALPHA * x_ref[0] + gate_ref[0, 0] * y
    mu = jnp.mean(r, axis=-1, keepdims=True)
    cen = r - mu
    var = jnp.mean(cen * cen, axis=-1, keepdims=True)
    o_ref[0] = cen * lax.rsqrt(var + EPS) * lng_ref[...] + lnb_ref[...]


def _combine(x1, ya, yb, row0, wexp, gate, ln_g, ln_b, tm):
    g, t, d = x1.shape
    steps = t // tm
    blk0 = row0 // tm
    y_spec = pl.BlockSpec((tm, d), lambda gi, i: (blk0 + gi * steps + i, 0))
    return pl.pallas_call(
        _combine_kernel,
        grid=(g, steps),
        in_specs=[_tok_spec(tm, d), y_spec, y_spec,
                  _tok_spec(tm, LANES), _mod_spec(gate, tm), _full_spec((1, d)), _full_spec((1, d))],
        out_specs=_tok_spec(tm, d),
        out_shape=jax.ShapeDtypeStruct((g, t, d), f32),
        compiler_params=_cparams(("parallel", "parallel")),
        name="moe_combine",
    )(x1, ya, yb, wexp, gate[0], ln_g, ln_b)


def _moe(groups, h2_all, ln_g, ln_b, wg, wu, wd, layer, tm):
    bm = MOE_BLOCK
    d = groups[0][0].shape[-1]
    sizes = [x1.shape[0] * x1.shape[1] for x1, _, _ in groups]
    n = sum(sizes)
    nk = 2 * n
    expert = jnp.concatenate([r[:, :, 0:2].reshape(-1, 2) for _, r, _ in groups], axis=0).astype(jnp.int32)
    e_flat = expert.reshape(nk)
    onehot = (e_flat[:, None] == jnp.arange(N_EXPERTS, dtype=jnp.int32)[None, :]).astype(jnp.int32)
    csum = jnp.cumsum(onehot, axis=0)
    counts = csum[-1]
    rank = jnp.sum(csum * onehot, axis=1) - 1
    padded = (counts + bm - 1) // bm * bm
    pends = jnp.cumsum(padded)
    pstarts = pends - padded
    dest = jnp.sum(onehot * pstarts[None, :], axis=1) + rank
    n_blocks = nk // bm + N_EXPERTS
    n_used = (pends[-1] // bm).astype(jnp.int32)
    blk = jnp.minimum(jnp.arange(n_blocks, dtype=jnp.int32), n_used - 1)
    block_e = jnp.minimum(jnp.sum((pends[None, :] <= (blk * bm)[:, None]).astype(jnp.int32), axis=1),
                          N_EXPERTS - 1).astype(jnp.int32)
    first = jnp.concatenate([jnp.ones((1,), jnp.int32), (block_e[1:] != block_e[:-1]).astype(jnp.int32)])
    tok = jnp.arange(nk, dtype=jnp.int32) // 2
    tok_pad = jnp.zeros((n_blocks * bm,), jnp.int32).at[dest].set(tok, unique_indices=True, mode="promise_in_bounds")
    part_rows = n_blocks // GATHER_PARTS * bm
    x_parts = [jnp.take(h2_all, tok_pad[c * part_rows:(c + 1) * part_rows], axis=0, mode="clip")
               for c in range(GATHER_PARTS)]
    y = _experts(x_parts, block_e, n_used.reshape(1), first, wg, wu, wd, layer)
    dest2 = dest.reshape(n, 2)
    ya = jnp.take(y, dest2[:, 0], axis=0, mode="clip")
    yb = jnp.take(y, dest2[:, 1], axis=0, mode="clip")
    outs, off = [], 0
    for (x1, route, gate2), sz in zip(groups, sizes):
        outs.append(_combine(x1, ya, yb, off, route, gate2, ln_g, ln_b, tm))
        off += sz
    return outs


def _kv_kernel(x_ref, w_ref, g_ref, cos_ref, sin_ref, lat_ref, kr_ref, latb_ref, krb_ref):
    kva = _dot(x_ref[0].astype(bf16), w_ref[...])
    latp = kva[:, :B_KV_RANK]
    lat = latp * lax.rsqrt(jnp.mean(latp * latp, axis=-1, keepdims=True) + EPS) * g_ref[...]
    kr = kva[:, B_KV_RANK:B_KV_RANK + B_D_ROPE] * cos_ref[0] \
        + kva[:, B_KV_RANK + LANES:B_KV_RANK + LANES + B_D_ROPE] * sin_ref[0]
    lat_ref[0] = lat
    kr_ref[0] = kr
    latb_ref[0] = lat.astype(bf16)
    krb_ref[0] = kr.astype(bf16)


def _shared_kv(x, w_pad, g, cos, sin, tm):
    gg, t, d = x.shape
    rope_spec = pl.BlockSpec((1, tm, B_D_ROPE), lambda gi, i: (0, i, 0)) if cos.shape[0] == 1 else \
        _tok_spec(tm, B_D_ROPE)
    return pl.pallas_call(
        _kv_kernel,
        grid=(gg, t // tm),
        in_specs=[_tok_spec(tm, d), _full_spec(w_pad.shape), _full_spec((1, B_KV_RANK)), rope_spec, rope_spec],
        out_specs=[_tok_spec(tm, B_KV_RANK), _tok_spec(tm, B_D_ROPE),
                   _tok_spec(tm, B_KV_RANK), _tok_spec(tm, B_D_ROPE)],
        out_shape=[jax.ShapeDtypeStruct((gg, t, B_KV_RANK), f32), jax.ShapeDtypeStruct((gg, t, B_D_ROPE), f32),
                   jax.ShapeDtypeStruct((gg, t, B_KV_RANK), bf16), jax.ShapeDtypeStruct((gg, t, B_D_ROPE), bf16)],
        compiler_params=_cparams(("parallel", "parallel")),
        name="shared_kv",
    )(x, w_pad, g, cos, sin)


def _mla_q_kernel(x_ref, sc_ref, sh_ref, wdq_ref, qg_ref, wuq_ref, wuk_ref, cos_ref, sin_ref, ql_ref, qr_ref):
    h = x_ref[0] * (1.0 + sc_ref[0, 0]) + sh_ref[0, 0]
    cq = _dot(h.astype(bf16), wdq_ref[...])
    cq = cq * lax.rsqrt(jnp.mean(cq * cq, axis=-1, keepdims=True) + EPS) * qg_ref[...]
    q = _dot(cq.astype(bf16), wuq_ref[...])
    n_nope = HEADS * B_D_NOPE
    n_rope = HEADS * B_D_ROPE
    rot = q[:, n_nope:n_nope + n_rope] * cos_ref[0] + q[:, n_nope + n_rope:] * sin_ref[0]
    for hd in range(HEADS):
        qn = q[:, B_D_NOPE * hd:B_D_NOPE * (hd + 1)].astype(bf16)
        ql_ref[0, hd] = _dot(qn, wuk_ref[hd]).astype(bf16)
        qr_ref[0, hd] = rot[:, B_D_ROPE * hd:B_D_ROPE * (hd + 1)].astype(bf16)


def _mla_q(x, scale, shift, wdq, qg, wuq, wuk, cos8, sin8, tm):
    g, t, d = x.shape
    n_rope = HEADS * B_D_ROPE
    rope_spec = pl.BlockSpec((1, tm, n_rope), lambda gi, i: (0, i, 0)) if cos8.shape[0] == 1 else \
        _tok_spec(tm, n_rope)
    return pl.pallas_call(
        _mla_q_kernel,
        grid=(g, t // tm),
        in_specs=[_tok_spec(tm, d), _mod_spec(scale, tm), _mod_spec(shift, tm),
                  _full_spec(wdq.shape), _full_spec(qg.shape), _full_spec(wuq.shape), _full_spec(wuk.shape),
                  rope_spec, rope_spec],
        out_specs=[pl.BlockSpec((1, HEADS, tm, B_KV_RANK), lambda gi, i: (gi, 0, i, 0)),
                   pl.BlockSpec((1, HEADS, tm, B_D_ROPE), lambda gi, i: (gi, 0, i, 0))],
        out_shape=[jax.ShapeDtypeStruct((g, HEADS, t, B_KV_RANK), bf16),
                   jax.ShapeDtypeStruct((g, HEADS, t, B_D_ROPE), bf16)],
        compiler_params=_cparams(("parallel", "parallel")),
        name="mla_q",
    )(x, scale[0], shift[0], wdq, qg, wuq, wuk, cos8, sin8)


def _softmax_step(s, v, m_s, l_s, acc_s, slot):
    rows, n = s.shape
    m_old = m_s[slot]
    if n % LANES == 0:
        _softmax_steps([s], [v], m_s, l_s, acc_s, [slot])
        return
    m_new = jnp.maximum(m_old, jnp.max(s, axis=-1, keepdims=True))
    p = jnp.exp2((s - m_new[:, 0:1]) * SOFTMAX_EXP2_SCALE)
    lane = lax.broadcasted_iota(jnp.int32, (rows, LANES), 1)
    psum = jnp.where(lane == 0, jnp.sum(p, axis=-1, keepdims=True), 0.0)
    alpha = jnp.exp2((m_old - m_new) * SOFTMAX_EXP2_SCALE)
    l_s[slot] = alpha * l_s[slot] + psum
    alpha_v = jnp.concatenate([alpha] * (v.shape[1] // LANES), axis=1)
    acc_s[slot] = alpha_v * acc_s[slot] + _dot(p.astype(bf16), v)
    m_s[slot] = m_new


def _softmax_steps(s_list, v_list, m_s, l_s, acc_s, slots):
    ids = range(len(s_list))
    nv = v_list[0].shape[1] // LANES
    m_old = [m_s[slot] for slot in slots]
    chunks = [[s[:, LANES * c:LANES * (c + 1)] for c in range(s.shape[1] // LANES)] for s in s_list]
    c_max = [functools.reduce(jnp.maximum, chunks[k]) for k in ids]
    r_max = [jnp.max(c_max[k], axis=-1, keepdims=True) for k in ids]
    m_new = [jnp.maximum(m_old[k], r_max[k]) for k in ids]
    ps = [[jnp.exp2((ch - m_new[k]) * SOFTMAX_EXP2_SCALE) for ch in chunks[k]] for k in ids]
    alpha = [jnp.exp2((m_old[k] - m_new[k]) * SOFTMAX_EXP2_SCALE) for k in ids]
    pv = [_dot(jnp.concatenate(ps[k], axis=1).astype(bf16), v_list[k]) for k in ids]
    for k, slot in enumerate(slots):
        l_s[slot] = alpha[k] * l_s[slot] + functools.reduce(jnp.add, ps[k])
        acc_s[slot] = jnp.concatenate([alpha[k]] * nv, axis=1) * acc_s[slot] + pv[k]
        m_s[slot] = m_new[k]


def _flash_kernel(ql_ref, qr_ref, lat_ref, kr_ref, o_ref, m_s, l_s, acc_s, *, tq):
    i = pl.program_id(1)
    nh = FLASH_HEADS
    r_i = lax.broadcasted_iota(jnp.int32, (tq, tq), 0)
    c_i = lax.broadcasted_iota(jnp.int32, (tq, tq), 1)
    causal = c_i <= r_i

    def group_body(hg, carry):
        m_s[...] = jnp.full_like(m_s, NEG_INF)
        l_s[...] = jnp.zeros_like(l_s)
        acc_s[...] = jnp.zeros_like(acc_s)

        def block(j, masked):
            start = pl.multiple_of(j * tq, tq)
            k_lat = lat_ref[0, pl.ds(start, tq), :]
            k_r = kr_ref[0, pl.ds(start, tq), :]
            s_list = []
            for hs in range(nh):
                h = hg * nh + hs
                s = _dot_nt(ql_ref[0, h], k_lat) + _dot_nt(qr_ref[0, h], k_r)
                s_list.append(jnp.where(causal, s, NEG_INF) if masked else s)
            _softmax_steps(s_list, [k_lat] * nh, m_s, l_s, acc_s, list(range(nh)))

        def kv_body(j, c2):
            block(j, False)
            return c2

        lax.fori_loop(0, i, kv_body, 0)
        block(i, True)
        for hs in range(nh):
            l = jnp.sum(l_s[hs], axis=-1, keepdims=True)
            o_ref[0, hg * nh + hs] = (acc_s[hs] / l).astype(bf16)
        return carry

    lax.fori_loop(0, HEADS // nh, group_body, 0)


def _flash(ql, qr, latb, krb, tq):
    b, _, t, _ = ql.shape
    nh = FLASH_HEADS
    return pl.pallas_call(
        functools.partial(_flash_kernel, tq=tq),
        grid=(b, t // tq),
        in_specs=[
            pl.BlockSpec((1, HEADS, tq, B_KV_RANK), lambda g, i: (g, 0, i, 0)),
            pl.BlockSpec((1, HEADS, tq, B_D_ROPE), lambda g, i: (g, 0, i, 0)),
            pl.BlockSpec((1, t, B_KV_RANK), lambda g, i: (g, 0, 0)),
            pl.BlockSpec((1, t, B_D_ROPE), lambda g, i: (g, 0, 0)),
        ],
        out_specs=pl.BlockSpec((1, HEADS, tq, B_KV_RANK), lambda g, i: (g, 0, i, 0)),
        out_shape=jax.ShapeDtypeStruct((b, HEADS, t, B_KV_RANK), bf16),
        scratch_shapes=[pltpu.VMEM((nh, tq, LANES), f32), pltpu.VMEM((nh, tq, LANES), f32),
                        pltpu.VMEM((nh, tq, B_KV_RANK), f32)],
        compiler_params=_cparams(("parallel", "arbitrary")),
        name="mla_flash",
    )(ql, qr, latb, krb)


def _paged_kernel(pt_ref, ql_ref, qr_ref, *rest, ts, new_pad):
    np_ = PAGES_PER_STEP
    nc = PAGED_CHAINS
    per = np_ // nc
    lat_pages = rest[:np_]
    krt_pages = rest[np_:2 * np_]
    nl_ref, nk_ref, o_ref, m_s, l_s, acc_s = rest[2 * np_:]
    s_id = pl.program_id(1)
    ql = ql_ref[0]
    qr = qr_ref[0]

    @pl.when(s_id == 0)
    def _():
        m_s[...] = jnp.full_like(m_s, NEG_INF)
        l_s[...] = jnp.zeros_like(l_s)
        acc_s[...] = jnp.zeros_like(acc_s)

    k_lats = [jnp.concatenate([p[0] for p in lat_pages[c * per:(c + 1) * per]], axis=0).astype(bf16)
              for c in range(nc)]
    k_rts = [jnp.concatenate([p[0] for p in krt_pages[c * per:(c + 1) * per]], axis=1).astype(bf16)
             for c in range(nc)]
    _softmax_steps([_dot_nt(ql, k_lats[c]) + _dot(qr, k_rts[c]) for c in range(nc)], k_lats,
                   m_s, l_s, acc_s, list(range(nc)))

    @pl.when(s_id == pl.num_programs(1) - 1)
    def _():
        rows = ql.shape[0]
        n_lat = nl_ref[0]
        s_new = _dot_nt(ql, n_lat) + _dot_nt(qr, nk_ref[0])
        t_row = lax.broadcasted_iota(jnp.int32, (rows, new_pad), 0) & (ts - 1)
        c_new = lax.broadcasted_iota(jnp.int32, (rows, new_pad), 1)
        _softmax_step(jnp.where(c_new <= t_row, s_new, NEG_INF), n_lat, m_s, l_s, acc_s, 0)
        m = functools.reduce(jnp.maximum, [m_s[c] for c in range(nc)])
        nv = B_KV_RANK // LANES
        l = jnp.zeros_like(m)
        acc = jnp.zeros_like(acc_s[0])
        for c in range(nc):
            a_c = jnp.exp2((m_s[c] - m) * SOFTMAX_EXP2_SCALE)
            l = l + a_c * l_s[c]
            acc = acc + jnp.concatenate([a_c] * nv, axis=1) * acc_s[c]
        o_ref[0] = (acc / jnp.sum(l, axis=-1, keepdims=True)).astype(bf16)


def _paged_attention(page_table, ql, qr, cache_lat, cache_krt, new_lat, new_kr, ts):
    bs, rows, _ = ql.shape
    n_pages = page_table.shape[1]
    np_ = PAGES_PER_STEP
    new_pad = new_lat.shape[1]
    page_map = [functools.partial(lambda b, s, pt, r: (pt[b, s * np_ + r], 0, 0), r=r) for r in range(np_)]
    lat_specs = [pl.BlockSpec((1, PAGE, B_KV_RANK), page_map[r]) for r in range(np_)]
    kr_specs = [pl.BlockSpec((1, B_D_ROPE, PAGE), page_map[r]) for r in range(np_)]
    return pl.pallas_call(
        functools.partial(_paged_kernel, ts=ts, new_pad=new_pad),
        grid_spec=pltpu.PrefetchScalarGridSpec(
            num_scalar_prefetch=1,
            grid=(bs, n_pages // np_),
            in_specs=[pl.BlockSpec((1, rows, B_KV_RANK), lambda b, s, pt: (b, 0, 0)),
                      pl.BlockSpec((1, rows, B_D_ROPE), lambda b, s, pt: (b, 0, 0))]
            + lat_specs + kr_specs
            + [pl.BlockSpec((1, new_pad, B_KV_RANK), lambda b, s, pt: (b, 0, 0)),
               pl.BlockSpec((1, new_pad, B_D_ROPE), lambda b, s, pt: (b, 0, 0))],
            out_specs=pl.BlockSpec((1, rows, B_KV_RANK), lambda b, s, pt: (b, 0, 0)),
            scratch_shapes=[pltpu.VMEM((PAGED_CHAINS, rows, LANES), f32), pltpu.VMEM((PAGED_CHAINS, rows, LANES), f32),
                            pltpu.VMEM((PAGED_CHAINS, rows, B_KV_RANK), f32)],
        ),
        out_shape=jax.ShapeDtypeStruct((bs, rows, B_KV_RANK), bf16),
        compiler_params=_cparams(("parallel", "arbitrary")),
        name="mla_paged",
    )(page_table, ql, qr, *([cache_lat] * np_), *([cache_krt] * np_), new_lat, new_kr)


def _mla_out_kernel(ol_ref, wuv_ref, wo_ref, *refs):
    o = jnp.concatenate([_dot(ol_ref[0, hd], wuv_ref[hd]) for hd in range(HEADS)], axis=1)
    y = _dot(o.astype(bf16), wo_ref[...])
    _residual_ln_route(y, refs[:N_POST_IN], *refs[-3:])


def _mla_out(ol, wuv, wo, tm, **post):
    return _post_call(_mla_out_kernel, "mla_out", [ol, wuv, wo],
                      [pl.BlockSpec((1, HEADS, tm, B_KV_RANK), lambda gi, i: (gi, 0, i, 0)),
                       _full_spec(wuv.shape), _full_spec(wo.shape)], tm=tm, **post)


def _rope_tables(pos):
    half = B_D_ROPE // 2
    inv = jnp.power(ROPE_THETA, -jnp.arange(half, dtype=f32) / half)
    ang = pos.astype(f32)[:, None] * inv[None, :]
    cos, sin = jnp.cos(ang), jnp.sin(ang)
    return jnp.concatenate([cos, cos], axis=-1), jnp.concatenate([-sin, sin], axis=-1)


def _swap_halves(w, width):
    lead = w.shape[:-1]
    w2 = w.reshape(lead + (-1, 2, width // 2))
    return w2[..., ::-1, :].reshape(w.shape)


def kernel(x_prompt, x_sample, cache_latent, cache_krope, page_table, state_C, state_n, state_m, c_prompt, c_sample, ada_w, ada_b, ln_g, ln_b, a_w_in, a_b_gates, a_norm_g, a_w_out, b_w_kv_a, b_kv_norm_g, b_w_uk, b_w_uv, b_w_dq, b_q_norm_g, b_w_uq, b_w_o, router_w, router_b, e_w_gate, e_w_up, e_w_down):
    bp, tp, d = x_prompt.shape
    bs, ts, _ = x_sample.shape
    hqk = HEADS * A_DQK
    hv = HEADS * A_DV

    w_in = a_w_in.at[:, :, hqk:2 * hqk].multiply(A_DQK ** -0.5)
    w_in = jnp.pad(w_in, ((0, 0), (0, 0), (0, LANES - 2 * HEADS))).astype(bf16)
    bg = jnp.pad(a_b_gates, ((0, 0), (0, LANES - 2 * HEADS))).reshape(N_A_LAYERS, 1, LANES)
    w_out = a_w_out.astype(bf16)
    ng = a_norm_g.reshape(N_A_LAYERS, 1, hv)
    perm = jnp.array([(r % N_GROUPS) * PER_GROUP + r // N_GROUPS for r in range(N_EXPERTS)], jnp.int32)
    rw_perm = router_w[:, perm]
    rw_hi = rw_perm.astype(bf16)
    rw_lo = (rw_perm - rw_hi.astype(f32)).astype(bf16)
    rw = jnp.pad(jnp.concatenate([rw_hi, rw_lo], axis=1), ((0, 0), (0, LANES - 2 * N_EXPERTS)))
    rwh = jnp.pad(rw_hi, ((0, 0), (0, LANES - N_EXPERTS)))
    rb = router_b[perm].reshape(N_EXPERTS, 1)
    w_lat = b_w_kv_a[:, :B_KV_RANK]
    w_kr = b_w_kv_a[:, B_KV_RANK:]
    zpad = jnp.zeros((d, LANES - B_D_ROPE), f32)
    w_kv = jnp.concatenate([w_lat, w_kr, zpad, _swap_halves(w_kr, B_D_ROPE), zpad], axis=1).astype(bf16)
    kvg = b_kv_norm_g.reshape(1, B_KV_RANK)
    wdq = b_w_dq.astype(bf16)
    qg = b_q_norm_g.reshape(-1, 1, b_q_norm_g.shape[-1])
    uq = b_w_uq.reshape(b_w_uq.shape[0], b_w_uq.shape[1], HEADS, B_D_NOPE + B_D_ROPE)
    uq_nope = uq[..., :B_D_NOPE].reshape(uq.shape[0], uq.shape[1], HEADS * B_D_NOPE)
    uq_rope = uq[..., B_D_NOPE:].reshape(uq.shape[0], uq.shape[1], HEADS * B_D_ROPE)
    wuq = jnp.concatenate([uq_nope, uq_rope, _swap_halves(uq_rope, B_D_ROPE)], axis=-1).astype(bf16)
    wuk = jnp.transpose(b_w_uk, (1, 2, 0)).astype(bf16)
    wuv = jnp.transpose(b_w_uv, (1, 0, 2)).astype(bf16)
    wo = b_w_o.astype(bf16)

    mods = _ada_mods(jnp.concatenate([c_prompt, c_sample], axis=0), ada_w, ada_b)

    cos_p, sin_p = _rope_tables(jnp.arange(tp, dtype=jnp.int32))
    past_len = page_table.shape[1] * PAGE
    cos_s, sin_s = _rope_tables(jnp.tile(past_len + jnp.arange(ts, dtype=jnp.int32), bs))

    tm = 512
    cache_krt = jnp.swapaxes(cache_krope, 1, 2)
    streams = []
    mods_p = mods[:, :bp].reshape(DEPTH, bp, 1, N_MODS * d)
    mods_s = jnp.repeat(mods[:, bp:], ts, axis=1).reshape(DEPTH, 1, bs * ts, N_MODS * d)
    for x0, mods_g, cos, sin, sample in ((x_prompt, mods_p, cos_p, sin_p, False),
                                         (x_sample.reshape(1, bs * ts, d), mods_s, cos_s, sin_s, True)):
        streams.append(dict(x=x0, mods=mods_g, sample=sample, cos1=cos[None], sin1=sin[None],
                            cos8=jnp.tile(cos, (1, HEADS))[None], sin8=jnp.tile(sin, (1, HEADS))[None],
                            new_c=[], new_n=[], new_m=[]))

    for l in range(DEPTH):
        lg = ln_g[l].reshape(2, 1, d)
        lb = ln_b[l].reshape(2, 1, d)
        moe_in = []
        h2_all, row0 = None, 0
        n_all = sum(st["x"].shape[0] * st["x"].shape[1] for st in streams)
        for st in streams:
            x, sample = st["x"], st["sample"]
            cos8, sin8 = st["cos8"], st["sin8"]
            latb, krb = st.get("latb"), st.get("krb")
            new_c, new_n, new_m = st["new_c"], st["new_n"], st["new_m"]
            shift1, scale1, gate1, shift2, scale2, gate2 = [(st["mods"], l, i) for i in range(N_MODS)]
            post = dict(x=x, gate=gate1, ln_g=lg[0], ln_b=lb[0], scale2=scale2, shift2=shift2, rw=rw, rwh=rwh, rb=rb,
                        h2_rows=n_all, h2_row0=row0, h2_prev=h2_all)
            if l < N_A_LAYERS:
                qkv, o, gates = _mlstm_in(x, scale1, shift1, w_in[l], bg[l], tm)
                kt = jnp.swapaxes(qkv[:, :, hqk:2 * hqk], 1, 2)
                grow = jnp.swapaxes(gates[:, :, :2 * HEADS], 1, 2)
                if sample:
                    n0 = state_n[l].reshape(bs, hqk)
                    m0 = state_m[l]
                    nrows = jnp.repeat(n0, ts, axis=0)[None]
                    m_tok = jnp.repeat(m0, ts, axis=0)
                    mcol = jnp.pad(m_tok, ((0, 0), (0, LANES - HEADS)))[None]
                    mrow = m_tok.T[None]
                    u, c_stack, n_new, m_new = _mlstm_sample(qkv, kt, o, gates, grow, ng[l], state_C, l,
                                                             st.get("c_stack"), n0, nrows, m0, mcol, mrow, ts)
                    st["c_stack"] = c_stack
                    n_new = n_new.reshape(bs, HEADS, A_DQK)
                else:
                    u, c_new, n_rep, m_rep = _mlstm_prompt(qkv, kt, o, gates, grow, ng[l])
                    n_new = n_rep[..., 0]
                    m_new = m_rep[..., 0]
                    new_c.append(c_new)
                new_n.append(n_new)
                new_m.append(m_new)
                x1, h2_all, route = _mlstm_out(u, w_out[l], tm, **post)
            else:
                j = l - N_A_LAYERS
                ql, qr = _mla_q(x, scale1, shift1, wdq[j], qg[j], wuq[j], wuk, cos8, sin8, tm)
                if sample:
                    ql_s = jnp.transpose(ql.reshape(HEADS, bs, ts, B_KV_RANK), (1, 0, 2, 3)).reshape(
                        bs, HEADS * ts, B_KV_RANK)
                    qr_s = jnp.transpose(qr.reshape(HEADS, bs, ts, B_D_ROPE), (1, 0, 2, 3)).reshape(
                        bs, HEADS * ts, B_D_ROPE)
                    new_pad = 16
                    nl = jnp.pad(latb.reshape(bs, ts, B_KV_RANK), ((0, 0), (0, new_pad - ts), (0, 0)))
                    nk = jnp.pad(krb.reshape(bs, ts, B_D_ROPE), ((0, 0), (0, new_pad - ts), (0, 0)))
                    ol = _paged_attention(page_table, ql_s, qr_s, cache_latent, cache_krt, nl, nk, ts)
                    ol = jnp.transpose(ol.reshape(bs, HEADS, ts, B_KV_RANK), (1, 0, 2, 3)).reshape(
                        1, HEADS, bs * ts, B_KV_RANK)
                else:
                    ol = _flash(ql, qr, latb, krb, 512)
                x1, h2_all, route = _mla_out(ol, wuv, wo[j], tm, **post)
            moe_in.append((x1, route, gate2))
            row0 += x.shape[0] * x.shape[1]
        for st, x_new in zip(streams, _moe(moe_in, h2_all, lg[1], lb[1], e_w_gate, e_w_up, e_w_down, l, tm)):
            st["x"] = x_new
            if l == N_A_LAYERS - 1:
                st["lat"], st["kr"], st["latb"], st["krb"] = _shared_kv(x_new, w_kv, kvg, st["cos1"], st["sin1"], tm)

    sp, ss = streams
    return (sp["x"], ss["x"].reshape(bs, ts, d),
            jnp.stack(sp["new_c"]), jnp.stack(sp["new_n"]), jnp.stack(sp["new_m"]), sp["lat"], sp["kr"],
            ss["c_stack"], jnp.stack(ss["new_n"]), jnp.stack(ss["new_m"]),
            ss["lat"].reshape(bs, ts, B_KV_RANK), ss["kr"].reshape(bs, ts, B_D_ROPE))
```

```python
import functools

import jax
import jax.numpy as jnp
from jax import lax
from jax.experimental import pallas as pl
from jax.experimental.pallas import tpu as pltpu

f32 = jnp.float32
bf16 = jnp.bfloat16

DEPTH = 4
N_A_LAYERS = 2
HEADS = 8
A_DQK = 64
A_DV = 128
B_D_NOPE = 128
B_D_ROPE = 64
B_KV_RANK = 256
ROPE_THETA = 10000.0
ATTN_SCALE = (B_D_NOPE + B_D_ROPE) ** -0.5
N_EXPERTS = 16
N_GROUPS = 4
PER_GROUP = 4
PAGE = 128
ALPHA = (2 * DEPTH) ** 0.25
EPS = 1e-6
NEG_INF = float("-inf")
LOG2E = 1.4426950408889634
SOFTMAX_EXP2_SCALE = ATTN_SCALE * LOG2E

VMEM_LIMIT_BYTES = 56 * 1024 * 1024
LANES = 128
MLSTM_CHUNK = 128
SAMPLE_GROUP = 16
MOE_BLOCK = 256
GATHER_PARTS = 2
PAGES_PER_STEP = 16
PAGED_CHAINS = 4
FLASH_HEADS = 8


def _cparams(sem):
    return pltpu.CompilerParams(dimension_semantics=sem, vmem_limit_bytes=VMEM_LIMIT_BYTES)


def _dot(a, b):
    return jnp.dot(a, b, preferred_element_type=f32)


def _dot_nt(a, b):
    return lax.dot_general(a, b, (((1,), (1,)), ((), ())), preferred_element_type=f32)


def _dot_exact(a, b):
    return jnp.dot(a, b, preferred_element_type=f32, precision=lax.Precision.HIGHEST)


def _sigmoid(x):
    return 1.0 / (1.0 + jnp.exp(-x))


def _log_sigmoid(x):
    return jnp.minimum(x, 0.0) - jnp.log(1.0 + jnp.exp(-jnp.abs(x)))


def _ada_kernel(c_ref, w_ref, b_ref, o_ref):
    c = c_ref[...]
    sc = (c * _sigmoid(c)).astype(bf16)
    o_ref[0] = _dot(sc, w_ref[0].astype(bf16)) + b_ref[0]


def _ada_mods(c_all, ada_w, ada_b):
    depth, d, e = ada_w.shape
    bc = c_all.shape[0]
    tn = 1536
    return pl.pallas_call(
        _ada_kernel,
        grid=(depth, e // tn),
        in_specs=[
            pl.BlockSpec((bc, d), lambda l, j: (0, 0)),
            pl.BlockSpec((1, d, tn), lambda l, j: (l, 0, j)),
            pl.BlockSpec((1, 1, tn), lambda l, j: (l, 0, j)),
        ],
        out_specs=pl.BlockSpec((1, bc, tn), lambda l, j: (l, 0, j)),
        out_shape=jax.ShapeDtypeStruct((depth, bc, e), f32),
        compiler_params=_cparams(("parallel", "parallel")),
        name="ada_mods",
    )(c_all, ada_w, ada_b.reshape(depth, 1, e))


N_MODS = 6


def _mod_spec(mod, tm):
    mods, layer, chunk = mod
    d = mods.shape[-1] // N_MODS
    if mods.shape[2] == 1:
        return pl.BlockSpec((1, 1, 1, d), lambda g, i: (layer, g, 0, chunk))
    return pl.BlockSpec((1, 1, tm, d), lambda g, i: (layer, g, i, chunk))


def _tok_spec(tm, d):
    return pl.BlockSpec((1, tm, d), lambda g, i: (g, i, 0))


def _full_spec(shape):
    nd = len(shape)
    return pl.BlockSpec(shape, lambda g, i: (0,) * nd)


def _route(h2, rw_ref, rwh_ref, rb_ref):
    hi = h2.astype(bf16)
    lo = (h2 - hi.astype(f32)).astype(bf16)
    p = _dot(hi, rw_ref[...]) + _dot(lo, rwh_ref[...])
    pt = p.T
    logits = pt[0:N_EXPERTS] + pt[N_EXPERTS:2 * N_EXPERTS]
    sc = _sigmoid(logits)
    sel = sc + rb_ref[...]
    a = [sel[PER_GROUP * j:PER_GROUP * (j + 1)] for j in range(PER_GROUP)]
    s = [sc[PER_GROUP * j:PER_GROUP * (j + 1)] for j in range(PER_GROUP)]
    hi01, lo01 = jnp.maximum(a[0], a[1]), jnp.minimum(a[0], a[1])
    hi23, lo23 = jnp.maximum(a[2], a[3]), jnp.minimum(a[2], a[3])
    gs = jnp.maximum(hi01, hi23) + jnp.maximum(jnp.minimum(hi01, hi23), jnp.maximum(lo01, lo23))
    best = gs[0:1]
    grp = jnp.zeros_like(best)
    for g in range(1, N_GROUPS):
        better = gs[g:g + 1] > best
        grp = jnp.where(better, float(g), grp)
        best = jnp.where(better, gs[g:g + 1], best)
    mv, sv = [], []
    for j in range(PER_GROUP):
        m_j = a[j][0:1]
        s_j = s[j][0:1]
        for g in range(1, N_GROUPS):
            m_j = jnp.where(grp == float(g), a[j][g:g + 1], m_j)
            s_j = jnp.where(grp == float(g), s[j][g:g + 1], s_j)
        mv.append(m_j)
        sv.append(s_j)

    def first_argmax(vals):
        bv, bi = vals[0], jnp.zeros_like(vals[0])
        for j in range(1, PER_GROUP):
            better = vals[j] > bv
            bi = jnp.where(better, float(j), bi)
            bv = jnp.where(better, vals[j], bv)
        return bi

    i1 = first_argmax(mv)
    i2 = first_argmax([jnp.where(i1 == float(j), NEG_INF, mv[j]) for j in range(PER_GROUP)])
    w1, w2 = sv[0], sv[0]
    for j in range(1, PER_GROUP):
        w1 = jnp.where(i1 == float(j), sv[j], w1)
        w2 = jnp.where(i2 == float(j), sv[j], w2)
    tot = w1 + w2
    e1 = grp * float(PER_GROUP) + i1
    e2 = grp * float(PER_GROUP) + i2
    rows = jnp.concatenate([e1, e2, w1 / tot, w2 / tot, jnp.zeros((LANES - 4, e1.shape[1]), f32)], axis=0)
    return rows.T


N_POST_IN = 9


POST_ROW_SPLIT = 2


def _row_chunks(tm, split=1):
    nr = tm // split
    return [(r * nr, nr) for r in range(split)]


def _mod_rows(ref, r0, nr):
    if ref.shape[2] == 1:
        return ref[0, 0]
    return ref[0, 0, pl.ds(r0, nr), :]


def _residual_ln_route(y, post_refs, x1_ref, h2_ref, route_ref, r0, nr):
    x_ref, gate_ref, lng_ref, lnb_ref, sc_ref, sh_ref, rw_ref, rwh_ref, rb_ref = post_refs
    rows = pl.ds(r0, nr)
    r = ALPHA * x_ref[0, rows, :] + _mod_rows(gate_ref, r0, nr) * y
    mu = jnp.mean(r, axis=-1, keepdims=True)
    cen = r - mu
    var = jnp.mean(cen * cen, axis=-1, keepdims=True)
    x1 = cen * lax.rsqrt(var + EPS) * lng_ref[...] + lnb_ref[...]
    x1_ref[0, rows, :] = x1
    h2 = x1 * (1.0 + _mod_rows(sc_ref, r0, nr)) + _mod_rows(sh_ref, r0, nr)
    h2_ref[rows, :] = h2
    route_ref[0, rows, :] = _route(h2, rw_ref, rwh_ref, rb_ref)


def _post_call(kernel_fn, name, mixer_args, mixer_specs, x, gate, ln_g, ln_b, scale2, shift2, rw, rwh, rb, tm,
               h2_rows, h2_row0, h2_prev):
    g, t, d = x.shape
    steps = t // tm
    blk0 = h2_row0 // tm
    in_specs = list(mixer_specs) + [
        _tok_spec(tm, d), _mod_spec(gate, tm), _full_spec((1, d)), _full_spec((1, d)),
        _mod_spec(scale2, tm), _mod_spec(shift2, tm),
        _full_spec((d, LANES)), _full_spec((d, LANES)), _full_spec((N_EXPERTS, 1)),
    ]
    args = list(mixer_args) + [x, gate[0], ln_g, ln_b, scale2[0], shift2[0], rw, rwh, rb]
    aliases = {}
    if h2_prev is not None:
        aliases = {len(args): 1}
        in_specs.append(pl.BlockSpec(memory_space=pl.ANY))
        args.append(h2_prev)
    return pl.pallas_call(
        kernel_fn,
        grid=(g, steps),
        in_specs=in_specs,
        out_specs=[_tok_spec(tm, d),
                   pl.BlockSpec((tm, d), lambda gi, i: (blk0 + gi * steps + i, 0)),
                   _tok_spec(tm, LANES)],
        out_shape=[jax.ShapeDtypeStruct((g, t, d), f32),
                   jax.ShapeDtypeStruct((h2_rows, d), f32),
                   jax.ShapeDtypeStruct((g, t, LANES), f32)],
        input_output_aliases=aliases,
        compiler_params=_cparams(("parallel", "parallel")),
        name=name,
    )(*args)


def _mlstm_in_kernel(x_ref, sc_ref, sh_ref, w_ref, bg_ref, qkv_ref, o_ref, g_ref):
    hqk2 = 2 * HEADS * A_DQK
    hv = HEADS * A_DV
    for r0, nr in _row_chunks(x_ref.shape[1]):
        rows = pl.ds(r0, nr)
        h = x_ref[0, rows, :] * (1.0 + _mod_rows(sc_ref, r0, nr)) + _mod_rows(sh_ref, r0, nr)
        z = _dot(h.astype(bf16), w_ref[...])
        qkv_ref[0, rows, :] = z[:, :hqk2 + hv].astype(bf16)
        o_ref[0, rows, :] = z[:, hqk2 + hv:hqk2 + 2 * hv]
        g_ref[0, rows, :] = z[:, hqk2 + 2 * hv:] + bg_ref[...]


def _mlstm_in(x, scale, shift, w_pad, bg_pad, tm):
    g, t, d = x.shape
    n_qkv = 2 * HEADS * A_DQK + HEADS * A_DV
    hv = HEADS * A_DV
    return pl.pallas_call(
        _mlstm_in_kernel,
        grid=(g, t // tm),
        in_specs=[_tok_spec(tm, d), _mod_spec(scale, tm), _mod_spec(shift, tm),
                  _full_spec(w_pad.shape), _full_spec((1, LANES))],
        out_specs=[_tok_spec(tm, n_qkv), _tok_spec(tm, hv), _tok_spec(tm, LANES)],
        out_shape=[jax.ShapeDtypeStruct((g, t, n_qkv), bf16),
                   jax.ShapeDtypeStruct((g, t, hv), f32),
                   jax.ShapeDtypeStruct((g, t, LANES), f32)],
        compiler_params=_cparams(("parallel", "parallel")),
        name="mlstm_in",
    )(x, scale[0], shift[0], w_pad, bg_pad)


def _mlstm_out_kernel(u_ref, w_ref, *refs):
    for r0, nr in _row_chunks(u_ref.shape[1], POST_ROW_SPLIT):
        y = _dot(u_ref[0, pl.ds(r0, nr), :], w_ref[...])
        _residual_ln_route(y, refs[:N_POST_IN], *refs[-3:], r0, nr)


def _mlstm_out(u, w_out, tm, **post):
    return _post_call(_mlstm_out_kernel, "mlstm_out", [u, w_out],
                      [_tok_spec(tm, u.shape[-1]), _full_spec(w_out.shape)], tm=tm, **post)


def _head_norm_gate(hh, ng, o):
    mu = jnp.mean(hh, axis=-1, keepdims=True)
    cen = hh - mu
    var = jnp.mean(cen * cen, axis=-1, keepdims=True)
    return (cen * lax.rsqrt(var + EPS) * ng) * _sigmoid(o)


def _row_sum_replicated(x):
    ones = jnp.ones((LANES, LANES), bf16)
    hi = x.astype(bf16)
    lo = (x - hi.astype(f32)).astype(bf16)
    return _dot(hi, ones) + _dot(lo, ones)


def _mlstm_prompt_kernel(qkv_ref, kt_ref, o_ref, gcol_ref, grow_ref, ng_ref,
                         u_ref, c_out, n_out, m_out, c_s, n_s, m_s):
    c = pl.program_id(1)
    nc = pl.num_programs(1)
    L = qkv_ref.shape[1]
    hqk = HEADS * A_DQK

    @pl.when(c == 0)
    def _():
        c_s[...] = jnp.zeros_like(c_s)
        n_s[...] = jnp.zeros_like(n_s)
        m_s[...] = jnp.zeros_like(m_s)

    r_i = lax.broadcasted_iota(jnp.int32, (L, L), 0)
    c_i = lax.broadcasted_iota(jnp.int32, (L, L), 1)
    causal = c_i <= r_i
    tri = causal.astype(f32)
    tri_t = (r_i <= c_i).astype(f32)

    gcol = gcol_ref[0]
    grow = grow_ref[0]
    b_col = _dot_exact(tri, _log_sigmoid(gcol))
    b_row = _dot_exact(_log_sigmoid(grow[HEADS:2 * HEADS]), tri_t)
    li_row = grow[0:HEADS]

    qkv = qkv_ref[0]
    kt = kt_ref[0]
    o_all = o_ref[0]
    ng = ng_ref[...]
    c_old, n_old, m_old = c_s[...], n_s[...], m_s[...]
    hs = range(HEADS)
    q = [qkv[:, A_DQK * h:A_DQK * (h + 1)] for h in hs]
    v = [qkv[:, 2 * hqk + A_DV * h:2 * hqk + A_DV * (h + 1)] for h in hs]
    kth = [kt[A_DQK * h:A_DQK * (h + 1), :] for h in hs]
    s_qk = [_dot(q[h], kth[h]) for h in hs]
    r_e = lax.broadcasted_iota(jnp.int32, (LANES, HEADS * LANES), 0)
    c_e = lax.broadcasted_iota(jnp.int32, (LANES, HEADS * LANES), 1)
    sel = (r_e == HEADS + (c_e >> 7)).astype(bf16)
    b_hi = b_col.astype(bf16)
    b_r1 = b_col - b_hi.astype(f32)
    b_mid = b_r1.astype(bf16)
    b_lo = (b_r1 - b_mid.astype(f32)).astype(bf16)
    b_rep = _dot(b_hi, sel) + _dot(b_mid, sel) + _dot(b_lo, sel)
    bc = [b_rep[:, LANES * h:LANES * (h + 1)] for h in hs]
    br = [b_row[h:h + 1] for h in hs]
    lir = [li_row[h:h + 1] for h in hs]
    m_prev = [m_old[h:h + 1, 0:1] for h in hs]
    d = [jnp.where(causal, bc[h] - br[h] + lir[h], NEG_INF) for h in hs]
    a = [bc[h] + m_prev[h] for h in hs]
    d_max = [jnp.max(d[h], axis=-1, keepdims=True) for h in hs]
    m_t = [jnp.maximum(a[h], d_max[h]) for h in hs]
    p = [s_qk[h] * jnp.exp(d[h] - m_t[h]) for h in hs]
    inter = [jnp.exp(a[h] - m_t[h]) for h in hs]
    qc = [_dot(q[h], jnp.concatenate([c_old[h], n_old[h]], axis=1).astype(bf16)) for h in hs]
    p_sum = [_row_sum_replicated(p[h]) for h in hs]
    num = [_dot(p[h].astype(bf16), v[h]) + inter[h] * qc[h][:, :A_DV] for h in hs]
    den = [p_sum[h] + inter[h] * qc[h][:, A_DV:] for h in hs]
    hh = [num[h] / jnp.maximum(jnp.abs(den[h]), jnp.exp(-m_t[h])) for h in hs]
    mu = [_row_sum_replicated(hh[h]) * (1.0 / A_DV) for h in hs]
    cen = [hh[h] - mu[h] for h in hs]
    var = [_row_sum_replicated(cen[h] * cen[h]) * (1.0 / A_DV) for h in hs]
    u_parts = [((cen[h] * lax.rsqrt(var[h] + EPS) * ng[:, A_DV * h:A_DV * (h + 1)])
                * _sigmoid(o_all[:, A_DV * h:A_DV * (h + 1)])).astype(bf16) for h in hs]

    b_end = [br[h][:, L - 1:L] for h in hs]
    g_row = [b_end[h] - br[h] + lir[h] for h in hs]
    g_max = [jnp.max(g_row[h], axis=-1, keepdims=True) for h in hs]
    m_new = [jnp.maximum(b_end[h] + m_prev[h], g_max[h]) for h in hs]
    decay = [jnp.exp(b_end[h] + m_prev[h] - m_new[h]) for h in hs]
    kw = [kth[h].astype(f32) * jnp.exp(g_row[h] - m_new[h]) for h in hs]
    kw_sum = [_row_sum_replicated(kw[h]) for h in hs]
    c_parts = [decay[h] * c_old[h] + _dot(kw[h].astype(bf16), v[h]) for h in hs]
    n_parts = [decay[h] * n_old[h] + kw_sum[h] for h in hs]
    m_parts = [jnp.broadcast_to(m_new[h], (1, LANES)) for h in hs]

    u_ref[0] = jnp.concatenate(u_parts, axis=1)
    c_s[...] = jnp.stack(c_parts)
    n_s[...] = jnp.stack(n_parts)
    m_s[...] = jnp.concatenate(m_parts, axis=0)

    @pl.when(c == nc - 1)
    def _():
        c_out[0] = c_s[...]
        n_out[0] = n_s[...]
        m_out[0] = m_s[...]


def _mlstm_prompt(qkv, kt, o, gcol, grow, ng):
    b, t, _ = qkv.shape
    L = MLSTM_CHUNK
    assert L == LANES and t % L == 0
    hv = HEADS * A_DV
    return pl.pallas_call(
        _mlstm_prompt_kernel,
        grid=(b, t // L),
        in_specs=[
            pl.BlockSpec((1, L, qkv.shape[-1]), lambda g, c: (g, c, 0)),
            pl.BlockSpec((1, HEADS * A_DQK, L), lambda g, c: (g, 0, c)),
            pl.BlockSpec((1, L, hv), lambda g, c: (g, c, 0)),
            pl.BlockSpec((1, L, LANES), lambda g, c: (g, c, 0)),
            pl.BlockSpec((1, 2 * HEADS, L), lambda g, c: (g, 0, c)),
            pl.BlockSpec((1, hv), lambda g, c: (0, 0)),
        ],
        out_specs=[
            pl.BlockSpec((1, L, hv), lambda g, c: (g, c, 0)),
            pl.BlockSpec((1, HEADS, A_DQK, A_DV), lambda g, c: (g, 0, 0, 0)),
            pl.BlockSpec((1, HEADS, A_DQK, LANES), lambda g, c: (g, 0, 0, 0)),
            pl.BlockSpec((1, HEADS, LANES), lambda g, c: (g, 0, 0)),
        ],
        out_shape=[
            jax.ShapeDtypeStruct((b, t, hv), bf16),
            jax.ShapeDtypeStruct((b, HEADS, A_DQK, A_DV), f32),
            jax.ShapeDtypeStruct((b, HEADS, A_DQK, LANES), f32),
            jax.ShapeDtypeStruct((b, HEADS, LANES), f32),
        ],
        scratch_shapes=[
            pltpu.VMEM((HEADS, A_DQK, A_DV), f32),
            pltpu.VMEM((HEADS, A_DQK, LANES), f32),
            pltpu.VMEM((HEADS, LANES), f32),
        ],
        compiler_params=_cparams(("parallel", "arbitrary")),
        name="mlstm_prompt",
    )(qkv, kt, o, gcol, grow, ng)


def _mlstm_sample_kernel(qkv_ref, kt_ref, o_ref, gcol_ref, grow_ref, ng_ref, c0_ref, n0_ref, nrows_ref,
                         m0_ref, mcol_ref, mrow_ref, *refs, ts):
    u_ref, c_out, n_out, m_out = refs[-4:]
    R = qkv_ref.shape[1]
    G = R // ts
    shift = ts.bit_length() - 1
    hqk = HEADS * A_DQK

    r_i = lax.broadcasted_iota(jnp.int32, (R, R), 0)
    c_i = lax.broadcasted_iota(jnp.int32, (R, R), 1)
    same = (r_i >> shift) == (c_i >> shift)
    valid = same & (c_i <= r_i)
    tri = valid.astype(f32)
    tri_t = (same & (r_i <= c_i)).astype(f32)
    same_f = same.astype(f32)
    row_seq = lax.broadcasted_iota(jnp.int32, (R, 1), 0) >> shift
    g_i = lax.broadcasted_iota(jnp.int32, (G, R), 0)
    s_i = lax.broadcasted_iota(jnp.int32, (G, R), 1)
    bmask = (s_i >> shift) == g_i
    lastmask = s_i == g_i * ts + (ts - 1)
    g3 = lax.broadcasted_iota(jnp.int32, (G, A_DQK, R), 0)
    s3 = lax.broadcasted_iota(jnp.int32, (G, A_DQK, R), 2)
    bmask3 = (s3 >> shift) == g3
    lane_i = lax.broadcasted_iota(jnp.int32, (G, LANES), 1)

    gcol = gcol_ref[0]
    grow = grow_ref[0]
    lf_row = _log_sigmoid(grow[HEADS:2 * HEADS])
    b_col = _dot_exact(tri, _log_sigmoid(gcol))
    b_row = _dot_exact(lf_row, tri_t)
    bend_row = _dot_exact(lf_row, same_f)
    li_row = grow[0:HEADS]
    mcol = mcol_ref[0]
    mrow = mrow_ref[0]
    m0 = m0_ref[...]

    m_acc = jnp.zeros((G, LANES), f32)
    n_parts = []
    for h in range(HEADS):
        q_h = qkv_ref[0, :, A_DQK * h:A_DQK * (h + 1)]
        k_h = qkv_ref[0, :, hqk + A_DQK * h:hqk + A_DQK * (h + 1)]
        v_h = qkv_ref[0, :, 2 * hqk + A_DV * h:2 * hqk + A_DV * (h + 1)]
        kt_h = kt_ref[0, A_DQK * h:A_DQK * (h + 1), :]
        s_qk = _dot(q_h, kt_h)
        bc = b_col[:, HEADS + h:HEADS + h + 1]
        br = b_row[h:h + 1]
        lir = li_row[h:h + 1]
        d = jnp.where(valid, bc - br + lir, NEG_INF)
        a = bc + mcol[:, h:h + 1]
        m_t = jnp.maximum(a, jnp.max(d, axis=-1, keepdims=True))
        p = s_qk * jnp.exp(d - m_t)
        inter = jnp.exp(a - m_t)
        c_all = jnp.concatenate([c0_ref[0, g, h] for g in range(G)], axis=1).astype(bf16)
        qc = _dot(q_h, c_all)
        inter_c = jnp.zeros((R, A_DV), f32)
        for g in range(G):
            inter_c = jnp.where(row_seq == g, qc[:, A_DV * g:A_DV * (g + 1)], inter_c)
        qn = jnp.sum(q_h.astype(f32) * nrows_ref[0, :, A_DQK * h:A_DQK * (h + 1)], axis=-1, keepdims=True)
        num = _dot(p.astype(bf16), v_h) + inter * inter_c
        den = jnp.sum(p, axis=-1, keepdims=True) + inter * qn
        hh = num / jnp.maximum(jnp.abs(den), jnp.exp(-m_t))
        u = _head_norm_gate(hh, ng_ref[:, A_DV * h:A_DV * (h + 1)], o_ref[0, :, A_DV * h:A_DV * (h + 1)])
        u_ref[0, :, A_DV * h:A_DV * (h + 1)] = u.astype(bf16)

        g_row = bend_row[h:h + 1] - br + lir
        gmax = jnp.max(jnp.where(bmask, g_row, NEG_INF), axis=-1, keepdims=True)
        bend_b = jnp.sum(jnp.where(lastmask, br, 0.0), axis=-1, keepdims=True)
        m0_h = m0[:, h:h + 1]
        mnew_b = jnp.maximum(bend_b + m0_h, gmax)
        mnew_row = jnp.sum(jnp.where(bmask, mnew_b, 0.0), axis=0, keepdims=True)
        w_row = jnp.exp(g_row - mnew_row)
        decay_b = jnp.exp(bend_b + m0_h - mnew_b)
        kw = kt_h.astype(f32) * w_row
        kw3 = jnp.where(bmask3, jnp.broadcast_to(kw[None], (G, A_DQK, R)), 0.0)
        upd = _dot(kw3.reshape(G * A_DQK, R).astype(bf16), v_h)
        for g in range(G):
            c_out[0, g, h] = decay_b[g:g + 1, :] * c0_ref[0, g, h] + upd[A_DQK * g:A_DQK * (g + 1)]
        wm = jnp.where(bmask, w_row, 0.0)
        n_parts.append(decay_b * n0_ref[:, A_DQK * h:A_DQK * (h + 1)] + _dot(wm.astype(bf16), k_h))
        m_acc = jnp.where(lane_i == h, mnew_b, m_acc)

    n_out[...] = jnp.concatenate(n_parts, axis=1)
    m_out[...] = m_acc[:, :HEADS]


def _mlstm_sample(qkv, kt, o, gcol, grow, ng, c0_all, layer, c_prev, n0, nrows, m0, mcol, mrow, ts):
    _, t, _ = qkv.shape
    n_layers, bs = c0_all.shape[:2]
    G = SAMPLE_GROUP
    R = G * ts
    hv = HEADS * A_DV
    hqk = HEADS * A_DQK
    args = [qkv, kt, o, gcol, grow, ng, c0_all, n0, nrows, m0, mcol, mrow]
    extra_specs, aliases = [], {}
    if c_prev is not None:
        aliases = {len(args): 1}
        extra_specs = [pl.BlockSpec(memory_space=pl.ANY)]
        args.append(c_prev)
    return pl.pallas_call(
        functools.partial(_mlstm_sample_kernel, ts=ts),
        grid=(t // R,),
        input_output_aliases=aliases,
        in_specs=[
            pl.BlockSpec((1, R, qkv.shape[-1]), lambda i: (0, i, 0)),
            pl.BlockSpec((1, hqk, R), lambda i: (0, 0, i)),
            pl.BlockSpec((1, R, hv), lambda i: (0, i, 0)),
            pl.BlockSpec((1, R, LANES), lambda i: (0, i, 0)),
            pl.BlockSpec((1, 2 * HEADS, R), lambda i: (0, 0, i)),
            pl.BlockSpec((1, hv), lambda i: (0, 0)),
            pl.BlockSpec((1, G, HEADS, A_DQK, A_DV), lambda i: (layer, i, 0, 0, 0)),
            pl.BlockSpec((G, hqk), lambda i: (i, 0)),
            pl.BlockSpec((1, R, hqk), lambda i: (0, i, 0)),
            pl.BlockSpec((G, HEADS), lambda i: (i, 0)),
            pl.BlockSpec((1, R, LANES), lambda i: (0, i, 0)),
            pl.BlockSpec((1, HEADS, R), lambda i: (0, 0, i)),
        ] + extra_specs,
        out_specs=[
            pl.BlockSpec((1, R, hv), lambda i: (0, i, 0)),
            pl.BlockSpec((1, G, HEADS, A_DQK, A_DV), lambda i: (layer, i, 0, 0, 0)),
            pl.BlockSpec((G, hqk), lambda i: (i, 0)),
            pl.BlockSpec((G, HEADS), lambda i: (i, 0)),
        ],
        out_shape=[
            jax.ShapeDtypeStruct((1, t, hv), bf16),
            jax.ShapeDtypeStruct((n_layers, bs, HEADS, A_DQK, A_DV), f32),
            jax.ShapeDtypeStruct((bs, hqk), f32),
            jax.ShapeDtypeStruct((bs, HEADS), f32),
        ],
        compiler_params=_cparams(("parallel",)),
        name="mlstm_sample",
    )(*args)


def _expert_kernel(be_ref, nu_ref, first_ref, *refs, n_parts):
    x_refs = refs[:n_parts]
    wg_ref, wu_ref, wd_ref, y_ref, wg_s, wu_s, wd_s = refs[n_parts:]
    i = pl.program_id(0)
    part_blocks = pl.num_programs(0) // n_parts

    @pl.when(first_ref[i] == 1)
    def _():
        wg_s[...] = wg_ref[0, 0].astype(bf16)
        wu_s[...] = wu_ref[0, 0].astype(bf16)
        wd_s[...] = wd_ref[0, 0].astype(bf16)

    @pl.when(i < nu_ref[0])
    def _():
        x = x_refs[n_parts - 1][...]
        for part in range(n_parts - 2, -1, -1):
            x = jnp.where(i < (part + 1) * part_blocks, x_refs[part][...], x)
        x = x.astype(bf16)
        g = _dot(x, wg_s[...])
        u = _dot(x, wu_s[...])
        hid = (g * _sigmoid(g)) * u
        y_ref[...] = _dot(hid.astype(bf16), wd_s[...])

    @pl.when(i >= nu_ref[0])
    def _():
        y_ref[...] = jnp.zeros_like(y_ref)


def _experts(x_parts, block_e, n_used, first, wg, wu, wd, layer):
    n_parts = len(x_parts)
    pp, d = x_parts[0].shape
    de = wg.shape[-1]
    bm = MOE_BLOCK
    pb = pp // bm
    w_map = lambda i, be, nu, fi: (layer, be[i], 0, 0)
    part_specs = [pl.BlockSpec((bm, d), functools.partial(
        lambda i, be, nu, fi, part: (jnp.clip(i - part * pb, 0, pb - 1), 0), part=part)) for part in range(n_parts)]
    return pl.pallas_call(
        functools.partial(_expert_kernel, n_parts=n_parts),
        grid_spec=pltpu.PrefetchScalarGridSpec(
            num_scalar_prefetch=3,
            grid=(n_parts * pb,),
            in_specs=part_specs + [
                pl.BlockSpec((1, 1, d, de), w_map),
                pl.BlockSpec((1, 1, d, de), w_map),
                pl.BlockSpec((1, 1, de, d), w_map),
            ],
            out_specs=pl.BlockSpec((bm, d), lambda i, be, nu, fi: (i, 0)),
            scratch_shapes=[pltpu.VMEM((d, de), bf16), pltpu.VMEM((d, de), bf16), pltpu.VMEM((de, d), bf16)],
        ),
        out_shape=jax.ShapeDtypeStruct((n_parts * pp, d), f32),
        compiler_params=_cparams(("arbitrary",)),
        name="moe_experts",
    )(block_e, n_used, first, *x_parts, wg, wu, wd)


def _combine_kernel(x_ref, ya_ref, yb_ref, w_ref, gate_ref, lng_ref, lnb_ref, o_ref):
    w = w_ref[0]
    y = w[:, 2:3] * ya_ref[...] + w[:, 3:4] * yb_ref[...]
    r = ALPHA * x_ref[0] + gate_ref[0, 0] * y
    mu = jnp.mean(r, axis=-1, keepdims=True)
    cen = r - mu
    var = jnp.mean(cen * cen, axis=-1, keepdims=True)
    o_ref[0] = cen * lax.rsqrt(var + EPS) * lng_ref[...] + lnb_ref[...]


def _combine(x1, ya, yb, row0, wexp, gate, ln_g, ln_b, tm):
    g, t, d = x1.shape
    steps = t // tm
    blk0 = row0 // tm
    y_spec = pl.BlockSpec((tm, d), lambda gi, i: (blk0 + gi * steps + i, 0))
    return pl.pallas_call(
        _combine_kernel,
        grid=(g, steps),
        in_specs=[_tok_spec(tm, d), y_spec, y_spec,
                  _tok_spec(tm, LANES), _mod_spec(gate, tm), _full_spec((1, d)), _full_spec((1, d))],
        out_specs=_tok_spec(tm, d),
        out_shape=jax.ShapeDtypeStruct((g, t, d), f32),
        compiler_params=_cparams(("parallel", "parallel")),
        name="moe_combine",
    )(x1, ya, yb, wexp, gate[0], ln_g, ln_b)


def _moe(groups, h2_all, ln_g, ln_b, wg, wu, wd, layer, tm):
    bm = MOE_BLOCK
    d = groups[0][0].shape[-1]
    sizes = [x1.shape[0] * x1.shape[1] for x1, _, _ in groups]
    n = sum(sizes)
    nk = 2 * n
    expert = jnp.concatenate([r[:, :, 0:2].reshape(-1, 2) for _, r, _ in groups], axis=0).astype(jnp.int32)
    e_flat = expert.reshape(nk)
    onehot = (e_flat[:, None] == jnp.arange(N_EXPERTS, dtype=jnp.int32)[None, :]).astype(jnp.int32)
    csum = jnp.cumsum(onehot, axis=0)
    counts = csum[-1]
    rank = jnp.sum(csum * onehot, axis=1) - 1
    padded = (counts + bm - 1) // bm * bm
    pends = jnp.cumsum(padded)
    pstarts = pends - padded
    dest = jnp.sum(onehot * pstarts[None, :], axis=1) + rank
    n_blocks = nk // bm + N_EXPERTS
    n_used = (pends[-1] // bm).astype(jnp.int32)
    blk = jnp.minimum(jnp.arange(n_blocks, dtype=jnp.int32), n_used - 1)
    block_e = jnp.minimum(jnp.sum((pends[None, :] <= (blk * bm)[:, None]).astype(jnp.int32), axis=1),
                          N_EXPERTS - 1).astype(jnp.int32)
    first = jnp.concatenate([jnp.ones((1,), jnp.int32), (block_e[1:] != block_e[:-1]).astype(jnp.int32)])
    tok = jnp.arange(nk, dtype=jnp.int32) // 2
    tok_pad = jnp.zeros((n_blocks * bm,), jnp.int32).at[dest].set(tok, unique_indices=True, mode="promise_in_bounds")
    part_rows = n_blocks // GATHER_PARTS * bm
    x_parts = [jnp.take(h2_all, tok_pad[c * part_rows:(c + 1) * part_rows], axis=0, mode="clip")
               for c in range(GATHER_PARTS)]
    y = _experts(x_parts, block_e, n_used.reshape(1), first, wg, wu, wd, layer)
    dest2 = dest.reshape(n, 2)
    ya = jnp.take(y, dest2[:, 0], axis=0, mode="clip")
    yb = jnp.take(y, dest2[:, 1], axis=0, mode="clip")
    outs, off = [], 0
    for (x1, route, gate2), sz in zip(groups, sizes):
        outs.append(_combine(x1, ya, yb, off, route, gate2, ln_g, ln_b, tm))
        off += sz
    return outs


def _kv_kernel(x_ref, w_ref, g_ref, cos_ref, sin_ref, lat_ref, kr_ref, latb_ref, krb_ref):
    kva = _dot(x_ref[0].astype(bf16), w_ref[...])
    latp = kva[:, :B_KV_RANK]
    lat = latp * lax.rsqrt(jnp.mean(latp * latp, axis=-1, keepdims=True) + EPS) * g_ref[...]
    kr = kva[:, B_KV_RANK:B_KV_RANK + B_D_ROPE] * cos_ref[0] \
        + kva[:, B_KV_RANK + LANES:B_KV_RANK + LANES + B_D_ROPE] * sin_ref[0]
    lat_ref[0] = lat
    kr_ref[0] = kr
    latb_ref[0] = lat.astype(bf16)
    krb_ref[0] = kr.astype(bf16)


def _shared_kv(x, w_pad, g, cos, sin, tm):
    gg, t, d = x.shape
    rope_spec = pl.BlockSpec((1, tm, B_D_ROPE), lambda gi, i: (0, i, 0)) if cos.shape[0] == 1 else \
        _tok_spec(tm, B_D_ROPE)
    return pl.pallas_call(
        _kv_kernel,
        grid=(gg, t // tm),
        in_specs=[_tok_spec(tm, d), _full_spec(w_pad.shape), _full_spec((1, B_KV_RANK)), rope_spec, rope_spec],
        out_specs=[_tok_spec(tm, B_KV_RANK), _tok_spec(tm, B_D_ROPE),
                   _tok_spec(tm, B_KV_RANK), _tok_spec(tm, B_D_ROPE)],
        out_shape=[jax.ShapeDtypeStruct((gg, t, B_KV_RANK), f32), jax.ShapeDtypeStruct((gg, t, B_D_ROPE), f32),
                   jax.ShapeDtypeStruct((gg, t, B_KV_RANK), bf16), jax.ShapeDtypeStruct((gg, t, B_D_ROPE), bf16)],
        compiler_params=_cparams(("parallel", "parallel")),
        name="shared_kv",
    )(x, w_pad, g, cos, sin)


def _mla_q_kernel(x_ref, sc_ref, sh_ref, wdq_ref, qg_ref, wuq_ref, wuk_ref, cos_ref, sin_ref, ql_ref, qr_ref):
    n_nope = HEADS * B_D_NOPE
    n_rope = HEADS * B_D_ROPE
    for r0, nr in _row_chunks(x_ref.shape[1]):
        rows = pl.ds(r0, nr)
        h = x_ref[0, rows, :] * (1.0 + _mod_rows(sc_ref, r0, nr)) + _mod_rows(sh_ref, r0, nr)
        cq = _dot(h.astype(bf16), wdq_ref[...])
        cq = cq * lax.rsqrt(jnp.mean(cq * cq, axis=-1, keepdims=True) + EPS) * qg_ref[...]
        q = _dot(cq.astype(bf16), wuq_ref[...])
        rot = q[:, n_nope:n_nope + n_rope] * cos_ref[0, rows, :] + q[:, n_nope + n_rope:] * sin_ref[0, rows, :]
        for hd in range(HEADS):
            qn = q[:, B_D_NOPE * hd:B_D_NOPE * (hd + 1)].astype(bf16)
            ql_ref[0, hd, rows, :] = _dot(qn, wuk_ref[hd]).astype(bf16)
            qr_ref[0, hd, rows, :] = rot[:, B_D_ROPE * hd:B_D_ROPE * (hd + 1)].astype(bf16)


def _mla_q(x, scale, shift, wdq, qg, wuq, wuk, cos8, sin8, tm):
    g, t, d = x.shape
    n_rope = HEADS * B_D_ROPE
    rope_spec = pl.BlockSpec((1, tm, n_rope), lambda gi, i: (0, i, 0)) if cos8.shape[0] == 1 else \
        _tok_spec(tm, n_rope)
    return pl.pallas_call(
        _mla_q_kernel,
        grid=(g, t // tm),
        in_specs=[_tok_spec(tm, d), _mod_spec(scale, tm), _mod_spec(shift, tm),
                  _full_spec(wdq.shape), _full_spec(qg.shape), _full_spec(wuq.shape), _full_spec(wuk.shape),
                  rope_spec, rope_spec],
        out_specs=[pl.BlockSpec((1, HEADS, tm, B_KV_RANK), lambda gi, i: (gi, 0, i, 0)),
                   pl.BlockSpec((1, HEADS, tm, B_D_ROPE), lambda gi, i: (gi, 0, i, 0))],
        out_shape=[jax.ShapeDtypeStruct((g, HEADS, t, B_KV_RANK), bf16),
                   jax.ShapeDtypeStruct((g, HEADS, t, B_D_ROPE), bf16)],
        compiler_params=_cparams(("parallel", "parallel")),
        name="mla_q",
    )(x, scale[0], shift[0], wdq, qg, wuq, wuk, cos8, sin8)


def _softmax_step(s, v, m_s, l_s, acc_s, slot):
    rows, n = s.shape
    m_old = m_s[slot]
    if n % LANES == 0:
        _softmax_steps([s], [v], m_s, l_s, acc_s, [slot])
        return
    m_new = jnp.maximum(m_old, jnp.max(s, axis=-1, keepdims=True))
    p = jnp.exp2((s - m_new[:, 0:1]) * SOFTMAX_EXP2_SCALE)
    lane = lax.broadcasted_iota(jnp.int32, (rows, LANES), 1)
    psum = jnp.where(lane == 0, jnp.sum(p, axis=-1, keepdims=True), 0.0)
    alpha = jnp.exp2((m_old - m_new) * SOFTMAX_EXP2_SCALE)
    l_s[slot] = alpha * l_s[slot] + psum
    alpha_v = jnp.concatenate([alpha] * (v.shape[1] // LANES), axis=1)
    acc_s[slot] = alpha_v * acc_s[slot] + _dot(p.astype(bf16), v)
    m_s[slot] = m_new


def _softmax_steps(s_list, v_list, m_s, l_s, acc_s, slots):
    ids = range(len(s_list))
    nv = v_list[0].shape[1] // LANES
    m_old = [m_s[slot] for slot in slots]
    chunks = [[s[:, LANES * c:LANES * (c + 1)] for c in range(s.shape[1] // LANES)] for s in s_list]
    c_max = [functools.reduce(jnp.maximum, chunks[k]) for k in ids]
    r_max = [jnp.max(c_max[k], axis=-1, keepdims=True) for k in ids]
    m_new = [jnp.maximum(m_old[k], r_max[k]) for k in ids]
    ps = [[jnp.exp2((ch - m_new[k]) * SOFTMAX_EXP2_SCALE) for ch in chunks[k]] for k in ids]
    alpha = [jnp.exp2((m_old[k] - m_new[k]) * SOFTMAX_EXP2_SCALE) for k in ids]
    pv = [_dot(jnp.concatenate(ps[k], axis=1).astype(bf16), v_list[k]) for k in ids]
    for k, slot in enumerate(slots):
        l_s[slot] = alpha[k] * l_s[slot] + functools.reduce(jnp.add, ps[k])
        acc_s[slot] = jnp.concatenate([alpha[k]] * nv, axis=1) * acc_s[slot] + pv[k]
        m_s[slot] = m_new[k]


def _flash_kernel(ql_ref, qr_ref, lat_ref, kr_ref, o_ref, m_s, l_s, acc_s, *, tq):
    i = pl.program_id(1)
    nh = FLASH_HEADS
    r_i = lax.broadcasted_iota(jnp.int32, (tq, tq), 0)
    c_i = lax.broadcasted_iota(jnp.int32, (tq, tq), 1)
    causal = c_i <= r_i

    def group_body(hg, carry):
        m_s[...] = jnp.full_like(m_s, NEG_INF)
        l_s[...] = jnp.zeros_like(l_s)
        acc_s[...] = jnp.zeros_like(acc_s)

        def block(j, masked):
            start = pl.multiple_of(j * tq, tq)
            k_lat = lat_ref[0, pl.ds(start, tq), :]
            k_r = kr_ref[0, pl.ds(start, tq), :]
            s_list = []
            for hs in range(nh):
                h = hg * nh + hs
                s = _dot_nt(ql_ref[0, h], k_lat) + _dot_nt(qr_ref[0, h], k_r)
                s_list.append(jnp.where(causal, s, NEG_INF) if masked else s)
            _softmax_steps(s_list, [k_lat] * nh, m_s, l_s, acc_s, list(range(nh)))

        def kv_body(j, c2):
            block(j, False)
            return c2

        lax.fori_loop(0, i, kv_body, 0)
        block(i, True)
        for hs in range(nh):
            l = jnp.sum(l_s[hs], axis=-1, keepdims=True)
            o_ref[0, hg * nh + hs] = (acc_s[hs] / l).astype(bf16)
        return carry

    lax.fori_loop(0, HEADS // nh, group_body, 0)


def _flash(ql, qr, latb, krb, tq):
    b, _, t, _ = ql.shape
    nh = FLASH_HEADS
    return pl.pallas_call(
        functools.partial(_flash_kernel, tq=tq),
        grid=(b, t // tq),
        in_specs=[
            pl.BlockSpec((1, HEADS, tq, B_KV_RANK), lambda g, i: (g, 0, i, 0)),
            pl.BlockSpec((1, HEADS, tq, B_D_ROPE), lambda g, i: (g, 0, i, 0)),
            pl.BlockSpec((1, t, B_KV_RANK), lambda g, i: (g, 0, 0)),
            pl.BlockSpec((1, t, B_D_ROPE), lambda g, i: (g, 0, 0)),
        ],
        out_specs=pl.BlockSpec((1, HEADS, tq, B_KV_RANK), lambda g, i: (g, 0, i, 0)),
        out_shape=jax.ShapeDtypeStruct((b, HEADS, t, B_KV_RANK), bf16),
        scratch_shapes=[pltpu.VMEM((nh, tq, LANES), f32), pltpu.VMEM((nh, tq, LANES), f32),
                        pltpu.VMEM((nh, tq, B_KV_RANK), f32)],
        compiler_params=_cparams(("parallel", "arbitrary")),
        name="mla_flash",
    )(ql, qr, latb, krb)


def _paged_kernel(pt_ref, ql_ref, qr_ref, *rest, ts, new_pad):
    np_ = PAGES_PER_STEP
    nc = PAGED_CHAINS
    per = np_ // nc
    lat_pages = rest[:np_]
    krt_pages = rest[np_:2 * np_]
    nl_ref, nk_ref, o_ref, m_s, l_s, acc_s = rest[2 * np_:]
    s_id = pl.program_id(1)
    ql = ql_ref[0]
    qr = qr_ref[0]

    @pl.when(s_id == 0)
    def _():
        m_s[...] = jnp.full_like(m_s, NEG_INF)
        l_s[...] = jnp.zeros_like(l_s)
        acc_s[...] = jnp.zeros_like(acc_s)

    k_lats = [jnp.concatenate([p[0] for p in lat_pages[c * per:(c + 1) * per]], axis=0).astype(bf16)
              for c in range(nc)]
    k_rts = [jnp.concatenate([p[0] for p in krt_pages[c * per:(c + 1) * per]], axis=1).astype(bf16)
             for c in range(nc)]
    _softmax_steps([_dot_nt(ql, k_lats[c]) + _dot(qr, k_rts[c]) for c in range(nc)], k_lats,
                   m_s, l_s, acc_s, list(range(nc)))

    @pl.when(s_id == pl.num_programs(1) - 1)
    def _():
        rows = ql.shape[0]
        n_lat = nl_ref[0]
        s_new = _dot_nt(ql, n_lat) + _dot_nt(qr, nk_ref[0])
        t_row = lax.broadcasted_iota(jnp.int32, (rows, new_pad), 0) & (ts - 1)
        c_new = lax.broadcasted_iota(jnp.int32, (rows, new_pad), 1)
        _softmax_step(jnp.where(c_new <= t_row, s_new, NEG_INF), n_lat, m_s, l_s, acc_s, 0)
        m = functools.reduce(jnp.maximum, [m_s[c] for c in range(nc)])
        nv = B_KV_RANK // LANES
        l = jnp.zeros_like(m)
        acc = jnp.zeros_like(acc_s[0])
        for c in range(nc):
            a_c = jnp.exp2((m_s[c] - m) * SOFTMAX_EXP2_SCALE)
            l = l + a_c * l_s[c]
            acc = acc + jnp.concatenate([a_c] * nv, axis=1) * acc_s[c]
        o_ref[0] = (acc / jnp.sum(l, axis=-1, keepdims=True)).astype(bf16)


def _paged_attention(page_table, ql, qr, cache_lat, cache_krt, new_lat, new_kr, ts):
    bs, rows, _ = ql.shape
    n_pages = page_table.shape[1]
    np_ = PAGES_PER_STEP
    new_pad = new_lat.shape[1]
    page_map = [functools.partial(lambda b, s, pt, r: (pt[b, s * np_ + r], 0, 0), r=r) for r in range(np_)]
    lat_specs = [pl.BlockSpec((1, PAGE, B_KV_RANK), page_map[r]) for r in range(np_)]
    kr_specs = [pl.BlockSpec((1, B_D_ROPE, PAGE), page_map[r]) for r in range(np_)]
    return pl.pallas_call(
        functools.partial(_paged_kernel, ts=ts, new_pad=new_pad),
        grid_spec=pltpu.PrefetchScalarGridSpec(
            num_scalar_prefetch=1,
            grid=(bs, n_pages // np_),
            in_specs=[pl.BlockSpec((1, rows, B_KV_RANK), lambda b, s, pt: (b, 0, 0)),
                      pl.BlockSpec((1, rows, B_D_ROPE), lambda b, s, pt: (b, 0, 0))]
            + lat_specs + kr_specs
            + [pl.BlockSpec((1, new_pad, B_KV_RANK), lambda b, s, pt: (b, 0, 0)),
               pl.BlockSpec((1, new_pad, B_D_ROPE), lambda b, s, pt: (b, 0, 0))],
            out_specs=pl.BlockSpec((1, rows, B_KV_RANK), lambda b, s, pt: (b, 0, 0)),
            scratch_shapes=[pltpu.VMEM((PAGED_CHAINS, rows, LANES), f32), pltpu.VMEM((PAGED_CHAINS, rows, LANES), f32),
                            pltpu.VMEM((PAGED_CHAINS, rows, B_KV_RANK), f32)],
        ),
        out_shape=jax.ShapeDtypeStruct((bs, rows, B_KV_RANK), bf16),
        compiler_params=_cparams(("parallel", "arbitrary")),
        name="mla_paged",
    )(page_table, ql, qr, *([cache_lat] * np_), *([cache_krt] * np_), new_lat, new_kr)


def _mla_out_kernel(ol_ref, wuv_ref, wo_ref, *refs):
    for r0, nr in _row_chunks(ol_ref.shape[2], POST_ROW_SPLIT):
        rows = pl.ds(r0, nr)
        o = jnp.concatenate([_dot(ol_ref[0, hd, rows, :], wuv_ref[hd]) for hd in range(HEADS)], axis=1)
        y = _dot(o.astype(bf16), wo_ref[...])
        _residual_ln_route(y, refs[:N_POST_IN], *refs[-3:], r0, nr)


def _mla_out(ol, wuv, wo, tm, **post):
    return _post_call(_mla_out_kernel, "mla_out", [ol, wuv, wo],
                      [pl.BlockSpec((1, HEADS, tm, B_KV_RANK), lambda gi, i: (gi, 0, i, 0)),
                       _full_spec(wuv.shape), _full_spec(wo.shape)], tm=tm, **post)


def _rope_tables(pos):
    half = B_D_ROPE // 2
    inv = jnp.power(ROPE_THETA, -jnp.arange(half, dtype=f32) / half)
    ang = pos.astype(f32)[:, None] * inv[None, :]
    cos, sin = jnp.cos(ang), jnp.sin(ang)
    return jnp.concatenate([cos, cos], axis=-1), jnp.concatenate([-sin, sin], axis=-1)


def _swap_halves(w, width):
    lead = w.shape[:-1]
    w2 = w.reshape(lead + (-1, 2, width // 2))
    return w2[..., ::-1, :].reshape(w.shape)


def kernel(x_prompt, x_sample, cache_latent, cache_krope, page_table, state_C, state_n, state_m, c_prompt, c_sample, ada_w, ada_b, ln_g, ln_b, a_w_in, a_b_gates, a_norm_g, a_w_out, b_w_kv_a, b_kv_norm_g, b_w_uk, b_w_uv, b_w_dq, b_q_norm_g, b_w_uq, b_w_o, router_w, router_b, e_w_gate, e_w_up, e_w_down):
    bp, tp, d = x_prompt.shape
    bs, ts, _ = x_sample.shape
    hqk = HEADS * A_DQK
    hv = HEADS * A_DV

    w_in = a_w_in.at[:, :, hqk:2 * hqk].multiply(A_DQK ** -0.5)
    w_in = jnp.pad(w_in, ((0, 0), (0, 0), (0, LANES - 2 * HEADS))).astype(bf16)
    bg = jnp.pad(a_b_gates, ((0, 0), (0, LANES - 2 * HEADS))).reshape(N_A_LAYERS, 1, LANES)
    w_out = a_w_out.astype(bf16)
    ng = a_norm_g.reshape(N_A_LAYERS, 1, hv)
    perm = jnp.array([(r % N_GROUPS) * PER_GROUP + r // N_GROUPS for r in range(N_EXPERTS)], jnp.int32)
    rw_perm = router_w[:, perm]
    rw_hi = rw_perm.astype(bf16)
    rw_lo = (rw_perm - rw_hi.astype(f32)).astype(bf16)
    rw = jnp.pad(jnp.concatenate([rw_hi, rw_lo], axis=1), ((0, 0), (0, LANES - 2 * N_EXPERTS)))
    rwh = jnp.pad(rw_hi, ((0, 0), (0, LANES - N_EXPERTS)))
    rb = router_b[perm].reshape(N_EXPERTS, 1)
    w_lat = b_w_kv_a[:, :B_KV_RANK]
    w_kr = b_w_kv_a[:, B_KV_RANK:]
    zpad = jnp.zeros((d, LANES - B_D_ROPE), f32)
    w_kv = jnp.concatenate([w_lat, w_kr, zpad, _swap_halves(w_kr, B_D_ROPE), zpad], axis=1).astype(bf16)
    kvg = b_kv_norm_g.reshape(1, B_KV_RANK)
    wdq = b_w_dq.astype(bf16)
    qg = b_q_norm_g.reshape(-1, 1, b_q_norm_g.shape[-1])
    uq = b_w_uq.reshape(b_w_uq.shape[0], b_w_uq.shape[1], HEADS, B_D_NOPE + B_D_ROPE)
    uq_nope = uq[..., :B_D_NOPE].reshape(uq.shape[0], uq.shape[1], HEADS * B_D_NOPE)
    uq_rope = uq[..., B_D_NOPE:].reshape(uq.shape[0], uq.shape[1], HEADS * B_D_ROPE)
    wuq = jnp.concatenate([uq_nope, uq_rope, _swap_halves(uq_rope, B_D_ROPE)], axis=-1).astype(bf16)
    wuk = jnp.transpose(b_w_uk, (1, 2, 0)).astype(bf16)
    wuv = jnp.transpose(b_w_uv, (1, 0, 2)).astype(bf16)
    wo = b_w_o.astype(bf16)

    mods = _ada_mods(jnp.concatenate([c_prompt, c_sample], axis=0), ada_w, ada_b)

    cos_p, sin_p = _rope_tables(jnp.arange(tp, dtype=jnp.int32))
    past_len = page_table.shape[1] * PAGE
    cos_s, sin_s = _rope_tables(jnp.tile(past_len + jnp.arange(ts, dtype=jnp.int32), bs))

    tm = 512
    cache_krt = jnp.swapaxes(cache_krope, 1, 2)
    streams = []
    mods_p = mods[:, :bp].reshape(DEPTH, bp, 1, N_MODS * d)
    mods_s = jnp.repeat(mods[:, bp:], ts, axis=1).reshape(DEPTH, 1, bs * ts, N_MODS * d)
    for x0, mods_g, cos, sin, sample in ((x_prompt, mods_p, cos_p, sin_p, False),
                                         (x_sample.reshape(1, bs * ts, d), mods_s, cos_s, sin_s, True)):
        streams.append(dict(x=x0, mods=mods_g, sample=sample, cos1=cos[None], sin1=sin[None],
                            cos8=jnp.tile(cos, (1, HEADS))[None], sin8=jnp.tile(sin, (1, HEADS))[None],
                            new_c=[], new_n=[], new_m=[]))

    for l in range(DEPTH):
        lg = ln_g[l].reshape(2, 1, d)
        lb = ln_b[l].reshape(2, 1, d)
        moe_in = []
        h2_all, row0 = None, 0
        n_all = sum(st["x"].shape[0] * st["x"].shape[1] for st in streams)
        for st in streams:
            x, sample = st["x"], st["sample"]
            cos8, sin8 = st["cos8"], st["sin8"]
            latb, krb = st.get("latb"), st.get("krb")
            new_c, new_n, new_m = st["new_c"], st["new_n"], st["new_m"]
            shift1, scale1, gate1, shift2, scale2, gate2 = [(st["mods"], l, i) for i in range(N_MODS)]
            post = dict(x=x, gate=gate1, ln_g=lg[0], ln_b=lb[0], scale2=scale2, shift2=shift2, rw=rw, rwh=rwh, rb=rb,
                        h2_rows=n_all, h2_row0=row0, h2_prev=h2_all)
            if l < N_A_LAYERS:
                qkv, o, gates = _mlstm_in(x, scale1, shift1, w_in[l], bg[l], tm)
                kt = jnp.swapaxes(qkv[:, :, hqk:2 * hqk], 1, 2)
                grow = jnp.swapaxes(gates[:, :, :2 * HEADS], 1, 2)
                if sample:
                    n0 = state_n[l].reshape(bs, hqk)
                    m0 = state_m[l]
                    nrows = jnp.repeat(n0, ts, axis=0)[None]
                    m_tok = jnp.repeat(m0, ts, axis=0)
                    mcol = jnp.pad(m_tok, ((0, 0), (0, LANES - HEADS)))[None]
                    mrow = m_tok.T[None]
                    u, c_stack, n_new, m_new = _mlstm_sample(qkv, kt, o, gates, grow, ng[l], state_C, l,
                                                             st.get("c_stack"), n0, nrows, m0, mcol, mrow, ts)
                    st["c_stack"] = c_stack
                    n_new = n_new.reshape(bs, HEADS, A_DQK)
                else:
                    u, c_new, n_rep, m_rep = _mlstm_prompt(qkv, kt, o, gates, grow, ng[l])
                    n_new = n_rep[..., 0]
                    m_new = m_rep[..., 0]
                    new_c.append(c_new)
                new_n.append(n_new)
                new_m.append(m_new)
                x1, h2_all, route = _mlstm_out(u, w_out[l], tm, **post)
            else:
                j = l - N_A_LAYERS
                ql, qr = _mla_q(x, scale1, shift1, wdq[j], qg[j], wuq[j], wuk, cos8, sin8, tm)
                if sample:
                    ql_s = jnp.transpose(ql.reshape(HEADS, bs, ts, B_KV_RANK), (1, 0, 2, 3)).reshape(
                        bs, HEADS * ts, B_KV_RANK)
                    qr_s = jnp.transpose(qr.reshape(HEADS, bs, ts, B_D_ROPE), (1, 0, 2, 3)).reshape(
                        bs, HEADS * ts, B_D_ROPE)
                    new_pad = 16
                    nl = jnp.pad(latb.reshape(bs, ts, B_KV_RANK), ((0, 0), (0, new_pad - ts), (0, 0)))
                    nk = jnp.pad(krb.reshape(bs, ts, B_D_ROPE), ((0, 0), (0, new_pad - ts), (0, 0)))
                    ol = _paged_attention(page_table, ql_s, qr_s, cache_latent, cache_krt, nl, nk, ts)
                    ol = jnp.transpose(ol.reshape(bs, HEADS, ts, B_KV_RANK), (1, 0, 2, 3)).reshape(
                        1, HEADS, bs * ts, B_KV_RANK)
                else:
                    ol = _flash(ql, qr, latb, krb, 512)
                x1, h2_all, route = _mla_out(ol, wuv, wo[j], tm, **post)
            moe_in.append((x1, route, gate2))
            row0 += x.shape[0] * x.shape[1]
        for st, x_new in zip(streams, _moe(moe_in, h2_all, lg[1], lb[1], e_w_gate, e_w_up, e_w_down, l, tm)):
            st["x"] = x_new
            if l == N_A_LAYERS - 1:
                st["lat"], st["kr"], st["latb"], st["krb"] = _shared_kv(x_new, w_kv, kvg, st["cos1"], st["sin1"], tm)

    sp, ss = streams
    return (sp["x"], ss["x"].reshape(bs, ts, d),
            jnp.stack(sp["new_c"]), jnp.stack(sp["new_n"]), jnp.stack(sp["new_m"]), sp["lat"], sp["kr"],
            ss["c_stack"], jnp.stack(ss["new_n"]), jnp.stack(ss["new_m"]),
            ss["lat"].reshape(bs, ts, B_KV_RANK), ss["kr"].reshape(bs, ts, B_D_ROPE))
```

```python
import functools

import jax
import jax.numpy as jnp
from jax import lax
from jax.experimental import pallas as pl
from jax.experimental.pallas import tpu as pltpu

f32 = jnp.float32
bf16 = jnp.bfloat16

DEPTH = 4
N_A_LAYERS = 2
HEADS = 8
A_DQK = 64
A_DV = 128
B_D_NOPE = 128
B_D_ROPE = 64
B_KV_RANK = 256
ROPE_THETA = 10000.0
ATTN_SCALE = (B_D_NOPE + B_D_ROPE) ** -0.5
N_EXPERTS = 16
N_GROUPS = 4
PER_GROUP = 4
PAGE = 128
ALPHA = (2 * DEPTH) ** 0.25
EPS = 1e-6
NEG_INF = float("-inf")
LOG2E = 1.4426950408889634
SOFTMAX_EXP2_SCALE = ATTN_SCALE * LOG2E

VMEM_LIMIT_BYTES = 56 * 1024 * 1024
LANES = 128
MLSTM_CHUNK = 128
MLSTM_SEQS = 2
SAMPLE_GROUP = 16
MOE_BLOCK = 256
GATHER_PARTS = 2
PAGES_PER_STEP = 16
PAGED_CHAINS = 4
FLASH_HEADS = 8


def _cparams(sem):
    return pltpu.CompilerParams(dimension_semantics=sem, vmem_limit_bytes=VMEM_LIMIT_BYTES)


def _dot(a, b):
    return jnp.dot(a, b, preferred_element_type=f32)


def _dot_nt(a, b):
    return lax.dot_general(a, b, (((1,), (1,)), ((), ())), preferred_element_type=f32)


def _dot_exact(a, b):
    return jnp.dot(a, b, preferred_element_type=f32, precision=lax.Precision.HIGHEST)


def _sigmoid(x):
    return 1.0 / (1.0 + jnp.exp(-x))


def _log_sigmoid(x):
    return jnp.minimum(x, 0.0) - jnp.log(1.0 + jnp.exp(-jnp.abs(x)))


def _ada_kernel(c_ref, w_ref, b_ref, o_ref):
    c = c_ref[...]
    sc = (c * _sigmoid(c)).astype(bf16)
    o_ref[0] = _dot(sc, w_ref[0].astype(bf16)) + b_ref[0]


def _ada_mods(c_all, ada_w, ada_b):
    depth, d, e = ada_w.shape
    bc = c_all.shape[0]
    tn = 1536
    return pl.pallas_call(
        _ada_kernel,
        grid=(depth, e // tn),
        in_specs=[
            pl.BlockSpec((bc, d), lambda l, j: (0, 0)),
            pl.BlockSpec((1, d, tn), lambda l, j: (l, 0, j)),
            pl.BlockSpec((1, 1, tn), lambda l, j: (l, 0, j)),
        ],
        out_specs=pl.BlockSpec((1, bc, tn), lambda l, j: (l, 0, j)),
        out_shape=jax.ShapeDtypeStruct((depth, bc, e), f32),
        compiler_params=_cparams(("parallel", "parallel")),
        name="ada_mods",
    )(c_all, ada_w, ada_b.reshape(depth, 1, e))


N_MODS = 6


def _mod_spec(mod, tm):
    mods, layer, chunk = mod
    d = mods.shape[-1] // N_MODS
    if mods.shape[2] == 1:
        return pl.BlockSpec((1, 1, 1, d), lambda g, i: (layer, g, 0, chunk))
    return pl.BlockSpec((1, 1, tm, d), lambda g, i: (layer, g, i, chunk))


def _tok_spec(tm, d):
    return pl.BlockSpec((1, tm, d), lambda g, i: (g, i, 0))


def _full_spec(shape):
    nd = len(shape)
    return pl.BlockSpec(shape, lambda g, i: (0,) * nd)


def _route(h2, rw_ref, rwh_ref, rb_ref):
    hi = h2.astype(bf16)
    lo = (h2 - hi.astype(f32)).astype(bf16)
    p = _dot(hi, rw_ref[...]) + _dot(lo, rwh_ref[...])
    pt = p.T
    logits = pt[0:N_EXPERTS] + pt[N_EXPERTS:2 * N_EXPERTS]
    sc = _sigmoid(logits)
    sel = sc + rb_ref[...]
    a = [sel[PER_GROUP * j:PER_GROUP * (j + 1)] for j in range(PER_GROUP)]
    s = [sc[PER_GROUP * j:PER_GROUP * (j + 1)] for j in range(PER_GROUP)]
    hi01, lo01 = jnp.maximum(a[0], a[1]), jnp.minimum(a[0], a[1])
    hi23, lo23 = jnp.maximum(a[2], a[3]), jnp.minimum(a[2], a[3])
    gs = jnp.maximum(hi01, hi23) + jnp.maximum(jnp.minimum(hi01, hi23), jnp.maximum(lo01, lo23))
    best = gs[0:1]
    grp = jnp.zeros_like(best)
    for g in range(1, N_GROUPS):
        better = gs[g:g + 1] > best
        grp = jnp.where(better, float(g), grp)
        best = jnp.where(better, gs[g:g + 1], best)
    mv, sv = [], []
    for j in range(PER_GROUP):
        m_j = a[j][0:1]
        s_j = s[j][0:1]
        for g in range(1, N_GROUPS):
            m_j = jnp.where(grp == float(g), a[j][g:g + 1], m_j)
            s_j = jnp.where(grp == float(g), s[j][g:g + 1], s_j)
        mv.append(m_j)
        sv.append(s_j)

    def first_argmax(vals):
        bv, bi = vals[0], jnp.zeros_like(vals[0])
        for j in range(1, PER_GROUP):
            better = vals[j] > bv
            bi = jnp.where(better, float(j), bi)
            bv = jnp.where(better, vals[j], bv)
        return bi

    i1 = first_argmax(mv)
    i2 = first_argmax([jnp.where(i1 == float(j), NEG_INF, mv[j]) for j in range(PER_GROUP)])
    w1, w2 = sv[0], sv[0]
    for j in range(1, PER_GROUP):
        w1 = jnp.where(i1 == float(j), sv[j], w1)
        w2 = jnp.where(i2 == float(j), sv[j], w2)
    tot = w1 + w2
    e1 = grp * float(PER_GROUP) + i1
    e2 = grp * float(PER_GROUP) + i2
    rows = jnp.concatenate([e1, e2, w1 / tot, w2 / tot, jnp.zeros((LANES - 4, e1.shape[1]), f32)], axis=0)
    return rows.T


N_POST_IN = 9


POST_ROW_SPLIT = 2


def _row_chunks(tm, split=1):
    nr = tm // split
    return [(r * nr, nr) for r in range(split)]


def _mod_rows(ref, r0, nr):
    if ref.shape[2] == 1:
        return ref[0, 0]
    return ref[0, 0, pl.ds(r0, nr), :]


def _residual_ln_route(y, post_refs, x1_ref, h2_ref, route_ref, r0, nr):
    x_ref, gate_ref, lng_ref, lnb_ref, sc_ref, sh_ref, rw_ref, rwh_ref, rb_ref = post_refs
    rows = pl.ds(r0, nr)
    r = ALPHA * x_ref[0, rows, :] + _mod_rows(gate_ref, r0, nr) * y
    mu = jnp.mean(r, axis=-1, keepdims=True)
    cen = r - mu
    var = jnp.mean(cen * cen, axis=-1, keepdims=True)
    x1 = cen * lax.rsqrt(var + EPS) * lng_ref[...] + lnb_ref[...]
    x1_ref[0, rows, :] = x1
    h2 = x1 * (1.0 + _mod_rows(sc_ref, r0, nr)) + _mod_rows(sh_ref, r0, nr)
    h2_ref[rows, :] = h2
    route_ref[0, rows, :] = _route(h2, rw_ref, rwh_ref, rb_ref)


def _post_call(kernel_fn, name, mixer_args, mixer_specs, x, gate, ln_g, ln_b, scale2, shift2, rw, rwh, rb, tm,
               h2_rows, h2_row0, h2_prev):
    g, t, d = x.shape
    steps = t // tm
    blk0 = h2_row0 // tm
    in_specs = list(mixer_specs) + [
        _tok_spec(tm, d), _mod_spec(gate, tm), _full_spec((1, d)), _full_spec((1, d)),
        _mod_spec(scale2, tm), _mod_spec(shift2, tm),
        _full_spec((d, LANES)), _full_spec((d, LANES)), _full_spec((N_EXPERTS, 1)),
    ]
    args = list(mixer_args) + [x, gate[0], ln_g, ln_b, scale2[0], shift2[0], rw, rwh, rb]
    aliases = {}
    if h2_prev is not None:
        aliases = {len(args): 1}
        in_specs.append(pl.BlockSpec(memory_space=pl.ANY))
        args.append(h2_prev)
    return pl.pallas_call(
        kernel_fn,
        grid=(g, steps),
        in_specs=in_specs,
        out_specs=[_tok_spec(tm, d),
                   pl.BlockSpec((tm, d), lambda gi, i: (blk0 + gi * steps + i, 0)),
                   _tok_spec(tm, LANES)],
        out_shape=[jax.ShapeDtypeStruct((g, t, d), f32),
                   jax.ShapeDtypeStruct((h2_rows, d), f32),
                   jax.ShapeDtypeStruct((g, t, LANES), f32)],
        input_output_aliases=aliases,
        compiler_params=_cparams(("parallel", "parallel")),
        name=name,
    )(*args)


def _mlstm_in_kernel(x_ref, sc_ref, sh_ref, w_ref, bg_ref, qkv_ref, o_ref, g_ref):
    hqk2 = 2 * HEADS * A_DQK
    hv = HEADS * A_DV
    for r0, nr in _row_chunks(x_ref.shape[1]):
        rows = pl.ds(r0, nr)
        h = x_ref[0, rows, :] * (1.0 + _mod_rows(sc_ref, r0, nr)) + _mod_rows(sh_ref, r0, nr)
        z = _dot(h.astype(bf16), w_ref[...])
        qkv_ref[0, rows, :] = z[:, :hqk2 + hv].astype(bf16)
        o_ref[0, rows, :] = z[:, hqk2 + hv:hqk2 + 2 * hv]
        g_ref[0, rows, :] = z[:, hqk2 + 2 * hv:] + bg_ref[...]


def _mlstm_in(x, scale, shift, w_pad, bg_pad, tm):
    g, t, d = x.shape
    n_qkv = 2 * HEADS * A_DQK + HEADS * A_DV
    hv = HEADS * A_DV
    return pl.pallas_call(
        _mlstm_in_kernel,
        grid=(g, t // tm),
        in_specs=[_tok_spec(tm, d), _mod_spec(scale, tm), _mod_spec(shift, tm),
                  _full_spec(w_pad.shape), _full_spec((1, LANES))],
        out_specs=[_tok_spec(tm, n_qkv), _tok_spec(tm, hv), _tok_spec(tm, LANES)],
        out_shape=[jax.ShapeDtypeStruct((g, t, n_qkv), bf16),
                   jax.ShapeDtypeStruct((g, t, hv), f32),
                   jax.ShapeDtypeStruct((g, t, LANES), f32)],
        compiler_params=_cparams(("parallel", "parallel")),
        name="mlstm_in",
    )(x, scale[0], shift[0], w_pad, bg_pad)


def _mlstm_out_kernel(u_ref, w_ref, *refs):
    for r0, nr in _row_chunks(u_ref.shape[1], POST_ROW_SPLIT):
        y = _dot(u_ref[0, pl.ds(r0, nr), :], w_ref[...])
        _residual_ln_route(y, refs[:N_POST_IN], *refs[-3:], r0, nr)


def _mlstm_out(u, w_out, tm, **post):
    return _post_call(_mlstm_out_kernel, "mlstm_out", [u, w_out],
                      [_tok_spec(tm, u.shape[-1]), _full_spec(w_out.shape)], tm=tm, **post)


def _head_norm_gate(hh, ng, o):
    mu = jnp.mean(hh, axis=-1, keepdims=True)
    cen = hh - mu
    var = jnp.mean(cen * cen, axis=-1, keepdims=True)
    return (cen * lax.rsqrt(var + EPS) * ng) * _sigmoid(o)


def _row_sum_replicated(x):
    ones = jnp.ones((LANES, LANES), bf16)
    hi = x.astype(bf16)
    lo = (x - hi.astype(f32)).astype(bf16)
    return _dot(hi, ones) + _dot(lo, ones)


def _mlstm_prompt_kernel(qkv_ref, kt_ref, o_ref, gcol_ref, grow_ref, ng_ref,
                         u_ref, c_out, n_out, m_out, c_s, n_s, m_s):
    c = pl.program_id(1)
    nc = pl.num_programs(1)
    L = qkv_ref.shape[1]
    hqk = HEADS * A_DQK

    @pl.when(c == 0)
    def _():
        c_s[...] = jnp.zeros_like(c_s)
        n_s[...] = jnp.zeros_like(n_s)
        m_s[...] = jnp.zeros_like(m_s)

    r_i = lax.broadcasted_iota(jnp.int32, (L, L), 0)
    c_i = lax.broadcasted_iota(jnp.int32, (L, L), 1)
    causal = c_i <= r_i
    tri = causal.astype(f32)
    tri_t = (r_i <= c_i).astype(f32)

    G = qkv_ref.shape[0]
    gs = range(G)
    gcol = [gcol_ref[g] for g in gs]
    grow = [grow_ref[g] for g in gs]
    b_col = [_dot_exact(tri, _log_sigmoid(gcol[g])) for g in gs]
    b_row = [_dot_exact(_log_sigmoid(grow[g][HEADS:2 * HEADS]), tri_t) for g in gs]
    li_row = [grow[g][0:HEADS] for g in gs]

    qkv = [qkv_ref[g] for g in gs]
    kt = [kt_ref[g] for g in gs]
    o_all = [o_ref[g] for g in gs]
    ng = ng_ref[...]
    c_old, n_old, m_old = c_s[...], n_s[...], m_s[...]
    hs = range(G * HEADS)
    sq = [k // HEADS for k in hs]
    hd = [k % HEADS for k in hs]
    q = [qkv[sq[h]][:, A_DQK * hd[h]:A_DQK * (hd[h] + 1)] for h in hs]
    v = [qkv[sq[h]][:, 2 * hqk + A_DV * hd[h]:2 * hqk + A_DV * (hd[h] + 1)] for h in hs]
    kth = [kt[sq[h]][A_DQK * hd[h]:A_DQK * (hd[h] + 1), :] for h in hs]
    s_qk = [_dot(q[h], kth[h]) for h in hs]
    r_e = lax.broadcasted_iota(jnp.int32, (LANES, HEADS * LANES), 0)
    c_e = lax.broadcasted_iota(jnp.int32, (LANES, HEADS * LANES), 1)
    sel = (r_e == HEADS + (c_e >> 7)).astype(bf16)
    b_rep = []
    for g in gs:
        b_hi = b_col[g].astype(bf16)
        b_r1 = b_col[g] - b_hi.astype(f32)
        b_mid = b_r1.astype(bf16)
        b_lo = (b_r1 - b_mid.astype(f32)).astype(bf16)
        b_rep.append(_dot(b_hi, sel) + _dot(b_mid, sel) + _dot(b_lo, sel))
    bc = [b_rep[sq[h]][:, LANES * hd[h]:LANES * (hd[h] + 1)] for h in hs]
    br = [b_row[sq[h]][hd[h]:hd[h] + 1] for h in hs]
    lir = [li_row[sq[h]][hd[h]:hd[h] + 1] for h in hs]
    m_prev = [m_old[h:h + 1, 0:1] for h in hs]
    d = [jnp.where(causal, bc[h] - br[h] + lir[h], NEG_INF) for h in hs]
    a = [bc[h] + m_prev[h] for h in hs]
    d_max = [jnp.max(d[h], axis=-1, keepdims=True) for h in hs]
    m_t = [jnp.maximum(a[h], d_max[h]) for h in hs]
    p = [s_qk[h] * jnp.exp(d[h] - m_t[h]) for h in hs]
    inter = [jnp.exp(a[h] - m_t[h]) for h in hs]
    qc = [_dot(q[h], jnp.concatenate([c_old[h], n_old[h]], axis=1).astype(bf16)) for h in hs]
    p_sum = [_row_sum_replicated(p[h]) for h in hs]
    num = [_dot(p[h].astype(bf16), v[h]) + inter[h] * qc[h][:, :A_DV] for h in hs]
    den = [p_sum[h] + inter[h] * qc[h][:, A_DV:] for h in hs]
    hh = [num[h] / jnp.maximum(jnp.abs(den[h]), jnp.exp(-m_t[h])) for h in hs]
    mu = [_row_sum_replicated(hh[h]) * (1.0 / A_DV) for h in hs]
    cen = [hh[h] - mu[h] for h in hs]
    var = [_row_sum_replicated(cen[h] * cen[h]) * (1.0 / A_DV) for h in hs]
    u_parts = [((cen[h] * lax.rsqrt(var[h] + EPS) * ng[:, A_DV * hd[h]:A_DV * (hd[h] + 1)])
                * _sigmoid(o_all[sq[h]][:, A_DV * hd[h]:A_DV * (hd[h] + 1)])).astype(bf16) for h in hs]

    b_end = [br[h][:, L - 1:L] for h in hs]
    g_row = [b_end[h] - br[h] + lir[h] for h in hs]
    g_max = [jnp.max(g_row[h], axis=-1, keepdims=True) for h in hs]
    m_new = [jnp.maximum(b_end[h] + m_prev[h], g_max[h]) for h in hs]
    decay = [jnp.exp(b_end[h] + m_prev[h] - m_new[h]) for h in hs]
    kw = [kth[h].astype(f32) * jnp.exp(g_row[h] - m_new[h]) for h in hs]
    kw_sum = [_row_sum_replicated(kw[h]) for h in hs]
    c_parts = [decay[h] * c_old[h] + _dot(kw[h].astype(bf16), v[h]) for h in hs]
    n_parts = [decay[h] * n_old[h] + kw_sum[h] for h in hs]
    m_parts = [jnp.broadcast_to(m_new[h], (1, LANES)) for h in hs]

    for g in gs:
        u_ref[g] = jnp.concatenate(u_parts[g * HEADS:(g + 1) * HEADS], axis=1)
    c_s[...] = jnp.stack(c_parts)
    n_s[...] = jnp.stack(n_parts)
    m_s[...] = jnp.concatenate(m_parts, axis=0)

    @pl.when(c == nc - 1)
    def _():
        for g in gs:
            c_out[g] = c_s[g * HEADS:(g + 1) * HEADS]
            n_out[g] = n_s[g * HEADS:(g + 1) * HEADS]
            m_out[g] = m_s[g * HEADS:(g + 1) * HEADS]


def _mlstm_prompt(qkv, kt, o, gcol, grow, ng):
    b, t, _ = qkv.shape
    L = MLSTM_CHUNK
    assert L == LANES and t % L == 0
    hv = HEADS * A_DV
    G = MLSTM_SEQS
    assert b % G == 0
    return pl.pallas_call(
        _mlstm_prompt_kernel,
        grid=(b // G, t // L),
        in_specs=[
            pl.BlockSpec((G, L, qkv.shape[-1]), lambda g, c: (g, c, 0)),
            pl.BlockSpec((G, HEADS * A_DQK, L), lambda g, c: (g, 0, c)),
            pl.BlockSpec((G, L, hv), lambda g, c: (g, c, 0)),
            pl.BlockSpec((G, L, LANES), lambda g, c: (g, c, 0)),
            pl.BlockSpec((G, 2 * HEADS, L), lambda g, c: (g, 0, c)),
            pl.BlockSpec((1, hv), lambda g, c: (0, 0)),
        ],
        out_specs=[
            pl.BlockSpec((G, L, hv), lambda g, c: (g, c, 0)),
            pl.BlockSpec((G, HEADS, A_DQK, A_DV), lambda g, c: (g, 0, 0, 0)),
            pl.BlockSpec((G, HEADS, A_DQK, LANES), lambda g, c: (g, 0, 0, 0)),
            pl.BlockSpec((G, HEADS, LANES), lambda g, c: (g, 0, 0)),
        ],
        out_shape=[
            jax.ShapeDtypeStruct((b, t, hv), bf16),
            jax.ShapeDtypeStruct((b, HEADS, A_DQK, A_DV), f32),
            jax.ShapeDtypeStruct((b, HEADS, A_DQK, LANES), f32),
            jax.ShapeDtypeStruct((b, HEADS, LANES), f32),
        ],
        scratch_shapes=[
            pltpu.VMEM((G * HEADS, A_DQK, A_DV), f32),
            pltpu.VMEM((G * HEADS, A_DQK, LANES), f32),
            pltpu.VMEM((G * HEADS, LANES), f32),
        ],
        compiler_params=_cparams(("parallel", "arbitrary")),
        name="mlstm_prompt",
    )(qkv, kt, o, gcol, grow, ng)


def _mlstm_sample_kernel(qkv_ref, kt_ref, o_ref, gcol_ref, grow_ref, ng_ref, c0_ref, n0_ref, nrows_ref,
                         m0_ref, mcol_ref, mrow_ref, *refs, ts):
    u_ref, c_out, n_out, m_out = refs[-4:]
    R = qkv_ref.shape[1]
    G = R // ts
    shift = ts.bit_length() - 1
    hqk = HEADS * A_DQK

    r_i = lax.broadcasted_iota(jnp.int32, (R, R), 0)
    c_i = lax.broadcasted_iota(jnp.int32, (R, R), 1)
    same = (r_i >> shift) == (c_i >> shift)
    valid = same & (c_i <= r_i)
    tri = valid.astype(f32)
    tri_t = (same & (r_i <= c_i)).astype(f32)
    same_f = same.astype(f32)
    row_seq = lax.broadcasted_iota(jnp.int32, (R, 1), 0) >> shift
    g_i = lax.broadcasted_iota(jnp.int32, (G, R), 0)
    s_i = lax.broadcasted_iota(jnp.int32, (G, R), 1)
    bmask = (s_i >> shift) == g_i
    lastmask = s_i == g_i * ts + (ts - 1)
    g3 = lax.broadcasted_iota(jnp.int32, (G, A_DQK, R), 0)
    s3 = lax.broadcasted_iota(jnp.int32, (G, A_DQK, R), 2)
    bmask3 = (s3 >> shift) == g3
    lane_i = lax.broadcasted_iota(jnp.int32, (G, LANES), 1)

    gcol = gcol_ref[0]
    grow = grow_ref[0]
    lf_row = _log_sigmoid(grow[HEADS:2 * HEADS])
    b_col = _dot_exact(tri, _log_sigmoid(gcol))
    b_row = _dot_exact(lf_row, tri_t)
    bend_row = _dot_exact(lf_row, same_f)
    li_row = grow[0:HEADS]
    mcol = mcol_ref[0]
    mrow = mrow_ref[0]
    m0 = m0_ref[...]

    m_acc = jnp.zeros((G, LANES), f32)
    n_parts = []
    for h in range(HEADS):
        q_h = qkv_ref[0, :, A_DQK * h:A_DQK * (h + 1)]
        k_h = qkv_ref[0, :, hqk + A_DQK * h:hqk + A_DQK * (h + 1)]
        v_h = qkv_ref[0, :, 2 * hqk + A_DV * h:2 * hqk + A_DV * (h + 1)]
        kt_h = kt_ref[0, A_DQK * h:A_DQK * (h + 1), :]
        s_qk = _dot(q_h, kt_h)
        bc = b_col[:, HEADS + h:HEADS + h + 1]
        br = b_row[h:h + 1]
        lir = li_row[h:h + 1]
        d = jnp.where(valid, bc - br + lir, NEG_INF)
        a = bc + mcol[:, h:h + 1]
        m_t = jnp.maximum(a, jnp.max(d, axis=-1, keepdims=True))
        p = s_qk * jnp.exp(d - m_t)
        inter = jnp.exp(a - m_t)
        c_all = jnp.concatenate([c0_ref[0, g, h] for g in range(G)], axis=1).astype(bf16)
        qc = _dot(q_h, c_all)
        inter_c = jnp.zeros((R, A_DV), f32)
        for g in range(G):
            inter_c = jnp.where(row_seq == g, qc[:, A_DV * g:A_DV * (g + 1)], inter_c)
        qn = jnp.sum(q_h.astype(f32) * nrows_ref[0, :, A_DQK * h:A_DQK * (h + 1)], axis=-1, keepdims=True)
        num = _dot(p.astype(bf16), v_h) + inter * inter_c
        den = jnp.sum(p, axis=-1, keepdims=True) + inter * qn
        hh = num / jnp.maximum(jnp.abs(den), jnp.exp(-m_t))
        u = _head_norm_gate(hh, ng_ref[:, A_DV * h:A_DV * (h + 1)], o_ref[0, :, A_DV * h:A_DV * (h + 1)])
        u_ref[0, :, A_DV * h:A_DV * (h + 1)] = u.astype(bf16)

        g_row = bend_row[h:h + 1] - br + lir
        gmax = jnp.max(jnp.where(bmask, g_row, NEG_INF), axis=-1, keepdims=True)
        bend_b = jnp.sum(jnp.where(lastmask, br, 0.0), axis=-1, keepdims=True)
        m0_h = m0[:, h:h + 1]
        mnew_b = jnp.maximum(bend_b + m0_h, gmax)
        mnew_row = jnp.sum(jnp.where(bmask, mnew_b, 0.0), axis=0, keepdims=True)
        w_row = jnp.exp(g_row - mnew_row)
        decay_b = jnp.exp(bend_b + m0_h - mnew_b)
        kw = kt_h.astype(f32) * w_row
        kw3 = jnp.where(bmask3, jnp.broadcast_to(kw[None], (G, A_DQK, R)), 0.0)
        upd = _dot(kw3.reshape(G * A_DQK, R).astype(bf16), v_h)
        for g in range(G):
            c_out[0, g, h] = decay_b[g:g + 1, :] * c0_ref[0, g, h] + upd[A_DQK * g:A_DQK * (g + 1)]
        wm = jnp.where(bmask, w_row, 0.0)
        n_parts.append(decay_b * n0_ref[:, A_DQK * h:A_DQK * (h + 1)] + _dot(wm.astype(bf16), k_h))
        m_acc = jnp.where(lane_i == h, mnew_b, m_acc)

    n_out[...] = jnp.concatenate(n_parts, axis=1)
    m_out[...] = m_acc[:, :HEADS]


def _mlstm_sample(qkv, kt, o, gcol, grow, ng, c0_all, layer, c_prev, n0, nrows, m0, mcol, mrow, ts):
    _, t, _ = qkv.shape
    n_layers, bs = c0_all.shape[:2]
    G = SAMPLE_GROUP
    R = G * ts
    hv = HEADS * A_DV
    hqk = HEADS * A_DQK
    args = [qkv, kt, o, gcol, grow, ng, c0_all, n0, nrows, m0, mcol, mrow]
    extra_specs, aliases = [], {}
    if c_prev is not None:
        aliases = {len(args): 1}
        extra_specs = [pl.BlockSpec(memory_space=pl.ANY)]
        args.append(c_prev)
    return pl.pallas_call(
        functools.partial(_mlstm_sample_kernel, ts=ts),
        grid=(t // R,),
        input_output_aliases=aliases,
        in_specs=[
            pl.BlockSpec((1, R, qkv.shape[-1]), lambda i: (0, i, 0)),
            pl.BlockSpec((1, hqk, R), lambda i: (0, 0, i)),
            pl.BlockSpec((1, R, hv), lambda i: (0, i, 0)),
            pl.BlockSpec((1, R, LANES), lambda i: (0, i, 0)),
            pl.BlockSpec((1, 2 * HEADS, R), lambda i: (0, 0, i)),
            pl.BlockSpec((1, hv), lambda i: (0, 0)),
            pl.BlockSpec((1, G, HEADS, A_DQK, A_DV), lambda i: (layer, i, 0, 0, 0)),
            pl.BlockSpec((G, hqk), lambda i: (i, 0)),
            pl.BlockSpec((1, R, hqk), lambda i: (0, i, 0)),
            pl.BlockSpec((G, HEADS), lambda i: (i, 0)),
            pl.BlockSpec((1, R, LANES), lambda i: (0, i, 0)),
            pl.BlockSpec((1, HEADS, R), lambda i: (0, 0, i)),
        ] + extra_specs,
        out_specs=[
            pl.BlockSpec((1, R, hv), lambda i: (0, i, 0)),
            pl.BlockSpec((1, G, HEADS, A_DQK, A_DV), lambda i: (layer, i, 0, 0, 0)),
            pl.BlockSpec((G, hqk), lambda i: (i, 0)),
            pl.BlockSpec((G, HEADS), lambda i: (i, 0)),
        ],
        out_shape=[
            jax.ShapeDtypeStruct((1, t, hv), bf16),
            jax.ShapeDtypeStruct((n_layers, bs, HEADS, A_DQK, A_DV), f32),
            jax.ShapeDtypeStruct((bs, hqk), f32),
            jax.ShapeDtypeStruct((bs, HEADS), f32),
        ],
        compiler_params=_cparams(("parallel",)),
        name="mlstm_sample",
    )(*args)


def _expert_kernel(be_ref, nu_ref, first_ref, *refs, n_parts):
    x_refs = refs[:n_parts]
    wg_ref, wu_ref, wd_ref, y_ref, wg_s, wu_s, wd_s = refs[n_parts:]
    i = pl.program_id(0)
    part_blocks = pl.num_programs(0) // n_parts

    @pl.when(first_ref[i] == 1)
    def _():
        wg_s[...] = wg_ref[0, 0].astype(bf16)
        wu_s[...] = wu_ref[0, 0].astype(bf16)
        wd_s[...] = wd_ref[0, 0].astype(bf16)

    @pl.when(i < nu_ref[0])
    def _():
        x = x_refs[n_parts - 1][...]
        for part in range(n_parts - 2, -1, -1):
            x = jnp.where(i < (part + 1) * part_blocks, x_refs[part][...], x)
        x = x.astype(bf16)
        g = _dot(x, wg_s[...])
        u = _dot(x, wu_s[...])
        hid = (g * _sigmoid(g)) * u
        y_ref[...] = _dot(hid.astype(bf16), wd_s[...])

    @pl.when(i >= nu_ref[0])
    def _():
        y_ref[...] = jnp.zeros_like(y_ref)


def _experts(x_parts, block_e, n_used, first, wg, wu, wd, layer):
    n_parts = len(x_parts)
    pp, d = x_parts[0].shape
    de = wg.shape[-1]
    bm = MOE_BLOCK
    pb = pp // bm
    w_map = lambda i, be, nu, fi: (layer, be[i], 0, 0)
    part_specs = [pl.BlockSpec((bm, d), functools.partial(
        lambda i, be, nu, fi, part: (jnp.clip(i - part * pb, 0, pb - 1), 0), part=part)) for part in range(n_parts)]
    return pl.pallas_call(
        functools.partial(_expert_kernel, n_parts=n_parts),
        grid_spec=pltpu.PrefetchScalarGridSpec(
            num_scalar_prefetch=3,
            grid=(n_parts * pb,),
            in_specs=part_specs + [
                pl.BlockSpec((1, 1, d, de), w_map),
                pl.BlockSpec((1, 1, d, de), w_map),
                pl.BlockSpec((1, 1, de, d), w_map),
            ],
            out_specs=pl.BlockSpec((bm, d), lambda i, be, nu, fi: (i, 0)),
            scratch_shapes=[pltpu.VMEM((d, de), bf16), pltpu.VMEM((d, de), bf16), pltpu.VMEM((de, d), bf16)],
        ),
        out_shape=jax.ShapeDtypeStruct((n_parts * pp, d), f32),
        compiler_params=_cparams(("arbitrary",)),
        name="moe_experts",
    )(block_e, n_used, first, *x_parts, wg, wu, wd)


def _combine_kernel(x_ref, ya_ref, yb_ref, w_ref, gate_ref, lng_ref, lnb_ref, o_ref):
    w = w_ref[0]
    y = w[:, 2:3] * ya_ref[...] + w[:, 3:4] * yb_ref[...]
    r = ALPHA * x_ref[0] + gate_ref[0, 0] * y
    mu = jnp.mean(r, axis=-1, keepdims=True)
    cen = r - mu
    var = jnp.mean(cen * cen, axis=-1, keepdims=True)
    o_ref[0] = cen * lax.rsqrt(var + EPS) * lng_ref[...] + lnb_ref[...]


def _combine(x1, ya, yb, row0, wexp, gate, ln_g, ln_b, tm):
    g, t, d = x1.shape
    steps = t // tm
    blk0 = row0 // tm
    y_spec = pl.BlockSpec((tm, d), lambda gi, i: (blk0 + gi * steps + i, 0))
    return pl.pallas_call(
        _combine_kernel,
        grid=(g, steps),
        in_specs=[_tok_spec(tm, d), y_spec, y_spec,
                  _tok_spec(tm, LANES), _mod_spec(gate, tm), _full_spec((1, d)), _full_spec((1, d))],
        out_specs=_tok_spec(tm, d),
        out_shape=jax.ShapeDtypeStruct((g, t, d), f32),
        compiler_params=_cparams(("parallel", "parallel")),
        name="moe_combine",
    )(x1, ya, yb, wexp, gate[0], ln_g, ln_b)


def _moe(groups, h2_all, ln_g, ln_b, wg, wu, wd, layer, tm):
    bm = MOE_BLOCK
    d = groups[0][0].shape[-1]
    sizes = [x1.shape[0] * x1.shape[1] for x1, _, _ in groups]
    n = sum(sizes)
    nk = 2 * n
    expert = jnp.concatenate([r[:, :, 0:2].reshape(-1, 2) for _, r, _ in groups], axis=0).astype(jnp.int32)
    e_flat = expert.reshape(nk)
    onehot = (e_flat[:, None] == jnp.arange(N_EXPERTS, dtype=jnp.int32)[None, :]).astype(jnp.int32)
    csum = jnp.cumsum(onehot, axis=0)
    counts = csum[-1]
    rank = jnp.sum(csum * onehot, axis=1) - 1
    padded = (counts + bm - 1) // bm * bm
    pends = jnp.cumsum(padded)
    pstarts = pends - padded
    dest = jnp.sum(onehot * pstarts[None, :], axis=1) + rank
    n_blocks = nk // bm + N_EXPERTS
    n_used = (pends[-1] // bm).astype(jnp.int32)
    blk = jnp.minimum(jnp.arange(n_blocks, dtype=jnp.int32), n_used - 1)
    block_e = jnp.minimum(jnp.sum((pends[None, :] <= (blk * bm)[:, None]).astype(jnp.int32), axis=1),
                          N_EXPERTS - 1).astype(jnp.int32)
    first = jnp.concatenate([jnp.ones((1,), jnp.int32), (block_e[1:] != block_e[:-1]).astype(jnp.int32)])
    tok = jnp.arange(nk, dtype=jnp.int32) // 2
    tok_pad = jnp.zeros((n_blocks * bm,), jnp.int32).at[dest].set(tok, unique_indices=True, mode="promise_in_bounds")
    part_rows = n_blocks // GATHER_PARTS * bm
    x_parts = [jnp.take(h2_all, tok_pad[c * part_rows:(c + 1) * part_rows], axis=0, mode="clip")
               for c in range(GATHER_PARTS)]
    y = _experts(x_parts, block_e, n_used.reshape(1), first, wg, wu, wd, layer)
    dest2 = dest.reshape(n, 2)
    ya = jnp.take(y, dest2[:, 0], axis=0, mode="clip")
    yb = jnp.take(y, dest2[:, 1], axis=0, mode="clip")
    outs, off = [], 0
    for (x1, route, gate2), sz in zip(groups, sizes):
        outs.append(_combine(x1, ya, yb, off, route, gate2, ln_g, ln_b, tm))
        off += sz
    return outs


def _kv_kernel(x_ref, w_ref, g_ref, cos_ref, sin_ref, lat_ref, kr_ref, latb_ref, krb_ref):
    kva = _dot(x_ref[0].astype(bf16), w_ref[...])
    latp = kva[:, :B_KV_RANK]
    lat = latp * lax.rsqrt(jnp.mean(latp * latp, axis=-1, keepdims=True) + EPS) * g_ref[...]
    kr = kva[:, B_KV_RANK:B_KV_RANK + B_D_ROPE] * cos_ref[0] \
        + kva[:, B_KV_RANK + LANES:B_KV_RANK + LANES + B_D_ROPE] * sin_ref[0]
    lat_ref[0] = lat
    kr_ref[0] = kr
    latb_ref[0] = lat.astype(bf16)
    krb_ref[0] = kr.astype(bf16)


def _shared_kv(x, w_pad, g, cos, sin, tm):
    gg, t, d = x.shape
    rope_spec = pl.BlockSpec((1, tm, B_D_ROPE), lambda gi, i: (0, i, 0)) if cos.shape[0] == 1 else \
        _tok_spec(tm, B_D_ROPE)
    return pl.pallas_call(
        _kv_kernel,
        grid=(gg, t // tm),
        in_specs=[_tok_spec(tm, d), _full_spec(w_pad.shape), _full_spec((1, B_KV_RANK)), rope_spec, rope_spec],
        out_specs=[_tok_spec(tm, B_KV_RANK), _tok_spec(tm, B_D_ROPE),
                   _tok_spec(tm, B_KV_RANK), _tok_spec(tm, B_D_ROPE)],
        out_shape=[jax.ShapeDtypeStruct((gg, t, B_KV_RANK), f32), jax.ShapeDtypeStruct((gg, t, B_D_ROPE), f32),
                   jax.ShapeDtypeStruct((gg, t, B_KV_RANK), bf16), jax.ShapeDtypeStruct((gg, t, B_D_ROPE), bf16)],
        compiler_params=_cparams(("parallel", "parallel")),
        name="shared_kv",
    )(x, w_pad, g, cos, sin)


def _mla_q_kernel(x_ref, sc_ref, sh_ref, wdq_ref, qg_ref, wuq_ref, wuk_ref, cos_ref, sin_ref, ql_ref, qr_ref):
    n_nope = HEADS * B_D_NOPE
    n_rope = HEADS * B_D_ROPE
    for r0, nr in _row_chunks(x_ref.shape[1]):
        rows = pl.ds(r0, nr)
        h = x_ref[0, rows, :] * (1.0 + _mod_rows(sc_ref, r0, nr)) + _mod_rows(sh_ref, r0, nr)
        cq = _dot(h.astype(bf16), wdq_ref[...])
        cq = cq * lax.rsqrt(jnp.mean(cq * cq, axis=-1, keepdims=True) + EPS) * qg_ref[...]
        q = _dot(cq.astype(bf16), wuq_ref[...])
        rot = q[:, n_nope:n_nope + n_rope] * cos_ref[0, rows, :] + q[:, n_nope + n_rope:] * sin_ref[0, rows, :]
        for hd in range(HEADS):
            qn = q[:, B_D_NOPE * hd:B_D_NOPE * (hd + 1)].astype(bf16)
            ql_ref[0, hd, rows, :] = _dot(qn, wuk_ref[hd]).astype(bf16)
            qr_ref[0, hd, rows, :] = rot[:, B_D_ROPE * hd:B_D_ROPE * (hd + 1)].astype(bf16)


def _mla_q(x, scale, shift, wdq, qg, wuq, wuk, cos8, sin8, tm):
    g, t, d = x.shape
    n_rope = HEADS * B_D_ROPE
    rope_spec = pl.BlockSpec((1, tm, n_rope), lambda gi, i: (0, i, 0)) if cos8.shape[0] == 1 else \
        _tok_spec(tm, n_rope)
    return pl.pallas_call(
        _mla_q_kernel,
        grid=(g, t // tm),
        in_specs=[_tok_spec(tm, d), _mod_spec(scale, tm), _mod_spec(shift, tm),
                  _full_spec(wdq.shape), _full_spec(qg.shape), _full_spec(wuq.shape), _full_spec(wuk.shape),
                  rope_spec, rope_spec],
        out_specs=[pl.BlockSpec((1, HEADS, tm, B_KV_RANK), lambda gi, i: (gi, 0, i, 0)),
                   pl.BlockSpec((1, HEADS, tm, B_D_ROPE), lambda gi, i: (gi, 0, i, 0))],
        out_shape=[jax.ShapeDtypeStruct((g, HEADS, t, B_KV_RANK), bf16),
                   jax.ShapeDtypeStruct((g, HEADS, t, B_D_ROPE), bf16)],
        compiler_params=_cparams(("parallel", "parallel")),
        name="mla_q",
    )(x, scale[0], shift[0], wdq, qg, wuq, wuk, cos8, sin8)


def _softmax_step(s, v, m_s, l_s, acc_s, slot):
    rows, n = s.shape
    m_old = m_s[slot]
    if n % LANES == 0:
        _softmax_steps([s], [v], m_s, l_s, acc_s, [slot])
        return
    m_new = jnp.maximum(m_old, jnp.max(s, axis=-1, keepdims=True))
    p = jnp.exp2((s - m_new[:, 0:1]) * SOFTMAX_EXP2_SCALE)
    lane = lax.broadcasted_iota(jnp.int32, (rows, LANES), 1)
    psum = jnp.where(lane == 0, jnp.sum(p, axis=-1, keepdims=True), 0.0)
    alpha = jnp.exp2((m_old - m_new) * SOFTMAX_EXP2_SCALE)
    l_s[slot] = alpha * l_s[slot] + psum
    alpha_v = jnp.concatenate([alpha] * (v.shape[1] // LANES), axis=1)
    acc_s[slot] = alpha_v * acc_s[slot] + _dot(p.astype(bf16), v)
    m_s[slot] = m_new


def _softmax_steps(s_list, v_list, m_s, l_s, acc_s, slots):
    ids = range(len(s_list))
    nv = v_list[0].shape[1] // LANES
    m_old = [m_s[slot] for slot in slots]
    chunks = [[s[:, LANES * c:LANES * (c + 1)] for c in range(s.shape[1] // LANES)] for s in s_list]
    c_max = [functools.reduce(jnp.maximum, chunks[k]) for k in ids]
    r_max = [jnp.max(c_max[k], axis=-1, keepdims=True) for k in ids]
    m_new = [jnp.maximum(m_old[k], r_max[k]) for k in ids]
    ps = [[jnp.exp2((ch - m_new[k]) * SOFTMAX_EXP2_SCALE) for ch in chunks[k]] for k in ids]
    alpha = [jnp.exp2((m_old[k] - m_new[k]) * SOFTMAX_EXP2_SCALE) for k in ids]
    pv = [_dot(jnp.concatenate(ps[k], axis=1).astype(bf16), v_list[k]) for k in ids]
    for k, slot in enumerate(slots):
        l_s[slot] = alpha[k] * l_s[slot] + functools.reduce(jnp.add, ps[k])
        acc_s[slot] = jnp.concatenate([alpha[k]] * nv, axis=1) * acc_s[slot] + pv[k]
        m_s[slot] = m_new[k]


def _flash_kernel(ql_ref, qr_ref, lat_ref, kr_ref, o_ref, m_s, l_s, acc_s, *, tq):
    i = pl.program_id(1)
    nh = FLASH_HEADS
    r_i = lax.broadcasted_iota(jnp.int32, (tq, tq), 0)
    c_i = lax.broadcasted_iota(jnp.int32, (tq, tq), 1)
    causal = c_i <= r_i

    def group_body(hg, carry):
        m_s[...] = jnp.full_like(m_s, NEG_INF)
        l_s[...] = jnp.zeros_like(l_s)
        acc_s[...] = jnp.zeros_like(acc_s)

        def block(j, masked):
            start = pl.multiple_of(j * tq, tq)
            k_lat = lat_ref[0, pl.ds(start, tq), :]
            k_r = kr_ref[0, pl.ds(start, tq), :]
            s_list = []
            for hs in range(nh):
                h = hg * nh + hs
                s = _dot_nt(ql_ref[0, h], k_lat) + _dot_nt(qr_ref[0, h], k_r)
                s_list.append(jnp.where(causal, s, NEG_INF) if masked else s)
            _softmax_steps(s_list, [k_lat] * nh, m_s, l_s, acc_s, list(range(nh)))

        def kv_body(j, c2):
            block(j, False)
            return c2

        lax.fori_loop(0, i, kv_body, 0)
        block(i, True)
        for hs in range(nh):
            l = jnp.sum(l_s[hs], axis=-1, keepdims=True)
            o_ref[0, hg * nh + hs] = (acc_s[hs] / l).astype(bf16)
        return carry

    lax.fori_loop(0, HEADS // nh, group_body, 0)


def _flash(ql, qr, latb, krb, tq):
    b, _, t, _ = ql.shape
    nh = FLASH_HEADS
    return pl.pallas_call(
        functools.partial(_flash_kernel, tq=tq),
        grid=(b, t // tq),
        in_specs=[
            pl.BlockSpec((1, HEADS, tq, B_KV_RANK), lambda g, i: (g, 0, i, 0)),
            pl.BlockSpec((1, HEADS, tq, B_D_ROPE), lambda g, i: (g, 0, i, 0)),
            pl.BlockSpec((1, t, B_KV_RANK), lambda g, i: (g, 0, 0)),
            pl.BlockSpec((1, t, B_D_ROPE), lambda g, i: (g, 0, 0)),
        ],
        out_specs=pl.BlockSpec((1, HEADS, tq, B_KV_RANK), lambda g, i: (g, 0, i, 0)),
        out_shape=jax.ShapeDtypeStruct((b, HEADS, t, B_KV_RANK), bf16),
        scratch_shapes=[pltpu.VMEM((nh, tq, LANES), f32), pltpu.VMEM((nh, tq, LANES), f32),
                        pltpu.VMEM((nh, tq, B_KV_RANK), f32)],
        compiler_params=_cparams(("parallel", "arbitrary")),
        name="mla_flash",
    )(ql, qr, latb, krb)


def _paged_kernel(pt_ref, ql_ref, qr_ref, *rest, ts, new_pad):
    np_ = PAGES_PER_STEP
    nc = PAGED_CHAINS
    per = np_ // nc
    lat_pages = rest[:np_]
    krt_pages = rest[np_:2 * np_]
    nl_ref, nk_ref, o_ref, m_s, l_s, acc_s = rest[2 * np_:]
    s_id = pl.program_id(1)
    ql = ql_ref[0]
    qr = qr_ref[0]

    @pl.when(s_id == 0)
    def _():
        m_s[...] = jnp.full_like(m_s, NEG_INF)
        l_s[...] = jnp.zeros_like(l_s)
        acc_s[...] = jnp.zeros_like(acc_s)

    k_lats = [jnp.concatenate([p[0] for p in lat_pages[c * per:(c + 1) * per]], axis=0).astype(bf16)
              for c in range(nc)]
    k_rts = [jnp.concatenate([p[0] for p in krt_pages[c * per:(c + 1) * per]], axis=1).astype(bf16)
             for c in range(nc)]
    _softmax_steps([_dot_nt(ql, k_lats[c]) + _dot(qr, k_rts[c]) for c in range(nc)], k_lats,
                   m_s, l_s, acc_s, list(range(nc)))

    @pl.when(s_id == pl.num_programs(1) - 1)
    def _():
        rows = ql.shape[0]
        n_lat = nl_ref[0]
        s_new = _dot_nt(ql, n_lat) + _dot_nt(qr, nk_ref[0])
        t_row = lax.broadcasted_iota(jnp.int32, (rows, new_pad), 0) & (ts - 1)
        c_new = lax.broadcasted_iota(jnp.int32, (rows, new_pad), 1)
        _softmax_step(jnp.where(c_new <= t_row, s_new, NEG_INF), n_lat, m_s, l_s, acc_s, 0)
        m = functools.reduce(jnp.maximum, [m_s[c] for c in range(nc)])
        nv = B_KV_RANK // LANES
        l = jnp.zeros_like(m)
        acc = jnp.zeros_like(acc_s[0])
        for c in range(nc):
            a_c = jnp.exp2((m_s[c] - m) * SOFTMAX_EXP2_SCALE)
            l = l + a_c * l_s[c]
            acc = acc + jnp.concatenate([a_c] * nv, axis=1) * acc_s[c]
        o_ref[0] = (acc / jnp.sum(l, axis=-1, keepdims=True)).astype(bf16)


def _paged_attention(page_table, ql, qr, cache_lat, cache_krt, new_lat, new_kr, ts):
    bs, rows, _ = ql.shape
    n_pages = page_table.shape[1]
    np_ = PAGES_PER_STEP
    new_pad = new_lat.shape[1]
    page_map = [functools.partial(lambda b, s, pt, r: (pt[b, s * np_ + r], 0, 0), r=r) for r in range(np_)]
    lat_specs = [pl.BlockSpec((1, PAGE, B_KV_RANK), page_map[r]) for r in range(np_)]
    kr_specs = [pl.BlockSpec((1, B_D_ROPE, PAGE), page_map[r]) for r in range(np_)]
    return pl.pallas_call(
        functools.partial(_paged_kernel, ts=ts, new_pad=new_pad),
        grid_spec=pltpu.PrefetchScalarGridSpec(
            num_scalar_prefetch=1,
            grid=(bs, n_pages // np_),
            in_specs=[pl.BlockSpec((1, rows, B_KV_RANK), lambda b, s, pt: (b, 0, 0)),
                      pl.BlockSpec((1, rows, B_D_ROPE), lambda b, s, pt: (b, 0, 0))]
            + lat_specs + kr_specs
            + [pl.BlockSpec((1, new_pad, B_KV_RANK), lambda b, s, pt: (b, 0, 0)),
               pl.BlockSpec((1, new_pad, B_D_ROPE), lambda b, s, pt: (b, 0, 0))],
            out_specs=pl.BlockSpec((1, rows, B_KV_RANK), lambda b, s, pt: (b, 0, 0)),
            scratch_shapes=[pltpu.VMEM((PAGED_CHAINS, rows, LANES), f32), pltpu.VMEM((PAGED_CHAINS, rows, LANES), f32),
                            pltpu.VMEM((PAGED_CHAINS, rows, B_KV_RANK), f32)],
        ),
        out_shape=jax.ShapeDtypeStruct((bs, rows, B_KV_RANK), bf16),
        compiler_params=_cparams(("parallel", "arbitrary")),
        name="mla_paged",
    )(page_table, ql, qr, *([cache_lat] * np_), *([cache_krt] * np_), new_lat, new_kr)


def _mla_out_kernel(ol_ref, wuv_ref, wo_ref, *refs):
    for r0, nr in _row_chunks(ol_ref.shape[2], POST_ROW_SPLIT):
        rows = pl.ds(r0, nr)
        o = jnp.concatenate([_dot(ol_ref[0, hd, rows, :], wuv_ref[hd]) for hd in range(HEADS)], axis=1)
        y = _dot(o.astype(bf16), wo_ref[...])
        _residual_ln_route(y, refs[:N_POST_IN], *refs[-3:], r0, nr)


def _mla_out(ol, wuv, wo, tm, **post):
    return _post_call(_mla_out_kernel, "mla_out", [ol, wuv, wo],
                      [pl.BlockSpec((1, HEADS, tm, B_KV_RANK), lambda gi, i: (gi, 0, i, 0)),
                       _full_spec(wuv.shape), _full_spec(wo.shape)], tm=tm, **post)


def _rope_tables(pos):
    half = B_D_ROPE // 2
    inv = jnp.power(ROPE_THETA, -jnp.arange(half, dtype=f32) / half)
    ang = pos.astype(f32)[:, None] * inv[None, :]
    cos, sin = jnp.cos(ang), jnp.sin(ang)
    return jnp.concatenate([cos, cos], axis=-1), jnp.concatenate([-sin, sin], axis=-1)


def _swap_halves(w, width):
    lead = w.shape[:-1]
    w2 = w.reshape(lead + (-1, 2, width // 2))
    return w2[..., ::-1, :].reshape(w.shape)


def kernel(x_prompt, x_sample, cache_latent, cache_krope, page_table, state_C, state_n, state_m, c_prompt, c_sample, ada_w, ada_b, ln_g, ln_b, a_w_in, a_b_gates, a_norm_g, a_w_out, b_w_kv_a, b_kv_norm_g, b_w_uk, b_w_uv, b_w_dq, b_q_norm_g, b_w_uq, b_w_o, router_w, router_b, e_w_gate, e_w_up, e_w_down):
    bp, tp, d = x_prompt.shape
    bs, ts, _ = x_sample.shape
    hqk = HEADS * A_DQK
    hv = HEADS * A_DV

    w_in = a_w_in.at[:, :, hqk:2 * hqk].multiply(A_DQK ** -0.5)
    w_in = jnp.pad(w_in, ((0, 0), (0, 0), (0, LANES - 2 * HEADS))).astype(bf16)
    bg = jnp.pad(a_b_gates, ((0, 0), (0, LANES - 2 * HEADS))).reshape(N_A_LAYERS, 1, LANES)
    w_out = a_w_out.astype(bf16)
    ng = a_norm_g.reshape(N_A_LAYERS, 1, hv)
    perm = jnp.array([(r % N_GROUPS) * PER_GROUP + r // N_GROUPS for r in range(N_EXPERTS)], jnp.int32)
    rw_perm = router_w[:, perm]
    rw_hi = rw_perm.astype(bf16)
    rw_lo = (rw_perm - rw_hi.astype(f32)).astype(bf16)
    rw = jnp.pad(jnp.concatenate([rw_hi, rw_lo], axis=1), ((0, 0), (0, LANES - 2 * N_EXPERTS)))
    rwh = jnp.pad(rw_hi, ((0, 0), (0, LANES - N_EXPERTS)))
    rb = router_b[perm].reshape(N_EXPERTS, 1)
    w_lat = b_w_kv_a[:, :B_KV_RANK]
    w_kr = b_w_kv_a[:, B_KV_RANK:]
    zpad = jnp.zeros((d, LANES - B_D_ROPE), f32)
    w_kv = jnp.concatenate([w_lat, w_kr, zpad, _swap_halves(w_kr, B_D_ROPE), zpad], axis=1).astype(bf16)
    kvg = b_kv_norm_g.reshape(1, B_KV_RANK)
    wdq = b_w_dq.astype(bf16)
    qg = b_q_norm_g.reshape(-1, 1, b_q_norm_g.shape[-1])
    uq = b_w_uq.reshape(b_w_uq.shape[0], b_w_uq.shape[1], HEADS, B_D_NOPE + B_D_ROPE)
    uq_nope = uq[..., :B_D_NOPE].reshape(uq.shape[0], uq.shape[1], HEADS * B_D_NOPE)
    uq_rope = uq[..., B_D_NOPE:].reshape(uq.shape[0], uq.shape[1], HEADS * B_D_ROPE)
    wuq = jnp.concatenate([uq_nope, uq_rope, _swap_halves(uq_rope, B_D_ROPE)], axis=-1).astype(bf16)
    wuk = jnp.transpose(b_w_uk, (1, 2, 0)).astype(bf16)
    wuv = jnp.transpose(b_w_uv, (1, 0, 2)).astype(bf16)
    wo = b_w_o.astype(bf16)

    mods = _ada_mods(jnp.concatenate([c_prompt, c_sample], axis=0), ada_w, ada_b)

    cos_p, sin_p = _rope_tables(jnp.arange(tp, dtype=jnp.int32))
    past_len = page_table.shape[1] * PAGE
    cos_s, sin_s = _rope_tables(jnp.tile(past_len + jnp.arange(ts, dtype=jnp.int32), bs))

    tm = 512
    cache_krt = jnp.swapaxes(cache_krope, 1, 2)
    streams = []
    mods_p = mods[:, :bp].reshape(DEPTH, bp, 1, N_MODS * d)
    mods_s = jnp.repeat(mods[:, bp:], ts, axis=1).reshape(DEPTH, 1, bs * ts, N_MODS * d)
    for x0, mods_g, cos, sin, sample in ((x_prompt, mods_p, cos_p, sin_p, False),
                                         (x_sample.reshape(1, bs * ts, d), mods_s, cos_s, sin_s, True)):
        streams.append(dict(x=x0, mods=mods_g, sample=sample, cos1=cos[None], sin1=sin[None],
                            cos8=jnp.tile(cos, (1, HEADS))[None], sin8=jnp.tile(sin, (1, HEADS))[None],
                            new_c=[], new_n=[], new_m=[]))

    for l in range(DEPTH):
        lg = ln_g[l].reshape(2, 1, d)
        lb = ln_b[l].reshape(2, 1, d)
        moe_in = []
        h2_all, row0 = None, 0
        n_all = sum(st["x"].shape[0] * st["x"].shape[1] for st in streams)
        for st in streams:
            x, sample = st["x"], st["sample"]
            cos8, sin8 = st["cos8"], st["sin8"]
            latb, krb = st.get("latb"), st.get("krb")
            new_c, new_n, new_m = st["new_c"], st["new_n"], st["new_m"]
            shift1, scale1, gate1, shift2, scale2, gate2 = [(st["mods"], l, i) for i in range(N_MODS)]
            post = dict(x=x, gate=gate1, ln_g=lg[0], ln_b=lb[0], scale2=scale2, shift2=shift2, rw=rw, rwh=rwh, rb=rb,
                        h2_rows=n_all, h2_row0=row0, h2_prev=h2_all)
            if l < N_A_LAYERS:
                qkv, o, gates = _mlstm_in(x, scale1, shift1, w_in[l], bg[l], tm)
                kt = jnp.swapaxes(qkv[:, :, hqk:2 * hqk], 1, 2)
                grow = jnp.swapaxes(gates[:, :, :2 * HEADS], 1, 2)
                if sample:
                    n0 = state_n[l].reshape(bs, hqk)
                    m0 = state_m[l]
                    nrows = jnp.repeat(n0, ts, axis=0)[None]
                    m_tok = jnp.repeat(m0, ts, axis=0)
                    mcol = jnp.pad(m_tok, ((0, 0), (0, LANES - HEADS)))[None]
                    mrow = m_tok.T[None]
                    u, c_stack, n_new, m_new = _mlstm_sample(qkv, kt, o, gates, grow, ng[l], state_C, l,
                                                             st.get("c_stack"), n0, nrows, m0, mcol, mrow, ts)
                    st["c_stack"] = c_stack
                    n_new = n_new.reshape(bs, HEADS, A_DQK)
                else:
                    u, c_new, n_rep, m_rep = _mlstm_prompt(qkv, kt, o, gates, grow, ng[l])
                    n_new = n_rep[..., 0]
                    m_new = m_rep[..., 0]
                    new_c.append(c_new)
                new_n.append(n_new)
                new_m.append(m_new)
                x1, h2_all, route = _mlstm_out(u, w_out[l], tm, **post)
            else:
                j = l - N_A_LAYERS
                ql, qr = _mla_q(x, scale1, shift1, wdq[j], qg[j], wuq[j], wuk, cos8, sin8, tm)
                if sample:
                    ql_s = jnp.transpose(ql.reshape(HEADS, bs, ts, B_KV_RANK), (1, 0, 2, 3)).reshape(
                        bs, HEADS * ts, B_KV_RANK)
                    qr_s = jnp.transpose(qr.reshape(HEADS, bs, ts, B_D_ROPE), (1, 0, 2, 3)).reshape(
                        bs, HEADS * ts, B_D_ROPE)
                    new_pad = 16
                    nl = jnp.pad(latb.reshape(bs, ts, B_KV_RANK), ((0, 0), (0, new_pad - ts), (0, 0)))
                    nk = jnp.pad(krb.reshape(bs, ts, B_D_ROPE), ((0, 0), (0, new_pad - ts), (0, 0)))
                    ol = _paged_attention(page_table, ql_s, qr_s, cache_latent, cache_krt, nl, nk, ts)
                    ol = jnp.transpose(ol.reshape(bs, HEADS, ts, B_KV_RANK), (1, 0, 2, 3)).reshape(
                        1, HEADS, bs * ts, B_KV_RANK)
                else:
                    ol = _flash(ql, qr, latb, krb, 512)
                x1, h2_all, route = _mla_out(ol, wuv, wo[j], tm, **post)
            moe_in.append((x1, route, gate2))
            row0 += x.shape[0] * x.shape[1]
        for st, x_new in zip(streams, _moe(moe_in, h2_all, lg[1], lb[1], e_w_gate, e_w_up, e_w_down, l, tm)):
            st["x"] = x_new
            if l == N_A_LAYERS - 1:
                st["lat"], st["kr"], st["latb"], st["krb"] = _shared_kv(x_new, w_kv, kvg, st["cos1"], st["sin1"], tm)

    sp, ss = streams
    return (sp["x"], ss["x"].reshape(bs, ts, d),
            jnp.stack(sp["new_c"]), jnp.stack(sp["new_n"]), jnp.stack(sp["new_m"]), sp["lat"], sp["kr"],
            ss["c_stack"], jnp.stack(ss["new_n"]), jnp.stack(ss["new_m"]),
            ss["lat"].reshape(bs, ts, B_KV_RANK), ss["kr"].reshape(bs, ts, B_D_ROPE))
```

```python
import functools

import jax
import jax.numpy as jnp
from jax import lax
from jax.experimental import pallas as pl
from jax.experimental.pallas import tpu as pltpu

f32 = jnp.float32
bf16 = jnp.bfloat16

DEPTH = 4
N_A_LAYERS = 2
HEADS = 8
A_DQK = 64
A_DV = 128
B_D_NOPE = 128
B_D_ROPE = 64
B_KV_RANK = 256
ROPE_THETA = 10000.0
ATTN_SCALE = (B_D_NOPE + B_D_ROPE) ** -0.5
N_EXPERTS = 16
N_GROUPS = 4
PER_GROUP = 4
PAGE = 128
ALPHA = (2 * DEPTH) ** 0.25
EPS = 1e-6
NEG_INF = float("-inf")
LOG2E = 1.4426950408889634
SOFTMAX_EXP2_SCALE = ATTN_SCALE * LOG2E

VMEM_LIMIT_BYTES = 56 * 1024 * 1024
LANES = 128
MLSTM_CHUNK = 128
MLSTM_SEQS = 4
SAMPLE_GROUP = 16
MOE_BLOCK = 256
GATHER_PARTS = 2
PAGES_PER_STEP = 16
PAGED_CHAINS = 4
FLASH_HEADS = 8


def _cparams(sem):
    return pltpu.CompilerParams(dimension_semantics=sem, vmem_limit_bytes=VMEM_LIMIT_BYTES)


def _dot(a, b):
    return jnp.dot(a, b, preferred_element_type=f32)


def _dot_nt(a, b):
    return lax.dot_general(a, b, (((1,), (1,)), ((), ())), preferred_element_type=f32)


def _dot_exact(a, b):
    return jnp.dot(a, b, preferred_element_type=f32, precision=lax.Precision.HIGHEST)


def _sigmoid(x):
    return 1.0 / (1.0 + jnp.exp(-x))


def _log_sigmoid(x):
    return jnp.minimum(x, 0.0) - jnp.log(1.0 + jnp.exp(-jnp.abs(x)))


def _ada_kernel(c_ref, w_ref, b_ref, o_ref):
    c = c_ref[...]
    sc = (c * _sigmoid(c)).astype(bf16)
    o_ref[0] = _dot(sc, w_ref[0].astype(bf16)) + b_ref[0]


def _ada_mods(c_all, ada_w, ada_b):
    depth, d, e = ada_w.shape
    bc = c_all.shape[0]
    tn = 1536
    return pl.pallas_call(
        _ada_kernel,
        grid=(depth, e // tn),
        in_specs=[
            pl.BlockSpec((bc, d), lambda l, j: (0, 0)),
            pl.BlockSpec((1, d, tn), lambda l, j: (l, 0, j)),
            pl.BlockSpec((1, 1, tn), lambda l, j: (l, 0, j)),
        ],
        out_specs=pl.BlockSpec((1, bc, tn), lambda l, j: (l, 0, j)),
        out_shape=jax.ShapeDtypeStruct((depth, bc, e), f32),
        compiler_params=_cparams(("parallel", "parallel")),
        name="ada_mods",
    )(c_all, ada_w, ada_b.reshape(depth, 1, e))


N_MODS = 6


def _mod_spec(mod, tm):
    mods, layer, chunk = mod
    d = mods.shape[-1] // N_MODS
    if mods.shape[2] == 1:
        return pl.BlockSpec((1, 1, 1, d), lambda g, i: (layer, g, 0, chunk))
    return pl.BlockSpec((1, 1, tm, d), lambda g, i: (layer, g, i, chunk))


def _tok_spec(tm, d):
    return pl.BlockSpec((1, tm, d), lambda g, i: (g, i, 0))


def _full_spec(shape):
    nd = len(shape)
    return pl.BlockSpec(shape, lambda g, i: (0,) * nd)


def _route(h2, rw_ref, rwh_ref, rb_ref):
    hi = h2.astype(bf16)
    lo = (h2 - hi.astype(f32)).astype(bf16)
    p = _dot(hi, rw_ref[...]) + _dot(lo, rwh_ref[...])
    pt = p.T
    logits = pt[0:N_EXPERTS] + pt[N_EXPERTS:2 * N_EXPERTS]
    sc = _sigmoid(logits)
    sel = sc + rb_ref[...]
    a = [sel[PER_GROUP * j:PER_GROUP * (j + 1)] for j in range(PER_GROUP)]
    s = [sc[PER_GROUP * j:PER_GROUP * (j + 1)] for j in range(PER_GROUP)]
    hi01, lo01 = jnp.maximum(a[0], a[1]), jnp.minimum(a[0], a[1])
    hi23, lo23 = jnp.maximum(a[2], a[3]), jnp.minimum(a[2], a[3])
    gs = jnp.maximum(hi01, hi23) + jnp.maximum(jnp.minimum(hi01, hi23), jnp.maximum(lo01, lo23))
    best = gs[0:1]
    grp = jnp.zeros_like(best)
    for g in range(1, N_GROUPS):
        better = gs[g:g + 1] > best
        grp = jnp.where(better, float(g), grp)
        best = jnp.where(better, gs[g:g + 1], best)
    mv, sv = [], []
    for j in range(PER_GROUP):
        m_j = a[j][0:1]
        s_j = s[j][0:1]
        for g in range(1, N_GROUPS):
            m_j = jnp.where(grp == float(g), a[j][g:g + 1], m_j)
            s_j = jnp.where(grp == float(g), s[j][g:g + 1], s_j)
        mv.append(m_j)
        sv.append(s_j)

    def first_argmax(vals):
        bv, bi = vals[0], jnp.zeros_like(vals[0])
        for j in range(1, PER_GROUP):
            better = vals[j] > bv
            bi = jnp.where(better, float(j), bi)
            bv = jnp.where(better, vals[j], bv)
        return bi

    i1 = first_argmax(mv)
    i2 = first_argmax([jnp.where(i1 == float(j), NEG_INF, mv[j]) for j in range(PER_GROUP)])
    w1, w2 = sv[0], sv[0]
    for j in range(1, PER_GROUP):
        w1 = jnp.where(i1 == float(j), sv[j], w1)
        w2 = jnp.where(i2 == float(j), sv[j], w2)
    tot = w1 + w2
    e1 = grp * float(PER_GROUP) + i1
    e2 = grp * float(PER_GROUP) + i2
    rows = jnp.concatenate([e1, e2, w1 / tot, w2 / tot, jnp.zeros((LANES - 4, e1.shape[1]), f32)], axis=0)
    return rows.T


N_POST_IN = 9


POST_ROW_SPLIT = 2


def _row_chunks(tm, split=1):
    nr = tm // split
    return [(r * nr, nr) for r in range(split)]


def _mod_rows(ref, r0, nr):
    if ref.shape[2] == 1:
        return ref[0, 0]
    return ref[0, 0, pl.ds(r0, nr), :]


def _residual_ln_route(y, post_refs, x1_ref, h2_ref, route_ref, r0, nr):
    x_ref, gate_ref, lng_ref, lnb_ref, sc_ref, sh_ref, rw_ref, rwh_ref, rb_ref = post_refs
    rows = pl.ds(r0, nr)
    r = ALPHA * x_ref[0, rows, :] + _mod_rows(gate_ref, r0, nr) * y
    mu = jnp.mean(r, axis=-1, keepdims=True)
    cen = r - mu
    var = jnp.mean(cen * cen, axis=-1, keepdims=True)
    x1 = cen * lax.rsqrt(var + EPS) * lng_ref[...] + lnb_ref[...]
    x1_ref[0, rows, :] = x1
    h2 = x1 * (1.0 + _mod_rows(sc_ref, r0, nr)) + _mod_rows(sh_ref, r0, nr)
    h2_ref[rows, :] = h2
    route_ref[0, rows, :] = _route(h2, rw_ref, rwh_ref, rb_ref)


def _post_call(kernel_fn, name, mixer_args, mixer_specs, x, gate, ln_g, ln_b, scale2, shift2, rw, rwh, rb, tm,
               h2_rows, h2_row0, h2_prev):
    g, t, d = x.shape
    steps = t // tm
    blk0 = h2_row0 // tm
    in_specs = list(mixer_specs) + [
        _tok_spec(tm, d), _mod_spec(gate, tm), _full_spec((1, d)), _full_spec((1, d)),
        _mod_spec(scale2, tm), _mod_spec(shift2, tm),
        _full_spec((d, LANES)), _full_spec((d, LANES)), _full_spec((N_EXPERTS, 1)),
    ]
    args = list(mixer_args) + [x, gate[0], ln_g, ln_b, scale2[0], shift2[0], rw, rwh, rb]
    aliases = {}
    if h2_prev is not None:
        aliases = {len(args): 1}
        in_specs.append(pl.BlockSpec(memory_space=pl.ANY))
        args.append(h2_prev)
    return pl.pallas_call(
        kernel_fn,
        grid=(g, steps),
        in_specs=in_specs,
        out_specs=[_tok_spec(tm, d),
                   pl.BlockSpec((tm, d), lambda gi, i: (blk0 + gi * steps + i, 0)),
                   _tok_spec(tm, LANES)],
        out_shape=[jax.ShapeDtypeStruct((g, t, d), f32),
                   jax.ShapeDtypeStruct((h2_rows, d), f32),
                   jax.ShapeDtypeStruct((g, t, LANES), f32)],
        input_output_aliases=aliases,
        compiler_params=_cparams(("parallel", "parallel")),
        name=name,
    )(*args)


def _mlstm_in_kernel(x_ref, sc_ref, sh_ref, w_ref, bg_ref, qkv_ref, o_ref, g_ref):
    hqk2 = 2 * HEADS * A_DQK
    hv = HEADS * A_DV
    for r0, nr in _row_chunks(x_ref.shape[1]):
        rows = pl.ds(r0, nr)
        h = x_ref[0, rows, :] * (1.0 + _mod_rows(sc_ref, r0, nr)) + _mod_rows(sh_ref, r0, nr)
        z = _dot(h.astype(bf16), w_ref[...])
        qkv_ref[0, rows, :] = z[:, :hqk2 + hv].astype(bf16)
        o_ref[0, rows, :] = z[:, hqk2 + hv:hqk2 + 2 * hv]
        g_ref[0, rows, :] = z[:, hqk2 + 2 * hv:] + bg_ref[...]


def _mlstm_in(x, scale, shift, w_pad, bg_pad, tm):
    g, t, d = x.shape
    n_qkv = 2 * HEADS * A_DQK + HEADS * A_DV
    hv = HEADS * A_DV
    return pl.pallas_call(
        _mlstm_in_kernel,
        grid=(g, t // tm),
        in_specs=[_tok_spec(tm, d), _mod_spec(scale, tm), _mod_spec(shift, tm),
                  _full_spec(w_pad.shape), _full_spec((1, LANES))],
        out_specs=[_tok_spec(tm, n_qkv), _tok_spec(tm, hv), _tok_spec(tm, LANES)],
        out_shape=[jax.ShapeDtypeStruct((g, t, n_qkv), bf16),
                   jax.ShapeDtypeStruct((g, t, hv), f32),
                   jax.ShapeDtypeStruct((g, t, LANES), f32)],
        compiler_params=_cparams(("parallel", "parallel")),
        name="mlstm_in",
    )(x, scale[0], shift[0], w_pad, bg_pad)


def _mlstm_out_kernel(u_ref, w_ref, *refs):
    for r0, nr in _row_chunks(u_ref.shape[1], POST_ROW_SPLIT):
        y = _dot(u_ref[0, pl.ds(r0, nr), :], w_ref[...])
        _residual_ln_route(y, refs[:N_POST_IN], *refs[-3:], r0, nr)


def _mlstm_out(u, w_out, tm, **post):
    return _post_call(_mlstm_out_kernel, "mlstm_out", [u, w_out],
                      [_tok_spec(tm, u.shape[-1]), _full_spec(w_out.shape)], tm=tm, **post)


def _head_norm_gate(hh, ng, o):
    mu = jnp.mean(hh, axis=-1, keepdims=True)
    cen = hh - mu
    var = jnp.mean(cen * cen, axis=-1, keepdims=True)
    return (cen * lax.rsqrt(var + EPS) * ng) * _sigmoid(o)


def _row_sum_replicated(x):
    ones = jnp.ones((LANES, LANES), bf16)
    hi = x.astype(bf16)
    lo = (x - hi.astype(f32)).astype(bf16)
    return _dot(hi, ones) + _dot(lo, ones)


def _mlstm_prompt_kernel(qkv_ref, kt_ref, o_ref, gcol_ref, grow_ref, ng_ref,
                         u_ref, c_out, n_out, m_out, c_s, n_s, m_s):
    c = pl.program_id(1)
    nc = pl.num_programs(1)
    L = qkv_ref.shape[1]
    hqk = HEADS * A_DQK

    @pl.when(c == 0)
    def _():
        c_s[...] = jnp.zeros_like(c_s)
        n_s[...] = jnp.zeros_like(n_s)
        m_s[...] = jnp.zeros_like(m_s)

    r_i = lax.broadcasted_iota(jnp.int32, (L, L), 0)
    c_i = lax.broadcasted_iota(jnp.int32, (L, L), 1)
    causal = c_i <= r_i
    tri = causal.astype(f32)
    tri_t = (r_i <= c_i).astype(f32)

    G = qkv_ref.shape[0]
    gs = range(G)
    gcol = [gcol_ref[g] for g in gs]
    grow = [grow_ref[g] for g in gs]
    b_col = [_dot_exact(tri, _log_sigmoid(gcol[g])) for g in gs]
    b_row = [_dot_exact(_log_sigmoid(grow[g][HEADS:2 * HEADS]), tri_t) for g in gs]
    li_row = [grow[g][0:HEADS] for g in gs]

    qkv = [qkv_ref[g] for g in gs]
    kt = [kt_ref[g] for g in gs]
    o_all = [o_ref[g] for g in gs]
    ng = ng_ref[...]
    c_old, n_old, m_old = c_s[...], n_s[...], m_s[...]
    hs = range(G * HEADS)
    sq = [k // HEADS for k in hs]
    hd = [k % HEADS for k in hs]
    q = [qkv[sq[h]][:, A_DQK * hd[h]:A_DQK * (hd[h] + 1)] for h in hs]
    v = [qkv[sq[h]][:, 2 * hqk + A_DV * hd[h]:2 * hqk + A_DV * (hd[h] + 1)] for h in hs]
    kth = [kt[sq[h]][A_DQK * hd[h]:A_DQK * (hd[h] + 1), :] for h in hs]
    s_qk = [_dot(q[h], kth[h]) for h in hs]
    r_e = lax.broadcasted_iota(jnp.int32, (LANES, HEADS * LANES), 0)
    c_e = lax.broadcasted_iota(jnp.int32, (LANES, HEADS * LANES), 1)
    sel = (r_e == HEADS + (c_e >> 7)).astype(bf16)
    b_rep = []
    for g in gs:
        b_hi = b_col[g].astype(bf16)
        b_r1 = b_col[g] - b_hi.astype(f32)
        b_mid = b_r1.astype(bf16)
        b_lo = (b_r1 - b_mid.astype(f32)).astype(bf16)
        b_rep.append(_dot(b_hi, sel) + _dot(b_mid, sel) + _dot(b_lo, sel))
    bc = [b_rep[sq[h]][:, LANES * hd[h]:LANES * (hd[h] + 1)] for h in hs]
    br = [b_row[sq[h]][hd[h]:hd[h] + 1] for h in hs]
    lir = [li_row[sq[h]][hd[h]:hd[h] + 1] for h in hs]
    m_prev = [m_old[h:h + 1, 0:1] for h in hs]
    d = [jnp.where(causal, bc[h] - br[h] + lir[h], NEG_INF) for h in hs]
    a = [bc[h] + m_prev[h] for h in hs]
    d_max = [jnp.max(d[h], axis=-1, keepdims=True) for h in hs]
    m_t = [jnp.maximum(a[h], d_max[h]) for h in hs]
    p = [s_qk[h] * jnp.exp(d[h] - m_t[h]) for h in hs]
    inter = [jnp.exp(a[h] - m_t[h]) for h in hs]
    qc = [_dot(q[h], jnp.concatenate([c_old[h], n_old[h]], axis=1).astype(bf16)) for h in hs]
    p_sum = [_row_sum_replicated(p[h]) for h in hs]
    num = [_dot(p[h].astype(bf16), v[h]) + inter[h] * qc[h][:, :A_DV] for h in hs]
    den = [p_sum[h] + inter[h] * qc[h][:, A_DV:] for h in hs]
    hh = [num[h] / jnp.maximum(jnp.abs(den[h]), jnp.exp(-m_t[h])) for h in hs]
    mu = [_row_sum_replicated(hh[h]) * (1.0 / A_DV) for h in hs]
    cen = [hh[h] - mu[h] for h in hs]
    var = [_row_sum_replicated(cen[h] * cen[h]) * (1.0 / A_DV) for h in hs]
    u_parts = [((cen[h] * lax.rsqrt(var[h] + EPS) * ng[:, A_DV * hd[h]:A_DV * (hd[h] + 1)])
                * _sigmoid(o_all[sq[h]][:, A_DV * hd[h]:A_DV * (hd[h] + 1)])).astype(bf16) for h in hs]

    b_end = [br[h][:, L - 1:L] for h in hs]
    g_row = [b_end[h] - br[h] + lir[h] for h in hs]
    g_max = [jnp.max(g_row[h], axis=-1, keepdims=True) for h in hs]
    m_new = [jnp.maximum(b_end[h] + m_prev[h], g_max[h]) for h in hs]
    decay = [jnp.exp(b_end[h] + m_prev[h] - m_new[h]) for h in hs]
    kw = [kth[h].astype(f32) * jnp.exp(g_row[h] - m_new[h]) for h in hs]
    kw_sum = [_row_sum_replicated(kw[h]) for h in hs]
    c_parts = [decay[h] * c_old[h] + _dot(kw[h].astype(bf16), v[h]) for h in hs]
    n_parts = [decay[h] * n_old[h] + kw_sum[h] for h in hs]
    m_parts = [jnp.broadcast_to(m_new[h], (1, LANES)) for h in hs]

    for g in gs:
        u_ref[g] = jnp.concatenate(u_parts[g * HEADS:(g + 1) * HEADS], axis=1)
    c_s[...] = jnp.stack(c_parts)
    n_s[...] = jnp.stack(n_parts)
    m_s[...] = jnp.concatenate(m_parts, axis=0)

    @pl.when(c == nc - 1)
    def _():
        for g in gs:
            c_out[g] = c_s[g * HEADS:(g + 1) * HEADS]
            n_out[g] = n_s[g * HEADS:(g + 1) * HEADS]
            m_out[g] = m_s[g * HEADS:(g + 1) * HEADS]


def _mlstm_prompt(qkv, kt, o, gcol, grow, ng):
    b, t, _ = qkv.shape
    L = MLSTM_CHUNK
    assert L == LANES and t % L == 0
    hv = HEADS * A_DV
    G = MLSTM_SEQS
    assert b % G == 0
    return pl.pallas_call(
        _mlstm_prompt_kernel,
        grid=(b // G, t // L),
        in_specs=[
            pl.BlockSpec((G, L, qkv.shape[-1]), lambda g, c: (g, c, 0)),
            pl.BlockSpec((G, HEADS * A_DQK, L), lambda g, c: (g, 0, c)),
            pl.BlockSpec((G, L, hv), lambda g, c: (g, c, 0)),
            pl.BlockSpec((G, L, LANES), lambda g, c: (g, c, 0)),
            pl.BlockSpec((G, 2 * HEADS, L), lambda g, c: (g, 0, c)),
            pl.BlockSpec((1, hv), lambda g, c: (0, 0)),
        ],
        out_specs=[
            pl.BlockSpec((G, L, hv), lambda g, c: (g, c, 0)),
            pl.BlockSpec((G, HEADS, A_DQK, A_DV), lambda g, c: (g, 0, 0, 0)),
            pl.BlockSpec((G, HEADS, A_DQK, LANES), lambda g, c: (g, 0, 0, 0)),
            pl.BlockSpec((G, HEADS, LANES), lambda g, c: (g, 0, 0)),
        ],
        out_shape=[
            jax.ShapeDtypeStruct((b, t, hv), bf16),
            jax.ShapeDtypeStruct((b, HEADS, A_DQK, A_DV), f32),
            jax.ShapeDtypeStruct((b, HEADS, A_DQK, LANES), f32),
            jax.ShapeDtypeStruct((b, HEADS, LANES), f32),
        ],
        scratch_shapes=[
            pltpu.VMEM((G * HEADS, A_DQK, A_DV), f32),
            pltpu.VMEM((G * HEADS, A_DQK, LANES), f32),
            pltpu.VMEM((G * HEADS, LANES), f32),
        ],
        compiler_params=_cparams(("parallel", "arbitrary")),
        name="mlstm_prompt",
    )(qkv, kt, o, gcol, grow, ng)


def _mlstm_sample_kernel(qkv_ref, kt_ref, o_ref, gcol_ref, grow_ref, ng_ref, c0_ref, n0_ref, nrows_ref,
                         m0_ref, mcol_ref, mrow_ref, *refs, ts):
    u_ref, c_out, n_out, m_out = refs[-4:]
    R = qkv_ref.shape[1]
    G = R // ts
    shift = ts.bit_length() - 1
    hqk = HEADS * A_DQK

    r_i = lax.broadcasted_iota(jnp.int32, (R, R), 0)
    c_i = lax.broadcasted_iota(jnp.int32, (R, R), 1)
    same = (r_i >> shift) == (c_i >> shift)
    valid = same & (c_i <= r_i)
    tri = valid.astype(f32)
    tri_t = (same & (r_i <= c_i)).astype(f32)
    same_f = same.astype(f32)
    row_seq = lax.broadcasted_iota(jnp.int32, (R, 1), 0) >> shift
    g_i = lax.broadcasted_iota(jnp.int32, (G, R), 0)
    s_i = lax.broadcasted_iota(jnp.int32, (G, R), 1)
    bmask = (s_i >> shift) == g_i
    lastmask = s_i == g_i * ts + (ts - 1)
    g3 = lax.broadcasted_iota(jnp.int32, (G, A_DQK, R), 0)
    s3 = lax.broadcasted_iota(jnp.int32, (G, A_DQK, R), 2)
    bmask3 = (s3 >> shift) == g3
    lane_i = lax.broadcasted_iota(jnp.int32, (G, LANES), 1)

    gcol = gcol_ref[0]
    grow = grow_ref[0]
    lf_row = _log_sigmoid(grow[HEADS:2 * HEADS])
    b_col = _dot_exact(tri, _log_sigmoid(gcol))
    b_row = _dot_exact(lf_row, tri_t)
    bend_row = _dot_exact(lf_row, same_f)
    li_row = grow[0:HEADS]
    mcol = mcol_ref[0]
    mrow = mrow_ref[0]
    m0 = m0_ref[...]

    m_acc = jnp.zeros((G, LANES), f32)
    n_parts = []
    for h in range(HEADS):
        q_h = qkv_ref[0, :, A_DQK * h:A_DQK * (h + 1)]
        k_h = qkv_ref[0, :, hqk + A_DQK * h:hqk + A_DQK * (h + 1)]
        v_h = qkv_ref[0, :, 2 * hqk + A_DV * h:2 * hqk + A_DV * (h + 1)]
        kt_h = kt_ref[0, A_DQK * h:A_DQK * (h + 1), :]
        s_qk = _dot(q_h, kt_h)
        bc = b_col[:, HEADS + h:HEADS + h + 1]
        br = b_row[h:h + 1]
        lir = li_row[h:h + 1]
        d = jnp.where(valid, bc - br + lir, NEG_INF)
        a = bc + mcol[:, h:h + 1]
        m_t = jnp.maximum(a, jnp.max(d, axis=-1, keepdims=True))
        p = s_qk * jnp.exp(d - m_t)
        inter = jnp.exp(a - m_t)
        c_all = jnp.concatenate([c0_ref[0, g, h] for g in range(G)], axis=1).astype(bf16)
        qc = _dot(q_h, c_all)
        inter_c = jnp.zeros((R, A_DV), f32)
        for g in range(G):
            inter_c = jnp.where(row_seq == g, qc[:, A_DV * g:A_DV * (g + 1)], inter_c)
        qn = jnp.sum(q_h.astype(f32) * nrows_ref[0, :, A_DQK * h:A_DQK * (h + 1)], axis=-1, keepdims=True)
        num = _dot(p.astype(bf16), v_h) + inter * inter_c
        den = jnp.sum(p, axis=-1, keepdims=True) + inter * qn
        hh = num / jnp.maximum(jnp.abs(den), jnp.exp(-m_t))
        u = _head_norm_gate(hh, ng_ref[:, A_DV * h:A_DV * (h + 1)], o_ref[0, :, A_DV * h:A_DV * (h + 1)])
        u_ref[0, :, A_DV * h:A_DV * (h + 1)] = u.astype(bf16)

        g_row = bend_row[h:h + 1] - br + lir
        gmax = jnp.max(jnp.where(bmask, g_row, NEG_INF), axis=-1, keepdims=True)
        bend_b = jnp.sum(jnp.where(lastmask, br, 0.0), axis=-1, keepdims=True)
        m0_h = m0[:, h:h + 1]
        mnew_b = jnp.maximum(bend_b + m0_h, gmax)
        mnew_row = jnp.sum(jnp.where(bmask, mnew_b, 0.0), axis=0, keepdims=True)
        w_row = jnp.exp(g_row - mnew_row)
        decay_b = jnp.exp(bend_b + m0_h - mnew_b)
        kw = kt_h.astype(f32) * w_row
        kw3 = jnp.where(bmask3, jnp.broadcast_to(kw[None], (G, A_DQK, R)), 0.0)
        upd = _dot(kw3.reshape(G * A_DQK, R).astype(bf16), v_h)
        for g in range(G):
            c_out[0, g, h] = decay_b[g:g + 1, :] * c0_ref[0, g, h] + upd[A_DQK * g:A_DQK * (g + 1)]
        wm = jnp.where(bmask, w_row, 0.0)
        n_parts.append(decay_b * n0_ref[:, A_DQK * h:A_DQK * (h + 1)] + _dot(wm.astype(bf16), k_h))
        m_acc = jnp.where(lane_i == h, mnew_b, m_acc)

    n_out[...] = jnp.concatenate(n_parts, axis=1)
    m_out[...] = m_acc[:, :HEADS]


def _mlstm_sample(qkv, kt, o, gcol, grow, ng, c0_all, layer, c_prev, n0, nrows, m0, mcol, mrow, ts):
    _, t, _ = qkv.shape
    n_layers, bs = c0_all.shape[:2]
    G = SAMPLE_GROUP
    R = G * ts
    hv = HEADS * A_DV
    hqk = HEADS * A_DQK
    args = [qkv, kt, o, gcol, grow, ng, c0_all, n0, nrows, m0, mcol, mrow]
    extra_specs, aliases = [], {}
    if c_prev is not None:
        aliases = {len(args): 1}
        extra_specs = [pl.BlockSpec(memory_space=pl.ANY)]
        args.append(c_prev)
    return pl.pallas_call(
        functools.partial(_mlstm_sample_kernel, ts=ts),
        grid=(t // R,),
        input_output_aliases=aliases,
        in_specs=[
            pl.BlockSpec((1, R, qkv.shape[-1]), lambda i: (0, i, 0)),
            pl.BlockSpec((1, hqk, R), lambda i: (0, 0, i)),
            pl.BlockSpec((1, R, hv), lambda i: (0, i, 0)),
            pl.BlockSpec((1, R, LANES), lambda i: (0, i, 0)),
            pl.BlockSpec((1, 2 * HEADS, R), lambda i: (0, 0, i)),
            pl.BlockSpec((1, hv), lambda i: (0, 0)),
            pl.BlockSpec((1, G, HEADS, A_DQK, A_DV), lambda i: (layer, i, 0, 0, 0)),
            pl.BlockSpec((G, hqk), lambda i: (i, 0)),
            pl.BlockSpec((1, R, hqk), lambda i: (0, i, 0)),
            pl.BlockSpec((G, HEADS), lambda i: (i, 0)),
            pl.BlockSpec((1, R, LANES), lambda i: (0, i, 0)),
            pl.BlockSpec((1, HEADS, R), lambda i: (0, 0, i)),
        ] + extra_specs,
        out_specs=[
            pl.BlockSpec((1, R, hv), lambda i: (0, i, 0)),
            pl.BlockSpec((1, G, HEADS, A_DQK, A_DV), lambda i: (layer, i, 0, 0, 0)),
            pl.BlockSpec((G, hqk), lambda i: (i, 0)),
            pl.BlockSpec((G, HEADS), lambda i: (i, 0)),
        ],
        out_shape=[
            jax.ShapeDtypeStruct((1, t, hv), bf16),
            jax.ShapeDtypeStruct((n_layers, bs, HEADS, A_DQK, A_DV), f32),
            jax.ShapeDtypeStruct((bs, hqk), f32),
            jax.ShapeDtypeStruct((bs, HEADS), f32),
        ],
        compiler_params=_cparams(("parallel",)),
        name="mlstm_sample",
    )(*args)


def _expert_kernel(be_ref, nu_ref, first_ref, *refs, n_parts):
    x_refs = refs[:n_parts]
    wg_ref, wu_ref, wd_ref, y_ref, wg_s, wu_s, wd_s = refs[n_parts:]
    i = pl.program_id(0)
    part_blocks = pl.num_programs(0) // n_parts

    @pl.when(first_ref[i] == 1)
    def _():
        wg_s[...] = wg_ref[0, 0].astype(bf16)
        wu_s[...] = wu_ref[0, 0].astype(bf16)
        wd_s[...] = wd_ref[0, 0].astype(bf16)

    @pl.when(i < nu_ref[0])
    def _():
        x = x_refs[n_parts - 1][...]
        for part in range(n_parts - 2, -1, -1):
            x = jnp.where(i < (part + 1) * part_blocks, x_refs[part][...], x)
        x = x.astype(bf16)
        g = _dot(x, wg_s[...])
        u = _dot(x, wu_s[...])
        hid = (g * _sigmoid(g)) * u
        y_ref[...] = _dot(hid.astype(bf16), wd_s[...])

    @pl.when(i >= nu_ref[0])
    def _():
        y_ref[...] = jnp.zeros_like(y_ref)


def _experts(x_parts, block_e, n_used, first, wg, wu, wd, layer):
    n_parts = len(x_parts)
    pp, d = x_parts[0].shape
    de = wg.shape[-1]
    bm = MOE_BLOCK
    pb = pp // bm
    w_map = lambda i, be, nu, fi: (layer, be[i], 0, 0)
    part_specs = [pl.BlockSpec((bm, d), functools.partial(
        lambda i, be, nu, fi, part: (jnp.clip(i - part * pb, 0, pb - 1), 0), part=part)) for part in range(n_parts)]
    return pl.pallas_call(
        functools.partial(_expert_kernel, n_parts=n_parts),
        grid_spec=pltpu.PrefetchScalarGridSpec(
            num_scalar_prefetch=3,
            grid=(n_parts * pb,),
            in_specs=part_specs + [
                pl.BlockSpec((1, 1, d, de), w_map),
                pl.BlockSpec((1, 1, d, de), w_map),
                pl.BlockSpec((1, 1, de, d), w_map),
            ],
            out_specs=pl.BlockSpec((bm, d), lambda i, be, nu, fi: (i, 0)),
            scratch_shapes=[pltpu.VMEM((d, de), bf16), pltpu.VMEM((d, de), bf16), pltpu.VMEM((de, d), bf16)],
        ),
        out_shape=jax.ShapeDtypeStruct((n_parts * pp, d), f32),
        compiler_params=_cparams(("arbitrary",)),
        name="moe_experts",
    )(block_e, n_used, first, *x_parts, wg, wu, wd)


def _combine_kernel(x_ref, ya_ref, yb_ref, w_ref, gate_ref, lng_ref, lnb_ref, o_ref):
    w = w_ref[0]
    y = w[:, 2:3] * ya_ref[...] + w[:, 3:4] * yb_ref[...]
    r = ALPHA * x_ref[0] + gate_ref[0, 0] * y
    mu = jnp.mean(r, axis=-1, keepdims=True)
    cen = r - mu
    var = jnp.mean(cen * cen, axis=-1, keepdims=True)
    o_ref[0] = cen * lax.rsqrt(var + EPS) * lng_ref[...] + lnb_ref[...]


def _combine(x1, ya, yb, row0, wexp, gate, ln_g, ln_b, tm):
    g, t, d = x1.shape
    steps = t // tm
    blk0 = row0 // tm
    y_spec = pl.BlockSpec((tm, d), lambda gi, i: (blk0 + gi * steps + i, 0))
    return pl.pallas_call(
        _combine_kernel,
        grid=(g, steps),
        in_specs=[_tok_spec(tm, d), y_spec, y_spec,
                  _tok_spec(tm, LANES), _mod_spec(gate, tm), _full_spec((1, d)), _full_spec((1, d))],
        out_specs=_tok_spec(tm, d),
        out_shape=jax.ShapeDtypeStruct((g, t, d), f32),
        compiler_params=_cparams(("parallel", "parallel")),
        name="moe_combine",
    )(x1, ya, yb, wexp, gate[0], ln_g, ln_b)


def _moe(groups, h2_all, ln_g, ln_b, wg, wu, wd, layer, tm):
    bm = MOE_BLOCK
    d = groups[0][0].shape[-1]
    sizes = [x1.shape[0] * x1.shape[1] for x1, _, _ in groups]
    n = sum(sizes)
    nk = 2 * n
    expert = jnp.concatenate([r[:, :, 0:2].reshape(-1, 2) for _, r, _ in groups], axis=0).astype(jnp.int32)
    e_flat = expert.reshape(nk)
    onehot = (e_flat[:, None] == jnp.arange(N_EXPERTS, dtype=jnp.int32)[None, :]).astype(jnp.int32)
    csum = jnp.cumsum(onehot, axis=0)
    counts = csum[-1]
    rank = jnp.sum(csum * onehot, axis=1) - 1
    padded = (counts + bm - 1) // bm * bm
    pends = jnp.cumsum(padded)
    pstarts = pends - padded
    dest = jnp.sum(onehot * pstarts[None, :], axis=1) + rank
    n_blocks = nk // bm + N_EXPERTS
    n_used = (pends[-1] // bm).astype(jnp.int32)
    blk = jnp.minimum(jnp.arange(n_blocks, dtype=jnp.int32), n_used - 1)
    block_e = jnp.minimum(jnp.sum((pends[None, :] <= (blk * bm)[:, None]).astype(jnp.int32), axis=1),
                          N_EXPERTS - 1).astype(jnp.int32)
    first = jnp.concatenate([jnp.ones((1,), jnp.int32), (block_e[1:] != block_e[:-1]).astype(jnp.int32)])
    tok = jnp.arange(nk, dtype=jnp.int32) // 2
    tok_pad = jnp.zeros((n_blocks * bm,), jnp.int32).at[dest].set(tok, unique_indices=True, mode="promise_in_bounds")
    part_rows = n_blocks // GATHER_PARTS * bm
    x_parts = [jnp.take(h2_all, tok_pad[c * part_rows:(c + 1) * part_rows], axis=0, mode="clip")
               for c in range(GATHER_PARTS)]
    y = _experts(x_parts, block_e, n_used.reshape(1), first, wg, wu, wd, layer)
    dest2 = dest.reshape(n, 2)
    ya = jnp.take(y, dest2[:, 0], axis=0, mode="clip")
    yb = jnp.take(y, dest2[:, 1], axis=0, mode="clip")
    outs, off = [], 0
    for (x1, route, gate2), sz in zip(groups, sizes):
        outs.append(_combine(x1, ya, yb, off, route, gate2, ln_g, ln_b, tm))
        off += sz
    return outs


def _kv_kernel(x_ref, w_ref, g_ref, cos_ref, sin_ref, lat_ref, kr_ref, latb_ref, krb_ref):
    kva = _dot(x_ref[0].astype(bf16), w_ref[...])
    latp = kva[:, :B_KV_RANK]
    lat = latp * lax.rsqrt(jnp.mean(latp * latp, axis=-1, keepdims=True) + EPS) * g_ref[...]
    kr = kva[:, B_KV_RANK:B_KV_RANK + B_D_ROPE] * cos_ref[0] \
        + kva[:, B_KV_RANK + LANES:B_KV_RANK + LANES + B_D_ROPE] * sin_ref[0]
    lat_ref[0] = lat
    kr_ref[0] = kr
    latb_ref[0] = lat.astype(bf16)
    krb_ref[0] = kr.astype(bf16)


def _shared_kv(x, w_pad, g, cos, sin, tm):
    gg, t, d = x.shape
    rope_spec = pl.BlockSpec((1, tm, B_D_ROPE), lambda gi, i: (0, i, 0)) if cos.shape[0] == 1 else \
        _tok_spec(tm, B_D_ROPE)
    return pl.pallas_call(
        _kv_kernel,
        grid=(gg, t // tm),
        in_specs=[_tok_spec(tm, d), _full_spec(w_pad.shape), _full_spec((1, B_KV_RANK)), rope_spec, rope_spec],
        out_specs=[_tok_spec(tm, B_KV_RANK), _tok_spec(tm, B_D_ROPE),
                   _tok_spec(tm, B_KV_RANK), _tok_spec(tm, B_D_ROPE)],
        out_shape=[jax.ShapeDtypeStruct((gg, t, B_KV_RANK), f32), jax.ShapeDtypeStruct((gg, t, B_D_ROPE), f32),
                   jax.ShapeDtypeStruct((gg, t, B_KV_RANK), bf16), jax.ShapeDtypeStruct((gg, t, B_D_ROPE), bf16)],
        compiler_params=_cparams(("parallel", "parallel")),
        name="shared_kv",
    )(x, w_pad, g, cos, sin)


def _mla_q_kernel(x_ref, sc_ref, sh_ref, wdq_ref, qg_ref, wuq_ref, wuk_ref, cos_ref, sin_ref, ql_ref, qr_ref):
    n_nope = HEADS * B_D_NOPE
    n_rope = HEADS * B_D_ROPE
    for r0, nr in _row_chunks(x_ref.shape[1]):
        rows = pl.ds(r0, nr)
        h = x_ref[0, rows, :] * (1.0 + _mod_rows(sc_ref, r0, nr)) + _mod_rows(sh_ref, r0, nr)
        cq = _dot(h.astype(bf16), wdq_ref[...])
        cq = cq * lax.rsqrt(jnp.mean(cq * cq, axis=-1, keepdims=True) + EPS) * qg_ref[...]
        q = _dot(cq.astype(bf16), wuq_ref[...])
        rot = q[:, n_nope:n_nope + n_rope] * cos_ref[0, rows, :] + q[:, n_nope + n_rope:] * sin_ref[0, rows, :]
        for hd in range(HEADS):
            qn = q[:, B_D_NOPE * hd:B_D_NOPE * (hd + 1)].astype(bf16)
            ql_ref[0, hd, rows, :] = _dot(qn, wuk_ref[hd]).astype(bf16)
            qr_ref[0, hd, rows, :] = rot[:, B_D_ROPE * hd:B_D_ROPE * (hd + 1)].astype(bf16)


def _mla_q(x, scale, shift, wdq, qg, wuq, wuk, cos8, sin8, tm):
    g, t, d = x.shape
    n_rope = HEADS * B_D_ROPE
    rope_spec = pl.BlockSpec((1, tm, n_rope), lambda gi, i: (0, i, 0)) if cos8.shape[0] == 1 else \
        _tok_spec(tm, n_rope)
    return pl.pallas_call(
        _mla_q_kernel,
        grid=(g, t // tm),
        in_specs=[_tok_spec(tm, d), _mod_spec(scale, tm), _mod_spec(shift, tm),
                  _full_spec(wdq.shape), _full_spec(qg.shape), _full_spec(wuq.shape), _full_spec(wuk.shape),
                  rope_spec, rope_spec],
        out_specs=[pl.BlockSpec((1, HEADS, tm, B_KV_RANK), lambda gi, i: (gi, 0, i, 0)),
                   pl.BlockSpec((1, HEADS, tm, B_D_ROPE), lambda gi, i: (gi, 0, i, 0))],
        out_shape=[jax.ShapeDtypeStruct((g, HEADS, t, B_KV_RANK), bf16),
                   jax.ShapeDtypeStruct((g, HEADS, t, B_D_ROPE), bf16)],
        compiler_params=_cparams(("parallel", "parallel")),
        name="mla_q",
    )(x, scale[0], shift[0], wdq, qg, wuq, wuk, cos8, sin8)


def _softmax_step(s, v, m_s, l_s, acc_s, slot):
    rows, n = s.shape
    m_old = m_s[slot]
    if n % LANES == 0:
        _softmax_steps([s], [v], m_s, l_s, acc_s, [slot])
        return
    m_new = jnp.maximum(m_old, jnp.max(s, axis=-1, keepdims=True))
    p = jnp.exp2((s - m_new[:, 0:1]) * SOFTMAX_EXP2_SCALE)
    lane = lax.broadcasted_iota(jnp.int32, (rows, LANES), 1)
    psum = jnp.where(lane == 0, jnp.sum(p, axis=-1, keepdims=True), 0.0)
    alpha = jnp.exp2((m_old - m_new) * SOFTMAX_EXP2_SCALE)
    l_s[slot] = alpha * l_s[slot] + psum
    alpha_v = jnp.concatenate([alpha] * (v.shape[1] // LANES), axis=1)
    acc_s[slot] = alpha_v * acc_s[slot] + _dot(p.astype(bf16), v)
    m_s[slot] = m_new


def _softmax_steps(s_list, v_list, m_s, l_s, acc_s, slots):
    ids = range(len(s_list))
    nv = v_list[0].shape[1] // LANES
    m_old = [m_s[slot] for slot in slots]
    chunks = [[s[:, LANES * c:LANES * (c + 1)] for c in range(s.shape[1] // LANES)] for s in s_list]
    c_max = [functools.reduce(jnp.maximum, chunks[k]) for k in ids]
    r_max = [jnp.max(c_max[k], axis=-1, keepdims=True) for k in ids]
    m_new = [jnp.maximum(m_old[k], r_max[k]) for k in ids]
    ps = [[jnp.exp2((ch - m_new[k]) * SOFTMAX_EXP2_SCALE) for ch in chunks[k]] for k in ids]
    alpha = [jnp.exp2((m_old[k] - m_new[k]) * SOFTMAX_EXP2_SCALE) for k in ids]
    pv = [_dot(jnp.concatenate(ps[k], axis=1).astype(bf16), v_list[k]) for k in ids]
    for k, slot in enumerate(slots):
        l_s[slot] = alpha[k] * l_s[slot] + functools.reduce(jnp.add, ps[k])
        acc_s[slot] = jnp.concatenate([alpha[k]] * nv, axis=1) * acc_s[slot] + pv[k]
        m_s[slot] = m_new[k]


def _flash_kernel(ql_ref, qr_ref, lat_ref, kr_ref, o_ref, m_s, l_s, acc_s, *, tq):
    i = pl.program_id(1)
    nh = FLASH_HEADS
    r_i = lax.broadcasted_iota(jnp.int32, (tq, tq), 0)
    c_i = lax.broadcasted_iota(jnp.int32, (tq, tq), 1)
    causal = c_i <= r_i

    def group_body(hg, carry):
        m_s[...] = jnp.full_like(m_s, NEG_INF)
        l_s[...] = jnp.zeros_like(l_s)
        acc_s[...] = jnp.zeros_like(acc_s)

        def block(j, masked):
            start = pl.multiple_of(j * tq, tq)
            k_lat = lat_ref[0, pl.ds(start, tq), :]
            k_r = kr_ref[0, pl.ds(start, tq), :]
            s_list = []
            for hs in range(nh):
                h = hg * nh + hs
                s = _dot_nt(ql_ref[0, h], k_lat) + _dot_nt(qr_ref[0, h], k_r)
                s_list.append(jnp.where(causal, s, NEG_INF) if masked else s)
            _softmax_steps(s_list, [k_lat] * nh, m_s, l_s, acc_s, list(range(nh)))

        def kv_body(j, c2):
            block(j, False)
            return c2

        lax.fori_loop(0, i, kv_body, 0)
        block(i, True)
        for hs in range(nh):
            l = jnp.sum(l_s[hs], axis=-1, keepdims=True)
            o_ref[0, hg * nh + hs] = (acc_s[hs] / l).astype(bf16)
        return carry

    lax.fori_loop(0, HEADS // nh, group_body, 0)


def _flash(ql, qr, latb, krb, tq):
    b, _, t, _ = ql.shape
    nh = FLASH_HEADS
    return pl.pallas_call(
        functools.partial(_flash_kernel, tq=tq),
        grid=(b, t // tq),
        in_specs=[
            pl.BlockSpec((1, HEADS, tq, B_KV_RANK), lambda g, i: (g, 0, i, 0)),
            pl.BlockSpec((1, HEADS, tq, B_D_ROPE), lambda g, i: (g, 0, i, 0)),
            pl.BlockSpec((1, t, B_KV_RANK), lambda g, i: (g, 0, 0)),
            pl.BlockSpec((1, t, B_D_ROPE), lambda g, i: (g, 0, 0)),
        ],
        out_specs=pl.BlockSpec((1, HEADS, tq, B_KV_RANK), lambda g, i: (g, 0, i, 0)),
        out_shape=jax.ShapeDtypeStruct((b, HEADS, t, B_KV_RANK), bf16),
        scratch_shapes=[pltpu.VMEM((nh, tq, LANES), f32), pltpu.VMEM((nh, tq, LANES), f32),
                        pltpu.VMEM((nh, tq, B_KV_RANK), f32)],
        compiler_params=_cparams(("parallel", "arbitrary")),
        name="mla_flash",
    )(ql, qr, latb, krb)


def _paged_kernel(pt_ref, ql_ref, qr_ref, *rest, ts, new_pad):
    np_ = PAGES_PER_STEP
    nc = PAGED_CHAINS
    per = np_ // nc
    lat_pages = rest[:np_]
    krt_pages = rest[np_:2 * np_]
    nl_ref, nk_ref, o_ref, m_s, l_s, acc_s = rest[2 * np_:]
    s_id = pl.program_id(1)
    ql = ql_ref[0]
    qr = qr_ref[0]

    @pl.when(s_id == 0)
    def _():
        m_s[...] = jnp.full_like(m_s, NEG_INF)
        l_s[...] = jnp.zeros_like(l_s)
        acc_s[...] = jnp.zeros_like(acc_s)

    k_lats = [jnp.concatenate([p[0] for p in lat_pages[c * per:(c + 1) * per]], axis=0).astype(bf16)
              for c in range(nc)]
    k_rts = [jnp.concatenate([p[0] for p in krt_pages[c * per:(c + 1) * per]], axis=1).astype(bf16)
             for c in range(nc)]
    _softmax_steps([_dot_nt(ql, k_lats[c]) + _dot(qr, k_rts[c]) for c in range(nc)], k_lats,
                   m_s, l_s, acc_s, list(range(nc)))

    @pl.when(s_id == pl.num_programs(1) - 1)
    def _():
        rows = ql.shape[0]
        n_lat = nl_ref[0]
        s_new = _dot_nt(ql, n_lat) + _dot_nt(qr, nk_ref[0])
        t_row = lax.broadcasted_iota(jnp.int32, (rows, new_pad), 0) & (ts - 1)
        c_new = lax.broadcasted_iota(jnp.int32, (rows, new_pad), 1)
        _softmax_step(jnp.where(c_new <= t_row, s_new, NEG_INF), n_lat, m_s, l_s, acc_s, 0)
        m = functools.reduce(jnp.maximum, [m_s[c] for c in range(nc)])
        nv = B_KV_RANK // LANES
        l = jnp.zeros_like(m)
        acc = jnp.zeros_like(acc_s[0])
        for c in range(nc):
            a_c = jnp.exp2((m_s[c] - m) * SOFTMAX_EXP2_SCALE)
            l = l + a_c * l_s[c]
            acc = acc + jnp.concatenate([a_c] * nv, axis=1) * acc_s[c]
        o_ref[0] = (acc / jnp.sum(l, axis=-1, keepdims=True)).astype(bf16)


def _paged_attention(page_table, ql, qr, cache_lat, cache_krt, new_lat, new_kr, ts):
    bs, rows, _ = ql.shape
    n_pages = page_table.shape[1]
    np_ = PAGES_PER_STEP
    new_pad = new_lat.shape[1]
    page_map = [functools.partial(lambda b, s, pt, r: (pt[b, s * np_ + r], 0, 0), r=r) for r in range(np_)]
    lat_specs = [pl.BlockSpec((1, PAGE, B_KV_RANK), page_map[r]) for r in range(np_)]
    kr_specs = [pl.BlockSpec((1, B_D_ROPE, PAGE), page_map[r]) for r in range(np_)]
    return pl.pallas_call(
        functools.partial(_paged_kernel, ts=ts, new_pad=new_pad),
        grid_spec=pltpu.PrefetchScalarGridSpec(
            num_scalar_prefetch=1,
            grid=(bs, n_pages // np_),
            in_specs=[pl.BlockSpec((1, rows, B_KV_RANK), lambda b, s, pt: (b, 0, 0)),
                      pl.BlockSpec((1, rows, B_D_ROPE), lambda b, s, pt: (b, 0, 0))]
            + lat_specs + kr_specs
            + [pl.BlockSpec((1, new_pad, B_KV_RANK), lambda b, s, pt: (b, 0, 0)),
               pl.BlockSpec((1, new_pad, B_D_ROPE), lambda b, s, pt: (b, 0, 0))],
            out_specs=pl.BlockSpec((1, rows, B_KV_RANK), lambda b, s, pt: (b, 0, 0)),
            scratch_shapes=[pltpu.VMEM((PAGED_CHAINS, rows, LANES), f32), pltpu.VMEM((PAGED_CHAINS, rows, LANES), f32),
                            pltpu.VMEM((PAGED_CHAINS, rows, B_KV_RANK), f32)],
        ),
        out_shape=jax.ShapeDtypeStruct((bs, rows, B_KV_RANK), bf16),
        compiler_params=_cparams(("parallel", "arbitrary")),
        name="mla_paged",
    )(page_table, ql, qr, *([cache_lat] * np_), *([cache_krt] * np_), new_lat, new_kr)


def _mla_out_kernel(ol_ref, wuv_ref, wo_ref, *refs):
    for r0, nr in _row_chunks(ol_ref.shape[2], POST_ROW_SPLIT):
        rows = pl.ds(r0, nr)
        o = jnp.concatenate([_dot(ol_ref[0, hd, rows, :], wuv_ref[hd]) for hd in range(HEADS)], axis=1)
        y = _dot(o.astype(bf16), wo_ref[...])
        _residual_ln_route(y, refs[:N_POST_IN], *refs[-3:], r0, nr)


def _mla_out(ol, wuv, wo, tm, **post):
    return _post_call(_mla_out_kernel, "mla_out", [ol, wuv, wo],
                      [pl.BlockSpec((1, HEADS, tm, B_KV_RANK), lambda gi, i: (gi, 0, i, 0)),
                       _full_spec(wuv.shape), _full_spec(wo.shape)], tm=tm, **post)


def _rope_tables(pos):
    half = B_D_ROPE // 2
    inv = jnp.power(ROPE_THETA, -jnp.arange(half, dtype=f32) / half)
    ang = pos.astype(f32)[:, None] * inv[None, :]
    cos, sin = jnp.cos(ang), jnp.sin(ang)
    return jnp.concatenate([cos, cos], axis=-1), jnp.concatenate([-sin, sin], axis=-1)


def _swap_halves(w, width):
    lead = w.shape[:-1]
    w2 = w.reshape(lead + (-1, 2, width // 2))
    return w2[..., ::-1, :].reshape(w.shape)


def kernel(x_prompt, x_sample, cache_latent, cache_krope, page_table, state_C, state_n, state_m, c_prompt, c_sample, ada_w, ada_b, ln_g, ln_b, a_w_in, a_b_gates, a_norm_g, a_w_out, b_w_kv_a, b_kv_norm_g, b_w_uk, b_w_uv, b_w_dq, b_q_norm_g, b_w_uq, b_w_o, router_w, router_b, e_w_gate, e_w_up, e_w_down):
    bp, tp, d = x_prompt.shape
    bs, ts, _ = x_sample.shape
    hqk = HEADS * A_DQK
    hv = HEADS * A_DV

    w_in = a_w_in.at[:, :, hqk:2 * hqk].multiply(A_DQK ** -0.5)
    w_in = jnp.pad(w_in, ((0, 0), (0, 0), (0, LANES - 2 * HEADS))).astype(bf16)
    bg = jnp.pad(a_b_gates, ((0, 0), (0, LANES - 2 * HEADS))).reshape(N_A_LAYERS, 1, LANES)
    w_out = a_w_out.astype(bf16)
    ng = a_norm_g.reshape(N_A_LAYERS, 1, hv)
    perm = jnp.array([(r % N_GROUPS) * PER_GROUP + r // N_GROUPS for r in range(N_EXPERTS)], jnp.int32)
    rw_perm = router_w[:, perm]
    rw_hi = rw_perm.astype(bf16)
    rw_lo = (rw_perm - rw_hi.astype(f32)).astype(bf16)
    rw = jnp.pad(jnp.concatenate([rw_hi, rw_lo], axis=1), ((0, 0), (0, LANES - 2 * N_EXPERTS)))
    rwh = jnp.pad(rw_hi, ((0, 0), (0, LANES - N_EXPERTS)))
    rb = router_b[perm].reshape(N_EXPERTS, 1)
    w_lat = b_w_kv_a[:, :B_KV_RANK]
    w_kr = b_w_kv_a[:, B_KV_RANK:]
    zpad = jnp.zeros((d, LANES - B_D_ROPE), f32)
    w_kv = jnp.concatenate([w_lat, w_kr, zpad, _swap_halves(w_kr, B_D_ROPE), zpad], axis=1).astype(bf16)
    kvg = b_kv_norm_g.reshape(1, B_KV_RANK)
    wdq = b_w_dq.astype(bf16)
    qg = b_q_norm_g.reshape(-1, 1, b_q_norm_g.shape[-1])
    uq = b_w_uq.reshape(b_w_uq.shape[0], b_w_uq.shape[1], HEADS, B_D_NOPE + B_D_ROPE)
    uq_nope = uq[..., :B_D_NOPE].reshape(uq.shape[0], uq.shape[1], HEADS * B_D_NOPE)
    uq_rope = uq[..., B_D_NOPE:].reshape(uq.shape[0], uq.shape[1], HEADS * B_D_ROPE)
    wuq = jnp.concatenate([uq_nope, uq_rope, _swap_halves(uq_rope, B_D_ROPE)], axis=-1).astype(bf16)
    wuk = jnp.transpose(b_w_uk, (1, 2, 0)).astype(bf16)
    wuv = jnp.transpose(b_w_uv, (1, 0, 2)).astype(bf16)
    wo = b_w_o.astype(bf16)

    mods = _ada_mods(jnp.concatenate([c_prompt, c_sample], axis=0), ada_w, ada_b)

    cos_p, sin_p = _rope_tables(jnp.arange(tp, dtype=jnp.int32))
    past_len = page_table.shape[1] * PAGE
    cos_s, sin_s = _rope_tables(jnp.tile(past_len + jnp.arange(ts, dtype=jnp.int32), bs))

    tm = 512
    cache_krt = jnp.swapaxes(cache_krope, 1, 2)
    streams = []
    mods_p = mods[:, :bp].reshape(DEPTH, bp, 1, N_MODS * d)
    mods_s = jnp.repeat(mods[:, bp:], ts, axis=1).reshape(DEPTH, 1, bs * ts, N_MODS * d)
    for x0, mods_g, cos, sin, sample in ((x_prompt, mods_p, cos_p, sin_p, False),
                                         (x_sample.reshape(1, bs * ts, d), mods_s, cos_s, sin_s, True)):
        streams.append(dict(x=x0, mods=mods_g, sample=sample, cos1=cos[None], sin1=sin[None],
                            cos8=jnp.tile(cos, (1, HEADS))[None], sin8=jnp.tile(sin, (1, HEADS))[None],
                            new_c=[], new_n=[], new_m=[]))

    for l in range(DEPTH):
        lg = ln_g[l].reshape(2, 1, d)
        lb = ln_b[l].reshape(2, 1, d)
        moe_in = []
        h2_all, row0 = None, 0
        n_all = sum(st["x"].shape[0] * st["x"].shape[1] for st in streams)
        for st in streams:
            x, sample = st["x"], st["sample"]
            cos8, sin8 = st["cos8"], st["sin8"]
            latb, krb = st.get("latb"), st.get("krb")
            new_c, new_n, new_m = st["new_c"], st["new_n"], st["new_m"]
            shift1, scale1, gate1, shift2, scale2, gate2 = [(st["mods"], l, i) for i in range(N_MODS)]
            post = dict(x=x, gate=gate1, ln_g=lg[0], ln_b=lb[0], scale2=scale2, shift2=shift2, rw=rw, rwh=rwh, rb=rb,
                        h2_rows=n_all, h2_row0=row0, h2_prev=h2_all)
            if l < N_A_LAYERS:
                qkv, o, gates = _mlstm_in(x, scale1, shift1, w_in[l], bg[l], tm)
                kt = jnp.swapaxes(qkv[:, :, hqk:2 * hqk], 1, 2)
                grow = jnp.swapaxes(gates[:, :, :2 * HEADS], 1, 2)
                if sample:
                    n0 = state_n[l].reshape(bs, hqk)
                    m0 = state_m[l]
                    nrows = jnp.repeat(n0, ts, axis=0)[None]
                    m_tok = jnp.repeat(m0, ts, axis=0)
                    mcol = jnp.pad(m_tok, ((0, 0), (0, LANES - HEADS)))[None]
                    mrow = m_tok.T[None]
                    u, c_stack, n_new, m_new = _mlstm_sample(qkv, kt, o, gates, grow, ng[l], state_C, l,
                                                             st.get("c_stack"), n0, nrows, m0, mcol, mrow, ts)
                    st["c_stack"] = c_stack
                    n_new = n_new.reshape(bs, HEADS, A_DQK)
                else:
                    u, c_new, n_rep, m_rep = _mlstm_prompt(qkv, kt, o, gates, grow, ng[l])
                    n_new = n_rep[..., 0]
                    m_new = m_rep[..., 0]
                    new_c.append(c_new)
                new_n.append(n_new)
                new_m.append(m_new)
                x1, h2_all, route = _mlstm_out(u, w_out[l], tm, **post)
            else:
                j = l - N_A_LAYERS
                ql, qr = _mla_q(x, scale1, shift1, wdq[j], qg[j], wuq[j], wuk, cos8, sin8, tm)
                if sample:
                    ql_s = jnp.transpose(ql.reshape(HEADS, bs, ts, B_KV_RANK), (1, 0, 2, 3)).reshape(
                        bs, HEADS * ts, B_KV_RANK)
                    qr_s = jnp.transpose(qr.reshape(HEADS, bs, ts, B_D_ROPE), (1, 0, 2, 3)).reshape(
                        bs, HEADS * ts, B_D_ROPE)
                    new_pad = 16
                    nl = jnp.pad(latb.reshape(bs, ts, B_KV_RANK), ((0, 0), (0, new_pad - ts), (0, 0)))
                    nk = jnp.pad(krb.reshape(bs, ts, B_D_ROPE), ((0, 0), (0, new_pad - ts), (0, 0)))
                    ol = _paged_attention(page_table, ql_s, qr_s, cache_latent, cache_krt, nl, nk, ts)
                    ol = jnp.transpose(ol.reshape(bs, HEADS, ts, B_KV_RANK), (1, 0, 2, 3)).reshape(
                        1, HEADS, bs * ts, B_KV_RANK)
                else:
                    ol = _flash(ql, qr, latb, krb, 512)
                x1, h2_all, route = _mla_out(ol, wuv, wo[j], tm, **post)
            moe_in.append((x1, route, gate2))
            row0 += x.shape[0] * x.shape[1]
        for st, x_new in zip(streams, _moe(moe_in, h2_all, lg[1], lb[1], e_w_gate, e_w_up, e_w_down, l, tm)):
            st["x"] = x_new
            if l == N_A_LAYERS - 1:
                st["lat"], st["kr"], st["latb"], st["krb"] = _shared_kv(x_new, w_kv, kvg, st["cos1"], st["sin1"], tm)

    sp, ss = streams
    return (sp["x"], ss["x"].reshape(bs, ts, d),
            jnp.stack(sp["new_c"]), jnp.stack(sp["new_n"]), jnp.stack(sp["new_m"]), sp["lat"], sp["kr"],
            ss["c_stack"], jnp.stack(ss["new_n"]), jnp.stack(ss["new_m"]),
            ss["lat"].reshape(bs, ts, B_KV_RANK), ss["kr"].reshape(bs, ts, B_D_ROPE))
```
